```python
import jax, jax.numpy as jnp
from jax import lax
import numpy as np

D_MODEL = 1024
BATCH = 16
SEQ = 4096
DEPTH = 2
DEC_BATCH = 32
DEC_SEQ = 16
PAST_LEN = 4096

CHUNK = 64
N_META = 16
N_HEADS = 8
QK_NOPE = 128
QK_ROPE = 64
V_DIM = 128
Q_RANK = 384
KV_RANK = 256
ROPE_THETA = 10000.0
ATTN_SCALE = (QK_NOPE + QK_ROPE) ** -0.5
Q_BLOCK = 128
POOL_WINDOWS = (2, 4, 8, 16)
POOL_GROUP = D_MODEL // len(POOL_WINDOWS)
POOL_HIST = max(POOL_WINDOWS) - 1
D_FF = 2816
N_EXPERTS = 8
TOP_K = 2
D_EXPERT = 2816
MOE_BLOCK = 128
RMS_EPS = 1e-6
N_ATTN_LAYERS = (DEPTH + 1) // 2
N_POOL_LAYERS = DEPTH // 2

kernel_name = "meta_mla_pool_moe_stream_step"


def rms_norm(x, g):
    xf = x.astype(jnp.float32)
    y = xf * lax.rsqrt(jnp.mean(xf * xf, axis=-1, keepdims=True) + RMS_EPS)
    return (y * g.astype(jnp.float32)).astype(x.dtype)


def chunk_id(pos):
    return jnp.where(pos < N_META, -1, (pos - N_META) // CHUNK)


def rope(x, pos):
    half = QK_ROPE // 2
    inv = ROPE_THETA ** (-jnp.arange(half, dtype=jnp.float32) / half)
    ang = pos.astype(jnp.float32)[:, None] * inv[None, :]
    shape = (1, pos.shape[0]) + (1,) * (x.ndim - 3) + (half,)
    cos = jnp.cos(ang).reshape(shape).astype(x.dtype)
    sin = jnp.sin(ang).reshape(shape).astype(x.dtype)
    x1, x2 = x[..., :half], x[..., half:]
    return jnp.concatenate([x1 * cos - x2 * sin, x2 * cos + x1 * sin], axis=-1)


def latent_attention(q_lat, q_pe, q_pos, c_kv, k_pe, k_pos):
    B, T = q_lat.shape[:2]
    blk = min(Q_BLOCK, T)
    nblk = -(-T // blk)
    pad = nblk * blk - T
    k_cid = chunk_id(k_pos)
    c32 = c_kv.astype(jnp.float32)

    def prep(a):
        a = jnp.pad(a, [(0, 0), (0, pad)] + [(0, 0)] * (a.ndim - 2))
        return jnp.moveaxis(a.reshape((B, nblk, blk) + a.shape[2:]), 1, 0)

    qpos_b = jnp.pad(q_pos, (0, pad), mode='edge').reshape(nblk, blk)

    def block(args):
        ql, qp, qpos = args
        s = (jnp.einsum('bqhc,bkc->bhqk', ql, c_kv, preferred_element_type=jnp.float32)
             + jnp.einsum('bqhr,bkr->bhqk', qp, k_pe, preferred_element_type=jnp.float32))
        mask = k_cid[None, :] <= chunk_id(qpos)[:, None]
        s = jnp.where(mask[None, None], s * ATTN_SCALE, -jnp.inf)
        p = jax.nn.softmax(s, axis=-1)
        return jnp.einsum('bhqk,bkc->bqhc', p, c32).astype(ql.dtype)

    o = lax.map(block, (prep(q_lat), prep(q_pe), qpos_b))
    return jnp.moveaxis(o, 0, 1).reshape(B, nblk * blk, N_HEADS, KV_RANK)[:, :T]


def mla_mixer(u, cache_c, cache_r, w_dq, g_q, w_uq, w_dkv, g_kv, w_uk, w_uv, w_o):
    B, T, _ = u.shape
    start = cache_c.shape[1]
    q_pos = start + jnp.arange(T)
    k_pos = jnp.arange(start + T)
    cq = rms_norm(u @ w_dq, g_q)
    q = (cq @ w_uq).reshape(B, T, N_HEADS, QK_NOPE + QK_ROPE)
    q_nope = q[..., :QK_NOPE]
    q_pe = rope(q[..., QK_NOPE:], q_pos)
    kv = u @ w_dkv
    c_new = rms_norm(kv[..., :KV_RANK], g_kv)
    r_new = rope(kv[..., KV_RANK:], q_pos)
    q_lat = jnp.einsum('bthn,chn->bthc', q_nope, w_uk)
    c_all = jnp.concatenate([cache_c.astype(c_new.dtype), c_new], axis=1)
    r_all = jnp.concatenate([cache_r.astype(r_new.dtype), r_new], axis=1)
    o_lat = latent_attention(q_lat, q_pe, q_pos, c_all, r_all, k_pos)
    o = jnp.einsum('bthc,chv->bthv', o_lat, w_uv).reshape(B, T, N_HEADS * V_DIM)
    return o @ w_o, c_new, r_new


def pool_mixer(u, hist, start, w_grp, scale):
    T = u.shape[1]
    ext = jnp.concatenate([hist.astype(u.dtype), u], axis=1)
    cs = jnp.cumsum(ext.astype(jnp.float32), axis=1)
    cs = jnp.pad(cs, [(0, 0), (1, 0), (0, 0)])
    abs_pos = start + jnp.arange(T)
    uf = u.astype(jnp.float32)
    hi = cs[:, POOL_HIST + 1:POOL_HIST + 1 + T]
    outs = []
    for g, w in enumerate(POOL_WINDOWS):
        sl = slice(g * POOL_GROUP, (g + 1) * POOL_GROUP)
        lo = cs[:, POOL_HIST + 1 - w:POOL_HIST + 1 - w + T, sl]
        cnt = jnp.minimum(w, abs_pos + 1).astype(jnp.float32)[None, :, None]
        m = (hi[..., sl] - lo) / cnt - uf[..., sl]
        outs.append(jnp.einsum('btg,gh->bth', m.astype(u.dtype), w_grp[g]))
    y = jnp.concatenate(outs, axis=-1) * scale
    return y, ext[:, -POOL_HIST:]


def swiglu(u, w_gate, w_up, w_down):
    return (jax.nn.silu(u @ w_gate) * (u @ w_up)) @ w_down


def moe_swiglu(u, w_router, b_router, w_gate, w_up, w_down):
    B, T, D = u.shape
    N = B * T
    x = u.reshape(N, D)
    logits = x.astype(jnp.float32) @ w_router.astype(jnp.float32) + b_router.astype(jnp.float32)
    top_v, top_e = lax.top_k(logits, TOP_K)
    gates = jax.nn.softmax(top_v, axis=-1)
    A = N * TOP_K
    flat_e = top_e.reshape(A)
    flat_t = jnp.arange(A) // TOP_K
    flat_g = gates.reshape(A)
    order = jnp.argsort(flat_e)
    se, st, sg = flat_e[order], flat_t[order], flat_g[order]
    counts = jnp.zeros((N_EXPERTS,), jnp.int32).at[flat_e].add(1)
    padded = (counts + MOE_BLOCK - 1) // MOE_BLOCK * MOE_BLOCK
    ustart = jnp.cumsum(counts) - counts
    pend = jnp.cumsum(padded)
    pstart = pend - padded
    dest = pstart[se] + jnp.arange(A) - ustart[se]
    P = (A + MOE_BLOCK - 1) // MOE_BLOCK * MOE_BLOCK + N_EXPERTS * MOE_BLOCK
    nb = P // MOE_BLOCK
    buf_t = jnp.full((P,), N, jnp.int32).at[dest].set(st)
    buf_g = jnp.zeros((P,), jnp.float32).at[dest].set(sg)
    blk_e = jnp.minimum(jnp.searchsorted(pend, jnp.arange(nb) * MOE_BLOCK, side='right'), N_EXPERTS - 1)
    x_pad = jnp.concatenate([x, jnp.zeros((1, D), x.dtype)], axis=0)
    xs = x_pad[buf_t].reshape(nb, MOE_BLOCK, D)

    def expert_block(args):
        xb, e = args
        return (jax.nn.silu(xb @ w_gate[e]) * (xb @ w_up[e])) @ w_down[e]

    ys = lax.map(expert_block, (xs, blk_e)).reshape(P, D)
    ys = ys * buf_g[:, None].astype(ys.dtype)
    y = jnp.zeros((N + 1, D), ys.dtype).at[buf_t].add(ys)[:N]
    return y.reshape(B, T, D)


def trunk(h, cache_c, cache_r, pool_hist, p):
    start = cache_c.shape[2]
    new_c, new_r, new_s = [], [], []
    for i in range(DEPTH):
        j = i // 2
        u = rms_norm(h, p['norm_mix'][i])
        if i % 2 == 0:
            mix, c_new, r_new = mla_mixer(u, cache_c[j], cache_r[j], p['mla_w_dq'][j], p['mla_g_q'][j],
                                          p['mla_w_uq'][j], p['mla_w_dkv'][j], p['mla_g_kv'][j],
                                          p['mla_w_uk'][j], p['mla_w_uv'][j], p['mla_w_o'][j])
            new_c.append(c_new)
            new_r.append(r_new)
        else:
            mix, s_new = pool_mixer(u, pool_hist[j], start, p['pool_w'][j], p['pool_scale'][j])
            new_s.append(s_new)
        h = h + mix
        u = rms_norm(h, p['norm_ffn'][i])
        if i % 2 == 0:
            ff = swiglu(u, p['ffn_w_gate'][j], p['ffn_w_up'][j], p['ffn_w_down'][j])
        else:
            ff = moe_swiglu(u, p['moe_w_router'][j], p['moe_b_router'][j], p['moe_w_gate'][j],
                            p['moe_w_up'][j], p['moe_w_down'][j])
        h = h + ff
    return rms_norm(h, p['norm_final']), jnp.stack(new_c), jnp.stack(new_r), jnp.stack(new_s)


def setup_inputs(seed: int = 0) -> dict:
    key = jax.random.key(seed)
    ks = jax.random.split(key, 32)

    def nrm(k, shape, scale=1.0):
        return jax.random.normal(k, shape, jnp.float32) * scale

    def gain(k, shape):
        return 1.0 + 0.05 * jax.random.normal(k, shape, jnp.float32)

    NA, NP = N_ATTN_LAYERS, N_POOL_LAYERS
    L_CACHE = N_META + PAST_LEN
    return {
        "x_prompt": nrm(ks[0], (BATCH, SEQ, D_MODEL)),
        "x_sample": nrm(ks[1], (DEC_BATCH, DEC_SEQ, D_MODEL)),
        "cache_kv_latent": nrm(ks[2], (NA, DEC_BATCH, L_CACHE, KV_RANK)),
        "cache_k_rope": nrm(ks[3], (NA, DEC_BATCH, L_CACHE, QK_ROPE)),
        "state_pool": nrm(ks[4], (NP, DEC_BATCH, POOL_HIST, D_MODEL)),
        "meta_tokens": nrm(ks[5], (N_META, D_MODEL)),
        "norm_mix": gain(ks[6], (DEPTH, D_MODEL)),
        "norm_ffn": gain(ks[7], (DEPTH, D_MODEL)),
        "norm_final": gain(ks[8], (D_MODEL,)),
        "mla_w_dq": nrm(ks[9], (NA, D_MODEL, Q_RANK), D_MODEL ** -0.5),
        "mla_g_q": gain(ks[10], (NA, Q_RANK)),
        "mla_w_uq": nrm(ks[11], (NA, Q_RANK, N_HEADS * (QK_NOPE + QK_ROPE)), Q_RANK ** -0.5),
        "mla_w_dkv": nrm(ks[12], (NA, D_MODEL, KV_RANK + QK_ROPE), D_MODEL ** -0.5),
        "mla_g_kv": gain(ks[13], (NA, KV_RANK)),
        "mla_w_uk": nrm(ks[14], (NA, KV_RANK, N_HEADS, QK_NOPE), KV_RANK ** -0.5),
        "mla_w_uv": nrm(ks[15], (NA, KV_RANK, N_HEADS, V_DIM), KV_RANK ** -0.5),
        "mla_w_o": nrm(ks[16], (NA, N_HEADS * V_DIM, D_MODEL), (N_HEADS * V_DIM) ** -0.5),
        "pool_w": nrm(ks[17], (NP, len(POOL_WINDOWS), POOL_GROUP, POOL_GROUP), POOL_GROUP ** -0.5),
        "pool_scale": gain(ks[18], (NP, D_MODEL)),
        "ffn_w_gate": nrm(ks[19], (NA, D_MODEL, D_FF), D_MODEL ** -0.5),
        "ffn_w_up": nrm(ks[20], (NA, D_MODEL, D_FF), D_MODEL ** -0.5),
        "ffn_w_down": nrm(ks[21], (NA, D_FF, D_MODEL), D_FF ** -0.5),
        "moe_w_router": nrm(ks[22], (NP, D_MODEL, N_EXPERTS), D_MODEL ** -0.5),
        "moe_b_router": nrm(ks[23], (NP, N_EXPERTS), 0.01),
        "moe_w_gate": nrm(ks[24], (NP, N_EXPERTS, D_MODEL, D_EXPERT), D_MODEL ** -0.5),
        "moe_w_up": nrm(ks[25], (NP, N_EXPERTS, D_MODEL, D_EXPERT), D_MODEL ** -0.5),
        "moe_w_down": nrm(ks[26], (NP, N_EXPERTS, D_EXPERT, D_MODEL), D_EXPERT ** -0.5),
    }


def reference(x_prompt, x_sample, cache_kv_latent, cache_k_rope, state_pool, meta_tokens,
              norm_mix, norm_ffn, norm_final,
              mla_w_dq, mla_g_q, mla_w_uq, mla_w_dkv, mla_g_kv, mla_w_uk, mla_w_uv, mla_w_o,
              pool_w, pool_scale, ffn_w_gate, ffn_w_up, ffn_w_down,
              moe_w_router, moe_b_router, moe_w_gate, moe_w_up, moe_w_down):
    p = {
        'norm_mix': norm_mix, 'norm_ffn': norm_ffn, 'norm_final': norm_final,
        'mla_w_dq': mla_w_dq, 'mla_g_q': mla_g_q, 'mla_w_uq': mla_w_uq, 'mla_w_dkv': mla_w_dkv,
        'mla_g_kv': mla_g_kv, 'mla_w_uk': mla_w_uk, 'mla_w_uv': mla_w_uv, 'mla_w_o': mla_w_o,
        'pool_w': pool_w, 'pool_scale': pool_scale,
        'ffn_w_gate': ffn_w_gate, 'ffn_w_up': ffn_w_up, 'ffn_w_down': ffn_w_down,
        'moe_w_router': moe_w_router, 'moe_b_router': moe_b_router, 'moe_w_gate': moe_w_gate,
        'moe_w_up': moe_w_up, 'moe_w_down': moe_w_down,
    }
    B = x_prompt.shape[0]
    meta = jnp.broadcast_to(meta_tokens[None].astype(x_prompt.dtype), (B, N_META, D_MODEL))
    h_p = jnp.concatenate([meta, x_prompt], axis=1)
    empty_c = jnp.zeros((N_ATTN_LAYERS, B, 0, KV_RANK), x_prompt.dtype)
    empty_r = jnp.zeros((N_ATTN_LAYERS, B, 0, QK_ROPE), x_prompt.dtype)
    zero_hist = jnp.zeros((N_POOL_LAYERS, B, POOL_HIST, D_MODEL), x_prompt.dtype)
    out_p, c_p, r_p, s_p = trunk(h_p, empty_c, empty_r, zero_hist, p)
    y_prompt = out_p[:, N_META:]
    y_sample, c_s, r_s, s_s = trunk(x_sample, cache_kv_latent, cache_k_rope, state_pool, p)
    return (y_prompt, y_sample, c_p, r_p, s_p, c_s, r_s, s_s)
```

```python
import functools

import jax
import jax.numpy as jnp
from jax import lax
from jax.experimental import pallas as pl
from jax.experimental.pallas import tpu as pltpu

CHUNK = 64
N_META = 16
N_HEADS = 8
QK_NOPE = 128
QK_ROPE = 64
ROPE_THETA = 10000.0
POOL_WINDOWS = (2, 4, 8, 16)
POOL_HIST = max(POOL_WINDOWS) - 1
TOP_K = 2
RMS_EPS = 1e-6

LANE = 128
SEQ_S = 16
HIST_ROWS = 16
TOKEN_BLOCK = 256
ROW_BLOCK = 256
ATTN_BLOCK = 256
CACHE_BLOCK = 512
VMEM_LIMIT = 56 * 1024 * 1024

F32 = jnp.float32
BF16 = jnp.bfloat16
NEG_INF = float("-inf")


def _dot(a, b):
    return jnp.dot(a, b, preferred_element_type=F32)


def _dot_nt(a, b):
    return lax.dot_general(a, b, (((1,), (1,)), ((), ())), preferred_element_type=F32)


def _rms(x, g):
    return x * lax.rsqrt(jnp.mean(x * x, axis=-1, keepdims=True) + RMS_EPS) * g


def _const_spec(shape):
    nd = len(shape)
    return pl.BlockSpec(shape, lambda *_: (0,) * nd, pipeline_mode=pl.Buffered(1))


def _params(sem):
    return pltpu.CompilerParams(dimension_semantics=sem, vmem_limit_bytes=VMEM_LIMIT)


def _qkv_kernel(n_fb, scale, xp_ref, xs_ref, tab_ref, g_ref, wdq_ref, gq_ref, wuq_ref, wuk_ref, wdkv_ref, gkv_ref,
                q_ref, kvb_ref, c_ref, r_ref):
    i = pl.program_id(0)
    x = jnp.where(i < n_fb, xp_ref[...], xs_ref[...])
    u = _rms(x, g_ref[...]).astype(BF16)
    cq = _rms(_dot(u, wdq_ref[...]), gq_ref[...]).astype(BF16)
    qa = _dot(cq, wuq_ref[...])
    cos = tab_ref[:, :LANE]
    sin = tab_ref[:, LANE:]
    hq = QK_NOPE + 2 * LANE
    for h in range(N_HEADS):
        qn = qa[:, h * LANE:(h + 1) * LANE].astype(BF16)
        ql = _dot(qn, wuk_ref[h])
        q_ref[:, h * hq:h * hq + 2 * LANE] = (ql * scale).astype(BF16)
        a = qa[:, (N_HEADS + h) * LANE:(N_HEADS + h + 1) * LANE]
        b = qa[:, (2 * N_HEADS + h) * LANE:(2 * N_HEADS + h + 1) * LANE]
        q_ref[:, h * hq + 2 * LANE:(h + 1) * hq] = ((a * cos + b * sin) * scale).astype(BF16)
    kv = _dot(u, wdkv_ref[...])
    c = _rms(kv[:, :2 * LANE], gkv_ref[...])
    r = kv[:, 2 * LANE:3 * LANE] * cos + kv[:, 3 * LANE:] * sin
    c_ref[...] = c
    r_ref[...] = r[:, :QK_ROPE]
    kvb_ref[:, :2 * LANE] = c.astype(BF16)
    kvb_ref[:, 2 * LANE:] = r.astype(BF16)


def _attn_frames_kernel(qb, n_q, q_ref, kv_ref, kvm_ref, o_ref):
    i = pl.program_id(0)

    @pl.when(i >= n_q)
    def _():
        o_ref[...] = jnp.zeros_like(o_ref)

    @pl.when(i < n_q)
    def _():
        j = i % qb
        tq = q_ref.shape[0]
        kvm = kvm_ref[...]
        hq = q_ref.shape[1] // N_HEADS
        row_chunk = (j * tq + lax.broadcasted_iota(jnp.int32, (tq, tq), 0)) // CHUNK
        col_in_blk = lax.broadcasted_iota(jnp.int32, (tq, tq), 1)
        for h in range(N_HEADS):
            qh = q_ref[:, h * hq:(h + 1) * hq]
            s = _dot_nt(qh, kvm)
            m = jnp.max(s, axis=1, keepdims=True)
            p = jnp.exp(s - m)
            l = jnp.sum(p, axis=1, keepdims=True)
            acc = _dot(p.astype(BF16), kvm[:, :2 * LANE])

            def body(kb, carry, qh=qh):
                m, l, acc = carry
                kblk = kv_ref[pl.ds(pl.multiple_of(kb * tq, tq), tq), :]
                s = _dot_nt(qh, kblk)
                s = jnp.where((kb * tq + col_in_blk) // CHUNK <= row_chunk, s, NEG_INF)
                m_new = jnp.maximum(m, jnp.max(s, axis=1, keepdims=True))
                alpha = jnp.exp(m - m_new)
                p = jnp.exp(s - m_new)
                l = alpha * l + jnp.sum(p, axis=1, keepdims=True)
                acc = alpha * acc + _dot(p.astype(BF16), kblk[:, :2 * LANE])
                return m_new, l, acc

            m, l, acc = lax.fori_loop(0, j + 1, body, (m, l, acc))
            o_ref[:, h * 2 * LANE:(h + 1) * 2 * LANE] = (acc / l).astype(BF16)


def _attn_short_kernel(n_cached_seq, o_frames_hbm, q_ref, kvn_ref, cc_ref, cr_ref, o_ref):
    del o_frames_hbm
    s_id = pl.program_id(0)
    hq = q_ref.shape[1] // N_HEADS
    n_cache = cc_ref.shape[2]
    qs = jnp.concatenate([q_ref[:, h * hq:(h + 1) * hq] for h in range(N_HEADS)], axis=0)
    q_lat = qs[:, :2 * LANE]
    q_pe = qs[:, 2 * LANE:2 * LANE + QK_ROPE]
    kvn = kvn_ref[...]
    s = _dot_nt(qs, kvn)
    m = jnp.max(s, axis=1, keepdims=True)
    p = jnp.exp(s - m)
    l = jnp.sum(p, axis=1, keepdims=True)
    acc = _dot(p.astype(BF16), kvn[:, :2 * LANE])
    has_cache = s_id < n_cached_seq
    start = 0
    while start < n_cache:
        size = min(CACHE_BLOCK, n_cache - start)
        ck = cc_ref[0, 0, start:start + size, :].astype(BF16)
        rk = cr_ref[0, 0, start:start + size, :].astype(BF16)
        s = _dot_nt(q_lat, ck) + _dot_nt(q_pe, rk)
        s = jnp.where(has_cache, s, NEG_INF)
        m_new = jnp.maximum(m, jnp.max(s, axis=1, keepdims=True))
        alpha = jnp.exp(m - m_new)
        p = jnp.exp(s - m_new)
        l = alpha * l + jnp.sum(p, axis=1, keepdims=True)
        acc = alpha * acc + _dot(p.astype(BF16), ck)
        m = m_new
        start += size
    o = (acc / l).astype(BF16)
    for h in range(N_HEADS):
        o_ref[:, h * 2 * LANE:(h + 1) * 2 * LANE] = o[h * SEQ_S:(h + 1) * SEQ_S, :]


def _proj_ffn_kernel(n_fb, xp_ref, xs_ref, ol_ref, wuv_ref, wo_ref, g_ref, wg_ref, wu_ref, wd_ref, h_ref):
    i = pl.program_id(0)
    x = jnp.where(i < n_fb, xp_ref[...], xs_ref[...])
    o = jnp.concatenate(
        [_dot(ol_ref[:, h * 2 * LANE:(h + 1) * 2 * LANE], wuv_ref[h]) for h in range(N_HEADS)], axis=1)
    h1 = x + _dot(o.astype(BF16), wo_ref[...])
    u = _rms(h1, g_ref[...]).astype(BF16)
    act = jax.nn.silu(_dot(u, wg_ref[...])) * _dot(u, wu_ref[...])
    h_ref[...] = h1 + _dot(act.astype(BF16), wd_ref[...])


def _pool_route(h1, ext_ref, cnt_rows, valid, base, gm_ref, pw_ref, ps_ref, gf_ref, wr_ref, br_ref, tri_ref):
    t = h1.shape[0]
    grp = h1.shape[1] // len(POOL_WINDOWS)
    u1 = _rms(h1, gm_ref[...])
    ext_ref[HIST_ROWS:HIST_ROWS + t, :] = u1
    ys = []
    for g, w in enumerate(POOL_WINDOWS):
        cols = slice(g * grp, (g + 1) * grp)
        acc = u1[:, cols]
        for k in range(1, w):
            acc = acc + ext_ref[HIST_ROWS - k:HIST_ROWS - k + t, cols]
        cnt = float(w) if cnt_rows is None else jnp.minimum(float(w), cnt_rows)
        mean = acc / cnt - u1[:, cols]
        ys.append(_dot(mean.astype(BF16), pw_ref[g]))
    h2 = h1 + jnp.concatenate(ys, axis=1) * ps_ref[...]
    u2 = _rms(h2, gf_ref[...])

    n_e = wr_ref.shape[1]
    logits = jnp.dot(u2, wr_ref[...], preferred_element_type=F32, precision=lax.Precision.HIGHEST) + br_ref[...]
    e_iota = lax.broadcasted_iota(jnp.int32, logits.shape, 1).astype(F32)
    v0 = jnp.max(logits, axis=1, keepdims=True)
    e0 = jnp.min(jnp.where(logits == v0, e_iota, float(n_e)), axis=1, keepdims=True)
    rest = jnp.where(e_iota == e0, NEG_INF, logits)
    v1 = jnp.max(rest, axis=1, keepdims=True)
    e1 = jnp.min(jnp.where(rest == v1, e_iota, float(n_e)), axis=1, keepdims=True)
    tt = jnp.exp(v1 - v0)
    g0 = 1.0 / (1.0 + tt)
    g1 = tt / (1.0 + tt)
    hit0 = e_iota == e0
    hit1 = e_iota == e1
    onehot = jnp.where(valid & (hit0 | hit1), 1.0, 0.0)
    rank = _dot(tri_ref[...], onehot.astype(BF16)) + base
    r0 = jnp.sum(jnp.where(hit0, rank, 0.0), axis=1, keepdims=True)
    r1 = jnp.sum(jnp.where(hit1, rank, 0.0), axis=1, keepdims=True)
    cols = (jnp.where(valid, e0, -1).astype(F32), jnp.where(valid, e1, -1).astype(F32), r0, r1, g0, g1)
    info = jnp.zeros(logits.shape, F32)
    for k, col in enumerate(cols):
        info = jnp.where(e_iota == k, col, info)
    new_base = base + jnp.sum(onehot, axis=0, keepdims=True)
    return h2, u2, info, new_base, u1


def _pool_frames_kernel(n_fb, h_ref, hist_ref, gm_ref, pw_ref, ps_ref, gf_ref, wr_ref, br_ref, tri_ref,
                        h2_ref, u2_ref, info_ref, cum_ref, tot_ref, state_ref, ext_ref, base_ref):
    i = pl.program_id(0)

    @pl.when(i == 0)
    def _():
        base_ref[...] = jnp.zeros_like(base_ref)

    @pl.when(i >= n_fb)
    def _():
        h2_ref[...] = jnp.zeros_like(h2_ref)
        u2_ref[...] = jnp.zeros_like(u2_ref)
        info_ref[...] = jnp.zeros_like(info_ref)

    @pl.when(i < n_fb)
    def _():
        ext_ref[:HIST_ROWS, :] = _rms(hist_ref[...], gm_ref[...])
        base = base_ref[...]
        cum_ref[0] = base
        h2, u2, info, new_base, u1 = _pool_route(h_ref[...], ext_ref, None, True, base, gm_ref, pw_ref, ps_ref,
                                                 gf_ref, wr_ref, br_ref, tri_ref)
        h2_ref[...] = h2
        u2_ref[...] = u2.astype(BF16)
        info_ref[...] = info
        base_ref[...] = new_base
        tot_ref[...] = new_base
        state_ref[0] = u1[u1.shape[0] - HIST_ROWS:, :]


def _pool_short_kernel(n_sample, n_valid, h2_hbm, u2_hbm, info_hbm, h_ref, hist_ref, base0_ref, gm_ref, pw_ref, ps_ref,
                       gf_ref, wr_ref, br_ref, tri_ref, h2_ref, u2_ref, info_ref, cum_ref, tot_ref, state_ref,
                       ext_ref, base_ref):
    del h2_hbm, u2_hbm, info_hbm
    s_id = pl.program_id(0)

    @pl.when(s_id == 0)
    def _():
        base_ref[...] = base0_ref[...]

    ext_ref[:HIST_ROWS, :] = hist_ref[0]
    base = base_ref[...]
    cum_ref[0] = base
    pos1 = (lax.broadcasted_iota(jnp.int32, (SEQ_S, 1), 0) + 1).astype(F32)
    cnt_rows = jnp.where(s_id < n_sample, float(max(POOL_WINDOWS)), pos1)
    h2, u2, info, new_base, u1 = _pool_route(h_ref[...], ext_ref, cnt_rows, s_id < n_valid, base, gm_ref, pw_ref,
                                             ps_ref, gf_ref, wr_ref, br_ref, tri_ref)
    h2_ref[...] = h2
    u2_ref[...] = u2.astype(BF16)
    info_ref[...] = info
    base_ref[...] = new_base
    tot_ref[...] = new_base
    state_ref[0] = u1


def _expert_kernel(be_ref, lo_ref, hi_ref, ok_ref, u_hbm, pos_ref, gate_ref, wg_ref, wu_ref, wd_ref, ys_ref,
                   buf_ref, acc_ref, gacc_ref, sem):
    j = pl.program_id(0)
    rb = ys_ref.shape[0]
    tc = buf_ref.shape[1]

    def chunk_copy(c, slot):
        return pltpu.make_async_copy(u_hbm.at[pl.ds(pl.multiple_of(c * tc, tc), tc), :], buf_ref.at[slot],
                                     sem.at[slot])

    @pl.when(ok_ref[j] == 0)
    def _():
        ys_ref[...] = jnp.zeros_like(ys_ref)

    @pl.when(ok_ref[j] != 0)
    def _():
        lo = lo_ref[j]
        hi = hi_ref[j]
        acc_ref[...] = jnp.zeros_like(acc_ref)
        gacc_ref[...] = jnp.zeros_like(gacc_ref)
        rows = j * rb + lax.broadcasted_iota(jnp.int32, (rb, tc), 0)
        chunk_copy(lo, 0).start()

        def body(c, carry):
            slot = (c - lo) % 2
            chunk_copy(c, slot).wait()

            @pl.when(c < hi)
            def _():
                chunk_copy(c + 1, 1 - slot).start()

            pos = pos_ref[c]
            gate = gate_ref[c]
            m0 = rows == pos[0:1, :]
            m1 = rows == pos[1:2, :]
            sel = jnp.where(m0 | m1, 1.0, 0.0).astype(BF16)
            acc_ref[...] += _dot(sel, buf_ref[slot])
            gacc_ref[...] += jnp.sum(jnp.where(m0, gate[0:1, :], 0.0) + jnp.where(m1, gate[1:2, :], 0.0),
                                     axis=1, keepdims=True)
            return carry

        lax.fori_loop(lo, hi + 1, body, 0)
        xg = acc_ref[...].astype(BF16)
        act = jax.nn.silu(_dot(xg, wg_ref[0])) * _dot(xg, wu_ref[0])
        ys_ref[...] = (_dot(act.astype(BF16), wd_ref[0]) * gacc_ref[...]).astype(BF16)


def _combine_kernel(max_n, n_ref, ids_ref, h2_ref, pos_ref, g_ref, ys_hbm, y_ref, buf_ref, acc_ref, sem):
    i = pl.program_id(0)
    tb = h2_ref.shape[0]
    rb = buf_ref.shape[1]
    n = n_ref[i]

    def chunk_copy(k, slot):
        ch = ids_ref[i * max_n + k]
        return pltpu.make_async_copy(ys_hbm.at[pl.ds(pl.multiple_of(ch * rb, rb), rb), :], buf_ref.at[slot],
                                     sem.at[slot])

    acc_ref[...] = jnp.zeros_like(acc_ref)
    p0 = pos_ref[:, 0:1]
    p1 = pos_ref[:, 1:2]
    lane_rows = lax.broadcasted_iota(jnp.int32, (tb, rb), 1)

    @pl.when(n > 0)
    def _():
        chunk_copy(0, 0).start()

    def body(k, carry):
        slot = k % 2
        chunk_copy(k, slot).wait()

        @pl.when(k + 1 < n)
        def _():
            chunk_copy(k + 1, 1 - slot).start()

        rows = ids_ref[i * max_n + k] * rb + lane_rows
        sel = jnp.where((rows == p0) | (rows == p1), 1.0, 0.0).astype(BF16)
        acc_ref[...] += _dot(sel, buf_ref[slot])
        return carry

    lax.fori_loop(0, n, body, 0)
    y_ref[...] = _rms(h2_ref[...] + acc_ref[...], g_ref[...])


def _rope_table(pos):
    half = QK_ROPE // 2
    inv = ROPE_THETA ** (-jnp.arange(half, dtype=F32) / half)
    ang = pos.astype(F32)[:, None] * inv[None, :]
    cos = jnp.cos(ang)
    sin = jnp.sin(ang)
    zero = jnp.zeros((pos.shape[0], LANE - QK_ROPE), F32)
    return jnp.concatenate([cos, cos, zero, -sin, sin, zero], axis=1)


def _half_swap(w):
    half = QK_ROPE // 2
    return jnp.concatenate([w[..., half:], w[..., :half]], axis=-1)


def _pad_last(w, width):
    return jnp.pad(w, [(0, 0)] * (w.ndim - 1) + [(0, width - w.shape[-1])])


def kernel(x_prompt, x_sample, cache_kv_latent, cache_k_rope, state_pool, meta_tokens, norm_mix, norm_ffn, norm_final,
           mla_w_dq, mla_g_q, mla_w_uq, mla_w_dkv, mla_g_kv, mla_w_uk, mla_w_uv, mla_w_o, pool_w, pool_scale,
           ffn_w_gate, ffn_w_up, ffn_w_down, moe_w_router, moe_b_router, moe_w_gate, moe_w_up, moe_w_down):
    nb, seq, d = x_prompt.shape
    db, dseq, _ = x_sample.shape
    n_cache = cache_kv_latent.shape[2]
    q_rank = mla_w_dq.shape[2]
    kv_rank = mla_g_kv.shape[1]
    v_dim = mla_w_uv.shape[3]
    d_ff = ffn_w_gate.shape[2]
    n_e = moe_w_router.shape[2]
    d_e = moe_w_gate.shape[3]
    tb, rb, tq = TOKEN_BLOCK, ROW_BLOCK, ATTN_BLOCK
    assert norm_mix.shape[0] == 2 and cache_kv_latent.shape[0] == 1 and state_pool.shape[0] == 1
    assert dseq == SEQ_S and N_META == SEQ_S and meta_tokens.shape[0] == N_META
    assert kv_rank == 2 * LANE and QK_NOPE == LANE and QK_ROPE <= LANE and HIST_ROWS >= POOL_HIST
    assert seq % tb == 0 and seq % tq == 0 and tb % tq == 0 and tq % CHUNK == 0 and tb % SEQ_S == 0 and d % LANE == 0
    assert (n_cache - N_META) % CHUNK == 0 and dseq <= CHUNK

    nf = nb * seq
    n_valid_seq = db + 1
    ns = -(-(n_valid_seq * SEQ_S) // tb) * tb
    n_seq = ns // SEQ_S
    nt = nf + ns
    n_fb, n_sb, n_tb = nf // tb, ns // tb, nt // tb
    sb = seq // tb
    meta_row = nf + db * SEQ_S
    scale = float((QK_NOPE + QK_ROPE) ** -0.5)
    hq = 3 * LANE

    xp = x_prompt.reshape(nf, d)
    xs = jnp.concatenate([x_sample.reshape(db * SEQ_S, d), meta_tokens.astype(x_prompt.dtype),
                          jnp.zeros((ns - n_valid_seq * SEQ_S, d), x_prompt.dtype)], axis=0)

    t_s = jnp.arange(SEQ_S)
    pos_short = jnp.concatenate([jnp.tile(n_cache + t_s, db), jnp.tile(t_s, n_seq - db)])
    tab = jnp.concatenate([_rope_table(N_META + jnp.arange(seq)), _rope_table(pos_short)], axis=0)

    wuq = mla_w_uq[0].reshape(q_rank, N_HEADS, QK_NOPE + QK_ROPE)
    wuq_pe = wuq[:, :, QK_NOPE:]
    wuq2 = jnp.concatenate([wuq[:, :, :QK_NOPE].reshape(q_rank, -1),
                            _pad_last(wuq_pe, LANE).reshape(q_rank, -1),
                            _pad_last(_half_swap(wuq_pe), LANE).reshape(q_rank, -1)], axis=1).astype(BF16)
    wdkv_r = mla_w_dkv[0][:, kv_rank:]
    wdkv2 = jnp.concatenate([mla_w_dkv[0][:, :kv_rank], _pad_last(wdkv_r, LANE),
                             _pad_last(_half_swap(wdkv_r), LANE)], axis=1).astype(BF16)
    wuk_t = jnp.transpose(mla_w_uk[0], (1, 2, 0)).astype(BF16)
    wuv = jnp.transpose(mla_w_uv[0], (1, 0, 2)).astype(BF16)
    row = lambda v: v.reshape(1, -1)

    tok_p = pl.BlockSpec((tb, d), lambda i: (jnp.minimum(i, n_fb - 1), 0))
    tok_s = pl.BlockSpec((tb, d), lambda i: (jnp.maximum(i - n_fb, 0), 0))
    q_all, kvb, c_all, r_all = pl.pallas_call(
        functools.partial(_qkv_kernel, n_fb, scale),
        grid=(n_tb,),
        in_specs=[tok_p, tok_s,
                  pl.BlockSpec((tb, 2 * LANE), lambda i: (jnp.where(i < n_fb, i % sb, sb + i - n_fb), 0)),
                  _const_spec((1, d)), _const_spec((d, q_rank)), _const_spec((1, q_rank)),
                  _const_spec(wuq2.shape), _const_spec(wuk_t.shape), _const_spec(wdkv2.shape),
                  _const_spec((1, kv_rank))],
        out_specs=[pl.BlockSpec((tb, N_HEADS * hq), lambda i: (i, 0)),
                   pl.BlockSpec((tb, hq), lambda i: (i, 0)),
                   pl.BlockSpec((tb, kv_rank), lambda i: (i, 0)),
                   pl.BlockSpec((tb, QK_ROPE), lambda i: (i, 0))],
        out_shape=[jax.ShapeDtypeStruct((nt, N_HEADS * hq), BF16), jax.ShapeDtypeStruct((nt, hq), BF16),
                   jax.ShapeDtypeStruct((nt, kv_rank), F32), jax.ShapeDtypeStruct((nt, QK_ROPE), F32)],
        compiler_params=_params(("parallel",)),
        name="qkv",
    )(xp, xs, tab, row(norm_mix[0]), mla_w_dq[0].astype(BF16), row(mla_g_q[0]), wuq2, wuk_t, wdkv2, row(mla_g_kv[0]))

    qb = seq // tq
    n_q = nb * qb
    o_frames = pl.pallas_call(
        functools.partial(_attn_frames_kernel, qb, n_q),
        grid=(nt // tq,),
        in_specs=[pl.BlockSpec((tq, N_HEADS * hq), lambda i: (i, 0)),
                  pl.BlockSpec((seq, hq), lambda i: (jnp.minimum(i // qb, nb - 1), 0)),
                  pl.BlockSpec((N_META, hq), lambda i: (meta_row // N_META, 0))],
        out_specs=pl.BlockSpec((tq, N_HEADS * kv_rank), lambda i: (i, 0)),
        out_shape=jax.ShapeDtypeStruct((nt, N_HEADS * kv_rank), BF16),
        compiler_params=_params(("parallel",)),
        name="attn_frames",
    )(q_all, kvb, kvb)

    short_blk = lambda s: (nf // SEQ_S + s, 0)
    o_lat = pl.pallas_call(
        functools.partial(_attn_short_kernel, db),
        grid=(n_seq,),
        in_specs=[pl.BlockSpec(memory_space=pl.ANY),
                  pl.BlockSpec((SEQ_S, N_HEADS * hq), short_blk),
                  pl.BlockSpec((SEQ_S, hq), short_blk),
                  pl.BlockSpec((1, 1, n_cache, kv_rank), lambda s: (0, jnp.minimum(s, db - 1), 0, 0)),
                  pl.BlockSpec((1, 1, n_cache, QK_ROPE), lambda s: (0, jnp.minimum(s, db - 1), 0, 0))],
        out_specs=pl.BlockSpec((SEQ_S, N_HEADS * kv_rank), short_blk),
        out_shape=jax.ShapeDtypeStruct((nt, N_HEADS * kv_rank), BF16),
        input_output_aliases={0: 0},
        compiler_params=_params(("parallel",)),
        name="attn_short",
    )(o_frames, q_all, kvb, cache_kv_latent, cache_k_rope)

    h1 = pl.pallas_call(
        functools.partial(_proj_ffn_kernel, n_fb),
        grid=(n_tb,),
        in_specs=[tok_p, tok_s,
                  pl.BlockSpec((tb, N_HEADS * kv_rank), lambda i: (i, 0)),
                  _const_spec(wuv.shape), _const_spec((N_HEADS * v_dim, d)), _const_spec((1, d)),
                  _const_spec((d, d_ff)), _const_spec((d, d_ff)), _const_spec((d_ff, d))],
        out_specs=pl.BlockSpec((tb, d), lambda i: (i, 0)),
        out_shape=jax.ShapeDtypeStruct((nt, d), F32),
        compiler_params=_params(("parallel",)),
        name="proj_ffn",
    )(xp, xs, o_lat, wuv, mla_w_o[0].astype(BF16), row(norm_ffn[0]),
      ffn_w_gate[0].astype(BF16), ffn_w_up[0].astype(BF16), ffn_w_down[0].astype(BF16))

    route_w = [_const_spec((1, d)), _const_spec(pool_w.shape[1:]), _const_spec((1, d)), _const_spec((1, d)),
               _const_spec((d, n_e)), _const_spec((1, n_e))]
    route_args = (row(norm_mix[1]), pool_w[0].astype(BF16), row(pool_scale[0]), row(norm_ffn[1]), moe_w_router[0],
                  row(moe_b_router[0]))
    tri = lambda n: (jnp.arange(n)[:, None] > jnp.arange(n)[None, :]).astype(BF16)
    hist_blk = lambda i: jnp.where(i % sb == 0, meta_row // HIST_ROWS, i * (tb // HIST_ROWS) - 1)
    h2_f, u2_f, info_f, cum_f, tot_f, state_f = pl.pallas_call(
        functools.partial(_pool_frames_kernel, n_fb),
        grid=(n_tb,),
        in_specs=[pl.BlockSpec((tb, d), lambda i: (i, 0)),
                  pl.BlockSpec((HIST_ROWS, d), lambda i: (hist_blk(i), 0))] + route_w + [_const_spec((tb, tb))],
        out_specs=[pl.BlockSpec((tb, d), lambda i: (i, 0)), pl.BlockSpec((tb, d), lambda i: (i, 0)),
                   pl.BlockSpec((tb, n_e), lambda i: (i, 0)),
                   pl.BlockSpec((1, 1, n_e), lambda i: (jnp.minimum(i, n_fb - 1), 0, 0)),
                   pl.BlockSpec((1, n_e), lambda i: (0, 0)),
                   pl.BlockSpec((1, HIST_ROWS, d), lambda i: (jnp.minimum(i // sb, nb - 1), 0, 0))],
        out_shape=[jax.ShapeDtypeStruct((nt, d), F32), jax.ShapeDtypeStruct((nt, d), BF16),
                   jax.ShapeDtypeStruct((nt, n_e), F32), jax.ShapeDtypeStruct((n_fb, 1, n_e), F32),
                   jax.ShapeDtypeStruct((1, n_e), F32), jax.ShapeDtypeStruct((nb, HIST_ROWS, d), F32)],
        scratch_shapes=[pltpu.VMEM((HIST_ROWS + tb, d), F32), pltpu.VMEM((1, n_e), F32)],
        compiler_params=_params(("arbitrary",)),
        name="pool_route_frames",
    )(h1, h1, *route_args, tri(tb))

    hist_s = jnp.concatenate([
        jnp.pad(state_pool[0].astype(F32), [(0, 0), (HIST_ROWS - POOL_HIST, 0), (0, 0)]),
        jnp.zeros((n_seq - db, HIST_ROWS, d), F32)], axis=0)
    any_spec = pl.BlockSpec(memory_space=pl.ANY)
    h2, u2, info, cum_s, tot_s, state_s = pl.pallas_call(
        functools.partial(_pool_short_kernel, db, n_valid_seq),
        grid=(n_seq,),
        in_specs=[any_spec, any_spec, any_spec,
                  pl.BlockSpec((SEQ_S, d), short_blk),
                  pl.BlockSpec((1, HIST_ROWS, d), lambda s: (s, 0, 0)),
                  _const_spec((1, n_e))] + route_w + [_const_spec((SEQ_S, SEQ_S))],
        out_specs=[pl.BlockSpec((SEQ_S, d), short_blk), pl.BlockSpec((SEQ_S, d), short_blk),
                   pl.BlockSpec((SEQ_S, n_e), short_blk), pl.BlockSpec((1, 1, n_e), lambda s: (s, 0, 0)),
                   pl.BlockSpec((1, n_e), lambda s: (0, 0)),
                   pl.BlockSpec((1, SEQ_S, d), lambda s: (s, 0, 0))],
        out_shape=[jax.ShapeDtypeStruct((nt, d), F32), jax.ShapeDtypeStruct((nt, d), BF16),
                   jax.ShapeDtypeStruct((nt, n_e), F32), jax.ShapeDtypeStruct((n_seq, 1, n_e), F32),
                   jax.ShapeDtypeStruct((1, n_e), F32), jax.ShapeDtypeStruct((n_seq, SEQ_S, d), F32)],
        scratch_shapes=[pltpu.VMEM((HIST_ROWS + SEQ_S, d), F32), pltpu.VMEM((1, n_e), F32)],
        input_output_aliases={0: 0, 1: 1, 2: 2},
        compiler_params=_params(("arbitrary",)),
        name="pool_route_short",
    )(h2_f, u2_f, info_f, h1, hist_s, tot_f, *route_args, tri(SEQ_S))

    counts = tot_s[0].astype(jnp.int32)
    cum = jnp.concatenate([cum_f[:, 0], cum_s[::tb // SEQ_S, 0], tot_s], axis=0).astype(jnp.int32)
    padded = (counts + rb - 1) // rb * rb
    pend = jnp.cumsum(padded)
    pstart = pend - padded
    n_rows_max = -(-(TOP_K * (nf + n_valid_seq * SEQ_S)) // rb) * rb + n_e * rb
    n_rb = n_rows_max // rb
    e_tok = info[:, 0:TOP_K].astype(jnp.int32)
    pos = jnp.where(e_tok >= 0, pstart[jnp.maximum(e_tok, 0)] + info[:, 2:2 + TOP_K].astype(jnp.int32), -1)
    gates = info[:, 4:4 + TOP_K]
    pos_l = pos.reshape(n_tb, tb, TOP_K).transpose(0, 2, 1)
    gate_l = gates.reshape(n_tb, tb, TOP_K).transpose(0, 2, 1)
    pos_c = jnp.pad(pos, [(0, 0), (0, n_e - TOP_K)], constant_values=-1)

    blk_row = jnp.arange(n_rb, dtype=jnp.int32) * rb
    blk_e = jnp.minimum(jnp.searchsorted(pend, blk_row, side="right"), n_e - 1).astype(jnp.int32)
    blk_ok = (blk_row < pend[-1]).astype(jnp.int32)
    r_lo = blk_row - pstart[blk_e]
    r_hi = jnp.minimum(r_lo + rb, counts[blk_e])
    cum_e = cum.T
    first = jax.vmap(lambda e, r: jnp.searchsorted(cum_e[e, 1:], r, side="right"))(blk_e, r_lo)
    last = jax.vmap(lambda e, r: jnp.searchsorted(cum_e[e, :-1], r, side="left"))(blk_e, r_hi) - 1
    blk_lo = jnp.clip(first, 0, n_tb - 1).astype(jnp.int32)
    blk_hi = jnp.clip(last, blk_lo, n_tb - 1).astype(jnp.int32)

    ys = pl.pallas_call(
        _expert_kernel,
        grid_spec=pltpu.PrefetchScalarGridSpec(
            num_scalar_prefetch=4,
            grid=(n_rb,),
            in_specs=[pl.BlockSpec(memory_space=pl.ANY),
                      _const_spec((n_tb, TOP_K, tb)), _const_spec((n_tb, TOP_K, tb)),
                      pl.BlockSpec((1, d, d_e), lambda j, be, lo, hi, ok: (be[j], 0, 0)),
                      pl.BlockSpec((1, d, d_e), lambda j, be, lo, hi, ok: (be[j], 0, 0)),
                      pl.BlockSpec((1, d_e, d), lambda j, be, lo, hi, ok: (be[j], 0, 0))],
            out_specs=pl.BlockSpec((rb, d), lambda j, be, lo, hi, ok: (j, 0)),
            scratch_shapes=[pltpu.VMEM((2, tb, d), BF16), pltpu.VMEM((rb, d), F32), pltpu.VMEM((rb, 1), F32),
                            pltpu.SemaphoreType.DMA((2,))]),
        out_shape=jax.ShapeDtypeStruct((n_rows_max, d), BF16),
        compiler_params=_params(("arbitrary",)),
        name="experts",
    )(blk_e, blk_lo, blk_hi, blk_ok, u2, pos_l, gate_l, moe_w_gate[0].astype(BF16), moe_w_up[0].astype(BF16),
      moe_w_down[0].astype(BF16))

    w_lo = pstart[None, :] + cum[:-1]
    w_hi = pstart[None, :] + cum[1:]
    has = w_hi > w_lo
    c_lo = w_lo // rb
    c_hi = jnp.where(has, (w_hi - 1) // rb, c_lo)
    cand = jnp.stack([c_lo, c_hi], axis=2).reshape(n_tb, 2 * n_e)
    keep = jnp.stack([has, has & (c_hi > c_lo)], axis=2).reshape(n_tb, 2 * n_e)
    order = jnp.argsort(~keep, axis=1, stable=True)
    ids = jnp.where(jnp.take_along_axis(keep, order, axis=1), jnp.take_along_axis(cand, order, axis=1), 0)
    n_ids = jnp.sum(keep, axis=1).astype(jnp.int32)
    y_all = pl.pallas_call(
        functools.partial(_combine_kernel, 2 * n_e),
        grid_spec=pltpu.PrefetchScalarGridSpec(
            num_scalar_prefetch=2,
            grid=(n_tb,),
            in_specs=[pl.BlockSpec((tb, d), lambda i, n, ids: (i, 0)),
                      pl.BlockSpec((tb, n_e), lambda i, n, ids: (i, 0)),
                      pl.BlockSpec((1, d), lambda i, n, ids: (0, 0)),
                      pl.BlockSpec(memory_space=pl.ANY)],
            out_specs=pl.BlockSpec((tb, d), lambda i, n, ids: (i, 0)),
            scratch_shapes=[pltpu.VMEM((2, rb, d), BF16), pltpu.VMEM((tb, d), F32),
                            pltpu.SemaphoreType.DMA((2,))]),
        out_shape=jax.ShapeDtypeStruct((nt, d), F32),
        compiler_params=_params(("arbitrary",)),
        name="combine",
    )(n_ids, ids.reshape(-1).astype(jnp.int32), h2, pos_c, row(norm_final), ys)

    y_prompt = y_all[:nf].reshape(nb, seq, d)
    y_sample = y_all[nf:nf + db * SEQ_S].reshape(db, SEQ_S, d)

    def with_meta(a, width):
        meta = jnp.broadcast_to(a[meta_row:meta_row + N_META][None], (nb, N_META, width))
        return jnp.concatenate([meta, a[:nf].reshape(nb, seq, width)], axis=1)[None]

    c_p = with_meta(c_all, kv_rank)
    r_p = with_meta(r_all, QK_ROPE)
    c_s = c_all[nf:nf + db * SEQ_S].reshape(1, db, SEQ_S, kv_rank)
    r_s = r_all[nf:nf + db * SEQ_S].reshape(1, db, SEQ_S, QK_ROPE)
    s_p = state_f[:, HIST_ROWS - POOL_HIST:][None]
    s_s = state_s[:db, SEQ_S - POOL_HIST:][None]
    return (y_prompt, y_sample, c_p, r_p, s_p, c_s, r_s, s_s)
```

```python
import functools

import jax
import jax.numpy as jnp
from jax import lax
from jax.experimental import pallas as pl
from jax.experimental.pallas import tpu as pltpu

CHUNK = 64
N_META = 16
N_HEADS = 8
QK_NOPE = 128
QK_ROPE = 64
ROPE_THETA = 10000.0
POOL_WINDOWS = (2, 4, 8, 16)
POOL_HIST = max(POOL_WINDOWS) - 1
TOP_K = 2
RMS_EPS = 1e-6

LANE = 128
SEQ_S = 16
HIST_ROWS = 16
TOKEN_BLOCK = 256
ROW_BLOCK = 256
ATTN_BLOCK = 256
CACHE_BLOCK = 512
SCORES_AHEAD = 3
GATHER_SLOTS = 6
VMEM_LIMIT = 56 * 1024 * 1024

F32 = jnp.float32
BF16 = jnp.bfloat16
NEG_INF = float("-inf")
LOG2_E = 1.4426950408889634


def _dot(a, b):
    return jnp.dot(a, b, preferred_element_type=F32)


def _dot_nt(a, b):
    return lax.dot_general(a, b, (((1,), (1,)), ((), ())), preferred_element_type=F32)


def _rms(x, g):
    return x * lax.rsqrt(jnp.mean(x * x, axis=-1, keepdims=True) + RMS_EPS) * g


def _const_spec(shape):
    nd = len(shape)
    return pl.BlockSpec(shape, lambda *_: (0,) * nd, pipeline_mode=pl.Buffered(1))


def _params(sem):
    return pltpu.CompilerParams(dimension_semantics=sem, vmem_limit_bytes=VMEM_LIMIT)


def _qkv_kernel(n_fb, scale, xp_ref, xs_ref, tab_ref, g_ref, wdq_ref, gq_ref, wuq_ref, wuk_ref, wdkv_ref, gkv_ref,
                q_ref, qt_ref, kvb_ref, kt_ref, c_ref, r_ref):
    i = pl.program_id(0)
    x = jnp.where(i < n_fb, xp_ref[...], xs_ref[...])
    u = _rms(x, g_ref[...]).astype(BF16)
    cq = _rms(_dot(u, wdq_ref[...]), gq_ref[...]).astype(BF16)
    qa = _dot(cq, wuq_ref[...])
    cos = tab_ref[:, :LANE]
    sin = tab_ref[:, LANE:]
    hq = QK_NOPE + 2 * LANE
    for h in range(N_HEADS):
        qn = qa[:, h * LANE:(h + 1) * LANE].astype(BF16)
        ql = _dot(qn, wuk_ref[h]) * scale
        a = qa[:, (N_HEADS + h) * LANE:(N_HEADS + h + 1) * LANE]
        b = qa[:, (2 * N_HEADS + h) * LANE:(2 * N_HEADS + h + 1) * LANE]
        pe = (a * cos + b * sin) * scale

        @pl.when(i < n_fb)
        def _():
            qt_ref[0, h] = jnp.concatenate([ql, pe], axis=1).T.astype(BF16)

        @pl.when(i >= n_fb)
        def _():
            q_ref[:, h * hq:h * hq + 2 * LANE] = ql.astype(BF16)
            q_ref[:, h * hq + 2 * LANE:(h + 1) * hq] = pe.astype(BF16)

    kv = _dot(u, wdkv_ref[...])
    c = _rms(kv[:, :2 * LANE], gkv_ref[...])
    r = kv[:, 2 * LANE:3 * LANE] * cos + kv[:, 3 * LANE:] * sin
    c_ref[...] = c
    r_ref[...] = r[:, :QK_ROPE]
    kvb_ref[:, :2 * LANE] = c.astype(BF16)
    kvb_ref[:, 2 * LANE:] = r.astype(BF16)
    kt_ref[0] = jnp.concatenate([c, r], axis=1).T.astype(BF16)


def _attn_frames_kernel(qb, n_q, meta_col, qt_ref, kv_ref, ct_ref, kvm_ref, cmt_ref, o_ref, m_ref, l_ref, acc_ref):
    i = pl.program_id(0)

    @pl.when(i >= n_q)
    def _():
        o_ref[...] = jnp.zeros_like(o_ref)

    @pl.when(i < n_q)
    def _():
        j = i % qb
        tq = o_ref.shape[0]
        kv_rank = ct_ref.shape[1]
        kvm = kvm_ref[...]
        cmt = cmt_ref[0, :, meta_col:meta_col + N_META]

        for h in range(N_HEADS):
            s = _dot(kvm, qt_ref[0, h])
            m = jnp.max(s, axis=0, keepdims=True)
            p = jnp.exp2(s - m)
            m_ref[h] = m
            l_ref[h] = jnp.sum(p, axis=0, keepdims=True)
            acc_ref[h] = _dot(cmt, p.astype(BF16))

        def online_update(h, s, ct):
            m_old = m_ref[h]
            m_new = jnp.maximum(m_old, jnp.max(s, axis=0, keepdims=True))
            alpha = jnp.exp2(m_old - m_new)
            p = jnp.exp2(s - m_new)
            l_new = alpha * l_ref[h] + jnp.sum(p, axis=0, keepdims=True)
            acc_new = alpha * acc_ref[h] + _dot(ct, p.astype(BF16))
            return m_new, l_new, acc_new

        def body(kb, carry):
            k = kv_ref[pl.ds(pl.multiple_of(kb * tq, tq), tq), :]
            ct = ct_ref[kb]
            ahead = [_dot(k, qt_ref[0, h]) for h in range(SCORES_AHEAD)]
            for h in range(N_HEADS):
                s = ahead.pop(0)
                if h + SCORES_AHEAD < N_HEADS:
                    ahead.append(_dot(k, qt_ref[0, h + SCORES_AHEAD]))
                m_ref[h], l_ref[h], acc_ref[h] = online_update(h, s, ct)
            return carry

        lax.fori_loop(0, j, body, 0)

        k = kv_ref[pl.ds(pl.multiple_of(j * tq, tq), tq), :]
        ct = ct_ref[j]
        visible = (lax.broadcasted_iota(jnp.int32, (tq, tq), 0) // CHUNK
                   <= lax.broadcasted_iota(jnp.int32, (tq, tq), 1) // CHUNK)
        ahead = [_dot(k, qt_ref[0, h]) for h in range(SCORES_AHEAD)]
        for h in range(N_HEADS):
            s = jnp.where(visible, ahead.pop(0), NEG_INF)
            if h + SCORES_AHEAD < N_HEADS:
                ahead.append(_dot(k, qt_ref[0, h + SCORES_AHEAD]))
            _, l, acc = online_update(h, s, ct)
            o_ref[:, h * kv_rank:(h + 1) * kv_rank] = (acc * (1.0 / l)).T.astype(BF16)


def _attn_short_kernel(n_cached_seq, o_frames_hbm, q_ref, kvn_ref, cc_ref, cr_ref, o_ref):
    del o_frames_hbm
    s_id = pl.program_id(0)
    hq = q_ref.shape[1] // N_HEADS
    n_cache = cc_ref.shape[2]
    qs = jnp.concatenate([q_ref[:, h * hq:(h + 1) * hq] for h in range(N_HEADS)], axis=0)
    q_lat = qs[:, :2 * LANE]
    q_pe = qs[:, 2 * LANE:2 * LANE + QK_ROPE]
    kvn = kvn_ref[...]
    s = _dot_nt(qs, kvn)
    m = jnp.max(s, axis=1, keepdims=True)
    p = jnp.exp2(s - m)
    l = jnp.sum(p, axis=1, keepdims=True)
    acc = _dot(p.astype(BF16), kvn[:, :2 * LANE])
    has_cache = s_id < n_cached_seq
    start = 0
    while start < n_cache:
        size = min(CACHE_BLOCK, n_cache - start)
        ck = cc_ref[0, 0, start:start + size, :].astype(BF16)
        rk = cr_ref[0, 0, start:start + size, :].astype(BF16)
        s = _dot_nt(q_lat, ck) + _dot_nt(q_pe, rk)
        s = jnp.where(has_cache, s, NEG_INF)
        m_new = jnp.maximum(m, jnp.max(s, axis=1, keepdims=True))
        alpha = jnp.exp2(m - m_new)
        p = jnp.exp2(s - m_new)
        l = alpha * l + jnp.sum(p, axis=1, keepdims=True)
        acc = alpha * acc + _dot(p.astype(BF16), ck)
        m = m_new
        start += size
    o = (acc / l).astype(BF16)
    for h in range(N_HEADS):
        o_ref[:, h * 2 * LANE:(h + 1) * 2 * LANE] = o[h * SEQ_S:(h + 1) * SEQ_S, :]


def _proj_ffn_kernel(n_fb, xp_ref, xs_ref, ol_ref, wuv_ref, wo_ref, g_ref, wg_ref, wu_ref, wd_ref, h_ref):
    i = pl.program_id(0)
    x = jnp.where(i < n_fb, xp_ref[...], xs_ref[...])
    o = jnp.concatenate(
        [_dot(ol_ref[:, h * 2 * LANE:(h + 1) * 2 * LANE], wuv_ref[h]) for h in range(N_HEADS)], axis=1)
    h1 = x + _dot(o.astype(BF16), wo_ref[...])
    u = _rms(h1, g_ref[...]).astype(BF16)
    act = jax.nn.silu(_dot(u, wg_ref[...])) * _dot(u, wu_ref[...])
    h_ref[...] = h1 + _dot(act.astype(BF16), wd_ref[...])


def _pool_route(h1, ext_ref, cnt_rows, valid, base, gm_ref, pw_ref, ps_ref, gf_ref, wr_ref, br_ref, tri_ref):
    t = h1.shape[0]
    grp = h1.shape[1] // len(POOL_WINDOWS)
    u1 = _rms(h1, gm_ref[...])
    ext_ref[HIST_ROWS:HIST_ROWS + t, :] = u1
    ys = []
    for g, w in enumerate(POOL_WINDOWS):
        cols = slice(g * grp, (g + 1) * grp)
        acc = u1[:, cols]
        for k in range(1, w):
            acc = acc + ext_ref[HIST_ROWS - k:HIST_ROWS - k + t, cols]
        cnt = float(w) if cnt_rows is None else jnp.minimum(float(w), cnt_rows)
        mean = acc / cnt - u1[:, cols]
        ys.append(_dot(mean.astype(BF16), pw_ref[g]))
    h2 = h1 + jnp.concatenate(ys, axis=1) * ps_ref[...]
    u2 = _rms(h2, gf_ref[...])

    n_e = wr_ref.shape[1]
    logits = jnp.dot(u2, wr_ref[...], preferred_element_type=F32, precision=lax.Precision.HIGHEST) + br_ref[...]
    e_iota = lax.broadcasted_iota(jnp.int32, logits.shape, 1).astype(F32)
    v0 = jnp.max(logits, axis=1, keepdims=True)
    e0 = jnp.min(jnp.where(logits == v0, e_iota, float(n_e)), axis=1, keepdims=True)
    rest = jnp.where(e_iota == e0, NEG_INF, logits)
    v1 = jnp.max(rest, axis=1, keepdims=True)
    e1 = jnp.min(jnp.where(rest == v1, e_iota, float(n_e)), axis=1, keepdims=True)
    tt = jnp.exp(v1 - v0)
    g0 = 1.0 / (1.0 + tt)
    g1 = tt / (1.0 + tt)
    hit0 = e_iota == e0
    hit1 = e_iota == e1
    onehot = jnp.where(valid & (hit0 | hit1), 1.0, 0.0)
    rank = _dot(tri_ref[...], onehot.astype(BF16)) + base
    r0 = jnp.sum(jnp.where(hit0, rank, 0.0), axis=1, keepdims=True)
    r1 = jnp.sum(jnp.where(hit1, rank, 0.0), axis=1, keepdims=True)
    cols = (jnp.where(valid, e0, -1).astype(F32), jnp.where(valid, e1, -1).astype(F32), r0, r1, g0, g1)
    info = jnp.zeros(logits.shape, F32)
    for k, col in enumerate(cols):
        info = jnp.where(e_iota == k, col, info)
    new_base = base + jnp.sum(onehot, axis=0, keepdims=True)
    return h2, u2, info, new_base, u1


def _pool_frames_kernel(n_fb, h_ref, hist_ref, gm_ref, pw_ref, ps_ref, gf_ref, wr_ref, br_ref, tri_ref,
                        h2_ref, u2_ref, info_ref, cum_ref, tot_ref, state_ref, ext_ref, base_ref):
    i = pl.program_id(0)

    @pl.when(i == 0)
    def _():
        base_ref[...] = jnp.zeros_like(base_ref)

    @pl.when(i >= n_fb)
    def _():
        h2_ref[...] = jnp.zeros_like(h2_ref)
        u2_ref[...] = jnp.zeros_like(u2_ref)
        info_ref[...] = jnp.zeros_like(info_ref)

    @pl.when(i < n_fb)
    def _():
        ext_ref[:HIST_ROWS, :] = _rms(hist_ref[...], gm_ref[...])
        base = base_ref[...]
        cum_ref[0] = base
        h2, u2, info, new_base, u1 = _pool_route(h_ref[...], ext_ref, None, True, base, gm_ref, pw_ref, ps_ref,
                                                 gf_ref, wr_ref, br_ref, tri_ref)
        h2_ref[...] = h2
        u2_ref[...] = u2.astype(BF16)
        info_ref[...] = info
        base_ref[...] = new_base
        tot_ref[...] = new_base
        state_ref[0] = u1[u1.shape[0] - HIST_ROWS:, :]


def _pool_short_kernel(n_sample, n_valid, h2_hbm, u2_hbm, info_hbm, h_ref, hist_ref, base0_ref, gm_ref, pw_ref, ps_ref,
                       gf_ref, wr_ref, br_ref, tri_ref, h2_ref, u2_ref, info_ref, cum_ref, tot_ref, state_ref,
                       ext_ref, base_ref):
    del h2_hbm, u2_hbm, info_hbm
    s_id = pl.program_id(0)

    @pl.when(s_id == 0)
    def _():
        base_ref[...] = base0_ref[...]

    ext_ref[:HIST_ROWS, :] = hist_ref[0]
    base = base_ref[...]
    cum_ref[0] = base
    pos1 = (lax.broadcasted_iota(jnp.int32, (SEQ_S, 1), 0) + 1).astype(F32)
    cnt_rows = jnp.where(s_id < n_sample, float(max(POOL_WINDOWS)), pos1)
    h2, u2, info, new_base, u1 = _pool_route(h_ref[...], ext_ref, cnt_rows, s_id < n_valid, base, gm_ref, pw_ref,
                                             ps_ref, gf_ref, wr_ref, br_ref, tri_ref)
    h2_ref[...] = h2
    u2_ref[...] = u2.astype(BF16)
    info_ref[...] = info
    base_ref[...] = new_base
    tot_ref[...] = new_base
    state_ref[0] = u1


def _expert_kernel(n_blocks, be_ref, lo_ref, hi_ref, ok_ref, u_hbm, pos_ref, gate_ref, wg_ref, wu_ref, wd_ref, ys_ref,
                   buf_ref, acc_ref, gacc_ref, sem):
    del be_ref
    j = pl.program_id(0)
    rb = ys_ref.shape[0]
    n_slots, tc, _ = buf_ref.shape

    def n_chunks(jj):
        return jnp.where(ok_ref[jj] != 0, hi_ref[jj] - lo_ref[jj] + 1, 0)

    def chunk_copy(c, slot):
        return pltpu.make_async_copy(u_hbm.at[pl.ds(pl.multiple_of(c * tc, tc), tc), :], buf_ref.at[slot],
                                     sem.at[slot])

    def request(jj):
        n_req = jnp.minimum(n_chunks(jj), n_slots)
        for k in range(n_slots):
            @pl.when(k < n_req)
            def _():
                chunk_copy(lo_ref[jj] + k, k).start()

    @pl.when(j == 0)
    def _():
        request(0)

    n = n_chunks(j)
    lo = lo_ref[j]

    @pl.when(n > 0)
    def _():
        acc_ref[...] = jnp.zeros_like(acc_ref)
        gacc_ref[...] = jnp.zeros_like(gacc_ref)
        rows = j * rb + lax.broadcasted_iota(jnp.int32, (rb, tc), 0)

        def consume(c, slot):
            pos = pos_ref[c]
            gate = gate_ref[c]
            m0 = rows == pos[0:1, :]
            m1 = rows == pos[1:2, :]
            sel = jnp.where(m0 | m1, 1.0, 0.0).astype(BF16)
            acc_ref[...] += _dot(sel, buf_ref[slot])
            gacc_ref[...] += jnp.sum(jnp.where(m0, gate[0:1, :], 0.0) + jnp.where(m1, gate[1:2, :], 0.0),
                                     axis=1, keepdims=True)

        def requested(k, carry):
            chunk_copy(lo + k, k).wait()
            consume(lo + k, k)
            return carry

        def overflow(k, carry):
            copy = chunk_copy(lo + k, 0)
            copy.start()
            copy.wait()
            consume(lo + k, 0)
            return carry

        lax.fori_loop(0, jnp.minimum(n, n_slots), requested, 0)
        lax.fori_loop(n_slots, jnp.maximum(n, n_slots), overflow, 0)

    @pl.when(j + 1 < n_blocks)
    def _():
        request(jnp.minimum(j + 1, n_blocks - 1))

    @pl.when(n > 0)
    def _():
        xg = acc_ref[...].astype(BF16)
        act = jax.nn.silu(_dot(xg, wg_ref[0])) * _dot(xg, wu_ref[0])
        ys_ref[...] = (_dot(act.astype(BF16), wd_ref[0]) * gacc_ref[...]).astype(BF16)

    @pl.when(n == 0)
    def _():
        ys_ref[...] = jnp.zeros_like(ys_ref)


def _combine_kernel(n_blocks, n_ref, ids_ref, h2_ref, pos_ref, g_ref, ys_hbm, y_ref, buf_ref, acc_ref, sem):
    i = pl.program_id(0)
    tb = h2_ref.shape[0]
    max_n, rb, _ = buf_ref.shape
    nxt = jnp.minimum(i + 1, n_blocks - 1)
    n = n_ref[i]
    n_next = jnp.where(i + 1 < n_blocks, n_ref[nxt], 0)

    def chunk_copy(ii, k):
        ch = ids_ref[ii * max_n + k]
        return pltpu.make_async_copy(ys_hbm.at[pl.ds(pl.multiple_of(ch * rb, rb), rb), :], buf_ref.at[k], sem.at[k])

    @pl.when(i == 0)
    def _():
        for k in range(max_n):
            @pl.when(k < n)
            def _():
                chunk_copy(i, k).start()

    acc_ref[...] = jnp.zeros_like(acc_ref)
    p0 = pos_ref[:, 0:1]
    p1 = pos_ref[:, 1:2]
    lane_rows = lax.broadcasted_iota(jnp.int32, (tb, rb), 1)

    def body(k, carry):
        chunk_copy(i, k).wait()
        rows = ids_ref[i * max_n + k] * rb + lane_rows
        sel = jnp.where((rows == p0) | (rows == p1), 1.0, 0.0).astype(BF16)
        acc_ref[...] += _dot(sel, buf_ref[k])

        @pl.when(k < n_next)
        def _():
            chunk_copy(nxt, k).start()

        return carry

    lax.fori_loop(0, n, body, 0)
    for k in range(max_n):
        @pl.when((k >= n) & (k < n_next))
        def _():
            chunk_copy(nxt, k).start()

    y_ref[...] = _rms(h2_ref[...] + acc_ref[...], g_ref[...])


def _rope_table(pos):
    half = QK_ROPE // 2
    inv = ROPE_THETA ** (-jnp.arange(half, dtype=F32) / half)
    ang = pos.astype(F32)[:, None] * inv[None, :]
    cos = jnp.cos(ang)
    sin = jnp.sin(ang)
    zero = jnp.zeros((pos.shape[0], LANE - QK_ROPE), F32)
    return jnp.concatenate([cos, cos, zero, -sin, sin, zero], axis=1)


def _half_swap(w):
    half = QK_ROPE // 2
    return jnp.concatenate([w[..., half:], w[..., :half]], axis=-1)


def _pad_last(w, width):
    return jnp.pad(w, [(0, 0)] * (w.ndim - 1) + [(0, width - w.shape[-1])])


def kernel(x_prompt, x_sample, cache_kv_latent, cache_k_rope, state_pool, meta_tokens, norm_mix, norm_ffn, norm_final,
           mla_w_dq, mla_g_q, mla_w_uq, mla_w_dkv, mla_g_kv, mla_w_uk, mla_w_uv, mla_w_o, pool_w, pool_scale,
           ffn_w_gate, ffn_w_up, ffn_w_down, moe_w_router, moe_b_router, moe_w_gate, moe_w_up, moe_w_down):
    nb, seq, d = x_prompt.shape
    db, dseq, _ = x_sample.shape
    n_cache = cache_kv_latent.shape[2]
    q_rank = mla_w_dq.shape[2]
    kv_rank = mla_g_kv.shape[1]
    v_dim = mla_w_uv.shape[3]
    d_ff = ffn_w_gate.shape[2]
    n_e = moe_w_router.shape[2]
    d_e = moe_w_gate.shape[3]
    tb, rb, tq = TOKEN_BLOCK, ROW_BLOCK, ATTN_BLOCK
    assert norm_mix.shape[0] == 2 and cache_kv_latent.shape[0] == 1 and state_pool.shape[0] == 1
    assert dseq == SEQ_S and N_META == SEQ_S and meta_tokens.shape[0] == N_META
    assert kv_rank == 2 * LANE and QK_NOPE == LANE and QK_ROPE <= LANE and HIST_ROWS >= POOL_HIST
    assert seq % tb == 0 and tb == tq and tq % CHUNK == 0 and tb % SEQ_S == 0 and d % LANE == 0
    assert (n_cache - N_META) % CHUNK == 0 and dseq <= CHUNK

    nf = nb * seq
    n_valid_seq = db + 1
    ns = -(-(n_valid_seq * SEQ_S) // tb) * tb
    n_seq = ns // SEQ_S
    nt = nf + ns
    n_fb, n_sb, n_tb = nf // tb, ns // tb, nt // tb
    sb = seq // tb
    meta_row = nf + db * SEQ_S
    scale = float((QK_NOPE + QK_ROPE) ** -0.5)
    hq = 3 * LANE

    xp = x_prompt.reshape(nf, d)
    xs = jnp.concatenate([x_sample.reshape(db * SEQ_S, d), meta_tokens.astype(x_prompt.dtype),
                          jnp.zeros((ns - n_valid_seq * SEQ_S, d), x_prompt.dtype)], axis=0)

    t_s = jnp.arange(SEQ_S)
    pos_short = jnp.concatenate([jnp.tile(n_cache + t_s, db), jnp.tile(t_s, n_seq - db)])
    tab = jnp.concatenate([_rope_table(N_META + jnp.arange(seq)), _rope_table(pos_short)], axis=0)

    wuq = mla_w_uq[0].reshape(q_rank, N_HEADS, QK_NOPE + QK_ROPE)
    wuq_pe = wuq[:, :, QK_NOPE:]
    wuq2 = jnp.concatenate([wuq[:, :, :QK_NOPE].reshape(q_rank, -1),
                            _pad_last(wuq_pe, LANE).reshape(q_rank, -1),
                            _pad_last(_half_swap(wuq_pe), LANE).reshape(q_rank, -1)], axis=1).astype(BF16)
    wdkv_r = mla_w_dkv[0][:, kv_rank:]
    wdkv2 = jnp.concatenate([mla_w_dkv[0][:, :kv_rank], _pad_last(wdkv_r, LANE),
                             _pad_last(_half_swap(wdkv_r), LANE)], axis=1).astype(BF16)
    wuk_t = jnp.transpose(mla_w_uk[0], (1, 2, 0)).astype(BF16)
    wuv = jnp.transpose(mla_w_uv[0], (1, 0, 2)).astype(BF16)
    row = lambda v: v.reshape(1, -1)

    tok_p = pl.BlockSpec((tb, d), lambda i: (jnp.minimum(i, n_fb - 1), 0))
    tok_s = pl.BlockSpec((tb, d), lambda i: (jnp.maximum(i - n_fb, 0), 0))
    q_short, qt_frames, kvb, kt_all, c_all, r_all = pl.pallas_call(
        functools.partial(_qkv_kernel, n_fb, scale * LOG2_E),
        grid=(n_tb,),
        in_specs=[tok_p, tok_s,
                  pl.BlockSpec((tb, 2 * LANE), lambda i: (jnp.where(i < n_fb, i % sb, sb + i - n_fb), 0)),
                  _const_spec((1, d)), _const_spec((d, q_rank)), _const_spec((1, q_rank)),
                  _const_spec(wuq2.shape), _const_spec(wuk_t.shape), _const_spec(wdkv2.shape),
                  _const_spec((1, kv_rank))],
        out_specs=[pl.BlockSpec((tb, N_HEADS * hq), lambda i: (jnp.maximum(i - n_fb, 0), 0)),
                   pl.BlockSpec((1, N_HEADS, hq, tb), lambda i: (jnp.minimum(i, n_fb - 1), 0, 0, 0)),
                   pl.BlockSpec((tb, hq), lambda i: (i, 0)),
                   pl.BlockSpec((1, hq, tb), lambda i: (i, 0, 0)),
                   pl.BlockSpec((tb, kv_rank), lambda i: (i, 0)),
                   pl.BlockSpec((tb, QK_ROPE), lambda i: (i, 0))],
        out_shape=[jax.ShapeDtypeStruct((ns, N_HEADS * hq), BF16),
                   jax.ShapeDtypeStruct((n_fb, N_HEADS, hq, tb), BF16),
                   jax.ShapeDtypeStruct((nt, hq), BF16), jax.ShapeDtypeStruct((n_tb, hq, tb), BF16),
                   jax.ShapeDtypeStruct((nt, kv_rank), F32), jax.ShapeDtypeStruct((nt, QK_ROPE), F32)],
        compiler_params=_params(("arbitrary",)),
        name="qkv",
    )(xp, xs, tab, row(norm_mix[0]), mla_w_dq[0].astype(BF16), row(mla_g_q[0]), wuq2, wuk_t, wdkv2, row(mla_g_kv[0]))

    qb = seq // tq
    n_q = nb * qb
    batch_of = lambda i: jnp.minimum(i // qb, nb - 1)
    o_frames = pl.pallas_call(
        functools.partial(_attn_frames_kernel, qb, n_q, meta_row % tb),
        grid=(nt // tq,),
        in_specs=[pl.BlockSpec((1, N_HEADS, hq, tq), lambda i: (jnp.minimum(i, n_q - 1), 0, 0, 0)),
                  pl.BlockSpec((seq, hq), lambda i: (batch_of(i), 0)),
                  pl.BlockSpec((qb, kv_rank, tq), lambda i: (batch_of(i), 0, 0)),
                  pl.BlockSpec((N_META, hq), lambda i: (meta_row // N_META, 0)),
                  pl.BlockSpec((1, kv_rank, tb), lambda i: (meta_row // tb, 0, 0))],
        out_specs=pl.BlockSpec((tq, N_HEADS * kv_rank), lambda i: (i, 0)),
        out_shape=jax.ShapeDtypeStruct((nt, N_HEADS * kv_rank), BF16),
        scratch_shapes=[pltpu.VMEM((N_HEADS, 1, tq), F32), pltpu.VMEM((N_HEADS, 1, tq), F32),
                        pltpu.VMEM((N_HEADS, kv_rank, tq), F32)],
        compiler_params=_params(("parallel",)),
        name="attn_frames",
    )(qt_frames, kvb, kt_all, kvb, kt_all)

    short_blk = lambda s: (nf // SEQ_S + s, 0)
    o_lat = pl.pallas_call(
        functools.partial(_attn_short_kernel, db),
        grid=(n_seq,),
        in_specs=[pl.BlockSpec(memory_space=pl.ANY),
                  pl.BlockSpec((SEQ_S, N_HEADS * hq), lambda s: (s, 0)),
                  pl.BlockSpec((SEQ_S, hq), short_blk),
                  pl.BlockSpec((1, 1, n_cache, kv_rank), lambda s: (0, jnp.minimum(s, db - 1), 0, 0)),
                  pl.BlockSpec((1, 1, n_cache, QK_ROPE), lambda s: (0, jnp.minimum(s, db - 1), 0, 0))],
        out_specs=pl.BlockSpec((SEQ_S, N_HEADS * kv_rank), short_blk),
        out_shape=jax.ShapeDtypeStruct((nt, N_HEADS * kv_rank), BF16),
        input_output_aliases={0: 0},
        compiler_params=_params(("parallel",)),
        name="attn_short",
    )(o_frames, q_short, kvb, cache_kv_latent, cache_k_rope)

    h1 = pl.pallas_call(
        functools.partial(_proj_ffn_kernel, n_fb),
        grid=(n_tb,),
        in_specs=[tok_p, tok_s,
                  pl.BlockSpec((tb, N_HEADS * kv_rank), lambda i: (i, 0)),
                  _const_spec(wuv.shape), _const_spec((N_HEADS * v_dim, d)), _const_spec((1, d)),
                  _const_spec((d, d_ff)), _const_spec((d, d_ff)), _const_spec((d_ff, d))],
        out_specs=pl.BlockSpec((tb, d), lambda i: (i, 0)),
        out_shape=jax.ShapeDtypeStruct((nt, d), F32),
        compiler_params=_params(("parallel",)),
        name="proj_ffn",
    )(xp, xs, o_lat, wuv, mla_w_o[0].astype(BF16), row(norm_ffn[0]),
      ffn_w_gate[0].astype(BF16), ffn_w_up[0].astype(BF16), ffn_w_down[0].astype(BF16))

    route_w = [_const_spec((1, d)), _const_spec(pool_w.shape[1:]), _const_spec((1, d)), _const_spec((1, d)),
               _const_spec((d, n_e)), _const_spec((1, n_e))]
    route_args = (row(norm_mix[1]), pool_w[0].astype(BF16), row(pool_scale[0]), row(norm_ffn[1]), moe_w_router[0],
                  row(moe_b_router[0]))
    tri = lambda n: (jnp.arange(n)[:, None] > jnp.arange(n)[None, :]).astype(BF16)
    hist_blk = lambda i: jnp.where(i % sb == 0, meta_row // HIST_ROWS, i * (tb // HIST_ROWS) - 1)
    h2_f, u2_f, info_f, cum_f, tot_f, state_f = pl.pallas_call(
        functools.partial(_pool_frames_kernel, n_fb),
        grid=(n_tb,),
        in_specs=[pl.BlockSpec((tb, d), lambda i: (i, 0)),
                  pl.BlockSpec((HIST_ROWS, d), lambda i: (hist_blk(i), 0))] + route_w + [_const_spec((tb, tb))],
        out_specs=[pl.BlockSpec((tb, d), lambda i: (i, 0)), pl.BlockSpec((tb, d), lambda i: (i, 0)),
                   pl.BlockSpec((tb, n_e), lambda i: (i, 0)),
                   pl.BlockSpec((1, 1, n_e), lambda i: (jnp.minimum(i, n_fb - 1), 0, 0)),
                   pl.BlockSpec((1, n_e), lambda i: (0, 0)),
                   pl.BlockSpec((1, HIST_ROWS, d), lambda i: (jnp.minimum(i // sb, nb - 1), 0, 0))],
        out_shape=[jax.ShapeDtypeStruct((nt, d), F32), jax.ShapeDtypeStruct((nt, d), BF16),
                   jax.ShapeDtypeStruct((nt, n_e), F32), jax.ShapeDtypeStruct((n_fb, 1, n_e), F32),
                   jax.ShapeDtypeStruct((1, n_e), F32), jax.ShapeDtypeStruct((nb, HIST_ROWS, d), F32)],
        scratch_shapes=[pltpu.VMEM((HIST_ROWS + tb, d), F32), pltpu.VMEM((1, n_e), F32)],
        compiler_params=_params(("arbitrary",)),
        name="pool_route_frames",
    )(h1, h1, *route_args, tri(tb))

    hist_s = jnp.concatenate([
        jnp.pad(state_pool[0].astype(F32), [(0, 0), (HIST_ROWS - POOL_HIST, 0), (0, 0)]),
        jnp.zeros((n_seq - db, HIST_ROWS, d), F32)], axis=0)
    any_spec = pl.BlockSpec(memory_space=pl.ANY)
    h2, u2, info, cum_s, tot_s, state_s = pl.pallas_call(
        functools.partial(_pool_short_kernel, db, n_valid_seq),
        grid=(n_seq,),
        in_specs=[any_spec, any_spec, any_spec,
                  pl.BlockSpec((SEQ_S, d), short_blk),
                  pl.BlockSpec((1, HIST_ROWS, d), lambda s: (s, 0, 0)),
                  _const_spec((1, n_e))] + route_w + [_const_spec((SEQ_S, SEQ_S))],
        out_specs=[pl.BlockSpec((SEQ_S, d), short_blk), pl.BlockSpec((SEQ_S, d), short_blk),
                   pl.BlockSpec((SEQ_S, n_e), short_blk), pl.BlockSpec((1, 1, n_e), lambda s: (s, 0, 0)),
                   pl.BlockSpec((1, n_e), lambda s: (0, 0)),
                   pl.BlockSpec((1, SEQ_S, d), lambda s: (s, 0, 0))],
        out_shape=[jax.ShapeDtypeStruct((nt, d), F32), jax.ShapeDtypeStruct((nt, d), BF16),
                   jax.ShapeDtypeStruct((nt, n_e), F32), jax.ShapeDtypeStruct((n_seq, 1, n_e), F32),
                   jax.ShapeDtypeStruct((1, n_e), F32), jax.ShapeDtypeStruct((n_seq, SEQ_S, d), F32)],
        scratch_shapes=[pltpu.VMEM((HIST_ROWS + SEQ_S, d), F32), pltpu.VMEM((1, n_e), F32)],
        input_output_aliases={0: 0, 1: 1, 2: 2},
        compiler_params=_params(("arbitrary",)),
        name="pool_route_short",
    )(h2_f, u2_f, info_f, h1, hist_s, tot_f, *route_args, tri(SEQ_S))

    counts = tot_s[0].astype(jnp.int32)
    cum = jnp.concatenate([cum_f[:, 0], cum_s[::tb // SEQ_S, 0], tot_s], axis=0).astype(jnp.int32)
    padded = (counts + rb - 1) // rb * rb
    pend = jnp.cumsum(padded)
    pstart = pend - padded
    n_rows_max = -(-(TOP_K * (nf + n_valid_seq * SEQ_S)) // rb) * rb + n_e * rb
    n_rb = n_rows_max // rb
    e_tok = info[:, 0:TOP_K].astype(jnp.int32)
    pos = jnp.where(e_tok >= 0, pstart[jnp.maximum(e_tok, 0)] + info[:, 2:2 + TOP_K].astype(jnp.int32), -1)
    gates = info[:, 4:4 + TOP_K]
    pos_l = pos.reshape(n_tb, tb, TOP_K).transpose(0, 2, 1)
    gate_l = gates.reshape(n_tb, tb, TOP_K).transpose(0, 2, 1)
    pos_c = jnp.pad(pos, [(0, 0), (0, n_e - TOP_K)], constant_values=-1)

    blk_row = jnp.arange(n_rb, dtype=jnp.int32) * rb
    blk_e = jnp.minimum(jnp.sum(pend[None, :] <= blk_row[:, None], axis=1), n_e - 1).astype(jnp.int32)
    blk_ok = (blk_row < pend[-1]).astype(jnp.int32)
    r_lo = blk_row - pstart[blk_e]
    r_hi = jnp.minimum(r_lo + rb, counts[blk_e])
    cum_b = cum.T[blk_e]
    first = jnp.sum(cum_b[:, 1:] <= r_lo[:, None], axis=1)
    last = jnp.sum(cum_b[:, :-1] < r_hi[:, None], axis=1) - 1
    blk_lo = jnp.clip(first, 0, n_tb - 1).astype(jnp.int32)
    blk_hi = jnp.clip(last, blk_lo, n_tb - 1).astype(jnp.int32)

    ys = pl.pallas_call(
        functools.partial(_expert_kernel, n_rb),
        grid_spec=pltpu.PrefetchScalarGridSpec(
            num_scalar_prefetch=4,
            grid=(n_rb,),
            in_specs=[pl.BlockSpec(memory_space=pl.ANY),
                      _const_spec((n_tb, TOP_K, tb)), _const_spec((n_tb, TOP_K, tb)),
                      pl.BlockSpec((1, d, d_e), lambda j, be, lo, hi, ok: (be[j], 0, 0)),
                      pl.BlockSpec((1, d, d_e), lambda j, be, lo, hi, ok: (be[j], 0, 0)),
                      pl.BlockSpec((1, d_e, d), lambda j, be, lo, hi, ok: (be[j], 0, 0))],
            out_specs=pl.BlockSpec((rb, d), lambda j, be, lo, hi, ok: (j, 0)),
            scratch_shapes=[pltpu.VMEM((GATHER_SLOTS, tb, d), BF16), pltpu.VMEM((rb, d), F32),
                            pltpu.VMEM((rb, 1), F32), pltpu.SemaphoreType.DMA((GATHER_SLOTS,))]),
        out_shape=jax.ShapeDtypeStruct((n_rows_max, d), BF16),
        compiler_params=_params(("arbitrary",)),
        name="experts",
    )(blk_e, blk_lo, blk_hi, blk_ok, u2, pos_l, gate_l, moe_w_gate[0].astype(BF16), moe_w_up[0].astype(BF16),
      moe_w_down[0].astype(BF16))

    w_lo = pstart[None, :] + cum[:-1]
    w_hi = pstart[None, :] + cum[1:]
    has = w_hi > w_lo
    c_lo = w_lo // rb
    c_hi = jnp.where(has, (w_hi - 1) // rb, c_lo)
    cand = jnp.stack([c_lo, c_hi], axis=2).reshape(n_tb, 2 * n_e)
    keep = jnp.stack([has, has & (c_hi > c_lo)], axis=2).reshape(n_tb, 2 * n_e)
    order = jnp.argsort(~keep, axis=1, stable=True)
    ids = jnp.where(jnp.take_along_axis(keep, order, axis=1), jnp.take_along_axis(cand, order, axis=1), 0)
    n_ids = jnp.sum(keep, axis=1).astype(jnp.int32)
    y_all = pl.pallas_call(
        functools.partial(_combine_kernel, n_tb),
        grid_spec=pltpu.PrefetchScalarGridSpec(
            num_scalar_prefetch=2,
            grid=(n_tb,),
            in_specs=[pl.BlockSpec((tb, d), lambda i, n, ids: (i, 0)),
                      pl.BlockSpec((tb, n_e), lambda i, n, ids: (i, 0)),
                      pl.BlockSpec((1, d), lambda i, n, ids: (0, 0)),
                      pl.BlockSpec(memory_space=pl.ANY)],
            out_specs=pl.BlockSpec((tb, d), lambda i, n, ids: (i, 0)),
            scratch_shapes=[pltpu.VMEM((2 * n_e, rb, d), BF16), pltpu.VMEM((tb, d), F32),
                            pltpu.SemaphoreType.DMA((2 * n_e,))]),
        out_shape=jax.ShapeDtypeStruct((nt, d), F32),
        compiler_params=_params(("arbitrary",)),
        name="combine",
    )(n_ids, ids.reshape(-1).astype(jnp.int32), h2, pos_c, row(norm_final), ys)

    y_prompt = y_all[:nf].reshape(nb, seq, d)
    y_sample = y_all[nf:nf + db * SEQ_S].reshape(db, SEQ_S, d)

    def with_meta(a, width):
        meta = jnp.broadcast_to(a[meta_row:meta_row + N_META][None], (nb, N_META, width))
        return jnp.concatenate([meta, a[:nf].reshape(nb, seq, width)], axis=1)[None]

    c_p = with_meta(c_all, kv_rank)
    r_p = with_meta(r_all, QK_ROPE)
    c_s = c_all[nf:nf + db * SEQ_S].reshape(1, db, SEQ_S, kv_rank)
    r_s = r_all[nf:nf + db * SEQ_S].reshape(1, db, SEQ_S, QK_ROPE)
    s_p = state_f[:, HIST_ROWS - POOL_HIST:][None]
    s_s = state_s[:db, SEQ_S - POOL_HIST:][None]
    return (y_prompt, y_sample, c_p, r_p, s_p, c_s, r_s, s_s)
```

```python
import functools

import jax
import jax.numpy as jnp
from jax import lax
from jax.experimental import pallas as pl
from jax.experimental.pallas import tpu as pltpu

CHUNK = 64
N_META = 16
N_HEADS = 8
QK_NOPE = 128
QK_ROPE = 64
ROPE_THETA = 10000.0
POOL_WINDOWS = (2, 4, 8, 16)
POOL_HIST = max(POOL_WINDOWS) - 1
TOP_K = 2
RMS_EPS = 1e-6

LANE = 128
SEQ_S = 16
HIST_ROWS = 16
TOKEN_BLOCK = 256
ROW_BLOCK = 256
ATTN_BLOCK = 256
CACHE_BLOCK = 512
SCORES_AHEAD = 3
GATHER_SLOTS = 6
VMEM_LIMIT = 56 * 1024 * 1024

F32 = jnp.float32
BF16 = jnp.bfloat16
NEG_INF = float("-inf")
LOG2_E = 1.4426950408889634


def _dot(a, b):
    return jnp.dot(a, b, preferred_element_type=F32)


def _dot_nt(a, b):
    return lax.dot_general(a, b, (((1,), (1,)), ((), ())), preferred_element_type=F32)


def _rms(x, g):
    return x * lax.rsqrt(jnp.mean(x * x, axis=-1, keepdims=True) + RMS_EPS) * g


def _const_spec(shape):
    nd = len(shape)
    return pl.BlockSpec(shape, lambda *_: (0,) * nd, pipeline_mode=pl.Buffered(1))


def _params(sem):
    return pltpu.CompilerParams(dimension_semantics=sem, vmem_limit_bytes=VMEM_LIMIT)


def _qkv_kernel(n_fb, scale, xp_ref, xs_ref, tab_ref, tabt_ref, g_ref, wdq_ref, gq_ref, wuq_ref, wuqt_ref, wuk_ref,
                wukt_ref, wdkv_ref, gkv_ref, q_ref, qt_ref, kvb_ref, kt_ref, c_ref, r_ref):
    i = pl.program_id(0)
    x = jnp.where(i < n_fb, xp_ref[...], xs_ref[...])
    u = _rms(x, g_ref[...]).astype(BF16)
    cq = _rms(_dot(u, wdq_ref[...]), gq_ref[...]).astype(BF16)
    cos = tab_ref[:, :LANE]
    sin = tab_ref[:, LANE:]
    hq = QK_NOPE + 2 * LANE

    @pl.when(i < n_fb)
    def _():
        qat = _dot_nt(wuqt_ref[...], cq)
        cos_t = tabt_ref[:QK_ROPE, :]
        sin_t = tabt_ref[LANE:LANE + QK_ROPE, :]
        pe0 = N_HEADS * QK_NOPE
        sw0 = pe0 + N_HEADS * QK_ROPE
        for h in range(N_HEADS):
            qn = qat[h * QK_NOPE:(h + 1) * QK_NOPE, :].astype(BF16)
            qt_ref[0, h, :2 * LANE, :] = (_dot(wuk_ref[h], qn) * scale).astype(BF16)
            a = qat[pe0 + h * QK_ROPE:pe0 + (h + 1) * QK_ROPE, :]
            b = qat[sw0 + h * QK_ROPE:sw0 + (h + 1) * QK_ROPE, :]
            qt_ref[0, h, 2 * LANE:2 * LANE + QK_ROPE, :] = ((a * cos_t + b * sin_t) * scale).astype(BF16)
            qt_ref[0, h, 2 * LANE + QK_ROPE:, :] = jnp.zeros((LANE - QK_ROPE, qt_ref.shape[3]), BF16)

    @pl.when(i >= n_fb)
    def _():
        qa = _dot(cq, wuq_ref[...])
        for h in range(N_HEADS):
            qn = qa[:, h * LANE:(h + 1) * LANE].astype(BF16)
            q_ref[:, h * hq:h * hq + 2 * LANE] = (_dot(qn, wukt_ref[h]) * scale).astype(BF16)
            a = qa[:, (N_HEADS + h) * LANE:(N_HEADS + h + 1) * LANE]
            b = qa[:, (2 * N_HEADS + h) * LANE:(2 * N_HEADS + h + 1) * LANE]
            q_ref[:, h * hq + 2 * LANE:(h + 1) * hq] = ((a * cos + b * sin) * scale).astype(BF16)

    kv = _dot(u, wdkv_ref[...])
    c = _rms(kv[:, :2 * LANE], gkv_ref[...])
    r = kv[:, 2 * LANE:3 * LANE] * cos + kv[:, 3 * LANE:] * sin
    c_ref[...] = c
    r_ref[...] = r[:, :QK_ROPE]
    kvb_ref[:, :2 * LANE] = c.astype(BF16)
    kvb_ref[:, 2 * LANE:] = r.astype(BF16)
    kt_ref[0] = jnp.concatenate([c, r], axis=1).T.astype(BF16)


def _attn_frames_kernel(qb, n_q, meta_col, qt_ref, kv_ref, ct_ref, kvm_ref, cmt_ref, o_ref, m_ref, l_ref, acc_ref):
    i = pl.program_id(0)

    @pl.when(i >= n_q)
    def _():
        o_ref[...] = jnp.zeros_like(o_ref)

    @pl.when(i < n_q)
    def _():
        j = i % qb
        tq = o_ref.shape[0]
        kv_rank = ct_ref.shape[1]
        kvm = kvm_ref[...]
        cmt = cmt_ref[0, :, meta_col:meta_col + N_META]

        for h in range(N_HEADS):
            s = _dot(kvm, qt_ref[0, h])
            m = jnp.max(s, axis=0, keepdims=True)
            p = jnp.exp2(s - m)
            m_ref[h] = m
            l_ref[h] = jnp.sum(p, axis=0, keepdims=True)
            acc_ref[h] = _dot(cmt, p.astype(BF16))

        def online_update(h, s, ct):
            m_old = m_ref[h]
            m_new = jnp.maximum(m_old, jnp.max(s, axis=0, keepdims=True))
            alpha = jnp.exp2(m_old - m_new)
            p = jnp.exp2(s - m_new)
            l_new = alpha * l_ref[h] + jnp.sum(p, axis=0, keepdims=True)
            acc_new = alpha * acc_ref[h] + _dot(ct, p.astype(BF16))
            return m_new, l_new, acc_new

        def body(kb, carry):
            k = kv_ref[pl.ds(pl.multiple_of(kb * tq, tq), tq), :]
            ct = ct_ref[kb]
            ahead = [_dot(k, qt_ref[0, h]) for h in range(SCORES_AHEAD)]
            for h in range(N_HEADS):
                s = ahead.pop(0)
                if h + SCORES_AHEAD < N_HEADS:
                    ahead.append(_dot(k, qt_ref[0, h + SCORES_AHEAD]))
                m_ref[h], l_ref[h], acc_ref[h] = online_update(h, s, ct)
            return carry

        lax.fori_loop(0, j, body, 0)

        k = kv_ref[pl.ds(pl.multiple_of(j * tq, tq), tq), :]
        ct = ct_ref[j]
        visible = (lax.broadcasted_iota(jnp.int32, (tq, tq), 0) // CHUNK
                   <= lax.broadcasted_iota(jnp.int32, (tq, tq), 1) // CHUNK)
        ahead = [_dot(k, qt_ref[0, h]) for h in range(SCORES_AHEAD)]
        for h in range(N_HEADS):
            s = jnp.where(visible, ahead.pop(0), NEG_INF)
            if h + SCORES_AHEAD < N_HEADS:
                ahead.append(_dot(k, qt_ref[0, h + SCORES_AHEAD]))
            _, l, acc = online_update(h, s, ct)
            o_ref[:, h * kv_rank:(h + 1) * kv_rank] = (acc * (1.0 / l)).T.astype(BF16)


def _attn_short_kernel(n_cached_seq, o_frames_hbm, q_ref, kvn_ref, cc_ref, cr_ref, o_ref):
    del o_frames_hbm
    s_id = pl.program_id(0)
    hq = q_ref.shape[1] // N_HEADS
    n_cache = cc_ref.shape[2]
    qs = jnp.concatenate([q_ref[:, h * hq:(h + 1) * hq] for h in range(N_HEADS)], axis=0)
    q_lat = qs[:, :2 * LANE]
    q_pe = qs[:, 2 * LANE:2 * LANE + QK_ROPE]
    kvn = kvn_ref[...]
    s = _dot_nt(qs, kvn)
    m = jnp.max(s, axis=1, keepdims=True)
    p = jnp.exp2(s - m)
    l = jnp.sum(p, axis=1, keepdims=True)
    acc = _dot(p.astype(BF16), kvn[:, :2 * LANE])
    has_cache = s_id < n_cached_seq
    start = 0
    while start < n_cache:
        size = min(CACHE_BLOCK, n_cache - start)
        ck = cc_ref[0, 0, start:start + size, :].astype(BF16)
        rk = cr_ref[0, 0, start:start + size, :].astype(BF16)
        s = _dot_nt(q_lat, ck) + _dot_nt(q_pe, rk)
        s = jnp.where(has_cache, s, NEG_INF)
        m_new = jnp.maximum(m, jnp.max(s, axis=1, keepdims=True))
        alpha = jnp.exp2(m - m_new)
        p = jnp.exp2(s - m_new)
        l = alpha * l + jnp.sum(p, axis=1, keepdims=True)
        acc = alpha * acc + _dot(p.astype(BF16), ck)
        m = m_new
        start += size
    o = (acc / l).astype(BF16)
    for h in range(N_HEADS):
        o_ref[:, h * 2 * LANE:(h + 1) * 2 * LANE] = o[h * SEQ_S:(h + 1) * SEQ_S, :]


def _proj_ffn_kernel(n_fb, xp_ref, xs_ref, ol_ref, wuv_ref, wo_ref, g_ref, wg_ref, wu_ref, wd_ref, h_ref):
    i = pl.program_id(0)
    x = jnp.where(i < n_fb, xp_ref[...], xs_ref[...])
    o = jnp.concatenate(
        [_dot(ol_ref[:, h * 2 * LANE:(h + 1) * 2 * LANE], wuv_ref[h]) for h in range(N_HEADS)], axis=1)
    h1 = x + _dot(o.astype(BF16), wo_ref[...])
    u = _rms(h1, g_ref[...]).astype(BF16)
    act = jax.nn.silu(_dot(u, wg_ref[...])) * _dot(u, wu_ref[...])
    h_ref[...] = h1 + _dot(act.astype(BF16), wd_ref[...])


def _pool_route(h1, ext_ref, cnt_rows, valid, base, gm_ref, pw_ref, ps_ref, gf_ref, wr_ref, br_ref, tri_ref):
    t = h1.shape[0]
    grp = h1.shape[1] // len(POOL_WINDOWS)
    u1 = _rms(h1, gm_ref[...])
    ext_ref[HIST_ROWS:HIST_ROWS + t, :] = u1
    ys = []
    for g, w in enumerate(POOL_WINDOWS):
        cols = slice(g * grp, (g + 1) * grp)
        acc = u1[:, cols]
        for k in range(1, w):
            acc = acc + ext_ref[HIST_ROWS - k:HIST_ROWS - k + t, cols]
        cnt = float(w) if cnt_rows is None else jnp.minimum(float(w), cnt_rows)
        mean = acc / cnt - u1[:, cols]
        ys.append(_dot(mean.astype(BF16), pw_ref[g]))
    h2 = h1 + jnp.concatenate(ys, axis=1) * ps_ref[...]
    u2 = _rms(h2, gf_ref[...])

    n_e = wr_ref.shape[1] // 2
    u2_hi = u2.astype(BF16)
    u2_lo = (u2 - u2_hi.astype(F32)).astype(BF16)
    parts = _dot(u2_hi, wr_ref[...]) + _dot(u2_lo, wr_ref[...])
    logits = parts[:, :n_e] + parts[:, n_e:] + br_ref[...]
    e_iota = lax.broadcasted_iota(jnp.int32, logits.shape, 1).astype(F32)
    v0 = jnp.max(logits, axis=1, keepdims=True)
    e0 = jnp.min(jnp.where(logits == v0, e_iota, float(n_e)), axis=1, keepdims=True)
    rest = jnp.where(e_iota == e0, NEG_INF, logits)
    v1 = jnp.max(rest, axis=1, keepdims=True)
    e1 = jnp.min(jnp.where(rest == v1, e_iota, float(n_e)), axis=1, keepdims=True)
    tt = jnp.exp(v1 - v0)
    g0 = 1.0 / (1.0 + tt)
    g1 = tt / (1.0 + tt)
    hit0 = e_iota == e0
    hit1 = e_iota == e1
    onehot = jnp.where(valid & (hit0 | hit1), 1.0, 0.0)
    rank = _dot(tri_ref[...], onehot.astype(BF16)) + base
    r0 = jnp.sum(jnp.where(hit0, rank, 0.0), axis=1, keepdims=True)
    r1 = jnp.sum(jnp.where(hit1, rank, 0.0), axis=1, keepdims=True)
    cols = (jnp.where(valid, e0, -1).astype(F32), jnp.where(valid, e1, -1).astype(F32), r0, r1, g0, g1)
    info = jnp.zeros(logits.shape, F32)
    for k, col in enumerate(cols):
        info = jnp.where(e_iota == k, col, info)
    new_base = base + jnp.sum(onehot, axis=0, keepdims=True)
    return h2, u2, info, new_base, u1


def _pool_frames_kernel(n_fb, h_ref, hist_ref, gm_ref, pw_ref, ps_ref, gf_ref, wr_ref, br_ref, tri_ref,
                        h2_ref, u2_ref, info_ref, cum_ref, tot_ref, state_ref, ext_ref, base_ref):
    i = pl.program_id(0)

    @pl.when(i == 0)
    def _():
        base_ref[...] = jnp.zeros_like(base_ref)

    @pl.when(i >= n_fb)
    def _():
        h2_ref[...] = jnp.zeros_like(h2_ref)
        u2_ref[...] = jnp.zeros_like(u2_ref)
        info_ref[...] = jnp.zeros_like(info_ref)

    @pl.when(i < n_fb)
    def _():
        ext_ref[:HIST_ROWS, :] = _rms(hist_ref[...], gm_ref[...])
        base = base_ref[...]
        cum_ref[0] = base
        h2, u2, info, new_base, u1 = _pool_route(h_ref[...], ext_ref, None, True, base, gm_ref, pw_ref, ps_ref,
                                                 gf_ref, wr_ref, br_ref, tri_ref)
        h2_ref[...] = h2
        u2_ref[...] = u2.astype(BF16)
        info_ref[...] = info
        base_ref[...] = new_base
        tot_ref[...] = new_base
        state_ref[0] = u1[u1.shape[0] - HIST_ROWS:, :]


def _pool_short_kernel(n_sample, n_valid, h2_hbm, u2_hbm, info_hbm, h_ref, hist_ref, base0_ref, gm_ref, pw_ref, ps_ref,
                       gf_ref, wr_ref, br_ref, tri_ref, h2_ref, u2_ref, info_ref, cum_ref, tot_ref, state_ref,
                       ext_ref, base_ref):
    del h2_hbm, u2_hbm, info_hbm
    s_id = pl.program_id(0)

    @pl.when(s_id == 0)
    def _():
        base_ref[...] = base0_ref[...]

    ext_ref[:HIST_ROWS, :] = hist_ref[0]
    base = base_ref[...]
    cum_ref[0] = base
    pos1 = (lax.broadcasted_iota(jnp.int32, (SEQ_S, 1), 0) + 1).astype(F32)
    cnt_rows = jnp.where(s_id < n_sample, float(max(POOL_WINDOWS)), pos1)
    h2, u2, info, new_base, u1 = _pool_route(h_ref[...], ext_ref, cnt_rows, s_id < n_valid, base, gm_ref, pw_ref,
                                             ps_ref, gf_ref, wr_ref, br_ref, tri_ref)
    h2_ref[...] = h2
    u2_ref[...] = u2.astype(BF16)
    info_ref[...] = info
    base_ref[...] = new_base
    tot_ref[...] = new_base
    state_ref[0] = u1


def _expert_kernel(n_blocks, be_ref, lo_ref, hi_ref, ok_ref, u_hbm, pos_ref, gate_ref, wg_ref, wu_ref, wd_ref, ys_ref,
                   buf_ref, acc_ref, gacc_ref, sem):
    del be_ref
    j = pl.program_id(0)
    rb = ys_ref.shape[0]
    n_slots, tc, _ = buf_ref.shape

    def n_chunks(jj):
        return jnp.where(ok_ref[jj] != 0, hi_ref[jj] - lo_ref[jj] + 1, 0)

    def chunk_copy(c, slot):
        return pltpu.make_async_copy(u_hbm.at[pl.ds(pl.multiple_of(c * tc, tc), tc), :], buf_ref.at[slot],
                                     sem.at[slot])

    def request(jj):
        n_req = jnp.minimum(n_chunks(jj), n_slots)
        for k in range(n_slots):
            @pl.when(k < n_req)
            def _():
                chunk_copy(lo_ref[jj] + k, k).start()

    @pl.when(j == 0)
    def _():
        request(0)

    n = n_chunks(j)
    lo = lo_ref[j]

    @pl.when(n > 0)
    def _():
        acc_ref[...] = jnp.zeros_like(acc_ref)
        gacc_ref[...] = jnp.zeros_like(gacc_ref)
        rows = j * rb + lax.broadcasted_iota(jnp.int32, (rb, tc), 0)

        def consume(c, slot):
            pos = pos_ref[c]
            gate = gate_ref[c]
            m0 = rows == pos[0:1, :]
            m1 = rows == pos[1:2, :]
            sel = jnp.where(m0 | m1, 1.0, 0.0).astype(BF16)
            acc_ref[...] += _dot(sel, buf_ref[slot])
            gacc_ref[...] += jnp.sum(jnp.where(m0, gate[0:1, :], 0.0) + jnp.where(m1, gate[1:2, :], 0.0),
                                     axis=1, keepdims=True)

        def requested(k, carry):
            chunk_copy(lo + k, k).wait()
            consume(lo + k, k)
            return carry

        def overflow(k, carry):
            copy = chunk_copy(lo + k, 0)
            copy.start()
            copy.wait()
            consume(lo + k, 0)
            return carry

        lax.fori_loop(0, jnp.minimum(n, n_slots), requested, 0)
        lax.fori_loop(n_slots, jnp.maximum(n, n_slots), overflow, 0)

    @pl.when(j + 1 < n_blocks)
    def _():
        request(jnp.minimum(j + 1, n_blocks - 1))

    @pl.when(n > 0)
    def _():
        xg = acc_ref[...].astype(BF16)
        act = jax.nn.silu(_dot(xg, wg_ref[0])) * _dot(xg, wu_ref[0])
        ys_ref[...] = (_dot(act.astype(BF16), wd_ref[0]) * gacc_ref[...]).astype(BF16)

    @pl.when(n == 0)
    def _():
        ys_ref[...] = jnp.zeros_like(ys_ref)


def _combine_kernel(n_blocks, n_fb, n_ref, ids_ref, h2_ref, pos_ref, g_ref, ys_hbm, yp_ref, yshort_ref, buf_ref,
                    acc_ref, sem):
    i = pl.program_id(0)
    tb = h2_ref.shape[0]
    max_n, rb, _ = buf_ref.shape
    nxt = jnp.minimum(i + 1, n_blocks - 1)
    n = n_ref[i]
    n_next = jnp.where(i + 1 < n_blocks, n_ref[nxt], 0)

    def chunk_copy(ii, k):
        ch = ids_ref[ii * max_n + k]
        return pltpu.make_async_copy(ys_hbm.at[pl.ds(pl.multiple_of(ch * rb, rb), rb), :], buf_ref.at[k], sem.at[k])

    @pl.when(i == 0)
    def _():
        for k in range(max_n):
            @pl.when(k < n)
            def _():
                chunk_copy(i, k).start()

    acc_ref[...] = jnp.zeros_like(acc_ref)
    p0 = pos_ref[:, 0:1]
    p1 = pos_ref[:, 1:2]
    lane_rows = lax.broadcasted_iota(jnp.int32, (tb, rb), 1)

    def body(k, carry):
        chunk_copy(i, k).wait()
        rows = ids_ref[i * max_n + k] * rb + lane_rows
        sel = jnp.where((rows == p0) | (rows == p1), 1.0, 0.0).astype(BF16)
        acc_ref[...] += _dot(sel, buf_ref[k])

        @pl.when(k < n_next)
        def _():
            chunk_copy(nxt, k).start()

        return carry

    lax.fori_loop(0, n, body, 0)
    for k in range(max_n):
        @pl.when((k >= n) & (k < n_next))
        def _():
            chunk_copy(nxt, k).start()

    y = _rms(h2_ref[...] + acc_ref[...], g_ref[...])

    @pl.when(i < n_fb)
    def _():
        yp_ref[...] = y

    @pl.when(i >= n_fb)
    def _():
        yshort_ref[...] = y


def _rope_table(pos):
    half = QK_ROPE // 2
    inv = ROPE_THETA ** (-jnp.arange(half, dtype=F32) / half)
    ang = pos.astype(F32)[:, None] * inv[None, :]
    cos = jnp.cos(ang)
    sin = jnp.sin(ang)
    zero = jnp.zeros((pos.shape[0], LANE - QK_ROPE), F32)
    return jnp.concatenate([cos, cos, zero, -sin, sin, zero], axis=1)


def _half_swap(w):
    half = QK_ROPE // 2
    return jnp.concatenate([w[..., half:], w[..., :half]], axis=-1)


def _pad_last(w, width):
    return jnp.pad(w, [(0, 0)] * (w.ndim - 1) + [(0, width - w.shape[-1])])


def kernel(x_prompt, x_sample, cache_kv_latent, cache_k_rope, state_pool, meta_tokens, norm_mix, norm_ffn, norm_final,
           mla_w_dq, mla_g_q, mla_w_uq, mla_w_dkv, mla_g_kv, mla_w_uk, mla_w_uv, mla_w_o, pool_w, pool_scale,
           ffn_w_gate, ffn_w_up, ffn_w_down, moe_w_router, moe_b_router, moe_w_gate, moe_w_up, moe_w_down):
    nb, seq, d = x_prompt.shape
    db, dseq, _ = x_sample.shape
    n_cache = cache_kv_latent.shape[2]
    q_rank = mla_w_dq.shape[2]
    kv_rank = mla_g_kv.shape[1]
    v_dim = mla_w_uv.shape[3]
    d_ff = ffn_w_gate.shape[2]
    n_e = moe_w_router.shape[2]
    d_e = moe_w_gate.shape[3]
    tb, rb, tq = TOKEN_BLOCK, ROW_BLOCK, ATTN_BLOCK
    assert norm_mix.shape[0] == 2 and cache_kv_latent.shape[0] == 1 and state_pool.shape[0] == 1
    assert dseq == SEQ_S and N_META == SEQ_S and meta_tokens.shape[0] == N_META
    assert kv_rank == 2 * LANE and QK_NOPE == LANE and QK_ROPE <= LANE and HIST_ROWS >= POOL_HIST
    assert seq % tb == 0 and tb == tq and tq % CHUNK == 0 and tb % SEQ_S == 0 and d % LANE == 0
    assert (n_cache - N_META) % CHUNK == 0 and dseq <= CHUNK

    nf = nb * seq
    n_valid_seq = db + 1
    ns = -(-(n_valid_seq * SEQ_S) // tb) * tb
    n_seq = ns // SEQ_S
    nt = nf + ns
    n_fb, n_sb, n_tb = nf // tb, ns // tb, nt // tb
    sb = seq // tb
    meta_row = nf + db * SEQ_S
    scale = float((QK_NOPE + QK_ROPE) ** -0.5)
    hq = 3 * LANE

    xp = x_prompt.reshape(nf, d)
    xs = jnp.concatenate([x_sample.reshape(db * SEQ_S, d), meta_tokens.astype(x_prompt.dtype),
                          jnp.zeros((ns - n_valid_seq * SEQ_S, d), x_prompt.dtype)], axis=0)

    t_s = jnp.arange(SEQ_S)
    pos_short = jnp.concatenate([jnp.tile(n_cache + t_s, db), jnp.tile(t_s, n_seq - db)])
    tab_frames = _rope_table(N_META + jnp.arange(seq))
    tab = jnp.concatenate([tab_frames, _rope_table(pos_short)], axis=0)

    wuq = mla_w_uq[0].reshape(q_rank, N_HEADS, QK_NOPE + QK_ROPE)
    wuq_pe = wuq[:, :, QK_NOPE:]
    wuq2 = jnp.concatenate([wuq[:, :, :QK_NOPE].reshape(q_rank, -1),
                            _pad_last(wuq_pe, LANE).reshape(q_rank, -1),
                            _pad_last(_half_swap(wuq_pe), LANE).reshape(q_rank, -1)], axis=1).astype(BF16)
    wuq_t = jnp.concatenate([wuq[:, :, :QK_NOPE].reshape(q_rank, -1), wuq_pe.reshape(q_rank, -1),
                             _half_swap(wuq_pe).reshape(q_rank, -1)], axis=1).T.astype(BF16)
    wdkv_r =mla_w_dkv[0][:, kv_rank:]
    wdkv2 = jnp.concatenate([mla_w_dkv[0][:, :kv_rank], _pad_last(wdkv_r, LANE),
                             _pad_last(_half_swap(wdkv_r), LANE)], axis=1).astype(BF16)
    wuk_t = jnp.transpose(mla_w_uk[0], (1, 2, 0)).astype(BF16)
    wuk = jnp.transpose(mla_w_uk[0], (1, 0, 2)).astype(BF16)
    wuv = jnp.transpose(mla_w_uv[0], (1, 0, 2)).astype(BF16)
    row = lambda v: v.reshape(1, -1)

    tok_p = pl.BlockSpec((tb, d), lambda i: (jnp.minimum(i, n_fb - 1), 0))
    tok_s = pl.BlockSpec((tb, d), lambda i: (jnp.maximum(i - n_fb, 0), 0))
    q_short, qt_frames, kvb, kt_all, c_all, r_all = pl.pallas_call(
        functools.partial(_qkv_kernel, n_fb, scale * LOG2_E),
        grid=(n_tb,),
        in_specs=[tok_p, tok_s,
                  pl.BlockSpec((tb, 2 * LANE), lambda i: (jnp.where(i < n_fb, i % sb, sb + i - n_fb), 0)),
                  pl.BlockSpec((2 * LANE, tb), lambda i: (0, i % sb)),
                  _const_spec((1, d)), _const_spec((d, q_rank)), _const_spec((1, q_rank)),
                  _const_spec(wuq2.shape), _const_spec(wuq_t.shape), _const_spec(wuk.shape),
                  _const_spec(wuk_t.shape), _const_spec(wdkv2.shape), _const_spec((1, kv_rank))],
        out_specs=[pl.BlockSpec((tb, N_HEADS * hq), lambda i: (jnp.maximum(i - n_fb, 0), 0)),
                   pl.BlockSpec((1, N_HEADS, hq, tb), lambda i: (jnp.minimum(i, n_fb - 1), 0, 0, 0)),
                   pl.BlockSpec((tb, hq), lambda i: (i, 0)),
                   pl.BlockSpec((1, hq, tb), lambda i: (i, 0, 0)),
                   pl.BlockSpec((tb, kv_rank), lambda i: (i, 0)),
                   pl.BlockSpec((tb, QK_ROPE), lambda i: (i, 0))],
        out_shape=[jax.ShapeDtypeStruct((ns, N_HEADS * hq), BF16),
                   jax.ShapeDtypeStruct((n_fb, N_HEADS, hq, tb), BF16),
                   jax.ShapeDtypeStruct((nt, hq), BF16), jax.ShapeDtypeStruct((n_tb, hq, tb), BF16),
                   jax.ShapeDtypeStruct((nt, kv_rank), F32), jax.ShapeDtypeStruct((nt, QK_ROPE), F32)],
        compiler_params=_params(("arbitrary",)),
        name="qkv",
    )(xp, xs, tab, tab_frames.T, row(norm_mix[0]), mla_w_dq[0].astype(BF16), row(mla_g_q[0]), wuq2, wuq_t, wuk, wuk_t,
      wdkv2, row(mla_g_kv[0]))

    qb = seq // tq
    n_q = nb * qb
    batch_of = lambda i: jnp.minimum(i // qb, nb - 1)
    o_frames = pl.pallas_call(
        functools.partial(_attn_frames_kernel, qb, n_q, meta_row % tb),
        grid=(nt // tq,),
        in_specs=[pl.BlockSpec((1, N_HEADS, hq, tq), lambda i: (jnp.minimum(i, n_q - 1), 0, 0, 0)),
                  pl.BlockSpec((seq, hq), lambda i: (batch_of(i), 0)),
                  pl.BlockSpec((qb, kv_rank, tq), lambda i: (batch_of(i), 0, 0)),
                  pl.BlockSpec((N_META, hq), lambda i: (meta_row // N_META, 0)),
                  pl.BlockSpec((1, kv_rank, tb), lambda i: (meta_row // tb, 0, 0))],
        out_specs=pl.BlockSpec((tq, N_HEADS * kv_rank), lambda i: (i, 0)),
        out_shape=jax.ShapeDtypeStruct((nt, N_HEADS * kv_rank), BF16),
        scratch_shapes=[pltpu.VMEM((N_HEADS, 1, tq), F32), pltpu.VMEM((N_HEADS, 1, tq), F32),
                        pltpu.VMEM((N_HEADS, kv_rank, tq), F32)],
        compiler_params=_params(("parallel",)),
        name="attn_frames",
    )(qt_frames, kvb, kt_all, kvb, kt_all)

    short_blk = lambda s: (nf // SEQ_S + s, 0)
    o_lat = pl.pallas_call(
        functools.partial(_attn_short_kernel, db),
        grid=(n_seq,),
        in_specs=[pl.BlockSpec(memory_space=pl.ANY),
                  pl.BlockSpec((SEQ_S, N_HEADS * hq), lambda s: (s, 0)),
                  pl.BlockSpec((SEQ_S, hq), short_blk),
                  pl.BlockSpec((1, 1, n_cache, kv_rank), lambda s: (0, jnp.minimum(s, db - 1), 0, 0)),
                  pl.BlockSpec((1, 1, n_cache, QK_ROPE), lambda s: (0, jnp.minimum(s, db - 1), 0, 0))],
        out_specs=pl.BlockSpec((SEQ_S, N_HEADS * kv_rank), short_blk),
        out_shape=jax.ShapeDtypeStruct((nt, N_HEADS * kv_rank), BF16),
        input_output_aliases={0: 0},
        compiler_params=_params(("parallel",)),
        name="attn_short",
    )(o_frames, q_short, kvb, cache_kv_latent, cache_k_rope)

    h1 = pl.pallas_call(
        functools.partial(_proj_ffn_kernel, n_fb),
        grid=(n_tb,),
        in_specs=[tok_p, tok_s,
                  pl.BlockSpec((tb, N_HEADS * kv_rank), lambda i: (i, 0)),
                  _const_spec(wuv.shape), _const_spec((N_HEADS * v_dim, d)), _const_spec((1, d)),
                  _const_spec((d, d_ff)), _const_spec((d, d_ff)), _const_spec((d_ff, d))],
        out_specs=pl.BlockSpec((tb, d), lambda i: (i, 0)),
        out_shape=jax.ShapeDtypeStruct((nt, d), F32),
        compiler_params=_params(("parallel",)),
        name="proj_ffn",
    )(xp, xs, o_lat, wuv, mla_w_o[0].astype(BF16), row(norm_ffn[0]),
      ffn_w_gate[0].astype(BF16), ffn_w_up[0].astype(BF16), ffn_w_down[0].astype(BF16))

    route_w = [_const_spec((1, d)), _const_spec(pool_w.shape[1:]), _const_spec((1, d)), _const_spec((1, d)),
               _const_spec((d, 2 * n_e)), _const_spec((1, n_e))]
    wr_hi = moe_w_router[0].astype(BF16)
    wr_lo = (moe_w_router[0].astype(F32) - wr_hi.astype(F32)).astype(BF16)
    route_args = (row(norm_mix[1]), pool_w[0].astype(BF16), row(pool_scale[0]), row(norm_ffn[1]),
                  jnp.concatenate([wr_hi, wr_lo], axis=1), row(moe_b_router[0]))
    tri = lambda n: (jnp.arange(n)[:, None] > jnp.arange(n)[None, :]).astype(BF16)
    hist_blk = lambda i: jnp.where(i % sb == 0, meta_row // HIST_ROWS, i * (tb // HIST_ROWS) - 1)
    h2_f, u2_f, info_f, cum_f, tot_f, state_f = pl.pallas_call(
        functools.partial(_pool_frames_kernel, n_fb),
        grid=(n_tb,),
        in_specs=[pl.BlockSpec((tb, d), lambda i: (i, 0)),
                  pl.BlockSpec((HIST_ROWS, d), lambda i: (hist_blk(i), 0))] + route_w + [_const_spec((tb, tb))],
        out_specs=[pl.BlockSpec((tb, d), lambda i: (i, 0)), pl.BlockSpec((tb, d), lambda i: (i, 0)),
                   pl.BlockSpec((tb, n_e), lambda i: (i, 0)),
                   pl.BlockSpec((1, 1, n_e), lambda i: (jnp.minimum(i, n_fb - 1), 0, 0)),
                   pl.BlockSpec((1, n_e), lambda i: (0, 0)),
                   pl.BlockSpec((1, HIST_ROWS, d), lambda i: (jnp.minimum(i // sb, nb - 1), 0, 0))],
        out_shape=[jax.ShapeDtypeStruct((nt, d), F32), jax.ShapeDtypeStruct((nt, d), BF16),
                   jax.ShapeDtypeStruct((nt, n_e), F32), jax.ShapeDtypeStruct((n_fb, 1, n_e), F32),
                   jax.ShapeDtypeStruct((1, n_e), F32), jax.ShapeDtypeStruct((nb, HIST_ROWS, d), F32)],
        scratch_shapes=[pltpu.VMEM((HIST_ROWS + tb, d), F32), pltpu.VMEM((1, n_e), F32)],
        compiler_params=_params(("arbitrary",)),
        name="pool_route_frames",
    )(h1, h1, *route_args, tri(tb))

    hist_s = jnp.concatenate([
        jnp.pad(state_pool[0].astype(F32), [(0, 0), (HIST_ROWS - POOL_HIST, 0), (0, 0)]),
        jnp.zeros((n_seq - db, HIST_ROWS, d), F32)], axis=0)
    any_spec = pl.BlockSpec(memory_space=pl.ANY)
    h2, u2, info, cum_s, tot_s, state_s = pl.pallas_call(
        functools.partial(_pool_short_kernel, db, n_valid_seq),
        grid=(n_seq,),
        in_specs=[any_spec, any_spec, any_spec,
                  pl.BlockSpec((SEQ_S, d), short_blk),
                  pl.BlockSpec((1, HIST_ROWS, d), lambda s: (s, 0, 0)),
                  _const_spec((1, n_e))] + route_w + [_const_spec((SEQ_S, SEQ_S))],
        out_specs=[pl.BlockSpec((SEQ_S, d), short_blk), pl.BlockSpec((SEQ_S, d), short_blk),
                   pl.BlockSpec((SEQ_S, n_e), short_blk), pl.BlockSpec((1, 1, n_e), lambda s: (s, 0, 0)),
                   pl.BlockSpec((1, n_e), lambda s: (0, 0)),
                   pl.BlockSpec((1, SEQ_S, d), lambda s: (s, 0, 0))],
        out_shape=[jax.ShapeDtypeStruct((nt, d), F32), jax.ShapeDtypeStruct((nt, d), BF16),
                   jax.ShapeDtypeStruct((nt, n_e), F32), jax.ShapeDtypeStruct((n_seq, 1, n_e), F32),
                   jax.ShapeDtypeStruct((1, n_e), F32), jax.ShapeDtypeStruct((n_seq, SEQ_S, d), F32)],
        scratch_shapes=[pltpu.VMEM((HIST_ROWS + SEQ_S, d), F32), pltpu.VMEM((1, n_e), F32)],
        input_output_aliases={0: 0, 1: 1, 2: 2},
        compiler_params=_params(("arbitrary",)),
        name="pool_route_short",
    )(h2_f, u2_f, info_f, h1, hist_s, tot_f, *route_args, tri(SEQ_S))

    counts = tot_s[0].astype(jnp.int32)
    cum = jnp.concatenate([cum_f[:, 0], cum_s[::tb // SEQ_S, 0], tot_s], axis=0).astype(jnp.int32)
    padded = (counts + rb - 1) // rb * rb
    pend = jnp.cumsum(padded)
    pstart = pend - padded
    n_rows_max = -(-(TOP_K * (nf + n_valid_seq * SEQ_S)) // rb) * rb + n_e * rb
    n_rb = n_rows_max // rb
    e_tok = info[:, 0:TOP_K].astype(jnp.int32)
    pos = jnp.where(e_tok >= 0, pstart[jnp.maximum(e_tok, 0)] + info[:, 2:2 + TOP_K].astype(jnp.int32), -1)
    gates = info[:, 4:4 + TOP_K]
    pos_l = pos.reshape(n_tb, tb, TOP_K).transpose(0, 2, 1)
    gate_l = gates.reshape(n_tb, tb, TOP_K).transpose(0, 2, 1)
    pos_c = jnp.pad(pos, [(0, 0), (0, n_e - TOP_K)], constant_values=-1)

    blk_row = jnp.arange(n_rb, dtype=jnp.int32) * rb
    blk_e = jnp.minimum(jnp.sum(pend[None, :] <= blk_row[:, None], axis=1), n_e - 1).astype(jnp.int32)
    blk_ok = (blk_row < pend[-1]).astype(jnp.int32)
    r_lo = blk_row - pstart[blk_e]
    r_hi = jnp.minimum(r_lo + rb, counts[blk_e])
    cum_b = cum.T[blk_e]
    first = jnp.sum(cum_b[:, 1:] <= r_lo[:, None], axis=1)
    last = jnp.sum(cum_b[:, :-1] < r_hi[:, None], axis=1) - 1
    blk_lo = jnp.clip(first, 0, n_tb - 1).astype(jnp.int32)
    blk_hi = jnp.clip(last, blk_lo, n_tb - 1).astype(jnp.int32)

    ys = pl.pallas_call(
        functools.partial(_expert_kernel, n_rb),
        grid_spec=pltpu.PrefetchScalarGridSpec(
            num_scalar_prefetch=4,
            grid=(n_rb,),
            in_specs=[pl.BlockSpec(memory_space=pl.ANY),
                      _const_spec((n_tb, TOP_K, tb)), _const_spec((n_tb, TOP_K, tb)),
                      pl.BlockSpec((1, d, d_e), lambda j, be, lo, hi, ok: (be[j], 0, 0)),
                      pl.BlockSpec((1, d, d_e), lambda j, be, lo, hi, ok: (be[j], 0, 0)),
                      pl.BlockSpec((1, d_e, d), lambda j, be, lo, hi, ok: (be[j], 0, 0))],
            out_specs=pl.BlockSpec((rb, d), lambda j, be, lo, hi, ok: (j, 0)),
            scratch_shapes=[pltpu.VMEM((GATHER_SLOTS, tb, d), BF16), pltpu.VMEM((rb, d), F32),
                            pltpu.VMEM((rb, 1), F32), pltpu.SemaphoreType.DMA((GATHER_SLOTS,))]),
        out_shape=jax.ShapeDtypeStruct((n_rows_max, d), BF16),
        compiler_params=_params(("arbitrary",)),
        name="experts",
    )(blk_e, blk_lo, blk_hi, blk_ok, u2, pos_l, gate_l, moe_w_gate[0].astype(BF16), moe_w_up[0].astype(BF16),
      moe_w_down[0].astype(BF16))

    w_lo = pstart[None, :] + cum[:-1]
    w_hi = pstart[None, :] + cum[1:]
    has = w_hi > w_lo
    c_lo = w_lo // rb
    c_hi = jnp.where(has, (w_hi - 1) // rb, c_lo)
    cand = jnp.stack([c_lo, c_hi], axis=2).reshape(n_tb, 2 * n_e)
    keep = jnp.stack([has, has & (c_hi > c_lo)], axis=2).reshape(n_tb, 2 * n_e)
    order = jnp.argsort(~keep, axis=1, stable=True)
    ids = jnp.where(jnp.take_along_axis(keep, order, axis=1), jnp.take_along_axis(cand, order, axis=1), 0)
    n_ids = jnp.sum(keep, axis=1).astype(jnp.int32)
    y_frames, y_short = pl.pallas_call(
        functools.partial(_combine_kernel, n_tb, n_fb),
        grid_spec=pltpu.PrefetchScalarGridSpec(
            num_scalar_prefetch=2,
            grid=(n_tb,),
            in_specs=[pl.BlockSpec((tb, d), lambda i, n, ids: (i, 0)),
                      pl.BlockSpec((tb, n_e), lambda i, n, ids: (i, 0)),
                      pl.BlockSpec((1, d), lambda i, n, ids: (0, 0)),
                      pl.BlockSpec(memory_space=pl.ANY)],
            out_specs=[pl.BlockSpec((tb, d), lambda i, n, ids: (jnp.minimum(i, n_fb - 1), 0)),
                       pl.BlockSpec((tb, d), lambda i, n, ids: (jnp.maximum(i - n_fb, 0), 0))],
            scratch_shapes=[pltpu.VMEM((2 * n_e, rb, d), BF16), pltpu.VMEM((tb, d), F32),
                            pltpu.SemaphoreType.DMA((2 * n_e,))]),
        out_shape=[jax.ShapeDtypeStruct((nf, d), F32), jax.ShapeDtypeStruct((ns, d), F32)],
        compiler_params=_params(("arbitrary",)),
        name="combine",
    )(n_ids, ids.reshape(-1).astype(jnp.int32), h2, pos_c, row(norm_final), ys)

    y_prompt = y_frames.reshape(nb, seq, d)
    y_sample = y_short[:db * SEQ_S].reshape(db, SEQ_S, d)

    def with_meta(a, width):
        meta = jnp.broadcast_to(a[meta_row:meta_row + N_META][None], (nb, N_META, width))
        return jnp.concatenate([meta, a[:nf].reshape(nb, seq, width)], axis=1)[None]

    c_p = with_meta(c_all, kv_rank)
    r_p = with_meta(r_all, QK_ROPE)
    c_s = c_all[nf:nf + db * SEQ_S].reshape(1, db, SEQ_S, kv_rank)
    r_s = r_all[nf:nf + db * SEQ_S].reshape(1, db, SEQ_S, QK_ROPE)
    s_p = state_f[:, HIST_ROWS - POOL_HIST:][None]
    s_s = state_s[:db, SEQ_S - POOL_HIST:][None]
    return (y_prompt, y_sample, c_p, r_p, s_p, c_s, r_s, s_s)
```

```python
import functools

import jax
import jax.numpy as jnp
from jax import lax
from jax.experimental import pallas as pl
from jax.experimental.pallas import tpu as pltpu

CHUNK = 64
N_META = 16
N_HEADS = 8
QK_NOPE = 128
QK_ROPE = 64
ROPE_THETA = 10000.0
POOL_WINDOWS = (2, 4, 8, 16)
POOL_HIST = max(POOL_WINDOWS) - 1
TOP_K = 2
RMS_EPS = 1e-6

LANE = 128
SEQ_S = 16
HIST_ROWS = 16
TOKEN_BLOCK = 256
ROW_BLOCK = 256
ATTN_BLOCK = 256
CACHE_BLOCK = 512
SCORES_AHEAD = 4
GATHER_SLOTS = 6
VMEM_LIMIT = 56 * 1024 * 1024

F32 = jnp.float32
BF16 = jnp.bfloat16
NEG_INF = float("-inf")
LOG2_E = 1.4426950408889634


def _dot(a, b):
    return jnp.dot(a, b, preferred_element_type=F32)


def _dot_nt(a, b):
    return lax.dot_general(a, b, (((1,), (1,)), ((), ())), preferred_element_type=F32)


def _rms(x, g):
    return x * lax.rsqrt(jnp.mean(x * x, axis=-1, keepdims=True) + RMS_EPS) * g


def _const_spec(shape):
    nd = len(shape)
    return pl.BlockSpec(shape, lambda *_: (0,) * nd, pipeline_mode=pl.Buffered(1))


def _params(sem):
    return pltpu.CompilerParams(dimension_semantics=sem, vmem_limit_bytes=VMEM_LIMIT)


def _qkv_kernel(n_fb, scale, xp_ref, xs_ref, tab_ref, tabt_ref, g_ref, wdq_ref, gq_ref, wuq_ref, wuqt_ref, wukt_ref,
                wdkv_ref, gkv_ref, wukf_ref, wuvt_ref, q_ref, qt_ref, kvb_ref, kh_ref, vt_ref, c_ref, r_ref):
    i = pl.program_id(0)
    x = jnp.where(i < n_fb, xp_ref[...], xs_ref[...])
    u = _rms(x, g_ref[...]).astype(BF16)
    cq = _rms(_dot(u, wdq_ref[...]), gq_ref[...]).astype(BF16)
    cos = tab_ref[:, :LANE]
    sin = tab_ref[:, LANE:]
    hq = QK_NOPE + 2 * LANE

    kv = _dot(u, wdkv_ref[...])
    c = _rms(kv[:, :2 * LANE], gkv_ref[...])
    r = kv[:, 2 * LANE:3 * LANE] * cos + kv[:, 3 * LANE:] * sin
    c_ref[...] = c
    r_ref[...] = r[:, :QK_ROPE]
    c_bf = c.astype(BF16)
    r_bf = r.astype(BF16)

    kn = _dot(c_bf, wukf_ref[...])
    for h in range(N_HEADS):
        kh_ref[:, 2 * h * LANE:(2 * h + 1) * LANE] = kn[:, h * LANE:(h + 1) * LANE].astype(BF16)
        kh_ref[:, (2 * h + 1) * LANE:(2 * h + 2) * LANE] = r_bf
    vt_ref[0] = _dot_nt(wuvt_ref[...], c_bf).astype(BF16)

    @pl.when(i < n_fb)
    def _():
        qat = _dot_nt(wuqt_ref[...], cq)
        cos_t = tabt_ref[:QK_ROPE, :]
        sin_t = tabt_ref[LANE:LANE + QK_ROPE, :]
        pe0 = N_HEADS * QK_NOPE
        sw0 = pe0 + N_HEADS * QK_ROPE
        for h in range(N_HEADS):
            qt_ref[0, h, :QK_NOPE, :] = (qat[h * QK_NOPE:(h + 1) * QK_NOPE, :] * scale).astype(BF16)
            a = qat[pe0 + h * QK_ROPE:pe0 + (h + 1) * QK_ROPE, :]
            b = qat[sw0 + h * QK_ROPE:sw0 + (h + 1) * QK_ROPE, :]
            qt_ref[0, h, QK_NOPE:QK_NOPE + QK_ROPE, :] = ((a * cos_t + b * sin_t) * scale).astype(BF16)
            qt_ref[0, h, QK_NOPE + QK_ROPE:, :] = jnp.zeros((LANE - QK_ROPE, qt_ref.shape[3]), BF16)

    @pl.when(i >= n_fb)
    def _():
        qa = _dot(cq, wuq_ref[...])
        for h in range(N_HEADS):
            qn = qa[:, h * LANE:(h + 1) * LANE].astype(BF16)
            q_ref[:, h * hq:h * hq + 2 * LANE] = (_dot(qn, wukt_ref[h]) * scale).astype(BF16)
            a = qa[:, (N_HEADS + h) * LANE:(N_HEADS + h + 1) * LANE]
            b = qa[:, (2 * N_HEADS + h) * LANE:(2 * N_HEADS + h + 1) * LANE]
            q_ref[:, h * hq + 2 * LANE:(h + 1) * hq] = ((a * cos + b * sin) * scale).astype(BF16)
        kvb_ref[:, :2 * LANE] = c_bf
        kvb_ref[:, 2 * LANE:] = r_bf


def _attn_frames_kernel(qb, meta_col, qt_ref, kh_ref, vt_ref, khm_ref, vtm_ref, o_ref, m_ref, l_ref, acc_ref):
    j = pl.program_id(0) % qb
    tq = o_ref.shape[0]
    hk = kh_ref.shape[1] // N_HEADS
    hv = vt_ref.shape[1] // N_HEADS

    def scores(h, rows):
        return _dot(kh_ref[rows, h * hk:(h + 1) * hk], qt_ref[0, h])

    m_ref[...] = jnp.full(m_ref.shape, NEG_INF, F32)
    l_ref[...] = jnp.zeros_like(l_ref)
    acc_ref[...] = jnp.zeros_like(acc_ref)

    def body(kb, carry):
        rows = pl.ds(pl.multiple_of(kb * tq, tq), tq)
        ahead = [scores(h, rows) for h in range(SCORES_AHEAD)]
        for h in range(N_HEADS):
            s = ahead.pop(0)
            if h + SCORES_AHEAD < N_HEADS:
                ahead.append(scores(h + SCORES_AHEAD, rows))
            m_old = m_ref[h]
            m_new = jnp.maximum(m_old, jnp.max(s, axis=0, keepdims=True))
            alpha = jnp.exp2(m_old - m_new)
            p = jnp.exp2(s - m_new)
            m_ref[h] = m_new
            l_ref[h] = alpha * l_ref[h] + jnp.sum(p, axis=0, keepdims=True)
            acc_ref[h] = alpha * acc_ref[h] + _dot(vt_ref[kb, h * hv:(h + 1) * hv, :], p.astype(BF16))
        return carry

    lax.fori_loop(0, j, body, 0)

    rows = pl.ds(pl.multiple_of(j * tq, tq), tq)
    visible = (lax.broadcasted_iota(jnp.int32, (tq, tq), 0) // CHUNK
               <= lax.broadcasted_iota(jnp.int32, (tq, tq), 1) // CHUNK)

    def last_scores(h):
        return scores(h, rows), _dot(khm_ref[:, h * hk:(h + 1) * hk], qt_ref[0, h])

    ahead = [last_scores(h) for h in range(SCORES_AHEAD)]
    for h in range(N_HEADS):
        s, s_meta = ahead.pop(0)
        s = jnp.where(visible, s, NEG_INF)
        if h + SCORES_AHEAD < N_HEADS:
            ahead.append(last_scores(h + SCORES_AHEAD))
        m_old = m_ref[h]
        m_new = jnp.maximum(m_old, jnp.maximum(jnp.max(s, axis=0, keepdims=True),
                                               jnp.max(s_meta, axis=0, keepdims=True)))
        alpha = jnp.exp2(m_old - m_new)
        p = jnp.exp2(s - m_new)
        p_meta = jnp.exp2(s_meta - m_new)
        l = alpha * l_ref[h] + jnp.sum(p, axis=0, keepdims=True) + jnp.sum(p_meta, axis=0, keepdims=True)
        acc = (alpha * acc_ref[h] + _dot(vt_ref[j, h * hv:(h + 1) * hv, :], p.astype(BF16))
               + _dot(vtm_ref[0, h * hv:(h + 1) * hv, meta_col:meta_col + N_META], p_meta.astype(BF16)))
        o_ref[:, h * hv:(h + 1) * hv] = (acc * (1.0 / l)).T.astype(BF16)


def _attn_short_kernel(n_cached_seq, q_ref, kvn_ref, cc_ref, cr_ref, wuv_ref, o_ref):
    s_id = pl.program_id(0)
    hq = q_ref.shape[1] // N_HEADS
    n_cache = cc_ref.shape[2]
    qs = jnp.concatenate([q_ref[:, h * hq:(h + 1) * hq] for h in range(N_HEADS)], axis=0)
    q_lat = qs[:, :2 * LANE]
    q_pe = qs[:, 2 * LANE:2 * LANE + QK_ROPE]
    kvn = kvn_ref[...]
    s = _dot_nt(qs, kvn)
    m = jnp.max(s, axis=1, keepdims=True)
    p = jnp.exp2(s - m)
    l = jnp.sum(p, axis=1, keepdims=True)
    acc = _dot(p.astype(BF16), kvn[:, :2 * LANE])
    has_cache = s_id < n_cached_seq
    start = 0
    while start < n_cache:
        size = min(CACHE_BLOCK, n_cache - start)
        ck = cc_ref[0, 0, start:start + size, :].astype(BF16)
        rk = cr_ref[0, 0, start:start + size, :].astype(BF16)
        s = _dot_nt(q_lat, ck) + _dot_nt(q_pe, rk)
        s = jnp.where(has_cache, s, NEG_INF)
        m_new = jnp.maximum(m, jnp.max(s, axis=1, keepdims=True))
        alpha = jnp.exp2(m - m_new)
        p = jnp.exp2(s - m_new)
        l = alpha * l + jnp.sum(p, axis=1, keepdims=True)
        acc = alpha * acc + _dot(p.astype(BF16), ck)
        m = m_new
        start += size
    o = (acc / l).astype(BF16)
    hv = wuv_ref.shape[2]
    for h in range(N_HEADS):
        o_ref[:, h * hv:(h + 1) * hv] = _dot(o[h * SEQ_S:(h + 1) * SEQ_S, :], wuv_ref[h]).astype(BF16)


def _proj_ffn_kernel(n_fb, xp_ref, xs_ref, op_ref, os_ref, wo_ref, g_ref, wg_ref, wu_ref, wd_ref, h_ref):
    i = pl.program_id(0)
    x = jnp.where(i < n_fb, xp_ref[...], xs_ref[...])
    o = jnp.where(i < n_fb, op_ref[...], os_ref[...])
    h1 = x + _dot(o, wo_ref[...])
    u = _rms(h1, g_ref[...]).astype(BF16)
    act = jax.nn.silu(_dot(u, wg_ref[...])) * _dot(u, wu_ref[...])
    h_ref[...] = h1 + _dot(act.astype(BF16), wd_ref[...])


def _pool_route(h1, ext_ref, cnt_rows, valid, base, gm_ref, pw_ref, ps_ref, gf_ref, wr_ref, br_ref, tri_ref):
    t = h1.shape[0]
    grp = h1.shape[1] // len(POOL_WINDOWS)
    u1 = _rms(h1, gm_ref[...])
    ext_ref[HIST_ROWS:HIST_ROWS + t, :] = u1
    ys = []
    for g, w in enumerate(POOL_WINDOWS):
        cols = slice(g * grp, (g + 1) * grp)
        acc = u1[:, cols]
        for k in range(1, w):
            acc = acc + ext_ref[HIST_ROWS - k:HIST_ROWS - k + t, cols]
        cnt = float(w) if cnt_rows is None else jnp.minimum(float(w), cnt_rows)
        mean = acc / cnt - u1[:, cols]
        ys.append(_dot(mean.astype(BF16), pw_ref[g]))
    h2 = h1 + jnp.concatenate(ys, axis=1) * ps_ref[...]
    u2 = _rms(h2, gf_ref[...])

    n_e = wr_ref.shape[1] // 2
    u2_hi = u2.astype(BF16)
    u2_lo = (u2 - u2_hi.astype(F32)).astype(BF16)
    parts = _dot(u2_hi, wr_ref[...]) + _dot(u2_lo, wr_ref[...])
    logits = parts[:, :n_e] + parts[:, n_e:] + br_ref[...]
    e_iota = lax.broadcasted_iota(jnp.int32, logits.shape, 1).astype(F32)
    v0 = jnp.max(logits, axis=1, keepdims=True)
    e0 = jnp.min(jnp.where(logits == v0, e_iota, float(n_e)), axis=1, keepdims=True)
    rest = jnp.where(e_iota == e0, NEG_INF, logits)
    v1 = jnp.max(rest, axis=1, keepdims=True)
    e1 = jnp.min(jnp.where(rest == v1, e_iota, float(n_e)), axis=1, keepdims=True)
    tt = jnp.exp(v1 - v0)
    g0 = 1.0 / (1.0 + tt)
    g1 = tt / (1.0 + tt)
    hit0 = e_iota == e0
    hit1 = e_iota == e1
    onehot = jnp.where(valid & (hit0 | hit1), 1.0, 0.0)
    rank = _dot(tri_ref[...], onehot.astype(BF16)) + base
    r0 = jnp.sum(jnp.where(hit0, rank, 0.0), axis=1, keepdims=True)
    r1 = jnp.sum(jnp.where(hit1, rank, 0.0), axis=1, keepdims=True)
    cols = (jnp.where(valid, e0, -1).astype(F32), jnp.where(valid, e1, -1).astype(F32), r0, r1, g0, g1)
    info = jnp.zeros(logits.shape, F32)
    for k, col in enumerate(cols):
        info = jnp.where(e_iota == k, col, info)
    new_base = base + jnp.sum(onehot, axis=0, keepdims=True)
    return h2, u2, info, new_base, u1


def _pool_frames_kernel(n_fb, h_ref, hist_ref, gm_ref, pw_ref, ps_ref, gf_ref, wr_ref, br_ref, tri_ref,
                        h2_ref, u2_ref, info_ref, cum_ref, tot_ref, state_ref, ext_ref, base_ref):
    i = pl.program_id(0)

    @pl.when(i == 0)
    def _():
        base_ref[...] = jnp.zeros_like(base_ref)

    @pl.when(i >= n_fb)
    def _():
        h2_ref[...] = jnp.zeros_like(h2_ref)
        u2_ref[...] = jnp.zeros_like(u2_ref)
        info_ref[...] = jnp.zeros_like(info_ref)

    @pl.when(i < n_fb)
    def _():
        ext_ref[:HIST_ROWS, :] = _rms(hist_ref[...], gm_ref[...])
        base = base_ref[...]
        cum_ref[0] = base
        h2, u2, info, new_base, u1 = _pool_route(h_ref[...], ext_ref, None, True, base, gm_ref, pw_ref, ps_ref,
                                                 gf_ref, wr_ref, br_ref, tri_ref)
        h2_ref[...] = h2
        u2_ref[...] = u2.astype(BF16)
        info_ref[...] = info
        base_ref[...] = new_base
        tot_ref[...] = new_base
        state_ref[0] = u1[u1.shape[0] - HIST_ROWS:, :]


def _pool_short_kernel(n_sample, n_valid, h2_hbm, u2_hbm, info_hbm, h_ref, hist_ref, base0_ref, gm_ref, pw_ref, ps_ref,
                       gf_ref, wr_ref, br_ref, tri_ref, h2_ref, u2_ref, info_ref, cum_ref, tot_ref, state_ref,
                       ext_ref, base_ref):
    del h2_hbm, u2_hbm, info_hbm
    s_id = pl.program_id(0)

    @pl.when(s_id == 0)
    def _():
        base_ref[...] = base0_ref[...]

    ext_ref[:HIST_ROWS, :] = hist_ref[0]
    base = base_ref[...]
    cum_ref[0] = base
    pos1 = (lax.broadcasted_iota(jnp.int32, (SEQ_S, 1), 0) + 1).astype(F32)
    cnt_rows = jnp.where(s_id < n_sample, float(max(POOL_WINDOWS)), pos1)
    h2, u2, info, new_base, u1 = _pool_route(h_ref[...], ext_ref, cnt_rows, s_id < n_valid, base, gm_ref, pw_ref,
                                             ps_ref, gf_ref, wr_ref, br_ref, tri_ref)
    h2_ref[...] = h2
    u2_ref[...] = u2.astype(BF16)
    info_ref[...] = info
    base_ref[...] = new_base
    tot_ref[...] = new_base
    state_ref[0] = u1


def _expert_kernel(n_blocks, be_ref, lo_ref, hi_ref, ok_ref, u_hbm, pos_ref, gate_ref, wg_ref, wu_ref, wd_ref, ys_ref,
                   buf_ref, acc_ref, gacc_ref, sem):
    del be_ref
    j = pl.program_id(0)
    rb = ys_ref.shape[0]
    n_slots, tc, _ = buf_ref.shape

    def n_chunks(jj):
        return jnp.where(ok_ref[jj] != 0, hi_ref[jj] - lo_ref[jj] + 1, 0)

    def chunk_copy(c, slot):
        return pltpu.make_async_copy(u_hbm.at[pl.ds(pl.multiple_of(c * tc, tc), tc), :], buf_ref.at[slot],
                                     sem.at[slot])

    def request(jj):
        n_req = jnp.minimum(n_chunks(jj), n_slots)
        for k in range(n_slots):
            @pl.when(k < n_req)
            def _():
                chunk_copy(lo_ref[jj] + k, k).start()

    @pl.when(j == 0)
    def _():
        request(0)

    n = n_chunks(j)
    lo = lo_ref[j]

    @pl.when(n > 0)
    def _():
        acc_ref[...] = jnp.zeros_like(acc_ref)
        gacc_ref[...] = jnp.zeros_like(gacc_ref)
        rows = j * rb + lax.broadcasted_iota(jnp.int32, (rb, tc), 0)

        def consume(c, slot):
            pos = pos_ref[c]
            gate = gate_ref[c]
            m0 = rows == pos[0:1, :]
            m1 = rows == pos[1:2, :]
            sel = jnp.where(m0 | m1, 1.0, 0.0).astype(BF16)
            acc_ref[...] += _dot(sel, buf_ref[slot])
            gacc_ref[...] += jnp.sum(jnp.where(m0, gate[0:1, :], 0.0) + jnp.where(m1, gate[1:2, :], 0.0),
                                     axis=1, keepdims=True)

        def requested(k, carry):
            chunk_copy(lo + k, k).wait()
            consume(lo + k, k)
            return carry

        def overflow(k, carry):
            copy = chunk_copy(lo + k, 0)
            copy.start()
            copy.wait()
            consume(lo + k, 0)
            return carry

        lax.fori_loop(0, jnp.minimum(n, n_slots), requested, 0)
        lax.fori_loop(n_slots, jnp.maximum(n, n_slots), overflow, 0)

    @pl.when(j + 1 < n_blocks)
    def _():
        request(jnp.minimum(j + 1, n_blocks - 1))

    @pl.when(n > 0)
    def _():
        xg = acc_ref[...].astype(BF16)
        act = jax.nn.silu(_dot(xg, wg_ref[0])) * _dot(xg, wu_ref[0])
        ys_ref[...] = (_dot(act.astype(BF16), wd_ref[0]) * gacc_ref[...]).astype(BF16)

    @pl.when(n == 0)
    def _():
        ys_ref[...] = jnp.zeros_like(ys_ref)


def _combine_kernel(n_blocks, n_fb, n_ref, ids_ref, h2_ref, pos_ref, g_ref, ys_hbm, yp_ref, yshort_ref, buf_ref,
                    acc_ref, sem):
    i = pl.program_id(0)
    tb = h2_ref.shape[0]
    max_n, rb, _ = buf_ref.shape
    nxt = jnp.minimum(i + 1, n_blocks - 1)
    n = n_ref[i]
    n_next = jnp.where(i + 1 < n_blocks, n_ref[nxt], 0)

    def chunk_copy(ii, k):
        ch = ids_ref[ii * max_n + k]
        return pltpu.make_async_copy(ys_hbm.at[pl.ds(pl.multiple_of(ch * rb, rb), rb), :], buf_ref.at[k], sem.at[k])

    @pl.when(i == 0)
    def _():
        for k in range(max_n):
            @pl.when(k < n)
            def _():
                chunk_copy(i, k).start()

    acc_ref[...] = jnp.zeros_like(acc_ref)
    p0 = pos_ref[:, 0:1]
    p1 = pos_ref[:, 1:2]
    lane_rows = lax.broadcasted_iota(jnp.int32, (tb, rb), 1)

    def body(k, carry):
        chunk_copy(i, k).wait()
        rows = ids_ref[i * max_n + k] * rb + lane_rows
        sel = jnp.where((rows == p0) | (rows == p1), 1.0, 0.0).astype(BF16)
        acc_ref[...] += _dot(sel, buf_ref[k])

        @pl.when(k < n_next)
        def _():
            chunk_copy(nxt, k).start()

        return carry

    lax.fori_loop(0, n, body, 0)
    for k in range(max_n):
        @pl.when((k >= n) & (k < n_next))
        def _():
            chunk_copy(nxt, k).start()

    y = _rms(h2_ref[...] + acc_ref[...], g_ref[...])

    @pl.when(i < n_fb)
    def _():
        yp_ref[...] = y

    @pl.when(i >= n_fb)
    def _():
        yshort_ref[...] = y


def _rope_table(pos):
    half = QK_ROPE // 2
    inv = ROPE_THETA ** (-jnp.arange(half, dtype=F32) / half)
    ang = pos.astype(F32)[:, None] * inv[None, :]
    cos = jnp.cos(ang)
    sin = jnp.sin(ang)
    zero = jnp.zeros((pos.shape[0], LANE - QK_ROPE), F32)
    return jnp.concatenate([cos, cos, zero, -sin, sin, zero], axis=1)


def _half_swap(w):
    half = QK_ROPE // 2
    return jnp.concatenate([w[..., half:], w[..., :half]], axis=-1)


def _pad_last(w, width):
    return jnp.pad(w, [(0, 0)] * (w.ndim - 1) + [(0, width - w.shape[-1])])


def kernel(x_prompt, x_sample, cache_kv_latent, cache_k_rope, state_pool, meta_tokens, norm_mix, norm_ffn, norm_final,
           mla_w_dq, mla_g_q, mla_w_uq, mla_w_dkv, mla_g_kv, mla_w_uk, mla_w_uv, mla_w_o, pool_w, pool_scale,
           ffn_w_gate, ffn_w_up, ffn_w_down, moe_w_router, moe_b_router, moe_w_gate, moe_w_up, moe_w_down):
    nb, seq, d = x_prompt.shape
    db, dseq, _ = x_sample.shape
    n_cache = cache_kv_latent.shape[2]
    q_rank = mla_w_dq.shape[2]
    kv_rank = mla_g_kv.shape[1]
    v_dim = mla_w_uv.shape[3]
    d_ff = ffn_w_gate.shape[2]
    n_e = moe_w_router.shape[2]
    d_e = moe_w_gate.shape[3]
    tb, rb, tq = TOKEN_BLOCK, ROW_BLOCK, ATTN_BLOCK
    assert norm_mix.shape[0] == 2 and cache_kv_latent.shape[0] == 1 and state_pool.shape[0] == 1
    assert dseq == SEQ_S and N_META == SEQ_S and meta_tokens.shape[0] == N_META
    assert kv_rank == 2 * LANE and QK_NOPE == LANE and QK_ROPE <= LANE and HIST_ROWS >= POOL_HIST
    assert seq % tb == 0 and tb == tq and tq % CHUNK == 0 and tb % SEQ_S == 0 and d % LANE == 0
    assert (n_cache - N_META) % CHUNK == 0 and dseq <= CHUNK

    nf = nb * seq
    n_valid_seq = db + 1
    ns = -(-(n_valid_seq * SEQ_S) // tb) * tb
    n_seq = ns // SEQ_S
    nt = nf + ns
    n_fb, n_sb, n_tb = nf // tb, ns // tb, nt // tb
    sb = seq // tb
    meta_row = nf + db * SEQ_S
    scale = float((QK_NOPE + QK_ROPE) ** -0.5)
    hq = 3 * LANE

    xp = x_prompt.reshape(nf, d)
    xs = jnp.concatenate([x_sample.reshape(db * SEQ_S, d), meta_tokens.astype(x_prompt.dtype),
                          jnp.zeros((ns - n_valid_seq * SEQ_S, d), x_prompt.dtype)], axis=0)

    t_s = jnp.arange(SEQ_S)
    pos_short = jnp.concatenate([jnp.tile(n_cache + t_s, db), jnp.tile(t_s, n_seq - db)])
    tab_frames = _rope_table(N_META + jnp.arange(seq))
    tab = jnp.concatenate([tab_frames, _rope_table(pos_short)], axis=0)

    wuq = mla_w_uq[0].reshape(q_rank, N_HEADS, QK_NOPE + QK_ROPE)
    wuq_pe = wuq[:, :, QK_NOPE:]
    wuq2 = jnp.concatenate([wuq[:, :, :QK_NOPE].reshape(q_rank, -1),
                            _pad_last(wuq_pe, LANE).reshape(q_rank, -1),
                            _pad_last(_half_swap(wuq_pe), LANE).reshape(q_rank, -1)], axis=1).astype(BF16)
    wuq_t = jnp.concatenate([wuq[:, :, :QK_NOPE].reshape(q_rank, -1), wuq_pe.reshape(q_rank, -1),
                             _half_swap(wuq_pe).reshape(q_rank, -1)], axis=1).T.astype(BF16)
    wdkv_r =mla_w_dkv[0][:, kv_rank:]
    wdkv2 = jnp.concatenate([mla_w_dkv[0][:, :kv_rank], _pad_last(wdkv_r, LANE),
                             _pad_last(_half_swap(wdkv_r), LANE)], axis=1).astype(BF16)
    wuk_t = jnp.transpose(mla_w_uk[0], (1, 2, 0)).astype(BF16)
    wuk_flat = mla_w_uk[0].reshape(kv_rank, N_HEADS * QK_NOPE).astype(BF16)
    wuv = jnp.transpose(mla_w_uv[0], (1, 0, 2)).astype(BF16)
    wuv_t = mla_w_uv[0].reshape(kv_rank, N_HEADS * v_dim).T.astype(BF16)
    row = lambda v: v.reshape(1, -1)
    hk = 2 * LANE

    tok_p = pl.BlockSpec((tb, d), lambda i: (jnp.minimum(i, n_fb - 1), 0))
    tok_s = pl.BlockSpec((tb, d), lambda i: (jnp.maximum(i - n_fb, 0), 0))
    short_tok = lambda i: (jnp.maximum(i - n_fb, 0), 0)
    q_short, qt_frames, kvb_short, kh_all, vt_all, c_all, r_all = pl.pallas_call(
        functools.partial(_qkv_kernel, n_fb, scale * LOG2_E),
        grid=(n_tb,),
        in_specs=[tok_p, tok_s,
                  pl.BlockSpec((tb, 2 * LANE), lambda i: (jnp.where(i < n_fb, i % sb, sb + i - n_fb), 0)),
                  pl.BlockSpec((2 * LANE, tb), lambda i: (0, i % sb)),
                  _const_spec((1, d)), _const_spec((d, q_rank)), _const_spec((1, q_rank)),
                  _const_spec(wuq2.shape), _const_spec(wuq_t.shape), _const_spec(wuk_t.shape),
                  _const_spec(wdkv2.shape), _const_spec((1, kv_rank)), _const_spec(wuk_flat.shape),
                  _const_spec(wuv_t.shape)],
        out_specs=[pl.BlockSpec((tb, N_HEADS * hq), short_tok),
                   pl.BlockSpec((1, N_HEADS, hk, tb), lambda i: (jnp.minimum(i, n_fb - 1), 0, 0, 0)),
                   pl.BlockSpec((tb, hq), short_tok),
                   pl.BlockSpec((tb, N_HEADS * hk), lambda i: (i, 0)),
                   pl.BlockSpec((1, N_HEADS * v_dim, tb), lambda i: (i, 0, 0)),
                   pl.BlockSpec((tb, kv_rank), lambda i: (i, 0)),
                   pl.BlockSpec((tb, QK_ROPE), lambda i: (i, 0))],
        out_shape=[jax.ShapeDtypeStruct((ns, N_HEADS * hq), BF16),
                   jax.ShapeDtypeStruct((n_fb, N_HEADS, hk, tb), BF16),
                   jax.ShapeDtypeStruct((ns, hq), BF16),
                   jax.ShapeDtypeStruct((nt, N_HEADS * hk), BF16),
                   jax.ShapeDtypeStruct((n_tb, N_HEADS * v_dim, tb), BF16),
                   jax.ShapeDtypeStruct((nt, kv_rank), F32), jax.ShapeDtypeStruct((nt, QK_ROPE), F32)],
        compiler_params=_params(("arbitrary",)),
        name="qkv",
    )(xp, xs, tab, tab_frames.T, row(norm_mix[0]), mla_w_dq[0].astype(BF16), row(mla_g_q[0]), wuq2, wuq_t, wuk_t,
      wdkv2, row(mla_g_kv[0]), wuk_flat, wuv_t)

    qb = seq // tq
    once = pl.Buffered(1)
    o_frames = pl.pallas_call(
        functools.partial(_attn_frames_kernel, qb, meta_row % tb),
        grid=(n_fb,),
        in_specs=[pl.BlockSpec((1, N_HEADS, hk, tq), lambda i: (i, 0, 0, 0)),
                  pl.BlockSpec((seq, N_HEADS * hk), lambda i: (i // qb, 0), pipeline_mode=once),
                  pl.BlockSpec((qb, N_HEADS * v_dim, tq), lambda i: (i // qb, 0, 0), pipeline_mode=once),
                  pl.BlockSpec((N_META, N_HEADS * hk), lambda i: (meta_row // N_META, 0), pipeline_mode=once),
                  pl.BlockSpec((1, N_HEADS * v_dim, tb), lambda i: (meta_row // tb, 0, 0), pipeline_mode=once)],
        out_specs=pl.BlockSpec((tq, N_HEADS * v_dim), lambda i: (i, 0)),
        out_shape=jax.ShapeDtypeStruct((nf, N_HEADS * v_dim), BF16),
        scratch_shapes=[pltpu.VMEM((N_HEADS, 1, tq), F32), pltpu.VMEM((N_HEADS, 1, tq), F32),
                        pltpu.VMEM((N_HEADS, v_dim, tq), F32)],
        compiler_params=_params(("parallel",)),
        name="attn_frames",
    )(qt_frames, kh_all, vt_all, kh_all, vt_all)

    o_short = pl.pallas_call(
        functools.partial(_attn_short_kernel, db),
        grid=(n_seq,),
        in_specs=[pl.BlockSpec((SEQ_S, N_HEADS * hq), lambda s: (s, 0)),
                  pl.BlockSpec((SEQ_S, hq), lambda s: (s, 0)),
                  pl.BlockSpec((1, 1, n_cache, kv_rank), lambda s: (0, jnp.minimum(s, db - 1), 0, 0)),
                  pl.BlockSpec((1, 1, n_cache, QK_ROPE), lambda s: (0, jnp.minimum(s, db - 1), 0, 0)),
                  _const_spec(wuv.shape)],
        out_specs=pl.BlockSpec((SEQ_S, N_HEADS * v_dim), lambda s: (s, 0)),
        out_shape=jax.ShapeDtypeStruct((ns, N_HEADS * v_dim), BF16),
        compiler_params=_params(("parallel",)),
        name="attn_short",
    )(q_short, kvb_short, cache_kv_latent, cache_k_rope, wuv)

    h1 = pl.pallas_call(
        functools.partial(_proj_ffn_kernel, n_fb),
        grid=(n_tb,),
        in_specs=[tok_p, tok_s,
                  pl.BlockSpec((tb, N_HEADS * v_dim), lambda i: (jnp.minimum(i, n_fb - 1), 0)),
                  pl.BlockSpec((tb, N_HEADS * v_dim), short_tok),
                  _const_spec((N_HEADS * v_dim, d)), _const_spec((1, d)),
                  _const_spec((d, d_ff)), _const_spec((d, d_ff)), _const_spec((d_ff, d))],
        out_specs=pl.BlockSpec((tb, d), lambda i: (i, 0)),
        out_shape=jax.ShapeDtypeStruct((nt, d), F32),
        compiler_params=_params(("parallel",)),
        name="proj_ffn",
    )(xp, xs, o_frames, o_short, mla_w_o[0].astype(BF16), row(norm_ffn[0]),
      ffn_w_gate[0].astype(BF16), ffn_w_up[0].astype(BF16), ffn_w_down[0].astype(BF16))

    route_w = [_const_spec((1, d)), _const_spec(pool_w.shape[1:]), _const_spec((1, d)), _const_spec((1, d)),
               _const_spec((d, 2 * n_e)), _const_spec((1, n_e))]
    wr_hi = moe_w_router[0].astype(BF16)
    wr_lo = (moe_w_router[0].astype(F32) - wr_hi.astype(F32)).astype(BF16)
    route_args = (row(norm_mix[1]), pool_w[0].astype(BF16), row(pool_scale[0]), row(norm_ffn[1]),
                  jnp.concatenate([wr_hi, wr_lo], axis=1), row(moe_b_router[0]))
    tri = lambda n: (jnp.arange(n)[:, None] > jnp.arange(n)[None, :]).astype(BF16)
    hist_blk = lambda i: jnp.where(i % sb == 0, meta_row // HIST_ROWS, i * (tb // HIST_ROWS) - 1)
    h2_f, u2_f, info_f, cum_f, tot_f, state_f = pl.pallas_call(
        functools.partial(_pool_frames_kernel, n_fb),
        grid=(n_tb,),
        in_specs=[pl.BlockSpec((tb, d), lambda i: (i, 0)),
                  pl.BlockSpec((HIST_ROWS, d), lambda i: (hist_blk(i), 0))] + route_w + [_const_spec((tb, tb))],
        out_specs=[pl.BlockSpec((tb, d), lambda i: (i, 0)), pl.BlockSpec((tb, d), lambda i: (i, 0)),
                   pl.BlockSpec((tb, n_e), lambda i: (i, 0)),
                   pl.BlockSpec((1, 1, n_e), lambda i: (jnp.minimum(i, n_fb - 1), 0, 0)),
                   pl.BlockSpec((1, n_e), lambda i: (0, 0)),
                   pl.BlockSpec((1, HIST_ROWS, d), lambda i: (jnp.minimum(i // sb, nb - 1), 0, 0))],
        out_shape=[jax.ShapeDtypeStruct((nt, d), F32), jax.ShapeDtypeStruct((nt, d), BF16),
                   jax.ShapeDtypeStruct((nt, n_e), F32), jax.ShapeDtypeStruct((n_fb, 1, n_e), F32),
                   jax.ShapeDtypeStruct((1, n_e), F32), jax.ShapeDtypeStruct((nb, HIST_ROWS, d), F32)],
        scratch_shapes=[pltpu.VMEM((HIST_ROWS + tb, d), F32), pltpu.VMEM((1, n_e), F32)],
        compiler_params=_params(("arbitrary",)),
        name="pool_route_frames",
    )(h1, h1, *route_args, tri(tb))

    hist_s = jnp.concatenate([
        jnp.pad(state_pool[0].astype(F32), [(0, 0), (HIST_ROWS - POOL_HIST, 0), (0, 0)]),
        jnp.zeros((n_seq - db, HIST_ROWS, d), F32)], axis=0)
    any_spec = pl.BlockSpec(memory_space=pl.ANY)
    short_blk = lambda s: (nf // SEQ_S + s, 0)
    h2, u2, info, cum_s, tot_s, state_s = pl.pallas_call(
        functools.partial(_pool_short_kernel, db, n_valid_seq),
        grid=(n_seq,),
        in_specs=[any_spec, any_spec, any_spec,
                  pl.BlockSpec((SEQ_S, d), short_blk),
                  pl.BlockSpec((1, HIST_ROWS, d), lambda s: (s, 0, 0)),
                  _const_spec((1, n_e))] + route_w + [_const_spec((SEQ_S, SEQ_S))],
        out_specs=[pl.BlockSpec((SEQ_S, d), short_blk), pl.BlockSpec((SEQ_S, d), short_blk),
                   pl.BlockSpec((SEQ_S, n_e), short_blk), pl.BlockSpec((1, 1, n_e), lambda s: (s, 0, 0)),
                   pl.BlockSpec((1, n_e), lambda s: (0, 0)),
                   pl.BlockSpec((1, SEQ_S, d), lambda s: (s, 0, 0))],
        out_shape=[jax.ShapeDtypeStruct((nt, d), F32), jax.ShapeDtypeStruct((nt, d), BF16),
                   jax.ShapeDtypeStruct((nt, n_e), F32), jax.ShapeDtypeStruct((n_seq, 1, n_e), F32),
                   jax.ShapeDtypeStruct((1, n_e), F32), jax.ShapeDtypeStruct((n_seq, SEQ_S, d), F32)],
        scratch_shapes=[pltpu.VMEM((HIST_ROWS + SEQ_S, d), F32), pltpu.VMEM((1, n_e), F32)],
        input_output_aliases={0: 0, 1: 1, 2: 2},
        compiler_params=_params(("arbitrary",)),
        name="pool_route_short",
    )(h2_f, u2_f, info_f, h1, hist_s, tot_f, *route_args, tri(SEQ_S))

    counts = tot_s[0].astype(jnp.int32)
    cum = jnp.concatenate([cum_f[:, 0], cum_s[::tb // SEQ_S, 0], tot_s], axis=0).astype(jnp.int32)
    padded = (counts + rb - 1) // rb * rb
    pend = jnp.cumsum(padded)
    pstart = pend - padded
    n_rows_max = -(-(TOP_K * (nf + n_valid_seq * SEQ_S)) // rb) * rb + n_e * rb
    n_rb = n_rows_max // rb
    e_tok = info[:, 0:TOP_K].astype(jnp.int32)
    pos = jnp.where(e_tok >= 0, pstart[jnp.maximum(e_tok, 0)] + info[:, 2:2 + TOP_K].astype(jnp.int32), -1)
    gates = info[:, 4:4 + TOP_K]
    pos_l = pos.reshape(n_tb, tb, TOP_K).transpose(0, 2, 1)
    gate_l = gates.reshape(n_tb, tb, TOP_K).transpose(0, 2, 1)
    pos_c = jnp.pad(pos, [(0, 0), (0, n_e - TOP_K)], constant_values=-1)

    blk_row = jnp.arange(n_rb, dtype=jnp.int32) * rb
    blk_e = jnp.minimum(jnp.sum(pend[None, :] <= blk_row[:, None], axis=1), n_e - 1).astype(jnp.int32)
    blk_ok = (blk_row < pend[-1]).astype(jnp.int32)
    r_lo = blk_row - pstart[blk_e]
    r_hi = jnp.minimum(r_lo + rb, counts[blk_e])
    cum_b = cum.T[blk_e]
    first = jnp.sum(cum_b[:, 1:] <= r_lo[:, None], axis=1)
    last = jnp.sum(cum_b[:, :-1] < r_hi[:, None], axis=1) - 1
    blk_lo = jnp.clip(first, 0, n_tb - 1).astype(jnp.int32)
    blk_hi = jnp.clip(last, blk_lo, n_tb - 1).astype(jnp.int32)

    ys = pl.pallas_call(
        functools.partial(_expert_kernel, n_rb),
        grid_spec=pltpu.PrefetchScalarGridSpec(
            num_scalar_prefetch=4,
            grid=(n_rb,),
            in_specs=[pl.BlockSpec(memory_space=pl.ANY),
                      _const_spec((n_tb, TOP_K, tb)), _const_spec((n_tb, TOP_K, tb)),
                      pl.BlockSpec((1, d, d_e), lambda j, be, lo, hi, ok: (be[j], 0, 0)),
                      pl.BlockSpec((1, d, d_e), lambda j, be, lo, hi, ok: (be[j], 0, 0)),
                      pl.BlockSpec((1, d_e, d), lambda j, be, lo, hi, ok: (be[j], 0, 0))],
            out_specs=pl.BlockSpec((rb, d), lambda j, be, lo, hi, ok: (j, 0)),
            scratch_shapes=[pltpu.VMEM((GATHER_SLOTS, tb, d), BF16), pltpu.VMEM((rb, d), F32),
                            pltpu.VMEM((rb, 1), F32), pltpu.SemaphoreType.DMA((GATHER_SLOTS,))]),
        out_shape=jax.ShapeDtypeStruct((n_rows_max, d), BF16),
        compiler_params=_params(("arbitrary",)),
        name="experts",
    )(blk_e, blk_lo, blk_hi, blk_ok, u2, pos_l, gate_l, moe_w_gate[0].astype(BF16), moe_w_up[0].astype(BF16),
      moe_w_down[0].astype(BF16))

    w_lo = pstart[None, :] + cum[:-1]
    w_hi = pstart[None, :] + cum[1:]
    has = w_hi > w_lo
    c_lo = w_lo // rb
    c_hi = jnp.where(has, (w_hi - 1) // rb, c_lo)
    cand = jnp.stack([c_lo, c_hi], axis=2).reshape(n_tb, 2 * n_e)
    keep = jnp.stack([has, has & (c_hi > c_lo)], axis=2).reshape(n_tb, 2 * n_e)
    order = jnp.argsort(~keep, axis=1, stable=True)
    ids = jnp.where(jnp.take_along_axis(keep, order, axis=1), jnp.take_along_axis(cand, order, axis=1), 0)
    n_ids = jnp.sum(keep, axis=1).astype(jnp.int32)
    y_frames, y_short = pl.pallas_call(
        functools.partial(_combine_kernel, n_tb, n_fb),
        grid_spec=pltpu.PrefetchScalarGridSpec(
            num_scalar_prefetch=2,
            grid=(n_tb,),
            in_specs=[pl.BlockSpec((tb, d), lambda i, n, ids: (i, 0)),
                      pl.BlockSpec((tb, n_e), lambda i, n, ids: (i, 0)),
                      pl.BlockSpec((1, d), lambda i, n, ids: (0, 0)),
                      pl.BlockSpec(memory_space=pl.ANY)],
            out_specs=[pl.BlockSpec((tb, d), lambda i, n, ids: (jnp.minimum(i, n_fb - 1), 0)),
                       pl.BlockSpec((tb, d), lambda i, n, ids: (jnp.maximum(i - n_fb, 0), 0))],
            scratch_shapes=[pltpu.VMEM((2 * n_e, rb, d), BF16), pltpu.VMEM((tb, d), F32),
                            pltpu.SemaphoreType.DMA((2 * n_e,))]),
        out_shape=[jax.ShapeDtypeStruct((nf, d), F32), jax.ShapeDtypeStruct((ns, d), F32)],
        compiler_params=_params(("arbitrary",)),
        name="combine",
    )(n_ids, ids.reshape(-1).astype(jnp.int32), h2, pos_c, row(norm_final), ys)

    y_prompt = y_frames.reshape(nb, seq, d)
    y_sample = y_short[:db * SEQ_S].reshape(db, SEQ_S, d)

    def with_meta(a, width):
        meta = jnp.broadcast_to(a[meta_row:meta_row + N_META][None], (nb, N_META, width))
        return jnp.concatenate([meta, a[:nf].reshape(nb, seq, width)], axis=1)[None]

    c_p = with_meta(c_all, kv_rank)
    r_p = with_meta(r_all, QK_ROPE)
    c_s = c_all[nf:nf + db * SEQ_S].reshape(1, db, SEQ_S, kv_rank)
    r_s = r_all[nf:nf + db * SEQ_S].reshape(1, db, SEQ_S, QK_ROPE)
    s_p = state_f[:, HIST_ROWS - POOL_HIST:][None]
    s_s = state_s[:db, SEQ_S - POOL_HIST:][None]
    return (y_prompt, y_sample, c_p, r_p, s_p, c_s, r_s, s_s)
```

```python
import functools

import jax
import jax.numpy as jnp
from jax import lax
from jax.experimental import pallas as pl
from jax.experimental.pallas import tpu as pltpu

CHUNK = 64
N_META = 16
N_HEADS = 8
QK_NOPE = 128
QK_ROPE = 64
ROPE_THETA = 10000.0
POOL_WINDOWS = (2, 4, 8, 16)
POOL_HIST = max(POOL_WINDOWS) - 1
TOP_K = 2
RMS_EPS = 1e-6

LANE = 128
SEQ_S = 16
HIST_ROWS = 16
TOKEN_BLOCK = 256
ROW_BLOCK = 256
ATTN_BLOCK = 256
CACHE_BLOCK = 512
SCORES_AHEAD = 4
COMBINE_GROUP = 4
GATHER_SLOTS = 6
VMEM_LIMIT = 56 * 1024 * 1024

F32 = jnp.float32
BF16 = jnp.bfloat16
NEG_INF = float("-inf")
LOG2_E = 1.4426950408889634


def _dot(a, b):
    return jnp.dot(a, b, preferred_element_type=F32)


def _dot_nt(a, b):
    return lax.dot_general(a, b, (((1,), (1,)), ((), ())), preferred_element_type=F32)


def _rms(x, g):
    return x * lax.rsqrt(jnp.mean(x * x, axis=-1, keepdims=True) + RMS_EPS) * g


def _const_spec(shape):
    nd = len(shape)
    return pl.BlockSpec(shape, lambda *_: (0,) * nd, pipeline_mode=pl.Buffered(1))


def _params(sem):
    return pltpu.CompilerParams(dimension_semantics=sem, vmem_limit_bytes=VMEM_LIMIT)


def _qkv_kernel(n_fb, scale, xp_ref, xs_ref, tab_ref, tabt_ref, g_ref, wdq_ref, gq_ref, wuq_ref, wuqt_ref, wukt_ref,
                wdkv_ref, gkv_ref, wukf_ref, wuvt_ref, q_ref, qt_ref, kvb_ref, kh_ref, vt_ref, c_ref, r_ref):
    i = pl.program_id(0)
    x = jnp.where(i < n_fb, xp_ref[...], xs_ref[...])
    u = _rms(x, g_ref[...]).astype(BF16)
    cq = _rms(_dot(u, wdq_ref[...]), gq_ref[...]).astype(BF16)
    cos = tab_ref[:, :LANE]
    sin = tab_ref[:, LANE:]
    hq = QK_NOPE + 2 * LANE

    kv = _dot(u, wdkv_ref[...])
    c = _rms(kv[:, :2 * LANE], gkv_ref[...])
    r = kv[:, 2 * LANE:3 * LANE] * cos + kv[:, 3 * LANE:] * sin
    c_ref[...] = c
    r_ref[...] = r[:, :QK_ROPE]
    c_bf = c.astype(BF16)
    r_bf = r.astype(BF16)

    kn = _dot(c_bf, wukf_ref[...])
    for h in range(N_HEADS):
        kh_ref[:, 2 * h * LANE:(2 * h + 1) * LANE] = kn[:, h * LANE:(h + 1) * LANE].astype(BF16)
        kh_ref[:, (2 * h + 1) * LANE:(2 * h + 2) * LANE] = r_bf
    vt_ref[0] = _dot_nt(wuvt_ref[...], c_bf).astype(BF16)

    @pl.when(i < n_fb)
    def _():
        qat = _dot_nt(wuqt_ref[...], cq)
        cos_t = tabt_ref[:QK_ROPE, :]
        sin_t = tabt_ref[LANE:LANE + QK_ROPE, :]
        pe0 = N_HEADS * QK_NOPE
        sw0 = pe0 + N_HEADS * QK_ROPE
        for h in range(N_HEADS):
            qt_ref[0, h, :QK_NOPE, :] = (qat[h * QK_NOPE:(h + 1) * QK_NOPE, :] * scale).astype(BF16)
            a = qat[pe0 + h * QK_ROPE:pe0 + (h + 1) * QK_ROPE, :]
            b = qat[sw0 + h * QK_ROPE:sw0 + (h + 1) * QK_ROPE, :]
            qt_ref[0, h, QK_NOPE:QK_NOPE + QK_ROPE, :] = ((a * cos_t + b * sin_t) * scale).astype(BF16)
            qt_ref[0, h, QK_NOPE + QK_ROPE:, :] = jnp.zeros((LANE - QK_ROPE, qt_ref.shape[3]), BF16)

    @pl.when(i >= n_fb)
    def _():
        qa = _dot(cq, wuq_ref[...])
        for h in range(N_HEADS):
            qn = qa[:, h * LANE:(h + 1) * LANE].astype(BF16)
            q_ref[:, h * hq:h * hq + 2 * LANE] = (_dot(qn, wukt_ref[h]) * scale).astype(BF16)
            a = qa[:, (N_HEADS + h) * LANE:(N_HEADS + h + 1) * LANE]
            b = qa[:, (2 * N_HEADS + h) * LANE:(2 * N_HEADS + h + 1) * LANE]
            q_ref[:, h * hq + 2 * LANE:(h + 1) * hq] = ((a * cos + b * sin) * scale).astype(BF16)
        kvb_ref[:, :2 * LANE] = c_bf
        kvb_ref[:, 2 * LANE:] = r_bf


def _attn_frames_kernel(qb, meta_col, qt_ref, kh_ref, vt_ref, khm_ref, vtm_ref, o_ref, m_ref, l_ref, acc_ref):
    j = pl.program_id(0) % qb
    tq = o_ref.shape[0]
    hk = kh_ref.shape[1] // N_HEADS
    hv = vt_ref.shape[1] // N_HEADS

    def scores(h, rows):
        return _dot(kh_ref[rows, h * hk:(h + 1) * hk], qt_ref[0, h])

    m_ref[...] = jnp.full(m_ref.shape, NEG_INF, F32)
    l_ref[...] = jnp.zeros_like(l_ref)
    acc_ref[...] = jnp.zeros_like(acc_ref)

    def body(kb, carry):
        rows = pl.ds(pl.multiple_of(kb * tq, tq), tq)
        ahead = [scores(h, rows) for h in range(SCORES_AHEAD)]
        for h in range(N_HEADS):
            s = ahead.pop(0)
            if h + SCORES_AHEAD < N_HEADS:
                ahead.append(scores(h + SCORES_AHEAD, rows))
            m_old = m_ref[h]
            m_new = jnp.maximum(m_old, jnp.max(s, axis=0, keepdims=True))
            alpha = jnp.exp2(m_old - m_new)
            p = jnp.exp2(s - m_new)
            m_ref[h] = m_new
            l_ref[h] = alpha * l_ref[h] + jnp.sum(p, axis=0, keepdims=True)
            acc_ref[h] = alpha * acc_ref[h] + _dot(vt_ref[kb, h * hv:(h + 1) * hv, :], p.astype(BF16))
        return carry

    lax.fori_loop(0, j, body, 0)

    rows = pl.ds(pl.multiple_of(j * tq, tq), tq)
    visible = (lax.broadcasted_iota(jnp.int32, (tq, tq), 0) // CHUNK
               <= lax.broadcasted_iota(jnp.int32, (tq, tq), 1) // CHUNK)

    def last_scores(h):
        return scores(h, rows), _dot(khm_ref[:, h * hk:(h + 1) * hk], qt_ref[0, h])

    ahead = [last_scores(h) for h in range(SCORES_AHEAD)]
    for h in range(N_HEADS):
        s, s_meta = ahead.pop(0)
        s = jnp.where(visible, s, NEG_INF)
        if h + SCORES_AHEAD < N_HEADS:
            ahead.append(last_scores(h + SCORES_AHEAD))
        m_old = m_ref[h]
        m_new = jnp.maximum(m_old, jnp.maximum(jnp.max(s, axis=0, keepdims=True),
                                               jnp.max(s_meta, axis=0, keepdims=True)))
        alpha = jnp.exp2(m_old - m_new)
        p = jnp.exp2(s - m_new)
        p_meta = jnp.exp2(s_meta - m_new)
        l = alpha * l_ref[h] + jnp.sum(p, axis=0, keepdims=True) + jnp.sum(p_meta, axis=0, keepdims=True)
        acc = (alpha * acc_ref[h] + _dot(vt_ref[j, h * hv:(h + 1) * hv, :], p.astype(BF16))
               + _dot(vtm_ref[0, h * hv:(h + 1) * hv, meta_col:meta_col + N_META], p_meta.astype(BF16)))
        o_ref[:, h * hv:(h + 1) * hv] = (acc * (1.0 / l)).T.astype(BF16)


def _attn_short_kernel(n_cached_seq, q_ref, kvn_ref, cc_ref, cr_ref, wuv_ref, o_ref):
    s_id = pl.program_id(0)
    hq = q_ref.shape[1] // N_HEADS
    n_cache = cc_ref.shape[2]
    qs = jnp.concatenate([q_ref[:, h * hq:(h + 1) * hq] for h in range(N_HEADS)], axis=0)
    q_lat = qs[:, :2 * LANE]
    q_pe = qs[:, 2 * LANE:2 * LANE + QK_ROPE]
    kvn = kvn_ref[...]
    s = _dot_nt(qs, kvn)
    m = jnp.max(s, axis=1, keepdims=True)
    p = jnp.exp2(s - m)
    l = jnp.sum(p, axis=1, keepdims=True)
    acc = _dot(p.astype(BF16), kvn[:, :2 * LANE])
    has_cache = s_id < n_cached_seq
    start = 0
    while start < n_cache:
        size = min(CACHE_BLOCK, n_cache - start)
        ck = cc_ref[0, 0, start:start + size, :].astype(BF16)
        rk = cr_ref[0, 0, start:start + size, :].astype(BF16)
        s = _dot_nt(q_lat, ck) + _dot_nt(q_pe, rk)
        s = jnp.where(has_cache, s, NEG_INF)
        m_new = jnp.maximum(m, jnp.max(s, axis=1, keepdims=True))
        alpha = jnp.exp2(m - m_new)
        p = jnp.exp2(s - m_new)
        l = alpha * l + jnp.sum(p, axis=1, keepdims=True)
        acc = alpha * acc + _dot(p.astype(BF16), ck)
        m = m_new
        start += size
    o = (acc / l).astype(BF16)
    hv = wuv_ref.shape[2]
    for h in range(N_HEADS):
        o_ref[:, h * hv:(h + 1) * hv] = _dot(o[h * SEQ_S:(h + 1) * SEQ_S, :], wuv_ref[h]).astype(BF16)


def _proj_ffn_kernel(n_fb, xp_ref, xs_ref, op_ref, os_ref, wo_ref, g_ref, wg_ref, wu_ref, wd_ref, h_ref):
    i = pl.program_id(0)
    x = jnp.where(i < n_fb, xp_ref[...], xs_ref[...])
    o = jnp.where(i < n_fb, op_ref[...], os_ref[...])
    h1 = x + _dot(o, wo_ref[...])
    u = _rms(h1, g_ref[...]).astype(BF16)
    act = jax.nn.silu(_dot(u, wg_ref[...])) * _dot(u, wu_ref[...])
    h_ref[...] = h1 + _dot(act.astype(BF16), wd_ref[...])


def _pool_route(h1, ext_ref, cnt_rows, valid, base, gm_ref, pw_ref, ps_ref, gf_ref, wr_ref, br_ref, tri_ref):
    t = h1.shape[0]
    grp = h1.shape[1] // len(POOL_WINDOWS)
    u1 = _rms(h1, gm_ref[...])
    ext_ref[HIST_ROWS:HIST_ROWS + t, :] = u1
    ys = []
    for g, w in enumerate(POOL_WINDOWS):
        cols = slice(g * grp, (g + 1) * grp)
        acc = u1[:, cols]
        for k in range(1, w):
            acc = acc + ext_ref[HIST_ROWS - k:HIST_ROWS - k + t, cols]
        cnt = float(w) if cnt_rows is None else jnp.minimum(float(w), cnt_rows)
        mean = acc / cnt - u1[:, cols]
        ys.append(_dot(mean.astype(BF16), pw_ref[g]))
    h2 = h1 + jnp.concatenate(ys, axis=1) * ps_ref[...]
    u2 = _rms(h2, gf_ref[...])

    n_e = wr_ref.shape[1] // 2
    u2_hi = u2.astype(BF16)
    u2_lo = (u2 - u2_hi.astype(F32)).astype(BF16)
    parts = _dot(u2_hi, wr_ref[...]) + _dot(u2_lo, wr_ref[...])
    logits = parts[:, :n_e] + parts[:, n_e:] + br_ref[...]
    e_iota = lax.broadcasted_iota(jnp.int32, logits.shape, 1).astype(F32)
    v0 = jnp.max(logits, axis=1, keepdims=True)
    e0 = jnp.min(jnp.where(logits == v0, e_iota, float(n_e)), axis=1, keepdims=True)
    rest = jnp.where(e_iota == e0, NEG_INF, logits)
    v1 = jnp.max(rest, axis=1, keepdims=True)
    e1 = jnp.min(jnp.where(rest == v1, e_iota, float(n_e)), axis=1, keepdims=True)
    tt = jnp.exp(v1 - v0)
    g0 = 1.0 / (1.0 + tt)
    g1 = tt / (1.0 + tt)
    hit0 = e_iota == e0
    hit1 = e_iota == e1
    onehot = jnp.where(valid & (hit0 | hit1), 1.0, 0.0)
    rank = _dot(tri_ref[...], onehot.astype(BF16)) + base
    r0 = jnp.sum(jnp.where(hit0, rank, 0.0), axis=1, keepdims=True)
    r1 = jnp.sum(jnp.where(hit1, rank, 0.0), axis=1, keepdims=True)
    cols = (jnp.where(valid, e0, -1).astype(F32), jnp.where(valid, e1, -1).astype(F32), r0, r1, g0, g1)
    info = jnp.zeros(logits.shape, F32)
    for k, col in enumerate(cols):
        info = jnp.where(e_iota == k, col, info)
    new_base = base + jnp.sum(onehot, axis=0, keepdims=True)
    return h2, u2, info, new_base, u1


def _pool_frames_kernel(n_fb, h_ref, hist_ref, gm_ref, pw_ref, ps_ref, gf_ref, wr_ref, br_ref, tri_ref,
                        h2_ref, u2_ref, info_ref, cum_ref, tot_ref, state_ref, ext_ref, base_ref):
    i = pl.program_id(0)

    @pl.when(i == 0)
    def _():
        base_ref[...] = jnp.zeros_like(base_ref)

    @pl.when(i >= n_fb)
    def _():
        h2_ref[...] = jnp.zeros_like(h2_ref)
        u2_ref[...] = jnp.zeros_like(u2_ref)
        info_ref[...] = jnp.zeros_like(info_ref)

    @pl.when(i < n_fb)
    def _():
        ext_ref[:HIST_ROWS, :] = _rms(hist_ref[...], gm_ref[...])
        base = base_ref[...]
        cum_ref[0] = base
        h2, u2, info, new_base, u1 = _pool_route(h_ref[...], ext_ref, None, True, base, gm_ref, pw_ref, ps_ref,
                                                 gf_ref, wr_ref, br_ref, tri_ref)
        h2_ref[...] = h2
        u2_ref[...] = u2.astype(BF16)
        info_ref[...] = info
        base_ref[...] = new_base
        tot_ref[...] = new_base
        state_ref[0] = u1[u1.shape[0] - HIST_ROWS:, :]


def _pool_short_kernel(n_sample, n_valid, h2_hbm, u2_hbm, info_hbm, h_ref, hist_ref, base0_ref, gm_ref, pw_ref, ps_ref,
                       gf_ref, wr_ref, br_ref, tri_ref, h2_ref, u2_ref, info_ref, cum_ref, tot_ref, state_ref,
                       ext_ref, base_ref):
    del h2_hbm, u2_hbm, info_hbm
    s_id = pl.program_id(0)

    @pl.when(s_id == 0)
    def _():
        base_ref[...] = base0_ref[...]

    ext_ref[:HIST_ROWS, :] = hist_ref[0]
    base = base_ref[...]
    cum_ref[0] = base
    pos1 = (lax.broadcasted_iota(jnp.int32, (SEQ_S, 1), 0) + 1).astype(F32)
    cnt_rows = jnp.where(s_id < n_sample, float(max(POOL_WINDOWS)), pos1)
    h2, u2, info, new_base, u1 = _pool_route(h_ref[...], ext_ref, cnt_rows, s_id < n_valid, base, gm_ref, pw_ref,
                                             ps_ref, gf_ref, wr_ref, br_ref, tri_ref)
    h2_ref[...] = h2
    u2_ref[...] = u2.astype(BF16)
    info_ref[...] = info
    base_ref[...] = new_base
    tot_ref[...] = new_base
    state_ref[0] = u1


def _expert_kernel(n_blocks, be_ref, lo_ref, hi_ref, ok_ref, u_hbm, pos_ref, gate_ref, wg_ref, wu_ref, wd_ref, ys_ref,
                   buf_ref, acc_ref, gacc_ref, sem):
    del be_ref
    j = pl.program_id(0)
    rb = ys_ref.shape[0]
    n_slots, tc, _ = buf_ref.shape

    def n_chunks(jj):
        return jnp.where(ok_ref[jj] != 0, hi_ref[jj] - lo_ref[jj] + 1, 0)

    def chunk_copy(c, slot):
        return pltpu.make_async_copy(u_hbm.at[pl.ds(pl.multiple_of(c * tc, tc), tc), :], buf_ref.at[slot],
                                     sem.at[slot])

    def request(jj):
        n_req = jnp.minimum(n_chunks(jj), n_slots)
        for k in range(n_slots):
            @pl.when(k < n_req)
            def _():
                chunk_copy(lo_ref[jj] + k, k).start()

    @pl.when(j == 0)
    def _():
        buf_ref[...] = jnp.zeros_like(buf_ref)
        request(0)

    n = n_chunks(j)
    lo = lo_ref[j]

    @pl.when(n > 0)
    def _():
        rows = j * rb + lax.broadcasted_iota(jnp.int32, (rb, tc), 0)
        n_req = jnp.minimum(n, n_slots)

        def select(c, live):
            pos = pos_ref[c]
            gate = gate_ref[c]
            m0 = (rows == pos[0:1, :]) & live
            m1 = (rows == pos[1:2, :]) & live
            sel = jnp.where(m0 | m1, 1.0, 0.0).astype(BF16)
            row_gate = jnp.sum(jnp.where(m0, gate[0:1, :], 0.0) + jnp.where(m1, gate[1:2, :], 0.0),
                               axis=1, keepdims=True)
            return sel, row_gate

        for k in range(n_slots):
            @pl.when(k < n_req)
            def _():
                chunk_copy(lo + k, k).wait()

        last_chunk = pos_ref.shape[0] - 1
        picks = [select(jnp.minimum(lo + k, last_chunk), k < n_req) for k in range(n_slots)]
        acc_ref[...] = _dot(jnp.concatenate([sel for sel, _ in picks], axis=1), buf_ref[...].reshape(n_slots * tc, -1))
        gacc_ref[...] = functools.reduce(lambda a, b: a + b, [g for _, g in picks])

        def overflow(k, carry):
            copy = chunk_copy(lo + k, 0)
            copy.start()
            copy.wait()
            sel, row_gate = select(lo + k, True)
            acc_ref[...] += _dot(sel, buf_ref[0])
            gacc_ref[...] += row_gate
            return carry

        lax.fori_loop(n_slots, jnp.maximum(n, n_slots), overflow, 0)

    @pl.when(j + 1 < n_blocks)
    def _():
        request(jnp.minimum(j + 1, n_blocks - 1))

    @pl.when(n > 0)
    def _():
        xg = acc_ref[...].astype(BF16)
        act = jax.nn.silu(_dot(xg, wg_ref[0])) * _dot(xg, wu_ref[0])
        ys_ref[...] = (_dot(act.astype(BF16), wd_ref[0]) * gacc_ref[...]).astype(BF16)

    @pl.when(n == 0)
    def _():
        ys_ref[...] = jnp.zeros_like(ys_ref)


def _combine_kernel(n_blocks, n_fb, n_ref, ids_ref, h2_ref, pos_ref, g_ref, ys_hbm, yp_ref, yshort_ref, buf_ref,
                    acc_ref, sem):
    i = pl.program_id(0)
    tb = h2_ref.shape[0]
    max_n, rb, _ = buf_ref.shape
    nxt = jnp.minimum(i + 1, n_blocks - 1)
    n = n_ref[i]
    n_next = jnp.where(i + 1 < n_blocks, n_ref[nxt], 0)
    n_groups = (n + COMBINE_GROUP - 1) // COMBINE_GROUP

    def chunk_copy(ii, k):
        ch = ids_ref[ii * max_n + k]
        return pltpu.make_async_copy(ys_hbm.at[pl.ds(pl.multiple_of(ch * rb, rb), rb), :], buf_ref.at[k], sem.at[k])

    @pl.when(i == 0)
    def _():
        buf_ref[...] = jnp.zeros_like(buf_ref)
        for k in range(max_n):
            @pl.when(k < n)
            def _():
                chunk_copy(i, k).start()

    acc_ref[...] = jnp.zeros_like(acc_ref)
    p0 = pos_ref[:, 0:1]
    p1 = pos_ref[:, 1:2]
    lane_rows = lax.broadcasted_iota(jnp.int32, (tb, rb), 1)

    def body(g, carry):
        first = g * COMBINE_GROUP
        for q in range(COMBINE_GROUP):
            @pl.when(first + q < n)
            def _():
                chunk_copy(i, first + q).wait()

        sels = []
        for q in range(COMBINE_GROUP):
            rows = ids_ref[i * max_n + first + q] * rb + lane_rows
            sels.append(jnp.where((rows == p0) | (rows == p1), 1.0, 0.0).astype(BF16))
        rows_g = buf_ref[pl.ds(first, COMBINE_GROUP)].reshape(COMBINE_GROUP * rb, -1)
        acc_ref[...] += _dot(jnp.concatenate(sels, axis=1), rows_g)

        for q in range(COMBINE_GROUP):
            @pl.when(first + q < n_next)
            def _():
                chunk_copy(nxt, first + q).start()

        return carry

    lax.fori_loop(0, n_groups, body, 0)
    for k in range(max_n):
        @pl.when((k >= n_groups * COMBINE_GROUP) & (k < n_next))
        def _():
            chunk_copy(nxt, k).start()

    y = _rms(h2_ref[...] + acc_ref[...], g_ref[...])

    @pl.when(i < n_fb)
    def _():
        yp_ref[...] = y

    @pl.when(i >= n_fb)
    def _():
        yshort_ref[...] = y


def _rope_table(pos):
    half = QK_ROPE // 2
    inv = ROPE_THETA ** (-jnp.arange(half, dtype=F32) / half)
    ang = pos.astype(F32)[:, None] * inv[None, :]
    cos = jnp.cos(ang)
    sin = jnp.sin(ang)
    zero = jnp.zeros((pos.shape[0], LANE - QK_ROPE), F32)
    return jnp.concatenate([cos, cos, zero, -sin, sin, zero], axis=1)


def _half_swap(w):
    half = QK_ROPE // 2
    return jnp.concatenate([w[..., half:], w[..., :half]], axis=-1)


def _pad_last(w, width):
    return jnp.pad(w, [(0, 0)] * (w.ndim - 1) + [(0, width - w.shape[-1])])


def kernel(x_prompt, x_sample, cache_kv_latent, cache_k_rope, state_pool, meta_tokens, norm_mix, norm_ffn, norm_final,
           mla_w_dq, mla_g_q, mla_w_uq, mla_w_dkv, mla_g_kv, mla_w_uk, mla_w_uv, mla_w_o, pool_w, pool_scale,
           ffn_w_gate, ffn_w_up, ffn_w_down, moe_w_router, moe_b_router, moe_w_gate, moe_w_up, moe_w_down):
    nb, seq, d = x_prompt.shape
    db, dseq, _ = x_sample.shape
    n_cache = cache_kv_latent.shape[2]
    q_rank = mla_w_dq.shape[2]
    kv_rank = mla_g_kv.shape[1]
    v_dim = mla_w_uv.shape[3]
    d_ff = ffn_w_gate.shape[2]
    n_e = moe_w_router.shape[2]
    d_e = moe_w_gate.shape[3]
    tb, rb, tq = TOKEN_BLOCK, ROW_BLOCK, ATTN_BLOCK
    assert norm_mix.shape[0] == 2 and cache_kv_latent.shape[0] == 1 and state_pool.shape[0] == 1
    assert dseq == SEQ_S and N_META == SEQ_S and meta_tokens.shape[0] == N_META
    assert kv_rank == 2 * LANE and QK_NOPE == LANE and QK_ROPE <= LANE and HIST_ROWS >= POOL_HIST
    assert seq % tb == 0 and tb == tq and tq % CHUNK == 0 and tb % SEQ_S == 0 and d % LANE == 0
    assert (n_cache - N_META) % CHUNK == 0 and dseq <= CHUNK

    nf = nb * seq
    n_valid_seq = db + 1
    ns = -(-(n_valid_seq * SEQ_S) // tb) * tb
    n_seq = ns // SEQ_S
    nt = nf + ns
    n_fb, n_sb, n_tb = nf // tb, ns // tb, nt // tb
    sb = seq // tb
    meta_row = nf + db * SEQ_S
    scale = float((QK_NOPE + QK_ROPE) ** -0.5)
    hq = 3 * LANE

    xp = x_prompt.reshape(nf, d)
    xs = jnp.concatenate([x_sample.reshape(db * SEQ_S, d), meta_tokens.astype(x_prompt.dtype),
                          jnp.zeros((ns - n_valid_seq * SEQ_S, d), x_prompt.dtype)], axis=0)

    t_s = jnp.arange(SEQ_S)
    pos_short = jnp.concatenate([jnp.tile(n_cache + t_s, db), jnp.tile(t_s, n_seq - db)])
    tab_frames = _rope_table(N_META + jnp.arange(seq))
    tab = jnp.concatenate([tab_frames, _rope_table(pos_short)], axis=0)

    wuq = mla_w_uq[0].reshape(q_rank, N_HEADS, QK_NOPE + QK_ROPE)
    wuq_pe = wuq[:, :, QK_NOPE:]
    wuq2 = jnp.concatenate([wuq[:, :, :QK_NOPE].reshape(q_rank, -1),
                            _pad_last(wuq_pe, LANE).reshape(q_rank, -1),
                            _pad_last(_half_swap(wuq_pe), LANE).reshape(q_rank, -1)], axis=1).astype(BF16)
    wuq_t = jnp.concatenate([wuq[:, :, :QK_NOPE].reshape(q_rank, -1), wuq_pe.reshape(q_rank, -1),
                             _half_swap(wuq_pe).reshape(q_rank, -1)], axis=1).T.astype(BF16)
    wdkv_r =mla_w_dkv[0][:, kv_rank:]
    wdkv2 = jnp.concatenate([mla_w_dkv[0][:, :kv_rank], _pad_last(wdkv_r, LANE),
                             _pad_last(_half_swap(wdkv_r), LANE)], axis=1).astype(BF16)
    wuk_t = jnp.transpose(mla_w_uk[0], (1, 2, 0)).astype(BF16)
    wuk_flat = mla_w_uk[0].reshape(kv_rank, N_HEADS * QK_NOPE).astype(BF16)
    wuv = jnp.transpose(mla_w_uv[0], (1, 0, 2)).astype(BF16)
    wuv_t = mla_w_uv[0].reshape(kv_rank, N_HEADS * v_dim).T.astype(BF16)
    row = lambda v: v.reshape(1, -1)
    hk = 2 * LANE

    tok_p = pl.BlockSpec((tb, d), lambda i: (jnp.minimum(i, n_fb - 1), 0))
    tok_s = pl.BlockSpec((tb, d), lambda i: (jnp.maximum(i - n_fb, 0), 0))
    short_tok = lambda i: (jnp.maximum(i - n_fb, 0), 0)
    q_short, qt_frames, kvb_short, kh_all, vt_all, c_all, r_all = pl.pallas_call(
        functools.partial(_qkv_kernel, n_fb, scale * LOG2_E),
        grid=(n_tb,),
        in_specs=[tok_p, tok_s,
                  pl.BlockSpec((tb, 2 * LANE), lambda i: (jnp.where(i < n_fb, i % sb, sb + i - n_fb), 0)),
                  pl.BlockSpec((2 * LANE, tb), lambda i: (0, i % sb)),
                  _const_spec((1, d)), _const_spec((d, q_rank)), _const_spec((1, q_rank)),
                  _const_spec(wuq2.shape), _const_spec(wuq_t.shape), _const_spec(wuk_t.shape),
                  _const_spec(wdkv2.shape), _const_spec((1, kv_rank)), _const_spec(wuk_flat.shape),
                  _const_spec(wuv_t.shape)],
        out_specs=[pl.BlockSpec((tb, N_HEADS * hq), short_tok),
                   pl.BlockSpec((1, N_HEADS, hk, tb), lambda i: (jnp.minimum(i, n_fb - 1), 0, 0, 0)),
                   pl.BlockSpec((tb, hq), short_tok),
                   pl.BlockSpec((tb, N_HEADS * hk), lambda i: (i, 0)),
                   pl.BlockSpec((1, N_HEADS * v_dim, tb), lambda i: (i, 0, 0)),
                   pl.BlockSpec((tb, kv_rank), lambda i: (i, 0)),
                   pl.BlockSpec((tb, QK_ROPE), lambda i: (i, 0))],
        out_shape=[jax.ShapeDtypeStruct((ns, N_HEADS * hq), BF16),
                   jax.ShapeDtypeStruct((n_fb, N_HEADS, hk, tb), BF16),
                   jax.ShapeDtypeStruct((ns, hq), BF16),
                   jax.ShapeDtypeStruct((nt, N_HEADS * hk), BF16),
                   jax.ShapeDtypeStruct((n_tb, N_HEADS * v_dim, tb), BF16),
                   jax.ShapeDtypeStruct((nt, kv_rank), F32), jax.ShapeDtypeStruct((nt, QK_ROPE), F32)],
        compiler_params=_params(("arbitrary",)),
        name="qkv",
    )(xp, xs, tab, tab_frames.T, row(norm_mix[0]), mla_w_dq[0].astype(BF16), row(mla_g_q[0]), wuq2, wuq_t, wuk_t,
      wdkv2, row(mla_g_kv[0]), wuk_flat, wuv_t)

    qb = seq // tq
    once = pl.Buffered(1)
    o_frames = pl.pallas_call(
        functools.partial(_attn_frames_kernel, qb, meta_row % tb),
        grid=(n_fb,),
        in_specs=[pl.BlockSpec((1, N_HEADS, hk, tq), lambda i: (i, 0, 0, 0)),
                  pl.BlockSpec((seq, N_HEADS * hk), lambda i: (i // qb, 0), pipeline_mode=once),
                  pl.BlockSpec((qb, N_HEADS * v_dim, tq), lambda i: (i // qb, 0, 0), pipeline_mode=once),
                  pl.BlockSpec((N_META, N_HEADS * hk), lambda i: (meta_row // N_META, 0), pipeline_mode=once),
                  pl.BlockSpec((1, N_HEADS * v_dim, tb), lambda i: (meta_row // tb, 0, 0), pipeline_mode=once)],
        out_specs=pl.BlockSpec((tq, N_HEADS * v_dim), lambda i: (i, 0)),
        out_shape=jax.ShapeDtypeStruct((nf, N_HEADS * v_dim), BF16),
        scratch_shapes=[pltpu.VMEM((N_HEADS, 1, tq), F32), pltpu.VMEM((N_HEADS, 1, tq), F32),
                        pltpu.VMEM((N_HEADS, v_dim, tq), F32)],
        compiler_params=_params(("parallel",)),
        name="attn_frames",
    )(qt_frames, kh_all, vt_all, kh_all, vt_all)

    o_short = pl.pallas_call(
        functools.partial(_attn_short_kernel, db),
        grid=(n_seq,),
        in_specs=[pl.BlockSpec((SEQ_S, N_HEADS * hq), lambda s: (s, 0)),
                  pl.BlockSpec((SEQ_S, hq), lambda s: (s, 0)),
                  pl.BlockSpec((1, 1, n_cache, kv_rank), lambda s: (0, jnp.minimum(s, db - 1), 0, 0)),
                  pl.BlockSpec((1, 1, n_cache, QK_ROPE), lambda s: (0, jnp.minimum(s, db - 1), 0, 0)),
                  _const_spec(wuv.shape)],
        out_specs=pl.BlockSpec((SEQ_S, N_HEADS * v_dim), lambda s: (s, 0)),
        out_shape=jax.ShapeDtypeStruct((ns, N_HEADS * v_dim), BF16),
        compiler_params=_params(("parallel",)),
        name="attn_short",
    )(q_short, kvb_short, cache_kv_latent, cache_k_rope, wuv)

    h1 = pl.pallas_call(
        functools.partial(_proj_ffn_kernel, n_fb),
        grid=(n_tb,),
        in_specs=[tok_p, tok_s,
                  pl.BlockSpec((tb, N_HEADS * v_dim), lambda i: (jnp.minimum(i, n_fb - 1), 0)),
                  pl.BlockSpec((tb, N_HEADS * v_dim), short_tok),
                  _const_spec((N_HEADS * v_dim, d)), _const_spec((1, d)),
                  _const_spec((d, d_ff)), _const_spec((d, d_ff)), _const_spec((d_ff, d))],
        out_specs=pl.BlockSpec((tb, d), lambda i: (i, 0)),
        out_shape=jax.ShapeDtypeStruct((nt, d), F32),
        compiler_params=_params(("parallel",)),
        name="proj_ffn",
    )(xp, xs, o_frames, o_short, mla_w_o[0].astype(BF16), row(norm_ffn[0]),
      ffn_w_gate[0].astype(BF16), ffn_w_up[0].astype(BF16), ffn_w_down[0].astype(BF16))

    route_w = [_const_spec((1, d)), _const_spec(pool_w.shape[1:]), _const_spec((1, d)), _const_spec((1, d)),
               _const_spec((d, 2 * n_e)), _const_spec((1, n_e))]
    wr_hi = moe_w_router[0].astype(BF16)
    wr_lo = (moe_w_router[0].astype(F32) - wr_hi.astype(F32)).astype(BF16)
    route_args = (row(norm_mix[1]), pool_w[0].astype(BF16), row(pool_scale[0]), row(norm_ffn[1]),
                  jnp.concatenate([wr_hi, wr_lo], axis=1), row(moe_b_router[0]))
    tri = lambda n: (jnp.arange(n)[:, None] > jnp.arange(n)[None, :]).astype(BF16)
    hist_blk = lambda i: jnp.where(i % sb == 0, meta_row // HIST_ROWS, i * (tb // HIST_ROWS) - 1)
    h2_f, u2_f, info_f, cum_f, tot_f, state_f = pl.pallas_call(
        functools.partial(_pool_frames_kernel, n_fb),
        grid=(n_tb,),
        in_specs=[pl.BlockSpec((tb, d), lambda i: (i, 0)),
                  pl.BlockSpec((HIST_ROWS, d), lambda i: (hist_blk(i), 0))] + route_w + [_const_spec((tb, tb))],
        out_specs=[pl.BlockSpec((tb, d), lambda i: (i, 0)), pl.BlockSpec((tb, d), lambda i: (i, 0)),
                   pl.BlockSpec((tb, n_e), lambda i: (i, 0)),
                   pl.BlockSpec((1, 1, n_e), lambda i: (jnp.minimum(i, n_fb - 1), 0, 0)),
                   pl.BlockSpec((1, n_e), lambda i: (0, 0)),
                   pl.BlockSpec((1, HIST_ROWS, d), lambda i: (jnp.minimum(i // sb, nb - 1), 0, 0))],
        out_shape=[jax.ShapeDtypeStruct((nt, d), F32), jax.ShapeDtypeStruct((nt, d), BF16),
                   jax.ShapeDtypeStruct((nt, n_e), F32), jax.ShapeDtypeStruct((n_fb, 1, n_e), F32),
                   jax.ShapeDtypeStruct((1, n_e), F32), jax.ShapeDtypeStruct((nb, HIST_ROWS, d), F32)],
        scratch_shapes=[pltpu.VMEM((HIST_ROWS + tb, d), F32), pltpu.VMEM((1, n_e), F32)],
        compiler_params=_params(("arbitrary",)),
        name="pool_route_frames",
    )(h1, h1, *route_args, tri(tb))

    hist_s = jnp.concatenate([
        jnp.pad(state_pool[0].astype(F32), [(0, 0), (HIST_ROWS - POOL_HIST, 0), (0, 0)]),
        jnp.zeros((n_seq - db, HIST_ROWS, d), F32)], axis=0)
    any_spec = pl.BlockSpec(memory_space=pl.ANY)
    short_blk = lambda s: (nf // SEQ_S + s, 0)
    h2, u2, info, cum_s, tot_s, state_s = pl.pallas_call(
        functools.partial(_pool_short_kernel, db, n_valid_seq),
        grid=(n_seq,),
        in_specs=[any_spec, any_spec, any_spec,
                  pl.BlockSpec((SEQ_S, d), short_blk),
                  pl.BlockSpec((1, HIST_ROWS, d), lambda s: (s, 0, 0)),
                  _const_spec((1, n_e))] + route_w + [_const_spec((SEQ_S, SEQ_S))],
        out_specs=[pl.BlockSpec((SEQ_S, d), short_blk), pl.BlockSpec((SEQ_S, d), short_blk),
                   pl.BlockSpec((SEQ_S, n_e), short_blk), pl.BlockSpec((1, 1, n_e), lambda s: (s, 0, 0)),
                   pl.BlockSpec((1, n_e), lambda s: (0, 0)),
                   pl.BlockSpec((1, SEQ_S, d), lambda s: (s, 0, 0))],
        out_shape=[jax.ShapeDtypeStruct((nt, d), F32), jax.ShapeDtypeStruct((nt, d), BF16),
                   jax.ShapeDtypeStruct((nt, n_e), F32), jax.ShapeDtypeStruct((n_seq, 1, n_e), F32),
                   jax.ShapeDtypeStruct((1, n_e), F32), jax.ShapeDtypeStruct((n_seq, SEQ_S, d), F32)],
        scratch_shapes=[pltpu.VMEM((HIST_ROWS + SEQ_S, d), F32), pltpu.VMEM((1, n_e), F32)],
        input_output_aliases={0: 0, 1: 1, 2: 2},
        compiler_params=_params(("arbitrary",)),
        name="pool_route_short",
    )(h2_f, u2_f, info_f, h1, hist_s, tot_f, *route_args, tri(SEQ_S))

    counts = tot_s[0].astype(jnp.int32)
    cum = jnp.concatenate([cum_f[:, 0], cum_s[::tb // SEQ_S, 0], tot_s], axis=0).astype(jnp.int32)
    padded = (counts + rb - 1) // rb * rb
    pend = jnp.cumsum(padded)
    pstart = pend - padded
    n_rows_max = -(-(TOP_K * (nf + n_valid_seq * SEQ_S)) // rb) * rb + n_e * rb
    n_rb = n_rows_max // rb
    e_tok = info[:, 0:TOP_K].astype(jnp.int32)
    pos = jnp.where(e_tok >= 0, pstart[jnp.maximum(e_tok, 0)] + info[:, 2:2 + TOP_K].astype(jnp.int32), -1)
    gates = info[:, 4:4 + TOP_K]
    pos_l = pos.reshape(n_tb, tb, TOP_K).transpose(0, 2, 1)
    gate_l = gates.reshape(n_tb, tb, TOP_K).transpose(0, 2, 1)
    pos_c = jnp.pad(pos, [(0, 0), (0, n_e - TOP_K)], constant_values=-1)

    blk_row = jnp.arange(n_rb, dtype=jnp.int32) * rb
    blk_e = jnp.minimum(jnp.sum(pend[None, :] <= blk_row[:, None], axis=1), n_e - 1).astype(jnp.int32)
    blk_ok = (blk_row < pend[-1]).astype(jnp.int32)
    r_lo = blk_row - pstart[blk_e]
    r_hi = jnp.minimum(r_lo + rb, counts[blk_e])
    cum_b = cum.T[blk_e]
    first = jnp.sum(cum_b[:, 1:] <= r_lo[:, None], axis=1)
    last = jnp.sum(cum_b[:, :-1] < r_hi[:, None], axis=1) - 1
    blk_lo = jnp.clip(first, 0, n_tb - 1).astype(jnp.int32)
    blk_hi = jnp.clip(last, blk_lo, n_tb - 1).astype(jnp.int32)

    ys = pl.pallas_call(
        functools.partial(_expert_kernel, n_rb),
        grid_spec=pltpu.PrefetchScalarGridSpec(
            num_scalar_prefetch=4,
            grid=(n_rb,),
            in_specs=[pl.BlockSpec(memory_space=pl.ANY),
                      _const_spec((n_tb, TOP_K, tb)), _const_spec((n_tb, TOP_K, tb)),
                      pl.BlockSpec((1, d, d_e), lambda j, be, lo, hi, ok: (be[j], 0, 0)),
                      pl.BlockSpec((1, d, d_e), lambda j, be, lo, hi, ok: (be[j], 0, 0)),
                      pl.BlockSpec((1, d_e, d), lambda j, be, lo, hi, ok: (be[j], 0, 0))],
            out_specs=pl.BlockSpec((rb, d), lambda j, be, lo, hi, ok: (j, 0)),
            scratch_shapes=[pltpu.VMEM((GATHER_SLOTS, tb, d), BF16), pltpu.VMEM((rb, d), F32),
                            pltpu.VMEM((rb, 1), F32), pltpu.SemaphoreType.DMA((GATHER_SLOTS,))]),
        out_shape=jax.ShapeDtypeStruct((n_rows_max, d), BF16),
        compiler_params=_params(("arbitrary",)),
        name="experts",
    )(blk_e, blk_lo, blk_hi, blk_ok, u2, pos_l, gate_l, moe_w_gate[0].astype(BF16), moe_w_up[0].astype(BF16),
      moe_w_down[0].astype(BF16))

    w_lo = pstart[None, :] + cum[:-1]
    w_hi = pstart[None, :] + cum[1:]
    has = w_hi > w_lo
    c_lo = w_lo // rb
    c_hi = jnp.where(has, (w_hi - 1) // rb, c_lo)
    cand = jnp.stack([c_lo, c_hi], axis=2).reshape(n_tb, 2 * n_e)
    keep = jnp.stack([has, has & (c_hi > c_lo)], axis=2).reshape(n_tb, 2 * n_e)
    order = jnp.argsort(~keep, axis=1, stable=True)
    ids = jnp.where(jnp.take_along_axis(keep, order, axis=1), jnp.take_along_axis(cand, order, axis=1), n_rb)
    assert (2 * n_e) % COMBINE_GROUP == 0
    n_ids = jnp.sum(keep, axis=1).astype(jnp.int32)
    y_frames, y_short = pl.pallas_call(
        functools.partial(_combine_kernel, n_tb, n_fb),
        grid_spec=pltpu.PrefetchScalarGridSpec(
            num_scalar_prefetch=2,
            grid=(n_tb,),
            in_specs=[pl.BlockSpec((tb, d), lambda i, n, ids: (i, 0)),
                      pl.BlockSpec((tb, n_e), lambda i, n, ids: (i, 0)),
                      pl.BlockSpec((1, d), lambda i, n, ids: (0, 0)),
                      pl.BlockSpec(memory_space=pl.ANY)],
            out_specs=[pl.BlockSpec((tb, d), lambda i, n, ids: (jnp.minimum(i, n_fb - 1), 0)),
                       pl.BlockSpec((tb, d), lambda i, n, ids: (jnp.maximum(i - n_fb, 0), 0))],
            scratch_shapes=[pltpu.VMEM((2 * n_e, rb, d), BF16), pltpu.VMEM((tb, d), F32),
                            pltpu.SemaphoreType.DMA((2 * n_e,))]),
        out_shape=[jax.ShapeDtypeStruct((nf, d), F32), jax.ShapeDtypeStruct((ns, d), F32)],
        compiler_params=_params(("arbitrary",)),
        name="combine",
    )(n_ids, ids.reshape(-1).astype(jnp.int32), h2, pos_c, row(norm_final), ys)

    y_prompt = y_frames.reshape(nb, seq, d)
    y_sample = y_short[:db * SEQ_S].reshape(db, SEQ_S, d)

    def with_meta(a, width):
        meta = jnp.broadcast_to(a[meta_row:meta_row + N_META][None], (nb, N_META, width))
        return jnp.concatenate([meta, a[:nf].reshape(nb, seq, width)], axis=1)[None]

    c_p = with_meta(c_all, kv_rank)
    r_p = with_meta(r_all, QK_ROPE)
    c_s = c_all[nf:nf + db * SEQ_S].reshape(1, db, SEQ_S, kv_rank)
    r_s = r_all[nf:nf + db * SEQ_S].reshape(1, db, SEQ_S, QK_ROPE)
    s_p = state_f[:, HIST_ROWS - POOL_HIST:][None]
    s_s = state_s[:db, SEQ_S - POOL_HIST:][None]
    return (y_prompt, y_sample, c_p, r_p, s_p, c_s, r_s, s_s)
```

```python
import functools

import jax
import jax.numpy as jnp
from jax import lax
from jax.experimental import pallas as pl
from jax.experimental.pallas import tpu as pltpu

CHUNK = 64
N_META = 16
N_HEADS = 8
QK_NOPE = 128
QK_ROPE = 64
ROPE_THETA = 10000.0
POOL_WINDOWS = (2, 4, 8, 16)
POOL_HIST = max(POOL_WINDOWS) - 1
TOP_K = 2
RMS_EPS = 1e-6

LANE = 128
BF16_SUBLANES = 16
SEQ_S = 16
HIST_ROWS = 16
TOKEN_BLOCK = 256
ROW_BLOCK = 256
ATTN_BLOCK = 256
CACHE_BLOCK = 512
SCORES_AHEAD = 6
COMBINE_ROWS = 64
GATHER_SLOTS = 6
VMEM_LIMIT = 56 * 1024 * 1024

F32 = jnp.float32
BF16 = jnp.bfloat16
NEG_INF = float("-inf")
LOG2_E = 1.4426950408889634


def _dot(a, b):
    return jnp.dot(a, b, preferred_element_type=F32)


def _dot_nt(a, b):
    return lax.dot_general(a, b, (((1,), (1,)), ((), ())), preferred_element_type=F32)


def _rms(x, g):
    return x * lax.rsqrt(jnp.mean(x * x, axis=-1, keepdims=True) + RMS_EPS) * g


def _const_spec(shape):
    nd = len(shape)
    return pl.BlockSpec(shape, lambda *_: (0,) * nd, pipeline_mode=pl.Buffered(1))


def _params(sem):
    return pltpu.CompilerParams(dimension_semantics=sem, vmem_limit_bytes=VMEM_LIMIT)


def _qkv_kernel(n_fb, scale, xp_ref, xs_ref, tab_ref, tabt_ref, g_ref, wdq_ref, gq_ref, wuq_ref, wuqt_ref, wukt_ref,
                wdkv_ref, gkv_ref, wukf_ref, wuvt_ref, q_ref, qt_ref, kvb_ref, kh_ref, vt_ref, c_ref, r_ref):
    i = pl.program_id(0)
    x = jnp.where(i < n_fb, xp_ref[...], xs_ref[...])
    u = _rms(x, g_ref[...]).astype(BF16)
    cq = _rms(_dot(u, wdq_ref[...]), gq_ref[...]).astype(BF16)
    cos = tab_ref[:, :LANE]
    sin = tab_ref[:, LANE:]
    hq = QK_NOPE + 2 * LANE

    kv = _dot(u, wdkv_ref[...])
    c = _rms(kv[:, :2 * LANE], gkv_ref[...])
    r = kv[:, 2 * LANE:3 * LANE] * cos + kv[:, 3 * LANE:] * sin
    c_ref[...] = c
    r_ref[...] = r[:, :QK_ROPE]
    c_bf = c.astype(BF16)
    r_bf = r.astype(BF16)

    kn = _dot(c_bf, wukf_ref[...])
    for h in range(N_HEADS):
        kh_ref[:, 2 * h * LANE:(2 * h + 1) * LANE] = kn[:, h * LANE:(h + 1) * LANE].astype(BF16)
        kh_ref[:, (2 * h + 1) * LANE:(2 * h + 2) * LANE] = r_bf
    vt = _dot_nt(wuvt_ref[...], c_bf).astype(BF16)
    hv = vt.shape[0] // N_HEADS
    hv_ext = vt_ref.shape[1] // N_HEADS
    for h in range(N_HEADS):
        vt_ref[0, h * hv_ext:h * hv_ext + hv, :] = vt[h * hv:(h + 1) * hv, :]
        vt_ref[0, h * hv_ext + hv:(h + 1) * hv_ext, :] = jnp.ones((hv_ext - hv, vt.shape[1]), BF16)

    @pl.when(i < n_fb)
    def _():
        qat = _dot_nt(wuqt_ref[...], cq)
        cos_t = tabt_ref[:QK_ROPE, :]
        sin_t = tabt_ref[LANE:LANE + QK_ROPE, :]
        pe0 = N_HEADS * QK_NOPE
        sw0 = pe0 + N_HEADS * QK_ROPE
        for h in range(N_HEADS):
            qt_ref[0, h, :QK_NOPE, :] = (qat[h * QK_NOPE:(h + 1) * QK_NOPE, :] * scale).astype(BF16)
            a = qat[pe0 + h * QK_ROPE:pe0 + (h + 1) * QK_ROPE, :]
            b = qat[sw0 + h * QK_ROPE:sw0 + (h + 1) * QK_ROPE, :]
            qt_ref[0, h, QK_NOPE:QK_NOPE + QK_ROPE, :] = ((a * cos_t + b * sin_t) * scale).astype(BF16)
            qt_ref[0, h, QK_NOPE + QK_ROPE:, :] = jnp.zeros((LANE - QK_ROPE, qt_ref.shape[3]), BF16)

    @pl.when(i >= n_fb)
    def _():
        qa = _dot(cq, wuq_ref[...])
        for h in range(N_HEADS):
            qn = qa[:, h * LANE:(h + 1) * LANE].astype(BF16)
            q_ref[:, h * hq:h * hq + 2 * LANE] = (_dot(qn, wukt_ref[h]) * scale).astype(BF16)
            a = qa[:, (N_HEADS + h) * LANE:(N_HEADS + h + 1) * LANE]
            b = qa[:, (2 * N_HEADS + h) * LANE:(2 * N_HEADS + h + 1) * LANE]
            q_ref[:, h * hq + 2 * LANE:(h + 1) * hq] = ((a * cos + b * sin) * scale).astype(BF16)
        kvb_ref[:, :2 * LANE] = c_bf
        kvb_ref[:, 2 * LANE:] = r_bf


def _attn_frames_kernel(qb, meta_col, qt_ref, kh_ref, vt_ref, khm_ref, vtm_ref, o_ref, m_ref, acc_ref):
    j = pl.program_id(0) % qb
    tq = o_ref.shape[0]
    hk = kh_ref.shape[1] // N_HEADS
    hv = o_ref.shape[1] // N_HEADS
    hx = vt_ref.shape[1] // N_HEADS

    def scores(h, rows):
        return _dot(kh_ref[rows, h * hk:(h + 1) * hk], qt_ref[0, h])

    m_ref[...] = jnp.full(m_ref.shape, NEG_INF, F32)
    acc_ref[...] = jnp.zeros_like(acc_ref)

    def body(kb, carry):
        rows = pl.ds(pl.multiple_of(kb * tq, tq), tq)
        ahead = [scores(h, rows) for h in range(SCORES_AHEAD)]
        for h in range(N_HEADS):
            s = ahead.pop(0)
            if h + SCORES_AHEAD < N_HEADS:
                ahead.append(scores(h + SCORES_AHEAD, rows))
            m_old = m_ref[h]
            m_new = jnp.maximum(m_old, jnp.max(s, axis=0, keepdims=True))
            alpha = jnp.exp2(m_old - m_new)
            p = jnp.exp2(s - m_new)
            m_ref[h] = m_new
            acc_ref[h] = alpha * acc_ref[h] + _dot(vt_ref[kb, h * hx:(h + 1) * hx, :], p.astype(BF16))
        return carry

    lax.fori_loop(0, j, body, 0)

    rows = pl.ds(pl.multiple_of(j * tq, tq), tq)
    visible = (lax.broadcasted_iota(jnp.int32, (tq, tq), 0) // CHUNK
               <= lax.broadcasted_iota(jnp.int32, (tq, tq), 1) // CHUNK)

    def last_scores(h):
        return scores(h, rows), _dot(khm_ref[:, h * hk:(h + 1) * hk], qt_ref[0, h])

    ahead = [last_scores(h) for h in range(SCORES_AHEAD)]
    for h in range(N_HEADS):
        s, s_meta = ahead.pop(0)
        s = jnp.where(visible, s, NEG_INF)
        if h + SCORES_AHEAD < N_HEADS:
            ahead.append(last_scores(h + SCORES_AHEAD))
        m_old = m_ref[h]
        m_new = jnp.maximum(m_old, jnp.maximum(jnp.max(s, axis=0, keepdims=True),
                                               jnp.max(s_meta, axis=0, keepdims=True)))
        alpha = jnp.exp2(m_old - m_new)
        p = jnp.exp2(s - m_new)
        p_meta = jnp.exp2(s_meta - m_new)
        acc = (alpha * acc_ref[h] + _dot(vt_ref[j, h * hx:(h + 1) * hx, :], p.astype(BF16))
               + _dot(vtm_ref[0, h * hx:(h + 1) * hx, meta_col:meta_col + N_META], p_meta.astype(BF16)))
        o_ref[:, h * hv:(h + 1) * hv] = (acc[:hv] * (1.0 / acc[hv:hv + 1])).T.astype(BF16)


def _attn_short_kernel(n_cached_seq, q_ref, kvn_ref, cc_ref, cr_ref, wuv_ref, o_ref):
    s_id = pl.program_id(0)
    hq = q_ref.shape[1] // N_HEADS
    n_cache = cc_ref.shape[2]
    qs = jnp.concatenate([q_ref[:, h * hq:(h + 1) * hq] for h in range(N_HEADS)], axis=0)
    q_lat = qs[:, :2 * LANE]
    q_pe = qs[:, 2 * LANE:2 * LANE + QK_ROPE]
    kvn = kvn_ref[...]
    s = _dot_nt(qs, kvn)
    m = jnp.max(s, axis=1, keepdims=True)
    p = jnp.exp2(s - m)
    l = jnp.sum(p, axis=1, keepdims=True)
    acc = _dot(p.astype(BF16), kvn[:, :2 * LANE])
    has_cache = s_id < n_cached_seq
    start = 0
    while start < n_cache:
        size = min(CACHE_BLOCK, n_cache - start)
        ck = cc_ref[0, 0, start:start + size, :].astype(BF16)
        rk = cr_ref[0, 0, start:start + size, :].astype(BF16)
        s = _dot_nt(q_lat, ck) + _dot_nt(q_pe, rk)
        s = jnp.where(has_cache, s, NEG_INF)
        m_new = jnp.maximum(m, jnp.max(s, axis=1, keepdims=True))
        alpha = jnp.exp2(m - m_new)
        p = jnp.exp2(s - m_new)
        l = alpha * l + jnp.sum(p, axis=1, keepdims=True)
        acc = alpha * acc + _dot(p.astype(BF16), ck)
        m = m_new
        start += size
    o = (acc / l).astype(BF16)
    hv = wuv_ref.shape[2]
    for h in range(N_HEADS):
        o_ref[:, h * hv:(h + 1) * hv] = _dot(o[h * SEQ_S:(h + 1) * SEQ_S, :], wuv_ref[h]).astype(BF16)


def _proj_ffn_kernel(n_fb, xp_ref, xs_ref, op_ref, os_ref, wo_ref, g_ref, wg_ref, wu_ref, wd_ref, h_ref):
    i = pl.program_id(0)
    x = jnp.where(i < n_fb, xp_ref[...], xs_ref[...])
    o = jnp.where(i < n_fb, op_ref[...], os_ref[...])
    h1 = x + _dot(o, wo_ref[...])
    u = _rms(h1, g_ref[...]).astype(BF16)
    act = jax.nn.silu(_dot(u, wg_ref[...])) * _dot(u, wu_ref[...])
    h_ref[...] = h1 + _dot(act.astype(BF16), wd_ref[...])


def _pool_route(h1, ext_ref, cnt_rows, valid, base, gm_ref, pw_ref, ps_ref, gf_ref, wr_ref, br_ref, tri_ref):
    t = h1.shape[0]
    grp = h1.shape[1] // len(POOL_WINDOWS)
    u1 = _rms(h1, gm_ref[...])
    ext_ref[HIST_ROWS:HIST_ROWS + t, :] = u1
    ys = []
    for g, w in enumerate(POOL_WINDOWS):
        cols = slice(g * grp, (g + 1) * grp)
        acc = u1[:, cols]
        for k in range(1, w):
            acc = acc + ext_ref[HIST_ROWS - k:HIST_ROWS - k + t, cols]
        cnt = float(w) if cnt_rows is None else jnp.minimum(float(w), cnt_rows)
        mean = acc / cnt - u1[:, cols]
        ys.append(_dot(mean.astype(BF16), pw_ref[g]))
    h2 = h1 + jnp.concatenate(ys, axis=1) * ps_ref[...]
    u2 = _rms(h2, gf_ref[...])

    n_e = wr_ref.shape[1] // 2
    u2_hi = u2.astype(BF16)
    u2_lo = (u2 - u2_hi.astype(F32)).astype(BF16)
    parts = _dot(u2_hi, wr_ref[...]) + _dot(u2_lo, wr_ref[...])
    logits = parts[:, :n_e] + parts[:, n_e:] + br_ref[...]
    e_iota = lax.broadcasted_iota(jnp.int32, logits.shape, 1).astype(F32)
    v0 = jnp.max(logits, axis=1, keepdims=True)
    e0 = jnp.min(jnp.where(logits == v0, e_iota, float(n_e)), axis=1, keepdims=True)
    rest = jnp.where(e_iota == e0, NEG_INF, logits)
    v1 = jnp.max(rest, axis=1, keepdims=True)
    e1 = jnp.min(jnp.where(rest == v1, e_iota, float(n_e)), axis=1, keepdims=True)
    tt = jnp.exp(v1 - v0)
    g0 = 1.0 / (1.0 + tt)
    g1 = tt / (1.0 + tt)
    hit0 = e_iota == e0
    hit1 = e_iota == e1
    onehot = jnp.where(valid & (hit0 | hit1), 1.0, 0.0)
    rank = _dot(tri_ref[...], onehot.astype(BF16)) + base
    r0 = jnp.sum(jnp.where(hit0, rank, 0.0), axis=1, keepdims=True)
    r1 = jnp.sum(jnp.where(hit1, rank, 0.0), axis=1, keepdims=True)
    cols = (jnp.where(valid, e0, -1).astype(F32), jnp.where(valid, e1, -1).astype(F32), r0, r1, g0, g1)
    info = jnp.zeros(logits.shape, F32)
    for k, col in enumerate(cols):
        info = jnp.where(e_iota == k, col, info)
    new_base = base + jnp.sum(onehot, axis=0, keepdims=True)
    return h2, u2, info, new_base, u1


def _pool_frames_kernel(n_fb, h_ref, hist_ref, gm_ref, pw_ref, ps_ref, gf_ref, wr_ref, br_ref, tri_ref,
                        h2_ref, u2_ref, info_ref, cum_ref, tot_ref, state_ref, ext_ref, base_ref):
    i = pl.program_id(0)

    @pl.when(i == 0)
    def _():
        base_ref[...] = jnp.zeros_like(base_ref)

    @pl.when(i >= n_fb)
    def _():
        h2_ref[...] = jnp.zeros_like(h2_ref)
        u2_ref[...] = jnp.zeros_like(u2_ref)
        info_ref[...] = jnp.zeros_like(info_ref)

    @pl.when(i < n_fb)
    def _():
        ext_ref[:HIST_ROWS, :] = _rms(hist_ref[...], gm_ref[...])
        base = base_ref[...]
        cum_ref[0] = base
        h2, u2, info, new_base, u1 = _pool_route(h_ref[...], ext_ref, None, True, base, gm_ref, pw_ref, ps_ref,
                                                 gf_ref, wr_ref, br_ref, tri_ref)
        h2_ref[...] = h2
        u2_ref[...] = u2.astype(BF16)
        info_ref[...] = info
        base_ref[...] = new_base
        tot_ref[...] = new_base
        state_ref[0] = u1[u1.shape[0] - HIST_ROWS:, :]


def _pool_short_kernel(n_sample, n_valid, h2_hbm, u2_hbm, info_hbm, h_ref, hist_ref, base0_ref, gm_ref, pw_ref, ps_ref,
                       gf_ref, wr_ref, br_ref, tri_ref, h2_ref, u2_ref, info_ref, cum_ref, tot_ref, state_ref,
                       ext_ref, base_ref):
    del h2_hbm, u2_hbm, info_hbm
    s_id = pl.program_id(0)

    @pl.when(s_id == 0)
    def _():
        base_ref[...] = base0_ref[...]

    ext_ref[:HIST_ROWS, :] = hist_ref[0]
    base = base_ref[...]
    cum_ref[0] = base
    pos1 = (lax.broadcasted_iota(jnp.int32, (SEQ_S, 1), 0) + 1).astype(F32)
    cnt_rows = jnp.where(s_id < n_sample, float(max(POOL_WINDOWS)), pos1)
    h2, u2, info, new_base, u1 = _pool_route(h_ref[...], ext_ref, cnt_rows, s_id < n_valid, base, gm_ref, pw_ref,
                                             ps_ref, gf_ref, wr_ref, br_ref, tri_ref)
    h2_ref[...] = h2
    u2_ref[...] = u2.astype(BF16)
    info_ref[...] = info
    base_ref[...] = new_base
    tot_ref[...] = new_base
    state_ref[0] = u1


def _expert_kernel(n_blocks, be_ref, lo_ref, hi_ref, ok_ref, u_hbm, pos_ref, gate_ref, wg_ref, wu_ref, wd_ref, ys_ref,
                   buf_ref, acc_ref, gacc_ref, sem):
    del be_ref
    j = pl.program_id(0)
    rb = ys_ref.shape[0]
    n_slots, tc, _ = buf_ref.shape

    def n_chunks(jj):
        return jnp.where(ok_ref[jj] != 0, hi_ref[jj] - lo_ref[jj] + 1, 0)

    def chunk_copy(c, slot):
        return pltpu.make_async_copy(u_hbm.at[pl.ds(pl.multiple_of(c * tc, tc), tc), :], buf_ref.at[slot],
                                     sem.at[slot])

    def request(jj):
        n_req = jnp.minimum(n_chunks(jj), n_slots)
        for k in range(n_slots):
            @pl.when(k < n_req)
            def _():
                chunk_copy(lo_ref[jj] + k, k).start()

    @pl.when(j == 0)
    def _():
        buf_ref[...] = jnp.zeros_like(buf_ref)
        request(0)

    n = n_chunks(j)
    lo = lo_ref[j]

    @pl.when(n > 0)
    def _():
        rows = j * rb + lax.broadcasted_iota(jnp.int32, (rb, tc), 0)
        n_req = jnp.minimum(n, n_slots)

        def select(c, live):
            pos = pos_ref[c]
            gate = gate_ref[c]
            m0 = (rows == pos[0:1, :]) & live
            m1 = (rows == pos[1:2, :]) & live
            sel = jnp.where(m0 | m1, 1.0, 0.0).astype(BF16)
            row_gate = jnp.sum(jnp.where(m0, gate[0:1, :], 0.0) + jnp.where(m1, gate[1:2, :], 0.0),
                               axis=1, keepdims=True)
            return sel, row_gate

        for k in range(n_slots):
            @pl.when(k < n_req)
            def _():
                chunk_copy(lo + k, k).wait()

        last_chunk = pos_ref.shape[0] - 1
        picks = [select(jnp.minimum(lo + k, last_chunk), k < n_req) for k in range(n_slots)]
        acc_ref[...] = _dot(jnp.concatenate([sel for sel, _ in picks], axis=1), buf_ref[...].reshape(n_slots * tc, -1))
        gacc_ref[...] = functools.reduce(lambda a, b: a + b, [g for _, g in picks])

        def overflow(k, carry):
            copy = chunk_copy(lo + k, 0)
            copy.start()
            copy.wait()
            sel, row_gate = select(lo + k, True)
            acc_ref[...] += _dot(sel, buf_ref[0])
            gacc_ref[...] += row_gate
            return carry

        lax.fori_loop(n_slots, jnp.maximum(n, n_slots), overflow, 0)

    @pl.when(j + 1 < n_blocks)
    def _():
        request(jnp.minimum(j + 1, n_blocks - 1))

    @pl.when(n > 0)
    def _():
        xg = acc_ref[...].astype(BF16)
        act = jax.nn.silu(_dot(xg, wg_ref[0])) * _dot(xg, wu_ref[0])
        ys_ref[...] = (_dot(act.astype(BF16), wd_ref[0]) * gacc_ref[...]).astype(BF16)

    @pl.when(n == 0)
    def _():
        ys_ref[...] = jnp.zeros_like(ys_ref)


def _combine_kernel(n_blocks, n_fb, n_ref, ids_ref, h2_ref, pos_ref, g_ref, ys_hbm, yp_ref, yshort_ref, buf_ref, sem):
    i = pl.program_id(0)
    _, n_slots, cr, _ = buf_ref.shape
    per_vreg = LANE // cr
    par = i % 2
    nxt = jnp.minimum(i + 1, n_blocks - 1)

    def chunk_copy(ii, k, which):
        ch = ids_ref[ii * n_slots + k]
        return pltpu.make_async_copy(ys_hbm.at[pl.ds(pl.multiple_of(ch * cr, cr), cr), :], buf_ref.at[which, k],
                                     sem.at[which * n_slots + k])

    def request(ii, which):
        for k in range(n_slots):
            @pl.when(k < n_ref[ii])
            def _():
                chunk_copy(ii, k, which).start()

    @pl.when(i == 0)
    def _():
        buf_ref[...] = jnp.zeros_like(buf_ref)
        request(0, 0)

    @pl.when(i + 1 < n_blocks)
    def _():
        request(nxt, 1 - par)

    lane = lax.broadcasted_iota(jnp.int32, (1, LANE), 1)
    pieces = []
    for v in range(n_slots // per_vreg):
        row = jnp.zeros((1, LANE), jnp.int32)
        for q in range(per_vreg):
            k = v * per_vreg + q
            row = jnp.where(lane // cr == q, ids_ref[i * n_slots + k] * cr + lane % cr, row)
        pieces.append(row)
    slot_rows = jnp.concatenate(pieces, axis=1)
    sel = jnp.where((slot_rows == pos_ref[:, 0:1]) | (slot_rows == pos_ref[:, 1:2]), 1.0, 0.0).astype(BF16)

    for k in range(n_slots):
        @pl.when(k < n_ref[i])
        def _():
            chunk_copy(i, k, par).wait()

    moe = _dot(sel, buf_ref[par].reshape(n_slots * cr, -1))
    y = _rms(h2_ref[...] + moe, g_ref[...])

    @pl.when(i < n_fb)
    def _():
        yp_ref[...] = y

    @pl.when(i >= n_fb)
    def _():
        yshort_ref[...] = y


def _rope_table(pos):
    half = QK_ROPE // 2
    inv = ROPE_THETA ** (-jnp.arange(half, dtype=F32) / half)
    ang = pos.astype(F32)[:, None] * inv[None, :]
    cos = jnp.cos(ang)
    sin = jnp.sin(ang)
    zero = jnp.zeros((pos.shape[0], LANE - QK_ROPE), F32)
    return jnp.concatenate([cos, cos, zero, -sin, sin, zero], axis=1)


def _half_swap(w):
    half = QK_ROPE // 2
    return jnp.concatenate([w[..., half:], w[..., :half]], axis=-1)


def _pad_last(w, width):
    return jnp.pad(w, [(0, 0)] * (w.ndim - 1) + [(0, width - w.shape[-1])])


def kernel(x_prompt, x_sample, cache_kv_latent, cache_k_rope, state_pool, meta_tokens, norm_mix, norm_ffn, norm_final,
           mla_w_dq, mla_g_q, mla_w_uq, mla_w_dkv, mla_g_kv, mla_w_uk, mla_w_uv, mla_w_o, pool_w, pool_scale,
           ffn_w_gate, ffn_w_up, ffn_w_down, moe_w_router, moe_b_router, moe_w_gate, moe_w_up, moe_w_down):
    nb, seq, d = x_prompt.shape
    db, dseq, _ = x_sample.shape
    n_cache = cache_kv_latent.shape[2]
    q_rank = mla_w_dq.shape[2]
    kv_rank = mla_g_kv.shape[1]
    v_dim = mla_w_uv.shape[3]
    d_ff = ffn_w_gate.shape[2]
    n_e = moe_w_router.shape[2]
    d_e = moe_w_gate.shape[3]
    tb, rb, tq = TOKEN_BLOCK, ROW_BLOCK, ATTN_BLOCK
    assert norm_mix.shape[0] == 2 and cache_kv_latent.shape[0] == 1 and state_pool.shape[0] == 1
    assert dseq == SEQ_S and N_META == SEQ_S and meta_tokens.shape[0] == N_META
    assert kv_rank == 2 * LANE and QK_NOPE == LANE and QK_ROPE <= LANE and HIST_ROWS >= POOL_HIST
    assert seq % tb == 0 and tb == tq and tq % CHUNK == 0 and tb % SEQ_S == 0 and d % LANE == 0
    assert (n_cache - N_META) % CHUNK == 0 and dseq <= CHUNK

    nf = nb * seq
    n_valid_seq = db + 1
    ns = -(-(n_valid_seq * SEQ_S) // tb) * tb
    n_seq = ns // SEQ_S
    nt = nf + ns
    n_fb, n_sb, n_tb = nf // tb, ns // tb, nt // tb
    sb = seq // tb
    meta_row = nf + db * SEQ_S
    scale = float((QK_NOPE + QK_ROPE) ** -0.5)
    hq = 3 * LANE

    xp = x_prompt.reshape(nf, d)
    xs = jnp.concatenate([x_sample.reshape(db * SEQ_S, d), meta_tokens.astype(x_prompt.dtype),
                          jnp.zeros((ns - n_valid_seq * SEQ_S, d), x_prompt.dtype)], axis=0)

    t_s = jnp.arange(SEQ_S)
    pos_short = jnp.concatenate([jnp.tile(n_cache + t_s, db), jnp.tile(t_s, n_seq - db)])
    tab_frames = _rope_table(N_META + jnp.arange(seq))
    tab = jnp.concatenate([tab_frames, _rope_table(pos_short)], axis=0)

    wuq = mla_w_uq[0].reshape(q_rank, N_HEADS, QK_NOPE + QK_ROPE)
    wuq_pe = wuq[:, :, QK_NOPE:]
    wuq2 = jnp.concatenate([wuq[:, :, :QK_NOPE].reshape(q_rank, -1),
                            _pad_last(wuq_pe, LANE).reshape(q_rank, -1),
                            _pad_last(_half_swap(wuq_pe), LANE).reshape(q_rank, -1)], axis=1).astype(BF16)
    wuq_t = jnp.concatenate([wuq[:, :, :QK_NOPE].reshape(q_rank, -1), wuq_pe.reshape(q_rank, -1),
                             _half_swap(wuq_pe).reshape(q_rank, -1)], axis=1).T.astype(BF16)
    wdkv_r =mla_w_dkv[0][:, kv_rank:]
    wdkv2 = jnp.concatenate([mla_w_dkv[0][:, :kv_rank], _pad_last(wdkv_r, LANE),
                             _pad_last(_half_swap(wdkv_r), LANE)], axis=1).astype(BF16)
    wuk_t = jnp.transpose(mla_w_uk[0], (1, 2, 0)).astype(BF16)
    wuk_flat = mla_w_uk[0].reshape(kv_rank, N_HEADS * QK_NOPE).astype(BF16)
    wuv = jnp.transpose(mla_w_uv[0], (1, 0, 2)).astype(BF16)
    wuv_t = mla_w_uv[0].reshape(kv_rank, N_HEADS * v_dim).T.astype(BF16)
    row = lambda v: v.reshape(1, -1)
    hk = 2 * LANE
    hx = v_dim + BF16_SUBLANES

    tok_p = pl.BlockSpec((tb, d), lambda i: (jnp.minimum(i, n_fb - 1), 0))
    tok_s = pl.BlockSpec((tb, d), lambda i: (jnp.maximum(i - n_fb, 0), 0))
    short_tok = lambda i: (jnp.maximum(i - n_fb, 0), 0)
    q_short, qt_frames, kvb_short, kh_all, vt_all, c_all, r_all = pl.pallas_call(
        functools.partial(_qkv_kernel, n_fb, scale * LOG2_E),
        grid=(n_tb,),
        in_specs=[tok_p, tok_s,
                  pl.BlockSpec((tb, 2 * LANE), lambda i: (jnp.where(i < n_fb, i % sb, sb + i - n_fb), 0)),
                  pl.BlockSpec((2 * LANE, tb), lambda i: (0, i % sb)),
                  _const_spec((1, d)), _const_spec((d, q_rank)), _const_spec((1, q_rank)),
                  _const_spec(wuq2.shape), _const_spec(wuq_t.shape), _const_spec(wuk_t.shape),
                  _const_spec(wdkv2.shape), _const_spec((1, kv_rank)), _const_spec(wuk_flat.shape),
                  _const_spec(wuv_t.shape)],
        out_specs=[pl.BlockSpec((tb, N_HEADS * hq), short_tok),
                   pl.BlockSpec((1, N_HEADS, hk, tb), lambda i: (jnp.minimum(i, n_fb - 1), 0, 0, 0)),
                   pl.BlockSpec((tb, hq), short_tok),
                   pl.BlockSpec((tb, N_HEADS * hk), lambda i: (i, 0)),
                   pl.BlockSpec((1, N_HEADS * hx, tb), lambda i: (i, 0, 0)),
                   pl.BlockSpec((tb, kv_rank), lambda i: (i, 0)),
                   pl.BlockSpec((tb, QK_ROPE), lambda i: (i, 0))],
        out_shape=[jax.ShapeDtypeStruct((ns, N_HEADS * hq), BF16),
                   jax.ShapeDtypeStruct((n_fb, N_HEADS, hk, tb), BF16),
                   jax.ShapeDtypeStruct((ns, hq), BF16),
                   jax.ShapeDtypeStruct((nt, N_HEADS * hk), BF16),
                   jax.ShapeDtypeStruct((n_tb, N_HEADS * hx, tb), BF16),
                   jax.ShapeDtypeStruct((nt, kv_rank), F32), jax.ShapeDtypeStruct((nt, QK_ROPE), F32)],
        compiler_params=_params(("arbitrary",)),
        name="qkv",
    )(xp, xs, tab, tab_frames.T, row(norm_mix[0]), mla_w_dq[0].astype(BF16), row(mla_g_q[0]), wuq2, wuq_t, wuk_t,
      wdkv2, row(mla_g_kv[0]), wuk_flat, wuv_t)

    qb = seq // tq
    once = pl.Buffered(1)
    o_frames = pl.pallas_call(
        functools.partial(_attn_frames_kernel, qb, meta_row % tb),
        grid=(n_fb,),
        in_specs=[pl.BlockSpec((1, N_HEADS, hk, tq), lambda i: (i, 0, 0, 0)),
                  pl.BlockSpec((seq, N_HEADS * hk), lambda i: (i // qb, 0), pipeline_mode=once),
                  pl.BlockSpec((qb, N_HEADS * hx, tq), lambda i: (i // qb, 0, 0), pipeline_mode=once),
                  pl.BlockSpec((N_META, N_HEADS * hk), lambda i: (meta_row // N_META, 0), pipeline_mode=once),
                  pl.BlockSpec((1, N_HEADS * hx, tb), lambda i: (meta_row // tb, 0, 0), pipeline_mode=once)],
        out_specs=pl.BlockSpec((tq, N_HEADS * v_dim), lambda i: (i, 0)),
        out_shape=jax.ShapeDtypeStruct((nf, N_HEADS * v_dim), BF16),
        scratch_shapes=[pltpu.VMEM((N_HEADS, 1, tq), F32), pltpu.VMEM((N_HEADS, hx, tq), F32)],
        compiler_params=_params(("parallel",)),
        name="attn_frames",
    )(qt_frames, kh_all, vt_all, kh_all, vt_all)

    o_short = pl.pallas_call(
        functools.partial(_attn_short_kernel, db),
        grid=(n_seq,),
        in_specs=[pl.BlockSpec((SEQ_S, N_HEADS * hq), lambda s: (s, 0)),
                  pl.BlockSpec((SEQ_S, hq), lambda s: (s, 0)),
                  pl.BlockSpec((1, 1, n_cache, kv_rank), lambda s: (0, jnp.minimum(s, db - 1), 0, 0)),
                  pl.BlockSpec((1, 1, n_cache, QK_ROPE), lambda s: (0, jnp.minimum(s, db - 1), 0, 0)),
                  _const_spec(wuv.shape)],
        out_specs=pl.BlockSpec((SEQ_S, N_HEADS * v_dim), lambda s: (s, 0)),
        out_shape=jax.ShapeDtypeStruct((ns, N_HEADS * v_dim), BF16),
        compiler_params=_params(("parallel",)),
        name="attn_short",
    )(q_short, kvb_short, cache_kv_latent, cache_k_rope, wuv)

    h1 = pl.pallas_call(
        functools.partial(_proj_ffn_kernel, n_fb),
        grid=(n_tb,),
        in_specs=[tok_p, tok_s,
                  pl.BlockSpec((tb, N_HEADS * v_dim), lambda i: (jnp.minimum(i, n_fb - 1), 0)),
                  pl.BlockSpec((tb, N_HEADS * v_dim), short_tok),
                  _const_spec((N_HEADS * v_dim, d)), _const_spec((1, d)),
                  _const_spec((d, d_ff)), _const_spec((d, d_ff)), _const_spec((d_ff, d))],
        out_specs=pl.BlockSpec((tb, d), lambda i: (i, 0)),
        out_shape=jax.ShapeDtypeStruct((nt, d), F32),
        compiler_params=_params(("parallel",)),
        name="proj_ffn",
    )(xp, xs, o_frames, o_short, mla_w_o[0].astype(BF16), row(norm_ffn[0]),
      ffn_w_gate[0].astype(BF16), ffn_w_up[0].astype(BF16), ffn_w_down[0].astype(BF16))

    route_w = [_const_spec((1, d)), _const_spec(pool_w.shape[1:]), _const_spec((1, d)), _const_spec((1, d)),
               _const_spec((d, 2 * n_e)), _const_spec((1, n_e))]
    wr_hi = moe_w_router[0].astype(BF16)
    wr_lo = (moe_w_router[0].astype(F32) - wr_hi.astype(F32)).astype(BF16)
    route_args = (row(norm_mix[1]), pool_w[0].astype(BF16), row(pool_scale[0]), row(norm_ffn[1]),
                  jnp.concatenate([wr_hi, wr_lo], axis=1), row(moe_b_router[0]))
    tri = lambda n: (jnp.arange(n)[:, None] > jnp.arange(n)[None, :]).astype(BF16)
    hist_blk = lambda i: jnp.where(i % sb == 0, meta_row // HIST_ROWS, i * (tb // HIST_ROWS) - 1)
    h2_f, u2_f, info_f, cum_f, tot_f, state_f = pl.pallas_call(
        functools.partial(_pool_frames_kernel, n_fb),
        grid=(n_tb,),
        in_specs=[pl.BlockSpec((tb, d), lambda i: (i, 0)),
                  pl.BlockSpec((HIST_ROWS, d), lambda i: (hist_blk(i), 0))] + route_w + [_const_spec((tb, tb))],
        out_specs=[pl.BlockSpec((tb, d), lambda i: (i, 0)), pl.BlockSpec((tb, d), lambda i: (i, 0)),
                   pl.BlockSpec((tb, n_e), lambda i: (i, 0)),
                   pl.BlockSpec((1, 1, n_e), lambda i: (jnp.minimum(i, n_fb - 1), 0, 0)),
                   pl.BlockSpec((1, n_e), lambda i: (0, 0)),
                   pl.BlockSpec((1, HIST_ROWS, d), lambda i: (jnp.minimum(i // sb, nb - 1), 0, 0))],
        out_shape=[jax.ShapeDtypeStruct((nt, d), F32), jax.ShapeDtypeStruct((nt, d), BF16),
                   jax.ShapeDtypeStruct((nt, n_e), F32), jax.ShapeDtypeStruct((n_fb, 1, n_e), F32),
                   jax.ShapeDtypeStruct((1, n_e), F32), jax.ShapeDtypeStruct((nb, HIST_ROWS, d), F32)],
        scratch_shapes=[pltpu.VMEM((HIST_ROWS + tb, d), F32), pltpu.VMEM((1, n_e), F32)],
        compiler_params=_params(("arbitrary",)),
        name="pool_route_frames",
    )(h1, h1, *route_args, tri(tb))

    hist_s = jnp.concatenate([
        jnp.pad(state_pool[0].astype(F32), [(0, 0), (HIST_ROWS - POOL_HIST, 0), (0, 0)]),
        jnp.zeros((n_seq - db, HIST_ROWS, d), F32)], axis=0)
    any_spec = pl.BlockSpec(memory_space=pl.ANY)
    short_blk = lambda s: (nf // SEQ_S + s, 0)
    h2, u2, info, cum_s, tot_s, state_s = pl.pallas_call(
        functools.partial(_pool_short_kernel, db, n_valid_seq),
        grid=(n_seq,),
        in_specs=[any_spec, any_spec, any_spec,
                  pl.BlockSpec((SEQ_S, d), short_blk),
                  pl.BlockSpec((1, HIST_ROWS, d), lambda s: (s, 0, 0)),
                  _const_spec((1, n_e))] + route_w + [_const_spec((SEQ_S, SEQ_S))],
        out_specs=[pl.BlockSpec((SEQ_S, d), short_blk), pl.BlockSpec((SEQ_S, d), short_blk),
                   pl.BlockSpec((SEQ_S, n_e), short_blk), pl.BlockSpec((1, 1, n_e), lambda s: (s, 0, 0)),
                   pl.BlockSpec((1, n_e), lambda s: (0, 0)),
                   pl.BlockSpec((1, SEQ_S, d), lambda s: (s, 0, 0))],
        out_shape=[jax.ShapeDtypeStruct((nt, d), F32), jax.ShapeDtypeStruct((nt, d), BF16),
                   jax.ShapeDtypeStruct((nt, n_e), F32), jax.ShapeDtypeStruct((n_seq, 1, n_e), F32),
                   jax.ShapeDtypeStruct((1, n_e), F32), jax.ShapeDtypeStruct((n_seq, SEQ_S, d), F32)],
        scratch_shapes=[pltpu.VMEM((HIST_ROWS + SEQ_S, d), F32), pltpu.VMEM((1, n_e), F32)],
        input_output_aliases={0: 0, 1: 1, 2: 2},
        compiler_params=_params(("arbitrary",)),
        name="pool_route_short",
    )(h2_f, u2_f, info_f, h1, hist_s, tot_f, *route_args, tri(SEQ_S))

    counts = tot_s[0].astype(jnp.int32)
    cum = jnp.concatenate([cum_f[:, 0], cum_s[::tb // SEQ_S, 0], tot_s], axis=0).astype(jnp.int32)
    padded = (counts + rb - 1) // rb * rb
    pend = jnp.cumsum(padded)
    pstart = pend - padded
    n_rows_max = -(-(TOP_K * (nf + n_valid_seq * SEQ_S)) // rb) * rb + n_e * rb
    n_rb = n_rows_max // rb
    e_tok = info[:, 0:TOP_K].astype(jnp.int32)
    pos = jnp.where(e_tok >= 0, pstart[jnp.maximum(e_tok, 0)] + info[:, 2:2 + TOP_K].astype(jnp.int32), -1)
    gates = info[:, 4:4 + TOP_K]
    pos_l = pos.reshape(n_tb, tb, TOP_K).transpose(0, 2, 1)
    gate_l = gates.reshape(n_tb, tb, TOP_K).transpose(0, 2, 1)
    pos_c = jnp.pad(pos, [(0, 0), (0, n_e - TOP_K)], constant_values=-1)

    blk_row = jnp.arange(n_rb, dtype=jnp.int32) * rb
    blk_e = jnp.minimum(jnp.sum(pend[None, :] <= blk_row[:, None], axis=1), n_e - 1).astype(jnp.int32)
    blk_ok = (blk_row < pend[-1]).astype(jnp.int32)
    r_lo = blk_row - pstart[blk_e]
    r_hi = jnp.minimum(r_lo + rb, counts[blk_e])
    cum_b = cum.T[blk_e]
    first = jnp.sum(cum_b[:, 1:] <= r_lo[:, None], axis=1)
    last = jnp.sum(cum_b[:, :-1] < r_hi[:, None], axis=1) - 1
    blk_lo = jnp.clip(first, 0, n_tb - 1).astype(jnp.int32)
    blk_hi = jnp.clip(last, blk_lo, n_tb - 1).astype(jnp.int32)

    ys = pl.pallas_call(
        functools.partial(_expert_kernel, n_rb),
        grid_spec=pltpu.PrefetchScalarGridSpec(
            num_scalar_prefetch=4,
            grid=(n_rb,),
            in_specs=[pl.BlockSpec(memory_space=pl.ANY),
                      _const_spec((n_tb, TOP_K, tb)), _const_spec((n_tb, TOP_K, tb)),
                      pl.BlockSpec((1, d, d_e), lambda j, be, lo, hi, ok: (be[j], 0, 0)),
                      pl.BlockSpec((1, d, d_e), lambda j, be, lo, hi, ok: (be[j], 0, 0)),
                      pl.BlockSpec((1, d_e, d), lambda j, be, lo, hi, ok: (be[j], 0, 0))],
            out_specs=pl.BlockSpec((rb, d), lambda j, be, lo, hi, ok: (j, 0)),
            scratch_shapes=[pltpu.VMEM((GATHER_SLOTS, tb, d), BF16), pltpu.VMEM((rb, d), F32),
                            pltpu.VMEM((rb, 1), F32), pltpu.SemaphoreType.DMA((GATHER_SLOTS,))]),
        out_shape=jax.ShapeDtypeStruct((n_rows_max, d), BF16),
        compiler_params=_params(("arbitrary",)),
        name="experts",
    )(blk_e, blk_lo, blk_hi, blk_ok, u2, pos_l, gate_l, moe_w_gate[0].astype(BF16), moe_w_up[0].astype(BF16),
      moe_w_down[0].astype(BF16))

    cr = COMBINE_ROWS
    per_expert = tb // cr + 1
    n_slots = TOP_K * tb // cr + 2 * n_e
    assert rb % cr == 0 and LANE % cr == 0 and n_slots % (LANE // cr) == 0
    w_lo = pstart[None, :] + cum[:-1]
    w_hi = pstart[None, :] + cum[1:]
    c_lo = w_lo // cr
    n_chunks = jnp.where(w_hi > w_lo, (w_hi - 1) // cr - c_lo + 1, 0)
    q = jnp.arange(per_expert)
    cand = (c_lo[:, :, None] + q).reshape(n_tb, n_e * per_expert)
    keep = (q < n_chunks[:, :, None]).reshape(n_tb, n_e * per_expert)
    order = jnp.argsort(~keep, axis=1, stable=True)[:, :n_slots]
    ids = jnp.where(jnp.take_along_axis(keep, order, axis=1), jnp.take_along_axis(cand, order, axis=1),
                    n_rows_max // cr)
    n_ids = jnp.sum(keep, axis=1).astype(jnp.int32)
    y_frames, y_short = pl.pallas_call(
        functools.partial(_combine_kernel, n_tb, n_fb),
        grid_spec=pltpu.PrefetchScalarGridSpec(
            num_scalar_prefetch=2,
            grid=(n_tb,),
            in_specs=[pl.BlockSpec((tb, d), lambda i, n, ids: (i, 0)),
                      pl.BlockSpec((tb, n_e), lambda i, n, ids: (i, 0)),
                      pl.BlockSpec((1, d), lambda i, n, ids: (0, 0)),
                      pl.BlockSpec(memory_space=pl.ANY)],
            out_specs=[pl.BlockSpec((tb, d), lambda i, n, ids: (jnp.minimum(i, n_fb - 1), 0)),
                       pl.BlockSpec((tb, d), lambda i, n, ids: (jnp.maximum(i - n_fb, 0), 0))],
            scratch_shapes=[pltpu.VMEM((2, n_slots, cr, d), BF16), pltpu.SemaphoreType.DMA((2 * n_slots,))]),
        out_shape=[jax.ShapeDtypeStruct((nf, d), F32), jax.ShapeDtypeStruct((ns, d), F32)],
        compiler_params=_params(("arbitrary",)),
        name="combine",
    )(n_ids, ids.reshape(-1).astype(jnp.int32), h2, pos_c, row(norm_final), ys)

    y_prompt = y_frames.reshape(nb, seq, d)
    y_sample = y_short[:db * SEQ_S].reshape(db, SEQ_S, d)

    def with_meta(a, width):
        meta = jnp.broadcast_to(a[meta_row:meta_row + N_META][None], (nb, N_META, width))
        return jnp.concatenate([meta, a[:nf].reshape(nb, seq, width)], axis=1)[None]

    c_p = with_meta(c_all, kv_rank)
    r_p = with_meta(r_all, QK_ROPE)
    c_s = c_all[nf:nf + db * SEQ_S].reshape(1, db, SEQ_S, kv_rank)
    r_s = r_all[nf:nf + db * SEQ_S].reshape(1, db, SEQ_S, QK_ROPE)
    s_p = state_f[:, HIST_ROWS - POOL_HIST:][None]
    s_s = state_s[:db, SEQ_S - POOL_HIST:][None]
    return (y_prompt, y_sample, c_p, r_p, s_p, c_s, r_s, s_s)
```

```python
import functools

import jax
import jax.numpy as jnp
from jax import lax
from jax.experimental import pallas as pl
from jax.experimental.pallas import tpu as pltpu

CHUNK = 64
N_META = 16
N_HEADS = 8
QK_NOPE = 128
QK_ROPE = 64
ROPE_THETA = 10000.0
POOL_WINDOWS = (2, 4, 8, 16)
POOL_HIST = max(POOL_WINDOWS) - 1
TOP_K = 2
RMS_EPS = 1e-6

LANE = 128
BF16_SUBLANES = 16
SEQ_S = 16
HIST_ROWS = 16
TOKEN_BLOCK = 256
ROW_BLOCK = 256
ATTN_BLOCK = 256
CACHE_BLOCK = 512
KEY_BLOCKS_PER_ITER = 4
SCORES_AHEAD = 6
COMBINE_ROWS = 64
GATHER_SLOTS = 6
VMEM_LIMIT = 56 * 1024 * 1024

F32 = jnp.float32
BF16 = jnp.bfloat16
NEG_INF = float("-inf")
LOG2_E = 1.4426950408889634


def _dot(a, b):
    return jnp.dot(a, b, preferred_element_type=F32)


def _dot_nt(a, b):
    return lax.dot_general(a, b, (((1,), (1,)), ((), ())), preferred_element_type=F32)


def _rms(x, g):
    return x * lax.rsqrt(jnp.mean(x * x, axis=-1, keepdims=True) + RMS_EPS) * g


def _const_spec(shape):
    nd = len(shape)
    return pl.BlockSpec(shape, lambda *_: (0,) * nd, pipeline_mode=pl.Buffered(1))


def _params(sem):
    return pltpu.CompilerParams(dimension_semantics=sem, vmem_limit_bytes=VMEM_LIMIT)


def _qkv_kernel(n_fb, scale, xp_ref, xs_ref, tab_ref, tabt_ref, g_ref, wdq_ref, gq_ref, wuq_ref, wuqt_ref, wukt_ref,
                wdkv_ref, gkv_ref, wukf_ref, wuvt_ref, q_ref, qt_ref, kvb_ref, kh_ref, vt_ref, c_ref, r_ref):
    i = pl.program_id(0)
    x = jnp.where(i < n_fb, xp_ref[...], xs_ref[...])
    u = _rms(x, g_ref[...]).astype(BF16)
    cq = _rms(_dot(u, wdq_ref[...]), gq_ref[...]).astype(BF16)
    cos = tab_ref[:, :LANE]
    sin = tab_ref[:, LANE:]
    hq = QK_NOPE + 2 * LANE

    kv = _dot(u, wdkv_ref[...])
    c = _rms(kv[:, :2 * LANE], gkv_ref[...])
    r = kv[:, 2 * LANE:3 * LANE] * cos + kv[:, 3 * LANE:] * sin
    c_ref[...] = c
    r_ref[...] = r[:, :QK_ROPE]
    c_bf = c.astype(BF16)
    r_bf = r.astype(BF16)

    kn = _dot(c_bf, wukf_ref[...])
    for h in range(N_HEADS):
        kh_ref[:, 2 * h * LANE:(2 * h + 1) * LANE] = kn[:, h * LANE:(h + 1) * LANE].astype(BF16)
        kh_ref[:, (2 * h + 1) * LANE:(2 * h + 2) * LANE] = r_bf
    vt = _dot_nt(wuvt_ref[...], c_bf).astype(BF16)
    hv = vt.shape[0] // N_HEADS
    hv_ext = vt_ref.shape[1] // N_HEADS
    for h in range(N_HEADS):
        vt_ref[0, h * hv_ext:h * hv_ext + hv, :] = vt[h * hv:(h + 1) * hv, :]
        vt_ref[0, h * hv_ext + hv:(h + 1) * hv_ext, :] = jnp.ones((hv_ext - hv, vt.shape[1]), BF16)

    @pl.when(i < n_fb)
    def _():
        qat = _dot_nt(wuqt_ref[...], cq)
        cos_t = tabt_ref[:QK_ROPE, :]
        sin_t = tabt_ref[LANE:LANE + QK_ROPE, :]
        pe0 = N_HEADS * QK_NOPE
        sw0 = pe0 + N_HEADS * QK_ROPE
        for h in range(N_HEADS):
            qt_ref[0, h, :QK_NOPE, :] = (qat[h * QK_NOPE:(h + 1) * QK_NOPE, :] * scale).astype(BF16)
            a = qat[pe0 + h * QK_ROPE:pe0 + (h + 1) * QK_ROPE, :]
            b = qat[sw0 + h * QK_ROPE:sw0 + (h + 1) * QK_ROPE, :]
            qt_ref[0, h, QK_NOPE:QK_NOPE + QK_ROPE, :] = ((a * cos_t + b * sin_t) * scale).astype(BF16)
            qt_ref[0, h, QK_NOPE + QK_ROPE:, :] = jnp.zeros((LANE - QK_ROPE, qt_ref.shape[3]), BF16)

    @pl.when(i >= n_fb)
    def _():
        qa = _dot(cq, wuq_ref[...])
        for h in range(N_HEADS):
            qn = qa[:, h * LANE:(h + 1) * LANE].astype(BF16)
            q_ref[:, h * hq:h * hq + 2 * LANE] = (_dot(qn, wukt_ref[h]) * scale).astype(BF16)
            a = qa[:, (N_HEADS + h) * LANE:(N_HEADS + h + 1) * LANE]
            b = qa[:, (2 * N_HEADS + h) * LANE:(2 * N_HEADS + h + 1) * LANE]
            q_ref[:, h * hq + 2 * LANE:(h + 1) * hq] = ((a * cos + b * sin) * scale).astype(BF16)
        kvb_ref[:, :2 * LANE] = c_bf
        kvb_ref[:, 2 * LANE:] = r_bf


def _attn_frames_kernel(qb, meta_col, qt_ref, kh_ref, vt_ref, khm_ref, vtm_ref, o_ref, m_ref, acc_ref):
    j = pl.program_id(0) % qb
    tq = o_ref.shape[0]
    hk = kh_ref.shape[1] // N_HEADS
    hv = o_ref.shape[1] // N_HEADS
    hx = vt_ref.shape[1] // N_HEADS

    def scores(h, rows):
        return _dot(kh_ref[rows, h * hk:(h + 1) * hk], qt_ref[0, h])

    m_ref[...] = jnp.full(m_ref.shape, NEG_INF, F32)
    acc_ref[...] = jnp.zeros_like(acc_ref)

    def unmasked(kbs):
        steps = [(kb, h) for kb in kbs for h in range(N_HEADS)]
        block_scores = lambda kb, h: scores(h, pl.ds(pl.multiple_of(kb * tq, tq), tq))
        ahead = [block_scores(*step) for step in steps[:SCORES_AHEAD]]
        for n, (kb, h) in enumerate(steps):
            s = ahead.pop(0)
            if n + SCORES_AHEAD < len(steps):
                ahead.append(block_scores(*steps[n + SCORES_AHEAD]))
            m_old = m_ref[h]
            m_new = jnp.maximum(m_old, jnp.max(s, axis=0, keepdims=True))
            alpha = jnp.exp2(m_old - m_new)
            p = jnp.exp2(s - m_new)
            m_ref[h] = m_new
            acc_ref[h] = alpha * acc_ref[h] + _dot(vt_ref[kb, h * hx:(h + 1) * hx, :], p.astype(BF16))

    def body(t, carry):
        unmasked([KEY_BLOCKS_PER_ITER * t + u for u in range(KEY_BLOCKS_PER_ITER)])
        return carry

    lax.fori_loop(0, j // KEY_BLOCKS_PER_ITER, body, 0)
    for left in range(1, KEY_BLOCKS_PER_ITER):
        @pl.when(j % KEY_BLOCKS_PER_ITER == left)
        def _():
            unmasked([j - left + u for u in range(left)])

    rows = pl.ds(pl.multiple_of(j * tq, tq), tq)
    visible = (lax.broadcasted_iota(jnp.int32, (tq, tq), 0) // CHUNK
               <= lax.broadcasted_iota(jnp.int32, (tq, tq), 1) // CHUNK)

    def last_scores(h):
        return scores(h, rows), _dot(khm_ref[:, h * hk:(h + 1) * hk], qt_ref[0, h])

    ahead = [last_scores(h) for h in range(SCORES_AHEAD)]
    for h in range(N_HEADS):
        s, s_meta = ahead.pop(0)
        s = jnp.where(visible, s, NEG_INF)
        if h + SCORES_AHEAD < N_HEADS:
            ahead.append(last_scores(h + SCORES_AHEAD))
        m_old = m_ref[h]
        m_new = jnp.maximum(m_old, jnp.maximum(jnp.max(s, axis=0, keepdims=True),
                                               jnp.max(s_meta, axis=0, keepdims=True)))
        alpha = jnp.exp2(m_old - m_new)
        p = jnp.exp2(s - m_new)
        p_meta = jnp.exp2(s_meta - m_new)
        acc = (alpha * acc_ref[h] + _dot(vt_ref[j, h * hx:(h + 1) * hx, :], p.astype(BF16))
               + _dot(vtm_ref[0, h * hx:(h + 1) * hx, meta_col:meta_col + N_META], p_meta.astype(BF16)))
        o_ref[:, h * hv:(h + 1) * hv] = (acc[:hv] * (1.0 / acc[hv:hv + 1])).T.astype(BF16)


def _attn_short_kernel(n_cached_seq, q_ref, kvn_ref, cc_ref, cr_ref, wuv_ref, o_ref):
    s_id = pl.program_id(0)
    hq = q_ref.shape[1] // N_HEADS
    n_cache = cc_ref.shape[2]
    qs = jnp.concatenate([q_ref[:, h * hq:(h + 1) * hq] for h in range(N_HEADS)], axis=0)
    q_lat = qs[:, :2 * LANE]
    q_pe = qs[:, 2 * LANE:2 * LANE + QK_ROPE]
    kvn = kvn_ref[...]
    s = _dot_nt(qs, kvn)
    m = jnp.max(s, axis=1, keepdims=True)
    p = jnp.exp2(s - m)
    l = jnp.sum(p, axis=1, keepdims=True)
    acc = _dot(p.astype(BF16), kvn[:, :2 * LANE])
    has_cache = s_id < n_cached_seq
    start = 0
    while start < n_cache:
        size = min(CACHE_BLOCK, n_cache - start)
        ck = cc_ref[0, 0, start:start + size, :].astype(BF16)
        rk = cr_ref[0, 0, start:start + size, :].astype(BF16)
        s = _dot_nt(q_lat, ck) + _dot_nt(q_pe, rk)
        s = jnp.where(has_cache, s, NEG_INF)
        m_new = jnp.maximum(m, jnp.max(s, axis=1, keepdims=True))
        alpha = jnp.exp2(m - m_new)
        p = jnp.exp2(s - m_new)
        l = alpha * l + jnp.sum(p, axis=1, keepdims=True)
        acc = alpha * acc + _dot(p.astype(BF16), ck)
        m = m_new
        start += size
    o = (acc / l).astype(BF16)
    hv = wuv_ref.shape[2]
    for h in range(N_HEADS):
        o_ref[:, h * hv:(h + 1) * hv] = _dot(o[h * SEQ_S:(h + 1) * SEQ_S, :], wuv_ref[h]).astype(BF16)


def _proj_ffn_kernel(n_fb, xp_ref, xs_ref, op_ref, os_ref, wo_ref, g_ref, wg_ref, wu_ref, wd_ref, h_ref):
    i = pl.program_id(0)
    x = jnp.where(i < n_fb, xp_ref[...], xs_ref[...])
    o = jnp.where(i < n_fb, op_ref[...], os_ref[...])
    h1 = x + _dot(o, wo_ref[...])
    u = _rms(h1, g_ref[...]).astype(BF16)
    act = jax.nn.silu(_dot(u, wg_ref[...])) * _dot(u, wu_ref[...])
    h_ref[...] = h1 + _dot(act.astype(BF16), wd_ref[...])


def _pool_route(h1, ext_ref, cnt_rows, valid, base, gm_ref, pw_ref, ps_ref, gf_ref, wr_ref, br_ref, tri_ref):
    t = h1.shape[0]
    grp = h1.shape[1] // len(POOL_WINDOWS)
    u1 = _rms(h1, gm_ref[...])
    ext_ref[HIST_ROWS:HIST_ROWS + t, :] = u1
    ys = []
    for g, w in enumerate(POOL_WINDOWS):
        cols = slice(g * grp, (g + 1) * grp)
        acc = u1[:, cols]
        for k in range(1, w):
            acc = acc + ext_ref[HIST_ROWS - k:HIST_ROWS - k + t, cols]
        cnt = float(w) if cnt_rows is None else jnp.minimum(float(w), cnt_rows)
        mean = acc / cnt - u1[:, cols]
        ys.append(_dot(mean.astype(BF16), pw_ref[g]))
    h2 = h1 + jnp.concatenate(ys, axis=1) * ps_ref[...]
    u2 = _rms(h2, gf_ref[...])

    n_e = wr_ref.shape[1] // 2
    u2_hi = u2.astype(BF16)
    u2_lo = (u2 - u2_hi.astype(F32)).astype(BF16)
    parts = _dot(u2_hi, wr_ref[...]) + _dot(u2_lo, wr_ref[...])
    logits = parts[:, :n_e] + parts[:, n_e:] + br_ref[...]
    e_iota = lax.broadcasted_iota(jnp.int32, logits.shape, 1).astype(F32)
    v0 = jnp.max(logits, axis=1, keepdims=True)
    e0 = jnp.min(jnp.where(logits == v0, e_iota, float(n_e)), axis=1, keepdims=True)
    rest = jnp.where(e_iota == e0, NEG_INF, logits)
    v1 = jnp.max(rest, axis=1, keepdims=True)
    e1 = jnp.min(jnp.where(rest == v1, e_iota, float(n_e)), axis=1, keepdims=True)
    tt = jnp.exp(v1 - v0)
    g0 = 1.0 / (1.0 + tt)
    g1 = tt / (1.0 + tt)
    hit0 = e_iota == e0
    hit1 = e_iota == e1
    onehot = jnp.where(valid & (hit0 | hit1), 1.0, 0.0)
    rank = _dot(tri_ref[...], onehot.astype(BF16)) + base
    r0 = jnp.sum(jnp.where(hit0, rank, 0.0), axis=1, keepdims=True)
    r1 = jnp.sum(jnp.where(hit1, rank, 0.0), axis=1, keepdims=True)
    cols = (jnp.where(valid, e0, -1).astype(F32), jnp.where(valid, e1, -1).astype(F32), r0, r1, g0, g1)
    info = jnp.zeros(logits.shape, F32)
    for k, col in enumerate(cols):
        info = jnp.where(e_iota == k, col, info)
    new_base = base + jnp.sum(onehot, axis=0, keepdims=True)
    return h2, u2, info, new_base, u1


def _pool_frames_kernel(n_fb, h_ref, hist_ref, gm_ref, pw_ref, ps_ref, gf_ref, wr_ref, br_ref, tri_ref,
                        h2_ref, u2_ref, info_ref, cum_ref, tot_ref, state_ref, ext_ref, base_ref):
    i = pl.program_id(0)

    @pl.when(i == 0)
    def _():
        base_ref[...] = jnp.zeros_like(base_ref)

    @pl.when(i >= n_fb)
    def _():
        h2_ref[...] = jnp.zeros_like(h2_ref)
        u2_ref[...] = jnp.zeros_like(u2_ref)
        info_ref[...] = jnp.zeros_like(info_ref)

    @pl.when(i < n_fb)
    def _():
        ext_ref[:HIST_ROWS, :] = _rms(hist_ref[...], gm_ref[...])
        base = base_ref[...]
        cum_ref[0] = base
        h2, u2, info, new_base, u1 = _pool_route(h_ref[...], ext_ref, None, True, base, gm_ref, pw_ref, ps_ref,
                                                 gf_ref, wr_ref, br_ref, tri_ref)
        h2_ref[...] = h2
        u2_ref[...] = u2.astype(BF16)
        info_ref[...] = info
        base_ref[...] = new_base
        tot_ref[...] = new_base
        state_ref[0] = u1[u1.shape[0] - HIST_ROWS:, :]


def _pool_short_kernel(n_sample, n_valid, h2_hbm, u2_hbm, info_hbm, h_ref, hist_ref, base0_ref, gm_ref, pw_ref, ps_ref,
                       gf_ref, wr_ref, br_ref, tri_ref, h2_ref, u2_ref, info_ref, cum_ref, tot_ref, state_ref,
                       ext_ref, base_ref):
    del h2_hbm, u2_hbm, info_hbm
    s_id = pl.program_id(0)

    @pl.when(s_id == 0)
    def _():
        base_ref[...] = base0_ref[...]

    ext_ref[:HIST_ROWS, :] = hist_ref[0]
    base = base_ref[...]
    cum_ref[0] = base
    pos1 = (lax.broadcasted_iota(jnp.int32, (SEQ_S, 1), 0) + 1).astype(F32)
    cnt_rows = jnp.where(s_id < n_sample, float(max(POOL_WINDOWS)), pos1)
    h2, u2, info, new_base, u1 = _pool_route(h_ref[...], ext_ref, cnt_rows, s_id < n_valid, base, gm_ref, pw_ref,
                                             ps_ref, gf_ref, wr_ref, br_ref, tri_ref)
    h2_ref[...] = h2
    u2_ref[...] = u2.astype(BF16)
    info_ref[...] = info
    base_ref[...] = new_base
    tot_ref[...] = new_base
    state_ref[0] = u1


def _expert_kernel(n_blocks, be_ref, lo_ref, hi_ref, ok_ref, u_hbm, pos_ref, gate_ref, wg_ref, wu_ref, wd_ref, ys_ref,
                   buf_ref, acc_ref, gacc_ref, sem):
    del be_ref
    j = pl.program_id(0)
    rb = ys_ref.shape[0]
    n_slots, tc, _ = buf_ref.shape

    def n_chunks(jj):
        return jnp.where(ok_ref[jj] != 0, hi_ref[jj] - lo_ref[jj] + 1, 0)

    def chunk_copy(c, slot):
        return pltpu.make_async_copy(u_hbm.at[pl.ds(pl.multiple_of(c * tc, tc), tc), :], buf_ref.at[slot],
                                     sem.at[slot])

    def request(jj):
        n_req = jnp.minimum(n_chunks(jj), n_slots)
        for k in range(n_slots):
            @pl.when(k < n_req)
            def _():
                chunk_copy(lo_ref[jj] + k, k).start()

    @pl.when(j == 0)
    def _():
        buf_ref[...] = jnp.zeros_like(buf_ref)
        request(0)

    n = n_chunks(j)
    lo = lo_ref[j]

    @pl.when(n > 0)
    def _():
        rows = j * rb + lax.broadcasted_iota(jnp.int32, (rb, tc), 0)
        n_req = jnp.minimum(n, n_slots)

        def select(c, live):
            pos = pos_ref[c]
            gate = gate_ref[c]
            m0 = (rows == pos[0:1, :]) & live
            m1 = (rows == pos[1:2, :]) & live
            sel = jnp.where(m0 | m1, 1.0, 0.0).astype(BF16)
            row_gate = jnp.sum(jnp.where(m0, gate[0:1, :], 0.0) + jnp.where(m1, gate[1:2, :], 0.0),
                               axis=1, keepdims=True)
            return sel, row_gate

        for k in range(n_slots):
            @pl.when(k < n_req)
            def _():
                chunk_copy(lo + k, k).wait()

        last_chunk = pos_ref.shape[0] - 1
        picks = [select(jnp.minimum(lo + k, last_chunk), k < n_req) for k in range(n_slots)]
        acc_ref[...] = _dot(jnp.concatenate([sel for sel, _ in picks], axis=1), buf_ref[...].reshape(n_slots * tc, -1))
        gacc_ref[...] = functools.reduce(lambda a, b: a + b, [g for _, g in picks])

        def overflow(k, carry):
            copy = chunk_copy(lo + k, 0)
            copy.start()
            copy.wait()
            sel, row_gate = select(lo + k, True)
            acc_ref[...] += _dot(sel, buf_ref[0])
            gacc_ref[...] += row_gate
            return carry

        lax.fori_loop(n_slots, jnp.maximum(n, n_slots), overflow, 0)

    @pl.when(j + 1 < n_blocks)
    def _():
        request(jnp.minimum(j + 1, n_blocks - 1))

    @pl.when(n > 0)
    def _():
        xg = acc_ref[...].astype(BF16)
        act = jax.nn.silu(_dot(xg, wg_ref[0])) * _dot(xg, wu_ref[0])
        ys_ref[...] = (_dot(act.astype(BF16), wd_ref[0]) * gacc_ref[...]).astype(BF16)

    @pl.when(n == 0)
    def _():
        ys_ref[...] = jnp.zeros_like(ys_ref)


def _combine_kernel(n_blocks, n_fb, n_ref, ids_ref, h2_ref, pos_ref, g_ref, ys_hbm, yp_ref, yshort_ref, buf_ref, sem):
    i = pl.program_id(0)
    _, n_slots, cr, _ = buf_ref.shape
    per_vreg = LANE // cr
    par = i % 2
    nxt = jnp.minimum(i + 1, n_blocks - 1)

    def chunk_copy(ii, k, which):
        ch = ids_ref[ii * n_slots + k]
        return pltpu.make_async_copy(ys_hbm.at[pl.ds(pl.multiple_of(ch * cr, cr), cr), :], buf_ref.at[which, k],
                                     sem.at[which * n_slots + k])

    def request(ii, which):
        for k in range(n_slots):
            @pl.when(k < n_ref[ii])
            def _():
                chunk_copy(ii, k, which).start()

    @pl.when(i == 0)
    def _():
        buf_ref[...] = jnp.zeros_like(buf_ref)
        request(0, 0)

    @pl.when(i + 1 < n_blocks)
    def _():
        request(nxt, 1 - par)

    lane = lax.broadcasted_iota(jnp.int32, (1, LANE), 1)
    pieces = []
    for v in range(n_slots // per_vreg):
        row = jnp.zeros((1, LANE), jnp.int32)
        for q in range(per_vreg):
            k = v * per_vreg + q
            row = jnp.where(lane // cr == q, ids_ref[i * n_slots + k] * cr + lane % cr, row)
        pieces.append(row)
    slot_rows = jnp.concatenate(pieces, axis=1)
    sel = jnp.where((slot_rows == pos_ref[:, 0:1]) | (slot_rows == pos_ref[:, 1:2]), 1.0, 0.0).astype(BF16)

    for k in range(n_slots):
        @pl.when(k < n_ref[i])
        def _():
            chunk_copy(i, k, par).wait()

    moe = _dot(sel, buf_ref[par].reshape(n_slots * cr, -1))
    y = _rms(h2_ref[...] + moe, g_ref[...])

    @pl.when(i < n_fb)
    def _():
        yp_ref[...] = y

    @pl.when(i >= n_fb)
    def _():
        yshort_ref[...] = y


def _rope_table(pos):
    half = QK_ROPE // 2
    inv = ROPE_THETA ** (-jnp.arange(half, dtype=F32) / half)
    ang = pos.astype(F32)[:, None] * inv[None, :]
    cos = jnp.cos(ang)
    sin = jnp.sin(ang)
    zero = jnp.zeros((pos.shape[0], LANE - QK_ROPE), F32)
    return jnp.concatenate([cos, cos, zero, -sin, sin, zero], axis=1)


def _half_swap(w):
    half = QK_ROPE // 2
    return jnp.concatenate([w[..., half:], w[..., :half]], axis=-1)


def _pad_last(w, width):
    return jnp.pad(w, [(0, 0)] * (w.ndim - 1) + [(0, width - w.shape[-1])])


def kernel(x_prompt, x_sample, cache_kv_latent, cache_k_rope, state_pool, meta_tokens, norm_mix, norm_ffn, norm_final,
           mla_w_dq, mla_g_q, mla_w_uq, mla_w_dkv, mla_g_kv, mla_w_uk, mla_w_uv, mla_w_o, pool_w, pool_scale,
           ffn_w_gate, ffn_w_up, ffn_w_down, moe_w_router, moe_b_router, moe_w_gate, moe_w_up, moe_w_down):
    nb, seq, d = x_prompt.shape
    db, dseq, _ = x_sample.shape
    n_cache = cache_kv_latent.shape[2]
    q_rank = mla_w_dq.shape[2]
    kv_rank = mla_g_kv.shape[1]
    v_dim = mla_w_uv.shape[3]
    d_ff = ffn_w_gate.shape[2]
    n_e = moe_w_router.shape[2]
    d_e = moe_w_gate.shape[3]
    tb, rb, tq = TOKEN_BLOCK, ROW_BLOCK, ATTN_BLOCK
    assert norm_mix.shape[0] == 2 and cache_kv_latent.shape[0] == 1 and state_pool.shape[0] == 1
    assert dseq == SEQ_S and N_META == SEQ_S and meta_tokens.shape[0] == N_META
    assert kv_rank == 2 * LANE and QK_NOPE == LANE and QK_ROPE <= LANE and HIST_ROWS >= POOL_HIST
    assert seq % tb == 0 and tb == tq and tq % CHUNK == 0 and tb % SEQ_S == 0 and d % LANE == 0
    assert (n_cache - N_META) % CHUNK == 0 and dseq <= CHUNK

    nf = nb * seq
    n_valid_seq = db + 1
    ns = -(-(n_valid_seq * SEQ_S) // tb) * tb
    n_seq = ns // SEQ_S
    nt = nf + ns
    n_fb, n_sb, n_tb = nf // tb, ns // tb, nt // tb
    sb = seq // tb
    meta_row = nf + db * SEQ_S
    scale = float((QK_NOPE + QK_ROPE) ** -0.5)
    hq = 3 * LANE

    xp = x_prompt.reshape(nf, d)
    xs = jnp.concatenate([x_sample.reshape(db * SEQ_S, d), meta_tokens.astype(x_prompt.dtype),
                          jnp.zeros((ns - n_valid_seq * SEQ_S, d), x_prompt.dtype)], axis=0)

    t_s = jnp.arange(SEQ_S)
    pos_short = jnp.concatenate([jnp.tile(n_cache + t_s, db), jnp.tile(t_s, n_seq - db)])
    tab_frames = _rope_table(N_META + jnp.arange(seq))
    tab = jnp.concatenate([tab_frames, _rope_table(pos_short)], axis=0)

    wuq = mla_w_uq[0].reshape(q_rank, N_HEADS, QK_NOPE + QK_ROPE)
    wuq_pe = wuq[:, :, QK_NOPE:]
    wuq2 = jnp.concatenate([wuq[:, :, :QK_NOPE].reshape(q_rank, -1),
                            _pad_last(wuq_pe, LANE).reshape(q_rank, -1),
                            _pad_last(_half_swap(wuq_pe), LANE).reshape(q_rank, -1)], axis=1).astype(BF16)
    wuq_t = jnp.concatenate([wuq[:, :, :QK_NOPE].reshape(q_rank, -1), wuq_pe.reshape(q_rank, -1),
                             _half_swap(wuq_pe).reshape(q_rank, -1)], axis=1).T.astype(BF16)
    wdkv_r =mla_w_dkv[0][:, kv_rank:]
    wdkv2 = jnp.concatenate([mla_w_dkv[0][:, :kv_rank], _pad_last(wdkv_r, LANE),
                             _pad_last(_half_swap(wdkv_r), LANE)], axis=1).astype(BF16)
    wuk_t = jnp.transpose(mla_w_uk[0], (1, 2, 0)).astype(BF16)
    wuk_flat = mla_w_uk[0].reshape(kv_rank, N_HEADS * QK_NOPE).astype(BF16)
    wuv = jnp.transpose(mla_w_uv[0], (1, 0, 2)).astype(BF16)
    wuv_t = mla_w_uv[0].reshape(kv_rank, N_HEADS * v_dim).T.astype(BF16)
    row = lambda v: v.reshape(1, -1)
    hk = 2 * LANE
    hx = v_dim + BF16_SUBLANES

    tok_p = pl.BlockSpec((tb, d), lambda i: (jnp.minimum(i, n_fb - 1), 0))
    tok_s = pl.BlockSpec((tb, d), lambda i: (jnp.maximum(i - n_fb, 0), 0))
    short_tok = lambda i: (jnp.maximum(i - n_fb, 0), 0)
    q_short, qt_frames, kvb_short, kh_all, vt_all, c_all, r_all = pl.pallas_call(
        functools.partial(_qkv_kernel, n_fb, scale * LOG2_E),
        grid=(n_tb,),
        in_specs=[tok_p, tok_s,
                  pl.BlockSpec((tb, 2 * LANE), lambda i: (jnp.where(i < n_fb, i % sb, sb + i - n_fb), 0)),
                  pl.BlockSpec((2 * LANE, tb), lambda i: (0, i % sb)),
                  _const_spec((1, d)), _const_spec((d, q_rank)), _const_spec((1, q_rank)),
                  _const_spec(wuq2.shape), _const_spec(wuq_t.shape), _const_spec(wuk_t.shape),
                  _const_spec(wdkv2.shape), _const_spec((1, kv_rank)), _const_spec(wuk_flat.shape),
                  _const_spec(wuv_t.shape)],
        out_specs=[pl.BlockSpec((tb, N_HEADS * hq), short_tok),
                   pl.BlockSpec((1, N_HEADS, hk, tb), lambda i: (jnp.minimum(i, n_fb - 1), 0, 0, 0)),
                   pl.BlockSpec((tb, hq), short_tok),
                   pl.BlockSpec((tb, N_HEADS * hk), lambda i: (i, 0)),
                   pl.BlockSpec((1, N_HEADS * hx, tb), lambda i: (i, 0, 0)),
                   pl.BlockSpec((tb, kv_rank), lambda i: (i, 0)),
                   pl.BlockSpec((tb, QK_ROPE), lambda i: (i, 0))],
        out_shape=[jax.ShapeDtypeStruct((ns, N_HEADS * hq), BF16),
                   jax.ShapeDtypeStruct((n_fb, N_HEADS, hk, tb), BF16),
                   jax.ShapeDtypeStruct((ns, hq), BF16),
                   jax.ShapeDtypeStruct((nt, N_HEADS * hk), BF16),
                   jax.ShapeDtypeStruct((n_tb, N_HEADS * hx, tb), BF16),
                   jax.ShapeDtypeStruct((nt, kv_rank), F32), jax.ShapeDtypeStruct((nt, QK_ROPE), F32)],
        compiler_params=_params(("arbitrary",)),
        name="qkv",
    )(xp, xs, tab, tab_frames.T, row(norm_mix[0]), mla_w_dq[0].astype(BF16), row(mla_g_q[0]), wuq2, wuq_t, wuk_t,
      wdkv2, row(mla_g_kv[0]), wuk_flat, wuv_t)

    qb = seq // tq
    once = pl.Buffered(1)
    o_frames = pl.pallas_call(
        functools.partial(_attn_frames_kernel, qb, meta_row % tb),
        grid=(n_fb,),
        in_specs=[pl.BlockSpec((1, N_HEADS, hk, tq), lambda i: (i, 0, 0, 0)),
                  pl.BlockSpec((seq, N_HEADS * hk), lambda i: (i // qb, 0), pipeline_mode=once),
                  pl.BlockSpec((qb, N_HEADS * hx, tq), lambda i: (i // qb, 0, 0), pipeline_mode=once),
                  pl.BlockSpec((N_META, N_HEADS * hk), lambda i: (meta_row // N_META, 0), pipeline_mode=once),
                  pl.BlockSpec((1, N_HEADS * hx, tb), lambda i: (meta_row // tb, 0, 0), pipeline_mode=once)],
        out_specs=pl.BlockSpec((tq, N_HEADS * v_dim), lambda i: (i, 0)),
        out_shape=jax.ShapeDtypeStruct((nf, N_HEADS * v_dim), BF16),
        scratch_shapes=[pltpu.VMEM((N_HEADS, 1, tq), F32), pltpu.VMEM((N_HEADS, hx, tq), F32)],
        compiler_params=_params(("parallel",)),
        name="attn_frames",
    )(qt_frames, kh_all, vt_all, kh_all, vt_all)

    o_short = pl.pallas_call(
        functools.partial(_attn_short_kernel, db),
        grid=(n_seq,),
        in_specs=[pl.BlockSpec((SEQ_S, N_HEADS * hq), lambda s: (s, 0)),
                  pl.BlockSpec((SEQ_S, hq), lambda s: (s, 0)),
                  pl.BlockSpec((1, 1, n_cache, kv_rank), lambda s: (0, jnp.minimum(s, db - 1), 0, 0)),
                  pl.BlockSpec((1, 1, n_cache, QK_ROPE), lambda s: (0, jnp.minimum(s, db - 1), 0, 0)),
                  _const_spec(wuv.shape)],
        out_specs=pl.BlockSpec((SEQ_S, N_HEADS * v_dim), lambda s: (s, 0)),
        out_shape=jax.ShapeDtypeStruct((ns, N_HEADS * v_dim), BF16),
        compiler_params=_params(("parallel",)),
        name="attn_short",
    )(q_short, kvb_short, cache_kv_latent, cache_k_rope, wuv)

    h1 = pl.pallas_call(
        functools.partial(_proj_ffn_kernel, n_fb),
        grid=(n_tb,),
        in_specs=[tok_p, tok_s,
                  pl.BlockSpec((tb, N_HEADS * v_dim), lambda i: (jnp.minimum(i, n_fb - 1), 0)),
                  pl.BlockSpec((tb, N_HEADS * v_dim), short_tok),
                  _const_spec((N_HEADS * v_dim, d)), _const_spec((1, d)),
                  _const_spec((d, d_ff)), _const_spec((d, d_ff)), _const_spec((d_ff, d))],
        out_specs=pl.BlockSpec((tb, d), lambda i: (i, 0)),
        out_shape=jax.ShapeDtypeStruct((nt, d), F32),
        compiler_params=_params(("parallel",)),
        name="proj_ffn",
    )(xp, xs, o_frames, o_short, mla_w_o[0].astype(BF16), row(norm_ffn[0]),
      ffn_w_gate[0].astype(BF16), ffn_w_up[0].astype(BF16), ffn_w_down[0].astype(BF16))

    route_w = [_const_spec((1, d)), _const_spec(pool_w.shape[1:]), _const_spec((1, d)), _const_spec((1, d)),
               _const_spec((d, 2 * n_e)), _const_spec((1, n_e))]
    wr_hi = moe_w_router[0].astype(BF16)
    wr_lo = (moe_w_router[0].astype(F32) - wr_hi.astype(F32)).astype(BF16)
    route_args = (row(norm_mix[1]), pool_w[0].astype(BF16), row(pool_scale[0]), row(norm_ffn[1]),
                  jnp.concatenate([wr_hi, wr_lo], axis=1), row(moe_b_router[0]))
    tri = lambda n: (jnp.arange(n)[:, None] > jnp.arange(n)[None, :]).astype(BF16)
    hist_blk = lambda i: jnp.where(i % sb == 0, meta_row // HIST_ROWS, i * (tb // HIST_ROWS) - 1)
    h2_f, u2_f, info_f, cum_f, tot_f, state_f = pl.pallas_call(
        functools.partial(_pool_frames_kernel, n_fb),
        grid=(n_tb,),
        in_specs=[pl.BlockSpec((tb, d), lambda i: (i, 0)),
                  pl.BlockSpec((HIST_ROWS, d), lambda i: (hist_blk(i), 0))] + route_w + [_const_spec((tb, tb))],
        out_specs=[pl.BlockSpec((tb, d), lambda i: (i, 0)), pl.BlockSpec((tb, d), lambda i: (i, 0)),
                   pl.BlockSpec((tb, n_e), lambda i: (i, 0)),
                   pl.BlockSpec((1, 1, n_e), lambda i: (jnp.minimum(i, n_fb - 1), 0, 0)),
                   pl.BlockSpec((1, n_e), lambda i: (0, 0)),
                   pl.BlockSpec((1, HIST_ROWS, d), lambda i: (jnp.minimum(i // sb, nb - 1), 0, 0))],
        out_shape=[jax.ShapeDtypeStruct((nt, d), F32), jax.ShapeDtypeStruct((nt, d), BF16),
                   jax.ShapeDtypeStruct((nt, n_e), F32), jax.ShapeDtypeStruct((n_fb, 1, n_e), F32),
                   jax.ShapeDtypeStruct((1, n_e), F32), jax.ShapeDtypeStruct((nb, HIST_ROWS, d), F32)],
        scratch_shapes=[pltpu.VMEM((HIST_ROWS + tb, d), F32), pltpu.VMEM((1, n_e), F32)],
        compiler_params=_params(("arbitrary",)),
        name="pool_route_frames",
    )(h1, h1, *route_args, tri(tb))

    hist_s = jnp.concatenate([
        jnp.pad(state_pool[0].astype(F32), [(0, 0), (HIST_ROWS - POOL_HIST, 0), (0, 0)]),
        jnp.zeros((n_seq - db, HIST_ROWS, d), F32)], axis=0)
    any_spec = pl.BlockSpec(memory_space=pl.ANY)
    short_blk = lambda s: (nf // SEQ_S + s, 0)
    h2, u2, info, cum_s, tot_s, state_s = pl.pallas_call(
        functools.partial(_pool_short_kernel, db, n_valid_seq),
        grid=(n_seq,),
        in_specs=[any_spec, any_spec, any_spec,
                  pl.BlockSpec((SEQ_S, d), short_blk),
                  pl.BlockSpec((1, HIST_ROWS, d), lambda s: (s, 0, 0)),
                  _const_spec((1, n_e))] + route_w + [_const_spec((SEQ_S, SEQ_S))],
        out_specs=[pl.BlockSpec((SEQ_S, d), short_blk), pl.BlockSpec((SEQ_S, d), short_blk),
                   pl.BlockSpec((SEQ_S, n_e), short_blk), pl.BlockSpec((1, 1, n_e), lambda s: (s, 0, 0)),
                   pl.BlockSpec((1, n_e), lambda s: (0, 0)),
                   pl.BlockSpec((1, SEQ_S, d), lambda s: (s, 0, 0))],
        out_shape=[jax.ShapeDtypeStruct((nt, d), F32), jax.ShapeDtypeStruct((nt, d), BF16),
                   jax.ShapeDtypeStruct((nt, n_e), F32), jax.ShapeDtypeStruct((n_seq, 1, n_e), F32),
                   jax.ShapeDtypeStruct((1, n_e), F32), jax.ShapeDtypeStruct((n_seq, SEQ_S, d), F32)],
        scratch_shapes=[pltpu.VMEM((HIST_ROWS + SEQ_S, d), F32), pltpu.VMEM((1, n_e), F32)],
        input_output_aliases={0: 0, 1: 1, 2: 2},
        compiler_params=_params(("arbitrary",)),
        name="pool_route_short",
    )(h2_f, u2_f, info_f, h1, hist_s, tot_f, *route_args, tri(SEQ_S))

    counts = tot_s[0].astype(jnp.int32)
    cum = jnp.concatenate([cum_f[:, 0], cum_s[::tb // SEQ_S, 0], tot_s], axis=0).astype(jnp.int32)
    padded = (counts + rb - 1) // rb * rb
    pend = jnp.cumsum(padded)
    pstart = pend - padded
    n_rows_max = -(-(TOP_K * (nf + n_valid_seq * SEQ_S)) // rb) * rb + n_e * rb
    n_rb = n_rows_max // rb
    e_tok = info[:, 0:TOP_K].astype(jnp.int32)
    pos = jnp.where(e_tok >= 0, pstart[jnp.maximum(e_tok, 0)] + info[:, 2:2 + TOP_K].astype(jnp.int32), -1)
    gates = info[:, 4:4 + TOP_K]
    pos_l = pos.reshape(n_tb, tb, TOP_K).transpose(0, 2, 1)
    gate_l = gates.reshape(n_tb, tb, TOP_K).transpose(0, 2, 1)
    pos_c = jnp.pad(pos, [(0, 0), (0, n_e - TOP_K)], constant_values=-1)

    blk_row = jnp.arange(n_rb, dtype=jnp.int32) * rb
    blk_e = jnp.minimum(jnp.sum(pend[None, :] <= blk_row[:, None], axis=1), n_e - 1).astype(jnp.int32)
    blk_ok = (blk_row < pend[-1]).astype(jnp.int32)
    r_lo = blk_row - pstart[blk_e]
    r_hi = jnp.minimum(r_lo + rb, counts[blk_e])
    cum_b = cum.T[blk_e]
    first = jnp.sum(cum_b[:, 1:] <= r_lo[:, None], axis=1)
    last = jnp.sum(cum_b[:, :-1] < r_hi[:, None], axis=1) - 1
    blk_lo = jnp.clip(first, 0, n_tb - 1).astype(jnp.int32)
    blk_hi = jnp.clip(last, blk_lo, n_tb - 1).astype(jnp.int32)

    ys = pl.pallas_call(
        functools.partial(_expert_kernel, n_rb),
        grid_spec=pltpu.PrefetchScalarGridSpec(
            num_scalar_prefetch=4,
            grid=(n_rb,),
            in_specs=[pl.BlockSpec(memory_space=pl.ANY),
                      _const_spec((n_tb, TOP_K, tb)), _const_spec((n_tb, TOP_K, tb)),
                      pl.BlockSpec((1, d, d_e), lambda j, be, lo, hi, ok: (be[j], 0, 0)),
                      pl.BlockSpec((1, d, d_e), lambda j, be, lo, hi, ok: (be[j], 0, 0)),
                      pl.BlockSpec((1, d_e, d), lambda j, be, lo, hi, ok: (be[j], 0, 0))],
            out_specs=pl.BlockSpec((rb, d), lambda j, be, lo, hi, ok: (j, 0)),
            scratch_shapes=[pltpu.VMEM((GATHER_SLOTS, tb, d), BF16), pltpu.VMEM((rb, d), F32),
                            pltpu.VMEM((rb, 1), F32), pltpu.SemaphoreType.DMA((GATHER_SLOTS,))]),
        out_shape=jax.ShapeDtypeStruct((n_rows_max, d), BF16),
        compiler_params=_params(("arbitrary",)),
        name="experts",
    )(blk_e, blk_lo, blk_hi, blk_ok, u2, pos_l, gate_l, moe_w_gate[0].astype(BF16), moe_w_up[0].astype(BF16),
      moe_w_down[0].astype(BF16))

    cr = COMBINE_ROWS
    per_expert = tb // cr + 1
    n_slots = TOP_K * tb // cr + 2 * n_e
    assert rb % cr == 0 and LANE % cr == 0 and n_slots % (LANE // cr) == 0
    w_lo = pstart[None, :] + cum[:-1]
    w_hi = pstart[None, :] + cum[1:]
    c_lo = w_lo // cr
    n_chunks = jnp.where(w_hi > w_lo, (w_hi - 1) // cr - c_lo + 1, 0)
    q = jnp.arange(per_expert)
    cand = (c_lo[:, :, None] + q).reshape(n_tb, n_e * per_expert)
    keep = (q < n_chunks[:, :, None]).reshape(n_tb, n_e * per_expert)
    order = jnp.argsort(~keep, axis=1, stable=True)[:, :n_slots]
    ids = jnp.where(jnp.take_along_axis(keep, order, axis=1), jnp.take_along_axis(cand, order, axis=1),
                    n_rows_max // cr)
    n_ids = jnp.sum(keep, axis=1).astype(jnp.int32)
    y_frames, y_short = pl.pallas_call(
        functools.partial(_combine_kernel, n_tb, n_fb),
        grid_spec=pltpu.PrefetchScalarGridSpec(
            num_scalar_prefetch=2,
            grid=(n_tb,),
            in_specs=[pl.BlockSpec((tb, d), lambda i, n, ids: (i, 0)),
                      pl.BlockSpec((tb, n_e), lambda i, n, ids: (i, 0)),
                      pl.BlockSpec((1, d), lambda i, n, ids: (0, 0)),
                      pl.BlockSpec(memory_space=pl.ANY)],
            out_specs=[pl.BlockSpec((tb, d), lambda i, n, ids: (jnp.minimum(i, n_fb - 1), 0)),
                       pl.BlockSpec((tb, d), lambda i, n, ids: (jnp.maximum(i - n_fb, 0), 0))],
            scratch_shapes=[pltpu.VMEM((2, n_slots, cr, d), BF16), pltpu.SemaphoreType.DMA((2 * n_slots,))]),
        out_shape=[jax.ShapeDtypeStruct((nf, d), F32), jax.ShapeDtypeStruct((ns, d), F32)],
        compiler_params=_params(("arbitrary",)),
        name="combine",
    )(n_ids, ids.reshape(-1).astype(jnp.int32), h2, pos_c, row(norm_final), ys)

    y_prompt = y_frames.reshape(nb, seq, d)
    y_sample = y_short[:db * SEQ_S].reshape(db, SEQ_S, d)

    def with_meta(a, width):
        meta = jnp.broadcast_to(a[meta_row:meta_row + N_META][None], (nb, N_META, width))
        return jnp.concatenate([meta, a[:nf].reshape(nb, seq, width)], axis=1)[None]

    c_p = with_meta(c_all, kv_rank)
    r_p = with_meta(r_all, QK_ROPE)
    c_s = c_all[nf:nf + db * SEQ_S].reshape(1, db, SEQ_S, kv_rank)
    r_s = r_all[nf:nf + db * SEQ_S].reshape(1, db, SEQ_S, QK_ROPE)
    s_p = state_f[:, HIST_ROWS - POOL_HIST:][None]
    s_s = state_s[:db, SEQ_S - POOL_HIST:][None]
    return (y_prompt, y_sample, c_p, r_p, s_p, c_s, r_s, s_s)
```

```python
import functools

import jax
import jax.numpy as jnp
from jax import lax
from jax.experimental import pallas as pl
from jax.experimental.pallas import tpu as pltpu

CHUNK = 64
N_META = 16
N_HEADS = 8
QK_NOPE = 128
QK_ROPE = 64
ROPE_THETA = 10000.0
POOL_WINDOWS = (2, 4, 8, 16)
POOL_HIST = max(POOL_WINDOWS) - 1
TOP_K = 2
RMS_EPS = 1e-6

LANE = 128
BF16_SUBLANES = 16
SEQ_S = 16
HIST_ROWS = 16
TOKEN_BLOCK = 256
ROW_BLOCK = 256
ATTN_BLOCK = 256
CACHE_BLOCK = 512
KEY_BLOCKS_PER_ITER = 4
SCORES_AHEAD = 6
COMBINE_ROWS = 64
GATHER_SLOTS = 6
VMEM_LIMIT = 56 * 1024 * 1024

F32 = jnp.float32
BF16 = jnp.bfloat16
NEG_INF = float("-inf")
LOG2_E = 1.4426950408889634


def _dot(a, b):
    return jnp.dot(a, b, preferred_element_type=F32)


def _dot_nt(a, b):
    return lax.dot_general(a, b, (((1,), (1,)), ((), ())), preferred_element_type=F32)


def _rms(x, g):
    return x * lax.rsqrt(jnp.mean(x * x, axis=-1, keepdims=True) + RMS_EPS) * g


def _const_spec(shape):
    nd = len(shape)
    return pl.BlockSpec(shape, lambda *_: (0,) * nd, pipeline_mode=pl.Buffered(1))


def _params(sem):
    return pltpu.CompilerParams(dimension_semantics=sem, vmem_limit_bytes=VMEM_LIMIT)


def _qkv_kernel(n_fb, scale, xp_ref, xs_ref, tab_ref, tabt_ref, g_ref, wdq_ref, gq_ref, wuq_ref, wuqt_ref, wukt_ref,
                wdkv_ref, gkv_ref, wukf_ref, wuvt_ref, q_ref, qt_ref, kvb_ref, kh_ref, vt_ref, c_ref, r_ref):
    i = pl.program_id(0)
    x = jnp.where(i < n_fb, xp_ref[...], xs_ref[...])
    u = _rms(x, g_ref[...]).astype(BF16)
    cq = _rms(_dot(u, wdq_ref[...]), gq_ref[...]).astype(BF16)
    cos = tab_ref[:, :LANE]
    sin = tab_ref[:, LANE:]
    hq = QK_NOPE + 2 * LANE

    kv = _dot(u, wdkv_ref[...])
    c = _rms(kv[:, :2 * LANE], gkv_ref[...])
    r = kv[:, 2 * LANE:3 * LANE] * cos + kv[:, 3 * LANE:] * sin
    c_ref[...] = c
    r_ref[...] = r[:, :QK_ROPE]
    c_bf = c.astype(BF16)
    r_bf = r.astype(BF16)

    kn = _dot(c_bf, wukf_ref[...])
    for h in range(N_HEADS):
        kh_ref[:, 2 * h * LANE:(2 * h + 1) * LANE] = kn[:, h * LANE:(h + 1) * LANE].astype(BF16)
        kh_ref[:, (2 * h + 1) * LANE:(2 * h + 2) * LANE] = r_bf
    vt = _dot_nt(wuvt_ref[...], c_bf).astype(BF16)
    hv = vt.shape[0] // N_HEADS
    hv_ext = vt_ref.shape[1] // N_HEADS
    for h in range(N_HEADS):
        vt_ref[0, h * hv_ext:h * hv_ext + hv, :] = vt[h * hv:(h + 1) * hv, :]
        vt_ref[0, h * hv_ext + hv:(h + 1) * hv_ext, :] = jnp.ones((hv_ext - hv, vt.shape[1]), BF16)

    @pl.when(i < n_fb)
    def _():
        qat = _dot_nt(wuqt_ref[...], cq)
        cos_t = tabt_ref[:QK_ROPE, :]
        sin_t = tabt_ref[LANE:LANE + QK_ROPE, :]
        pe0 = N_HEADS * QK_NOPE
        sw0 = pe0 + N_HEADS * QK_ROPE
        for h in range(N_HEADS):
            qt_ref[0, h, :QK_NOPE, :] = (qat[h * QK_NOPE:(h + 1) * QK_NOPE, :] * scale).astype(BF16)
            a = qat[pe0 + h * QK_ROPE:pe0 + (h + 1) * QK_ROPE, :]
            b = qat[sw0 + h * QK_ROPE:sw0 + (h + 1) * QK_ROPE, :]
            qt_ref[0, h, QK_NOPE:QK_NOPE + QK_ROPE, :] = ((a * cos_t + b * sin_t) * scale).astype(BF16)
            qt_ref[0, h, QK_NOPE + QK_ROPE:, :] = jnp.zeros((LANE - QK_ROPE, qt_ref.shape[3]), BF16)

    @pl.when(i >= n_fb)
    def _():
        qa = _dot(cq, wuq_ref[...])
        for h in range(N_HEADS):
            qn = qa[:, h * LANE:(h + 1) * LANE].astype(BF16)
            q_ref[:, h * hq:h * hq + 2 * LANE] = (_dot(qn, wukt_ref[h]) * scale).astype(BF16)
            a = qa[:, (N_HEADS + h) * LANE:(N_HEADS + h + 1) * LANE]
            b = qa[:, (2 * N_HEADS + h) * LANE:(2 * N_HEADS + h + 1) * LANE]
            q_ref[:, h * hq + 2 * LANE:(h + 1) * hq] = ((a * cos + b * sin) * scale).astype(BF16)
        kvb_ref[:, :2 * LANE] = c_bf
        kvb_ref[:, 2 * LANE:] = r_bf


def _attn_frames_kernel(qb, meta_col, qt_ref, kh_ref, vt_ref, khm_ref, vtm_ref, o_ref, m_ref, acc_ref):
    j = pl.program_id(0) % qb
    tq = o_ref.shape[0]
    hk = kh_ref.shape[1] // N_HEADS
    hv = o_ref.shape[1] // N_HEADS
    hx = vt_ref.shape[1] // N_HEADS

    def scores(h, rows):
        return _dot(kh_ref[rows, h * hk:(h + 1) * hk], qt_ref[0, h])

    m_ref[...] = jnp.full(m_ref.shape, NEG_INF, F32)
    acc_ref[...] = jnp.zeros_like(acc_ref)

    def unmasked(kbs):
        steps = [(kb, h) for kb in kbs for h in range(N_HEADS)]
        block_scores = lambda kb, h: scores(h, pl.ds(pl.multiple_of(kb * tq, tq), tq))
        ahead = [block_scores(*step) for step in steps[:SCORES_AHEAD]]
        for n, (kb, h) in enumerate(steps):
            s = ahead.pop(0)
            if n + SCORES_AHEAD < len(steps):
                ahead.append(block_scores(*steps[n + SCORES_AHEAD]))
            m_old = m_ref[h]
            m_new = jnp.maximum(m_old, jnp.max(s, axis=0, keepdims=True))
            alpha = jnp.exp2(m_old - m_new)
            p = jnp.exp2(s - m_new)
            m_ref[h] = m_new
            acc_ref[h] = alpha * acc_ref[h] + _dot(vt_ref[kb, h * hx:(h + 1) * hx, :], p.astype(BF16))

    def body(t, carry):
        unmasked([KEY_BLOCKS_PER_ITER * t + u for u in range(KEY_BLOCKS_PER_ITER)])
        return carry

    lax.fori_loop(0, j // KEY_BLOCKS_PER_ITER, body, 0)
    for left in range(1, KEY_BLOCKS_PER_ITER):
        @pl.when(j % KEY_BLOCKS_PER_ITER == left)
        def _():
            unmasked([j - left + u for u in range(left)])

    rows = pl.ds(pl.multiple_of(j * tq, tq), tq)
    visible = (lax.broadcasted_iota(jnp.int32, (tq, tq), 0) // CHUNK
               <= lax.broadcasted_iota(jnp.int32, (tq, tq), 1) // CHUNK)

    def last_scores(h):
        return scores(h, rows), _dot(khm_ref[:, h * hk:(h + 1) * hk], qt_ref[0, h])

    ahead = [last_scores(h) for h in range(SCORES_AHEAD)]
    for h in range(N_HEADS):
        s, s_meta = ahead.pop(0)
        s = jnp.where(visible, s, NEG_INF)
        if h + SCORES_AHEAD < N_HEADS:
            ahead.append(last_scores(h + SCORES_AHEAD))
        m_old = m_ref[h]
        m_new = jnp.maximum(m_old, jnp.maximum(jnp.max(s, axis=0, keepdims=True),
                                               jnp.max(s_meta, axis=0, keepdims=True)))
        alpha = jnp.exp2(m_old - m_new)
        p = jnp.exp2(s - m_new)
        p_meta = jnp.exp2(s_meta - m_new)
        acc = (alpha * acc_ref[h] + _dot(vt_ref[j, h * hx:(h + 1) * hx, :], p.astype(BF16))
               + _dot(vtm_ref[0, h * hx:(h + 1) * hx, meta_col:meta_col + N_META], p_meta.astype(BF16)))
        o_ref[:, h * hv:(h + 1) * hv] = (acc[:hv] * (1.0 / acc[hv:hv + 1])).T.astype(BF16)


def _attn_short_kernel(n_cached_seq, q_ref, kvn_ref, cc_ref, cr_ref, wuv_ref, o_ref):
    s_id = pl.program_id(0)
    hq = q_ref.shape[1] // N_HEADS
    n_cache = cc_ref.shape[2]
    qs = jnp.concatenate([q_ref[:, h * hq:(h + 1) * hq] for h in range(N_HEADS)], axis=0)
    q_lat = qs[:, :2 * LANE]
    q_pe = qs[:, 2 * LANE:2 * LANE + QK_ROPE]
    kvn = kvn_ref[...]
    s = _dot_nt(qs, kvn)
    m = jnp.max(s, axis=1, keepdims=True)
    p = jnp.exp2(s - m)
    l = jnp.sum(p, axis=1, keepdims=True)
    acc = _dot(p.astype(BF16), kvn[:, :2 * LANE])
    has_cache = s_id < n_cached_seq
    start = 0
    while start < n_cache:
        size = min(CACHE_BLOCK, n_cache - start)
        ck = cc_ref[0, 0, start:start + size, :].astype(BF16)
        rk = cr_ref[0, 0, start:start + size, :].astype(BF16)
        s = _dot_nt(q_lat, ck) + _dot_nt(q_pe, rk)
        s = jnp.where(has_cache, s, NEG_INF)
        m_new = jnp.maximum(m, jnp.max(s, axis=1, keepdims=True))
        alpha = jnp.exp2(m - m_new)
        p = jnp.exp2(s - m_new)
        l = alpha * l + jnp.sum(p, axis=1, keepdims=True)
        acc = alpha * acc + _dot(p.astype(BF16), ck)
        m = m_new
        start += size
    o = (acc / l).astype(BF16)
    hv = wuv_ref.shape[2]
    for h in range(N_HEADS):
        o_ref[:, h * hv:(h + 1) * hv] = _dot(o[h * SEQ_S:(h + 1) * SEQ_S, :], wuv_ref[h]).astype(BF16)


def _proj_ffn_kernel(n_fb, xp_ref, xs_ref, op_ref, os_ref, wo_ref, g_ref, wg_ref, wu_ref, wd_ref, *cast_refs):
    n_cast = (len(cast_refs) - 1) // 2
    h_ref = cast_refs[n_cast]
    for src, dst in zip(cast_refs[:n_cast], cast_refs[n_cast + 1:]):
        dst[...] = src[...].astype(BF16)
    i = pl.program_id(0)
    x = jnp.where(i < n_fb, xp_ref[...], xs_ref[...])
    o = jnp.where(i < n_fb, op_ref[...], os_ref[...])
    h1 = x + _dot(o, wo_ref[...])
    u = _rms(h1, g_ref[...]).astype(BF16)
    act = jax.nn.silu(_dot(u, wg_ref[...])) * _dot(u, wu_ref[...])
    h_ref[...] = h1 + _dot(act.astype(BF16), wd_ref[...])


def _pool_route(h1, ext_ref, cnt_rows, valid, base, gm_ref, pw_ref, ps_ref, gf_ref, wr_ref, br_ref, tri_ref):
    t = h1.shape[0]
    grp = h1.shape[1] // len(POOL_WINDOWS)
    u1 = _rms(h1, gm_ref[...])
    ext_ref[HIST_ROWS:HIST_ROWS + t, :] = u1
    ys = []
    for g, w in enumerate(POOL_WINDOWS):
        cols = slice(g * grp, (g + 1) * grp)
        acc = u1[:, cols]
        for k in range(1, w):
            acc = acc + ext_ref[HIST_ROWS - k:HIST_ROWS - k + t, cols]
        cnt = float(w) if cnt_rows is None else jnp.minimum(float(w), cnt_rows)
        mean = acc / cnt - u1[:, cols]
        ys.append(_dot(mean.astype(BF16), pw_ref[g]))
    h2 = h1 + jnp.concatenate(ys, axis=1) * ps_ref[...]
    u2 = _rms(h2, gf_ref[...])

    n_e = wr_ref.shape[1] // 2
    u2_hi = u2.astype(BF16)
    u2_lo = (u2 - u2_hi.astype(F32)).astype(BF16)
    parts = _dot(u2_hi, wr_ref[...]) + _dot(u2_lo, wr_ref[...])
    logits = parts[:, :n_e] + parts[:, n_e:] + br_ref[...]
    e_iota = lax.broadcasted_iota(jnp.int32, logits.shape, 1).astype(F32)
    v0 = jnp.max(logits, axis=1, keepdims=True)
    e0 = jnp.min(jnp.where(logits == v0, e_iota, float(n_e)), axis=1, keepdims=True)
    rest = jnp.where(e_iota == e0, NEG_INF, logits)
    v1 = jnp.max(rest, axis=1, keepdims=True)
    e1 = jnp.min(jnp.where(rest == v1, e_iota, float(n_e)), axis=1, keepdims=True)
    tt = jnp.exp(v1 - v0)
    g0 = 1.0 / (1.0 + tt)
    g1 = tt / (1.0 + tt)
    hit0 = e_iota == e0
    hit1 = e_iota == e1
    onehot = jnp.where(valid & (hit0 | hit1), 1.0, 0.0)
    rank = _dot(tri_ref[...], onehot.astype(BF16)) + base
    r0 = jnp.sum(jnp.where(hit0, rank, 0.0), axis=1, keepdims=True)
    r1 = jnp.sum(jnp.where(hit1, rank, 0.0), axis=1, keepdims=True)
    cols = (jnp.where(valid, e0, -1).astype(F32), jnp.where(valid, e1, -1).astype(F32), r0, r1, g0, g1)
    info = jnp.zeros(logits.shape, F32)
    for k, col in enumerate(cols):
        info = jnp.where(e_iota == k, col, info)
    new_base = base + jnp.sum(onehot, axis=0, keepdims=True)
    return h2, u2, info, new_base, u1


def _pool_frames_kernel(n_fb, h_ref, hist_ref, gm_ref, pw_ref, ps_ref, gf_ref, wr_ref, br_ref, tri_ref,
                        h2_ref, u2_ref, info_ref, cum_ref, tot_ref, state_ref, ext_ref, base_ref):
    i = pl.program_id(0)

    @pl.when(i == 0)
    def _():
        base_ref[...] = jnp.zeros_like(base_ref)

    @pl.when(i >= n_fb)
    def _():
        h2_ref[...] = jnp.zeros_like(h2_ref)
        u2_ref[...] = jnp.zeros_like(u2_ref)
        info_ref[...] = jnp.zeros_like(info_ref)

    @pl.when(i < n_fb)
    def _():
        ext_ref[:HIST_ROWS, :] = _rms(hist_ref[...], gm_ref[...])
        base = base_ref[...]
        cum_ref[0] = base
        h2, u2, info, new_base, u1 = _pool_route(h_ref[...], ext_ref, None, True, base, gm_ref, pw_ref, ps_ref,
                                                 gf_ref, wr_ref, br_ref, tri_ref)
        h2_ref[...] = h2
        u2_ref[...] = u2.astype(BF16)
        info_ref[...] = info
        base_ref[...] = new_base
        tot_ref[...] = new_base
        state_ref[0] = u1[u1.shape[0] - HIST_ROWS:, :]


def _pool_short_kernel(n_sample, n_valid, h2_hbm, u2_hbm, info_hbm, h_ref, hist_ref, base0_ref, gm_ref, pw_ref, ps_ref,
                       gf_ref, wr_ref, br_ref, tri_ref, h2_ref, u2_ref, info_ref, cum_ref, tot_ref, state_ref,
                       ext_ref, base_ref):
    del h2_hbm, u2_hbm, info_hbm
    s_id = pl.program_id(0)

    @pl.when(s_id == 0)
    def _():
        base_ref[...] = base0_ref[...]

    ext_ref[:HIST_ROWS, :] = hist_ref[0]
    base = base_ref[...]
    cum_ref[0] = base
    pos1 = (lax.broadcasted_iota(jnp.int32, (SEQ_S, 1), 0) + 1).astype(F32)
    cnt_rows = jnp.where(s_id < n_sample, float(max(POOL_WINDOWS)), pos1)
    h2, u2, info, new_base, u1 = _pool_route(h_ref[...], ext_ref, cnt_rows, s_id < n_valid, base, gm_ref, pw_ref,
                                             ps_ref, gf_ref, wr_ref, br_ref, tri_ref)
    h2_ref[...] = h2
    u2_ref[...] = u2.astype(BF16)
    info_ref[...] = info
    base_ref[...] = new_base
    tot_ref[...] = new_base
    state_ref[0] = u1


def _expert_kernel(n_blocks, be_ref, lo_ref, hi_ref, ok_ref, u_hbm, pos_ref, gate_ref, wg_ref, wu_ref, wd_ref, ys_ref,
                   buf_ref, sel_ref, gsel_ref, acc_ref, gacc_ref, sem):
    del be_ref
    j = pl.program_id(0)
    rb = ys_ref.shape[0]
    n_slots, tc, _ = buf_ref.shape
    last_chunk = pos_ref.shape[0] - 1

    def n_chunks(jj):
        return jnp.where(ok_ref[jj] != 0, hi_ref[jj] - lo_ref[jj] + 1, 0)

    def chunk_copy(c, slot):
        return pltpu.make_async_copy(u_hbm.at[pl.ds(pl.multiple_of(c * tc, tc), tc), :], buf_ref.at[slot],
                                     sem.at[slot])

    def request(jj):
        n_req = jnp.minimum(n_chunks(jj), n_slots)
        for k in range(n_slots):
            @pl.when(k < n_req)
            def _():
                chunk_copy(lo_ref[jj] + k, k).start()

    def select(jj, c, live):
        rows = jj * rb + lax.broadcasted_iota(jnp.int32, (rb, tc), 0)
        pos = pos_ref[c]
        gate = gate_ref[c]
        m0 = (rows == pos[0:1, :]) & live
        m1 = (rows == pos[1:2, :]) & live
        sel = jnp.where(m0 | m1, 1.0, 0.0).astype(BF16)
        row_gate = jnp.sum(jnp.where(m0, gate[0:1, :], 0.0) + jnp.where(m1, gate[1:2, :], 0.0), axis=1, keepdims=True)
        return sel, row_gate

    def build_selection(jj):
        n_req = jnp.minimum(n_chunks(jj), n_slots)
        row_gate = jnp.zeros(gsel_ref.shape, F32)
        for k in range(n_slots):
            sel, g = select(jj, jnp.minimum(lo_ref[jj] + k, last_chunk), k < n_req)
            sel_ref[:, k * tc:(k + 1) * tc] = sel
            row_gate = row_gate + g
        gsel_ref[...] = row_gate

    @pl.when(j == 0)
    def _():
        buf_ref[...] = jnp.zeros_like(buf_ref)
        request(0)
        build_selection(0)

    n = n_chunks(j)
    lo = lo_ref[j]

    @pl.when(n > 0)
    def _():
        for k in range(n_slots):
            @pl.when(k < jnp.minimum(n, n_slots))
            def _():
                chunk_copy(lo + k, k).wait()

        acc_ref[...] = _dot(sel_ref[...], buf_ref[...].reshape(n_slots * tc, -1))
        gacc_ref[...] = gsel_ref[...]

        def overflow(k, carry):
            copy = chunk_copy(lo + k, 0)
            copy.start()
            copy.wait()
            sel, row_gate = select(j, lo + k, True)
            acc_ref[...] += _dot(sel, buf_ref[0])
            gacc_ref[...] += row_gate
            return carry

        lax.fori_loop(n_slots, jnp.maximum(n, n_slots), overflow, 0)

    @pl.when(j + 1 < n_blocks)
    def _():
        request(jnp.minimum(j + 1, n_blocks - 1))

    @pl.when(n > 0)
    def _():
        xg = acc_ref[...].astype(BF16)
        act = jax.nn.silu(_dot(xg, wg_ref[0])) * _dot(xg, wu_ref[0])
        ys_ref[...] = (_dot(act.astype(BF16), wd_ref[0]) * gacc_ref[...]).astype(BF16)
        build_selection(jnp.minimum(j + 1, n_blocks - 1))

    @pl.when(n == 0)
    def _():
        ys_ref[...] = jnp.zeros_like(ys_ref)


def _combine_kernel(n_blocks, n_fb, n_ref, ids_ref, h2_ref, pos_ref, g_ref, ys_hbm, yp_ref, yshort_ref, buf_ref, sem):
    i = pl.program_id(0)
    _, n_slots, cr, _ = buf_ref.shape
    per_vreg = LANE // cr
    par = i % 2
    nxt = jnp.minimum(i + 1, n_blocks - 1)

    def chunk_copy(ii, k, which):
        ch = ids_ref[ii * n_slots + k]
        return pltpu.make_async_copy(ys_hbm.at[pl.ds(pl.multiple_of(ch * cr, cr), cr), :], buf_ref.at[which, k],
                                     sem.at[which * n_slots + k])

    def request(ii, which):
        for k in range(n_slots):
            @pl.when(k < n_ref[ii])
            def _():
                chunk_copy(ii, k, which).start()

    @pl.when(i == 0)
    def _():
        buf_ref[...] = jnp.zeros_like(buf_ref)
        request(0, 0)

    @pl.when(i + 1 < n_blocks)
    def _():
        request(nxt, 1 - par)

    lane = lax.broadcasted_iota(jnp.int32, (1, LANE), 1)
    pieces = []
    for v in range(n_slots // per_vreg):
        row = jnp.zeros((1, LANE), jnp.int32)
        for q in range(per_vreg):
            k = v * per_vreg + q
            row = jnp.where(lane // cr == q, ids_ref[i * n_slots + k] * cr + lane % cr, row)
        pieces.append(row)
    slot_rows = jnp.concatenate(pieces, axis=1)
    sel = jnp.where((slot_rows == pos_ref[:, 0:1]) | (slot_rows == pos_ref[:, 1:2]), 1.0, 0.0).astype(BF16)

    for k in range(n_slots):
        @pl.when(k < n_ref[i])
        def _():
            chunk_copy(i, k, par).wait()

    moe = _dot(sel, buf_ref[par].reshape(n_slots * cr, -1))
    y = _rms(h2_ref[...] + moe, g_ref[...])

    @pl.when(i < n_fb)
    def _():
        yp_ref[...] = y

    @pl.when(i >= n_fb)
    def _():
        yshort_ref[...] = y


def _rope_table(pos):
    half = QK_ROPE // 2
    inv = ROPE_THETA ** (-jnp.arange(half, dtype=F32) / half)
    ang = pos.astype(F32)[:, None] * inv[None, :]
    cos = jnp.cos(ang)
    sin = jnp.sin(ang)
    zero = jnp.zeros((pos.shape[0], LANE - QK_ROPE), F32)
    return jnp.concatenate([cos, cos, zero, -sin, sin, zero], axis=1)


def _half_swap(w):
    half = QK_ROPE // 2
    return jnp.concatenate([w[..., half:], w[..., :half]], axis=-1)


def _pad_last(w, width):
    return jnp.pad(w, [(0, 0)] * (w.ndim - 1) + [(0, width - w.shape[-1])])


def kernel(x_prompt, x_sample, cache_kv_latent, cache_k_rope, state_pool, meta_tokens, norm_mix, norm_ffn, norm_final,
           mla_w_dq, mla_g_q, mla_w_uq, mla_w_dkv, mla_g_kv, mla_w_uk, mla_w_uv, mla_w_o, pool_w, pool_scale,
           ffn_w_gate, ffn_w_up, ffn_w_down, moe_w_router, moe_b_router, moe_w_gate, moe_w_up, moe_w_down):
    nb, seq, d = x_prompt.shape
    db, dseq, _ = x_sample.shape
    n_cache = cache_kv_latent.shape[2]
    q_rank = mla_w_dq.shape[2]
    kv_rank = mla_g_kv.shape[1]
    v_dim = mla_w_uv.shape[3]
    d_ff = ffn_w_gate.shape[2]
    n_e = moe_w_router.shape[2]
    d_e = moe_w_gate.shape[3]
    tb, rb, tq = TOKEN_BLOCK, ROW_BLOCK, ATTN_BLOCK
    assert norm_mix.shape[0] == 2 and cache_kv_latent.shape[0] == 1 and state_pool.shape[0] == 1
    assert dseq == SEQ_S and N_META == SEQ_S and meta_tokens.shape[0] == N_META
    assert kv_rank == 2 * LANE and QK_NOPE == LANE and QK_ROPE <= LANE and HIST_ROWS >= POOL_HIST
    assert seq % tb == 0 and tb == tq and tq % CHUNK == 0 and tb % SEQ_S == 0 and d % LANE == 0
    assert (n_cache - N_META) % CHUNK == 0 and dseq <= CHUNK

    nf = nb * seq
    n_valid_seq = db + 1
    ns = -(-(n_valid_seq * SEQ_S) // tb) * tb
    n_seq = ns // SEQ_S
    nt = nf + ns
    n_fb, n_sb, n_tb = nf // tb, ns // tb, nt // tb
    sb = seq // tb
    meta_row = nf + db * SEQ_S
    scale = float((QK_NOPE + QK_ROPE) ** -0.5)
    hq = 3 * LANE

    xp = x_prompt.reshape(nf, d)
    xs = jnp.concatenate([x_sample.reshape(db * SEQ_S, d), meta_tokens.astype(x_prompt.dtype),
                          jnp.zeros((ns - n_valid_seq * SEQ_S, d), x_prompt.dtype)], axis=0)

    t_s = jnp.arange(SEQ_S)
    pos_short = jnp.concatenate([jnp.tile(n_cache + t_s, db), jnp.tile(t_s, n_seq - db)])
    tab_frames = _rope_table(N_META + jnp.arange(seq))
    tab = jnp.concatenate([tab_frames, _rope_table(pos_short)], axis=0)

    wuq = mla_w_uq[0].reshape(q_rank, N_HEADS, QK_NOPE + QK_ROPE)
    wuq_pe = wuq[:, :, QK_NOPE:]
    wuq2 = jnp.concatenate([wuq[:, :, :QK_NOPE].reshape(q_rank, -1),
                            _pad_last(wuq_pe, LANE).reshape(q_rank, -1),
                            _pad_last(_half_swap(wuq_pe), LANE).reshape(q_rank, -1)], axis=1).astype(BF16)
    wuq_t = jnp.concatenate([wuq[:, :, :QK_NOPE].reshape(q_rank, -1), wuq_pe.reshape(q_rank, -1),
                             _half_swap(wuq_pe).reshape(q_rank, -1)], axis=1).T.astype(BF16)
    wdkv_r =mla_w_dkv[0][:, kv_rank:]
    wdkv2 = jnp.concatenate([mla_w_dkv[0][:, :kv_rank], _pad_last(wdkv_r, LANE),
                             _pad_last(_half_swap(wdkv_r), LANE)], axis=1).astype(BF16)
    wuk_t = jnp.transpose(mla_w_uk[0], (1, 2, 0)).astype(BF16)
    wuk_flat = mla_w_uk[0].reshape(kv_rank, N_HEADS * QK_NOPE).astype(BF16)
    wuv = jnp.transpose(mla_w_uv[0], (1, 0, 2)).astype(BF16)
    wuv_t = mla_w_uv[0].reshape(kv_rank, N_HEADS * v_dim).T.astype(BF16)
    row = lambda v: v.reshape(1, -1)
    hk = 2 * LANE
    hx = v_dim + BF16_SUBLANES

    tok_p = pl.BlockSpec((tb, d), lambda i: (jnp.minimum(i, n_fb - 1), 0))
    tok_s = pl.BlockSpec((tb, d), lambda i: (jnp.maximum(i - n_fb, 0), 0))
    short_tok = lambda i: (jnp.maximum(i - n_fb, 0), 0)
    q_short, qt_frames, kvb_short, kh_all, vt_all, c_all, r_all = pl.pallas_call(
        functools.partial(_qkv_kernel, n_fb, scale * LOG2_E),
        grid=(n_tb,),
        in_specs=[tok_p, tok_s,
                  pl.BlockSpec((tb, 2 * LANE), lambda i: (jnp.where(i < n_fb, i % sb, sb + i - n_fb), 0)),
                  pl.BlockSpec((2 * LANE, tb), lambda i: (0, i % sb)),
                  _const_spec((1, d)), _const_spec((d, q_rank)), _const_spec((1, q_rank)),
                  _const_spec(wuq2.shape), _const_spec(wuq_t.shape), _const_spec(wuk_t.shape),
                  _const_spec(wdkv2.shape), _const_spec((1, kv_rank)), _const_spec(wuk_flat.shape),
                  _const_spec(wuv_t.shape)],
        out_specs=[pl.BlockSpec((tb, N_HEADS * hq), short_tok),
                   pl.BlockSpec((1, N_HEADS, hk, tb), lambda i: (jnp.minimum(i, n_fb - 1), 0, 0, 0)),
                   pl.BlockSpec((tb, hq), short_tok),
                   pl.BlockSpec((tb, N_HEADS * hk), lambda i: (i, 0)),
                   pl.BlockSpec((1, N_HEADS * hx, tb), lambda i: (i, 0, 0)),
                   pl.BlockSpec((tb, kv_rank), lambda i: (i, 0)),
                   pl.BlockSpec((tb, QK_ROPE), lambda i: (i, 0))],
        out_shape=[jax.ShapeDtypeStruct((ns, N_HEADS * hq), BF16),
                   jax.ShapeDtypeStruct((n_fb, N_HEADS, hk, tb), BF16),
                   jax.ShapeDtypeStruct((ns, hq), BF16),
                   jax.ShapeDtypeStruct((nt, N_HEADS * hk), BF16),
                   jax.ShapeDtypeStruct((n_tb, N_HEADS * hx, tb), BF16),
                   jax.ShapeDtypeStruct((nt, kv_rank), F32), jax.ShapeDtypeStruct((nt, QK_ROPE), F32)],
        compiler_params=_params(("arbitrary",)),
        name="qkv",
    )(xp, xs, tab, tab_frames.T, row(norm_mix[0]), mla_w_dq[0].astype(BF16), row(mla_g_q[0]), wuq2, wuq_t, wuk_t,
      wdkv2, row(mla_g_kv[0]), wuk_flat, wuv_t)

    qb = seq // tq
    once = pl.Buffered(1)
    o_frames = pl.pallas_call(
        functools.partial(_attn_frames_kernel, qb, meta_row % tb),
        grid=(n_fb,),
        in_specs=[pl.BlockSpec((1, N_HEADS, hk, tq), lambda i: (i, 0, 0, 0)),
                  pl.BlockSpec((seq, N_HEADS * hk), lambda i: (i // qb, 0), pipeline_mode=once),
                  pl.BlockSpec((qb, N_HEADS * hx, tq), lambda i: (i // qb, 0, 0), pipeline_mode=once),
                  pl.BlockSpec((N_META, N_HEADS * hk), lambda i: (meta_row // N_META, 0), pipeline_mode=once),
                  pl.BlockSpec((1, N_HEADS * hx, tb), lambda i: (meta_row // tb, 0, 0), pipeline_mode=once)],
        out_specs=pl.BlockSpec((tq, N_HEADS * v_dim), lambda i: (i, 0)),
        out_shape=jax.ShapeDtypeStruct((nf, N_HEADS * v_dim), BF16),
        scratch_shapes=[pltpu.VMEM((N_HEADS, 1, tq), F32), pltpu.VMEM((N_HEADS, hx, tq), F32)],
        compiler_params=_params(("parallel",)),
        name="attn_frames",
    )(qt_frames, kh_all, vt_all, kh_all, vt_all)

    o_short = pl.pallas_call(
        functools.partial(_attn_short_kernel, db),
        grid=(n_seq,),
        in_specs=[pl.BlockSpec((SEQ_S, N_HEADS * hq), lambda s: (s, 0)),
                  pl.BlockSpec((SEQ_S, hq), lambda s: (s, 0)),
                  pl.BlockSpec((1, 1, n_cache, kv_rank), lambda s: (0, jnp.minimum(s, db - 1), 0, 0)),
                  pl.BlockSpec((1, 1, n_cache, QK_ROPE), lambda s: (0, jnp.minimum(s, db - 1), 0, 0)),
                  _const_spec(wuv.shape)],
        out_specs=pl.BlockSpec((SEQ_S, N_HEADS * v_dim), lambda s: (s, 0)),
        out_shape=jax.ShapeDtypeStruct((ns, N_HEADS * v_dim), BF16),
        compiler_params=_params(("parallel",)),
        name="attn_short",
    )(q_short, kvb_short, cache_kv_latent, cache_k_rope, wuv)

    moe_f32 = [moe_w_gate[0].reshape(n_e * d, d_e), moe_w_up[0].reshape(n_e * d, d_e),
               moe_w_down[0].reshape(n_e * d_e, d)]
    cast_in, cast_out, cast_shapes = [], [], []
    for w in moe_f32:
        steps = max(s for s in range(1, n_tb + 1) if w.shape[0] % s == 0 and (w.shape[0] // s) % BF16_SUBLANES == 0)
        spec = pl.BlockSpec((w.shape[0] // steps, w.shape[1]), lambda i, last=steps - 1: (jnp.minimum(i, last), 0))
        cast_in.append(spec)
        cast_out.append(spec)
        cast_shapes.append(jax.ShapeDtypeStruct(w.shape, BF16))
    h1, wg_e, wu_e, wd_e = pl.pallas_call(
        functools.partial(_proj_ffn_kernel, n_fb),
        grid=(n_tb,),
        in_specs=[tok_p, tok_s,
                  pl.BlockSpec((tb, N_HEADS * v_dim), lambda i: (jnp.minimum(i, n_fb - 1), 0)),
                  pl.BlockSpec((tb, N_HEADS * v_dim), short_tok),
                  _const_spec((N_HEADS * v_dim, d)), _const_spec((1, d)),
                  _const_spec((d, d_ff)), _const_spec((d, d_ff)), _const_spec((d_ff, d))] + cast_in,
        out_specs=[pl.BlockSpec((tb, d), lambda i: (i, 0))] + cast_out,
        out_shape=[jax.ShapeDtypeStruct((nt, d), F32)] + cast_shapes,
        compiler_params=_params(("arbitrary",)),
        name="proj_ffn",
    )(xp, xs, o_frames, o_short, mla_w_o[0].astype(BF16), row(norm_ffn[0]),
      ffn_w_gate[0].astype(BF16), ffn_w_up[0].astype(BF16), ffn_w_down[0].astype(BF16), *moe_f32)
    wg_e = wg_e.reshape(n_e, d, d_e)
    wu_e = wu_e.reshape(n_e, d, d_e)
    wd_e = wd_e.reshape(n_e, d_e, d)

    route_w = [_const_spec((1, d)), _const_spec(pool_w.shape[1:]), _const_spec((1, d)), _const_spec((1, d)),
               _const_spec((d, 2 * n_e)), _const_spec((1, n_e))]
    wr_hi = moe_w_router[0].astype(BF16)
    wr_lo = (moe_w_router[0].astype(F32) - wr_hi.astype(F32)).astype(BF16)
    route_args = (row(norm_mix[1]), pool_w[0].astype(BF16), row(pool_scale[0]), row(norm_ffn[1]),
                  jnp.concatenate([wr_hi, wr_lo], axis=1), row(moe_b_router[0]))
    tri = lambda n: (jnp.arange(n)[:, None] > jnp.arange(n)[None, :]).astype(BF16)
    hist_blk = lambda i: jnp.where(i % sb == 0, meta_row // HIST_ROWS, i * (tb // HIST_ROWS) - 1)
    h2_f, u2_f, info_f, cum_f, tot_f, state_f = pl.pallas_call(
        functools.partial(_pool_frames_kernel, n_fb),
        grid=(n_tb,),
        in_specs=[pl.BlockSpec((tb, d), lambda i: (i, 0)),
                  pl.BlockSpec((HIST_ROWS, d), lambda i: (hist_blk(i), 0))] + route_w + [_const_spec((tb, tb))],
        out_specs=[pl.BlockSpec((tb, d), lambda i: (i, 0)), pl.BlockSpec((tb, d), lambda i: (i, 0)),
                   pl.BlockSpec((tb, n_e), lambda i: (i, 0)),
                   pl.BlockSpec((1, 1, n_e), lambda i: (jnp.minimum(i, n_fb - 1), 0, 0)),
                   pl.BlockSpec((1, n_e), lambda i: (0, 0)),
                   pl.BlockSpec((1, HIST_ROWS, d), lambda i: (jnp.minimum(i // sb, nb - 1), 0, 0))],
        out_shape=[jax.ShapeDtypeStruct((nt, d), F32), jax.ShapeDtypeStruct((nt, d), BF16),
                   jax.ShapeDtypeStruct((nt, n_e), F32), jax.ShapeDtypeStruct((n_fb, 1, n_e), F32),
                   jax.ShapeDtypeStruct((1, n_e), F32), jax.ShapeDtypeStruct((nb, HIST_ROWS, d), F32)],
        scratch_shapes=[pltpu.VMEM((HIST_ROWS + tb, d), F32), pltpu.VMEM((1, n_e), F32)],
        compiler_params=_params(("arbitrary",)),
        name="pool_route_frames",
    )(h1, h1, *route_args, tri(tb))

    hist_s = jnp.concatenate([
        jnp.pad(state_pool[0].astype(F32), [(0, 0), (HIST_ROWS - POOL_HIST, 0), (0, 0)]),
        jnp.zeros((n_seq - db, HIST_ROWS, d), F32)], axis=0)
    any_spec = pl.BlockSpec(memory_space=pl.ANY)
    short_blk = lambda s: (nf // SEQ_S + s, 0)
    h2, u2, info, cum_s, tot_s, state_s = pl.pallas_call(
        functools.partial(_pool_short_kernel, db, n_valid_seq),
        grid=(n_seq,),
        in_specs=[any_spec, any_spec, any_spec,
                  pl.BlockSpec((SEQ_S, d), short_blk),
                  pl.BlockSpec((1, HIST_ROWS, d), lambda s: (s, 0, 0)),
                  _const_spec((1, n_e))] + route_w + [_const_spec((SEQ_S, SEQ_S))],
        out_specs=[pl.BlockSpec((SEQ_S, d), short_blk), pl.BlockSpec((SEQ_S, d), short_blk),
                   pl.BlockSpec((SEQ_S, n_e), short_blk), pl.BlockSpec((1, 1, n_e), lambda s: (s, 0, 0)),
                   pl.BlockSpec((1, n_e), lambda s: (0, 0)),
                   pl.BlockSpec((1, SEQ_S, d), lambda s: (s, 0, 0))],
        out_shape=[jax.ShapeDtypeStruct((nt, d), F32), jax.ShapeDtypeStruct((nt, d), BF16),
                   jax.ShapeDtypeStruct((nt, n_e), F32), jax.ShapeDtypeStruct((n_seq, 1, n_e), F32),
                   jax.ShapeDtypeStruct((1, n_e), F32), jax.ShapeDtypeStruct((n_seq, SEQ_S, d), F32)],
        scratch_shapes=[pltpu.VMEM((HIST_ROWS + SEQ_S, d), F32), pltpu.VMEM((1, n_e), F32)],
        input_output_aliases={0: 0, 1: 1, 2: 2},
        compiler_params=_params(("arbitrary",)),
        name="pool_route_short",
    )(h2_f, u2_f, info_f, h1, hist_s, tot_f, *route_args, tri(SEQ_S))

    counts = tot_s[0].astype(jnp.int32)
    cum = jnp.concatenate([cum_f[:, 0], cum_s[::tb // SEQ_S, 0], tot_s], axis=0).astype(jnp.int32)
    padded = (counts + rb - 1) // rb * rb
    pend = jnp.cumsum(padded)
    pstart = pend - padded
    n_rows_max = -(-(TOP_K * (nf + n_valid_seq * SEQ_S)) // rb) * rb + n_e * rb
    n_rb = n_rows_max // rb
    e_tok = info[:, 0:TOP_K].astype(jnp.int32)
    pos = jnp.where(e_tok >= 0, pstart[jnp.maximum(e_tok, 0)] + info[:, 2:2 + TOP_K].astype(jnp.int32), -1)
    gates = info[:, 4:4 + TOP_K]
    pos_l = pos.reshape(n_tb, tb, TOP_K).transpose(0, 2, 1)
    gate_l = gates.reshape(n_tb, tb, TOP_K).transpose(0, 2, 1)
    pos_c = jnp.pad(pos, [(0, 0), (0, n_e - TOP_K)], constant_values=-1)

    blk_row = jnp.arange(n_rb, dtype=jnp.int32) * rb
    blk_e = jnp.minimum(jnp.sum(pend[None, :] <= blk_row[:, None], axis=1), n_e - 1).astype(jnp.int32)
    blk_ok = (blk_row < pend[-1]).astype(jnp.int32)
    r_lo = blk_row - pstart[blk_e]
    r_hi = jnp.minimum(r_lo + rb, counts[blk_e])
    cum_b = cum.T[blk_e]
    first = jnp.sum(cum_b[:, 1:] <= r_lo[:, None], axis=1)
    last = jnp.sum(cum_b[:, :-1] < r_hi[:, None], axis=1) - 1
    blk_lo = jnp.clip(first, 0, n_tb - 1).astype(jnp.int32)
    blk_hi = jnp.clip(last, blk_lo, n_tb - 1).astype(jnp.int32)

    ys = pl.pallas_call(
        functools.partial(_expert_kernel, n_rb),
        grid_spec=pltpu.PrefetchScalarGridSpec(
            num_scalar_prefetch=4,
            grid=(n_rb,),
            in_specs=[pl.BlockSpec(memory_space=pl.ANY),
                      _const_spec((n_tb, TOP_K, tb)), _const_spec((n_tb, TOP_K, tb)),
                      pl.BlockSpec((1, d, d_e), lambda j, be, lo, hi, ok: (be[j], 0, 0)),
                      pl.BlockSpec((1, d, d_e), lambda j, be, lo, hi, ok: (be[j], 0, 0)),
                      pl.BlockSpec((1, d_e, d), lambda j, be, lo, hi, ok: (be[j], 0, 0))],
            out_specs=pl.BlockSpec((rb, d), lambda j, be, lo, hi, ok: (j, 0)),
            scratch_shapes=[pltpu.VMEM((GATHER_SLOTS, tb, d), BF16), pltpu.VMEM((rb, GATHER_SLOTS * tb), BF16),
                            pltpu.VMEM((rb, 1), F32), pltpu.VMEM((rb, d), F32), pltpu.VMEM((rb, 1), F32),
                            pltpu.SemaphoreType.DMA((GATHER_SLOTS,))]),
        out_shape=jax.ShapeDtypeStruct((n_rows_max, d), BF16),
        compiler_params=_params(("arbitrary",)),
        name="experts",
    )(blk_e, blk_lo, blk_hi, blk_ok, u2, pos_l, gate_l, wg_e, wu_e, wd_e)

    cr = COMBINE_ROWS
    per_expert = tb // cr + 1
    n_slots = TOP_K * tb // cr + 2 * n_e
    assert rb % cr == 0 and LANE % cr == 0 and n_slots % (LANE // cr) == 0
    w_lo = pstart[None, :] + cum[:-1]
    w_hi = pstart[None, :] + cum[1:]
    c_lo = w_lo // cr
    n_chunks = jnp.where(w_hi > w_lo, (w_hi - 1) // cr - c_lo + 1, 0)
    q = jnp.arange(per_expert)
    cand = (c_lo[:, :, None] + q).reshape(n_tb, n_e * per_expert)
    keep = (q < n_chunks[:, :, None]).reshape(n_tb, n_e * per_expert)
    order = jnp.argsort(~keep, axis=1, stable=True)[:, :n_slots]
    ids = jnp.where(jnp.take_along_axis(keep, order, axis=1), jnp.take_along_axis(cand, order, axis=1),
                    n_rows_max // cr)
    n_ids = jnp.sum(keep, axis=1).astype(jnp.int32)
    y_frames, y_short = pl.pallas_call(
        functools.partial(_combine_kernel, n_tb, n_fb),
        grid_spec=pltpu.PrefetchScalarGridSpec(
            num_scalar_prefetch=2,
            grid=(n_tb,),
            in_specs=[pl.BlockSpec((tb, d), lambda i, n, ids: (i, 0)),
                      pl.BlockSpec((tb, n_e), lambda i, n, ids: (i, 0)),
                      pl.BlockSpec((1, d), lambda i, n, ids: (0, 0)),
                      pl.BlockSpec(memory_space=pl.ANY)],
            out_specs=[pl.BlockSpec((tb, d), lambda i, n, ids: (jnp.minimum(i, n_fb - 1), 0)),
                       pl.BlockSpec((tb, d), lambda i, n, ids: (jnp.maximum(i - n_fb, 0), 0))],
            scratch_shapes=[pltpu.VMEM((2, n_slots, cr, d), BF16), pltpu.SemaphoreType.DMA((2 * n_slots,))]),
        out_shape=[jax.ShapeDtypeStruct((nf, d), F32), jax.ShapeDtypeStruct((ns, d), F32)],
        compiler_params=_params(("arbitrary",)),
        name="combine",
    )(n_ids, ids.reshape(-1).astype(jnp.int32), h2, pos_c, row(norm_final), ys)

    y_prompt = y_frames.reshape(nb, seq, d)
    y_sample = y_short[:db * SEQ_S].reshape(db, SEQ_S, d)

    def with_meta(a, width):
        meta = jnp.broadcast_to(a[meta_row:meta_row + N_META][None], (nb, N_META, width))
        return jnp.concatenate([meta, a[:nf].reshape(nb, seq, width)], axis=1)[None]

    c_p = with_meta(c_all, kv_rank)
    r_p = with_meta(r_all, QK_ROPE)
    c_s = c_all[nf:nf + db * SEQ_S].reshape(1, db, SEQ_S, kv_rank)
    r_s = r_all[nf:nf + db * SEQ_S].reshape(1, db, SEQ_S, QK_ROPE)
    s_p = state_f[:, HIST_ROWS - POOL_HIST:][None]
    s_s = state_s[:db, SEQ_S - POOL_HIST:][None]
    return (y_prompt, y_sample, c_p, r_p, s_p, c_s, r_s, s_s)
```

```python
import functools

import jax
import jax.numpy as jnp
from jax import lax
from jax.experimental import pallas as pl
from jax.experimental.pallas import tpu as pltpu

CHUNK = 64
N_META = 16
N_HEADS = 8
QK_NOPE = 128
QK_ROPE = 64
ROPE_THETA = 10000.0
POOL_WINDOWS = (2, 4, 8, 16)
POOL_HIST = max(POOL_WINDOWS) - 1
TOP_K = 2
RMS_EPS = 1e-6

LANE = 128
BF16_SUBLANES = 16
SEQ_S = 16
HIST_ROWS = 16
TOKEN_BLOCK = 256
ROW_BLOCK = 256
ATTN_BLOCK = 256
CACHE_BLOCK = 512
KEY_BLOCKS_PER_ITER = 4
SCORES_AHEAD = 6
COMBINE_ROWS = 64
GATHER_SLOTS = 6
VMEM_LIMIT = 56 * 1024 * 1024

F32 = jnp.float32
BF16 = jnp.bfloat16
NEG_INF = float("-inf")
LOG2_E = 1.4426950408889634


def _dot(a, b):
    return jnp.dot(a, b, preferred_element_type=F32)


def _dot_nt(a, b):
    return lax.dot_general(a, b, (((1,), (1,)), ((), ())), preferred_element_type=F32)


def _rms(x, g):
    return x * lax.rsqrt(jnp.mean(x * x, axis=-1, keepdims=True) + RMS_EPS) * g


def _const_spec(shape):
    nd = len(shape)
    return pl.BlockSpec(shape, lambda *_: (0,) * nd, pipeline_mode=pl.Buffered(1))


def _params(sem):
    return pltpu.CompilerParams(dimension_semantics=sem, vmem_limit_bytes=VMEM_LIMIT)


def _qkv_kernel(n_fb, scale, xp_ref, xs_ref, tab_ref, tabt_ref, g_ref, wdq_ref, gq_ref, wuq_ref, wuqt_ref, wukt_ref,
                wdkv_ref, gkv_ref, wukf_ref, wuvt_ref, q_ref, qt_ref, kvb_ref, kn_ref, kr_ref, vt_ref, c_ref, r_ref):
    i = pl.program_id(0)
    x = jnp.where(i < n_fb, xp_ref[...], xs_ref[...])
    u = _rms(x, g_ref[...]).astype(BF16)
    cq = _rms(_dot(u, wdq_ref[...]), gq_ref[...]).astype(BF16)
    cos = tab_ref[:, :LANE]
    sin = tab_ref[:, LANE:]
    hq = QK_NOPE + 2 * LANE

    kv = _dot(u, wdkv_ref[...])
    c = _rms(kv[:, :2 * LANE], gkv_ref[...])
    r = kv[:, 2 * LANE:3 * LANE] * cos + kv[:, 3 * LANE:] * sin
    c_ref[...] = c
    r_ref[...] = r[:, :QK_ROPE]
    c_bf = c.astype(BF16)
    r_bf = r.astype(BF16)

    kn_ref[...] = _dot(c_bf, wukf_ref[...]).astype(BF16)
    kr_ref[...] = r_bf
    vt = _dot_nt(wuvt_ref[...], c_bf).astype(BF16)
    hv = vt.shape[0] // N_HEADS
    hv_ext = vt_ref.shape[1] // N_HEADS
    for h in range(N_HEADS):
        vt_ref[0, h * hv_ext:h * hv_ext + hv, :] = vt[h * hv:(h + 1) * hv, :]
        vt_ref[0, h * hv_ext + hv:(h + 1) * hv_ext, :] = jnp.ones((hv_ext - hv, vt.shape[1]), BF16)

    @pl.when(i < n_fb)
    def _():
        qat = _dot_nt(wuqt_ref[...], cq)
        cos_t = tabt_ref[:QK_ROPE, :]
        sin_t = tabt_ref[LANE:LANE + QK_ROPE, :]
        pe0 = N_HEADS * QK_NOPE
        sw0 = pe0 + N_HEADS * QK_ROPE
        for h in range(N_HEADS):
            qt_ref[0, h, :QK_NOPE, :] = (qat[h * QK_NOPE:(h + 1) * QK_NOPE, :] * scale).astype(BF16)
            a = qat[pe0 + h * QK_ROPE:pe0 + (h + 1) * QK_ROPE, :]
            b = qat[sw0 + h * QK_ROPE:sw0 + (h + 1) * QK_ROPE, :]
            qt_ref[0, h, QK_NOPE:QK_NOPE + QK_ROPE, :] = ((a * cos_t + b * sin_t) * scale).astype(BF16)
            qt_ref[0, h, QK_NOPE + QK_ROPE:, :] = jnp.zeros((LANE - QK_ROPE, qt_ref.shape[3]), BF16)

    @pl.when(i >= n_fb)
    def _():
        qa = _dot(cq, wuq_ref[...])
        for h in range(N_HEADS):
            qn = qa[:, h * LANE:(h + 1) * LANE].astype(BF16)
            q_ref[:, h * hq:h * hq + 2 * LANE] = (_dot(qn, wukt_ref[h]) * scale).astype(BF16)
            a = qa[:, (N_HEADS + h) * LANE:(N_HEADS + h + 1) * LANE]
            b = qa[:, (2 * N_HEADS + h) * LANE:(2 * N_HEADS + h + 1) * LANE]
            q_ref[:, h * hq + 2 * LANE:(h + 1) * hq] = ((a * cos + b * sin) * scale).astype(BF16)
        kvb_ref[:, :2 * LANE] = c_bf
        kvb_ref[:, 2 * LANE:] = r_bf


def _attn_frames_kernel(qb, meta_col, qt_ref, kn_ref, kr_ref, vt_ref, knm_ref, krm_ref, vtm_ref, o_ref, m_ref, acc_ref):
    j = pl.program_id(0) % qb
    tq = o_ref.shape[0]
    hn = kn_ref.shape[1] // N_HEADS
    hv = o_ref.shape[1] // N_HEADS
    hx = vt_ref.shape[1] // N_HEADS

    def head_keys(kn, kr, rows, h):
        return jnp.concatenate([kn[rows, h * hn:(h + 1) * hn], kr[rows, :]], axis=1)

    def scores(h, rows):
        return _dot(head_keys(kn_ref, kr_ref, rows, h), qt_ref[0, h])

    m_ref[...] = jnp.full(m_ref.shape, NEG_INF, F32)
    acc_ref[...] = jnp.zeros_like(acc_ref)

    def unmasked(kbs):
        steps = [(kb, h) for kb in kbs for h in range(N_HEADS)]
        block_scores = lambda kb, h: scores(h, pl.ds(pl.multiple_of(kb * tq, tq), tq))
        ahead = [block_scores(*step) for step in steps[:SCORES_AHEAD]]
        for n, (kb, h) in enumerate(steps):
            s = ahead.pop(0)
            if n + SCORES_AHEAD < len(steps):
                ahead.append(block_scores(*steps[n + SCORES_AHEAD]))
            m_old = m_ref[h]
            m_new = jnp.maximum(m_old, jnp.max(s, axis=0, keepdims=True))
            alpha = jnp.exp2(m_old - m_new)
            p = jnp.exp2(s - m_new)
            m_ref[h] = m_new
            acc_ref[h] = alpha * acc_ref[h] + _dot(vt_ref[kb, h * hx:(h + 1) * hx, :], p.astype(BF16))

    def body(t, carry):
        unmasked([KEY_BLOCKS_PER_ITER * t + u for u in range(KEY_BLOCKS_PER_ITER)])
        return carry

    lax.fori_loop(0, j // KEY_BLOCKS_PER_ITER, body, 0)
    for left in range(1, KEY_BLOCKS_PER_ITER):
        @pl.when(j % KEY_BLOCKS_PER_ITER == left)
        def _():
            unmasked([j - left + u for u in range(left)])

    rows = pl.ds(pl.multiple_of(j * tq, tq), tq)
    visible = (lax.broadcasted_iota(jnp.int32, (tq, tq), 0) // CHUNK
               <= lax.broadcasted_iota(jnp.int32, (tq, tq), 1) // CHUNK)

    def last_scores(h):
        return scores(h, rows), _dot(head_keys(knm_ref, krm_ref, slice(None), h), qt_ref[0, h])

    ahead = [last_scores(h) for h in range(SCORES_AHEAD)]
    for h in range(N_HEADS):
        s, s_meta = ahead.pop(0)
        s = jnp.where(visible, s, NEG_INF)
        if h + SCORES_AHEAD < N_HEADS:
            ahead.append(last_scores(h + SCORES_AHEAD))
        m_old = m_ref[h]
        m_new = jnp.maximum(m_old, jnp.maximum(jnp.max(s, axis=0, keepdims=True),
                                               jnp.max(s_meta, axis=0, keepdims=True)))
        alpha = jnp.exp2(m_old - m_new)
        p = jnp.exp2(s - m_new)
        p_meta = jnp.exp2(s_meta - m_new)
        acc = (alpha * acc_ref[h] + _dot(vt_ref[j, h * hx:(h + 1) * hx, :], p.astype(BF16))
               + _dot(vtm_ref[0, h * hx:(h + 1) * hx, meta_col:meta_col + N_META], p_meta.astype(BF16)))
        o_ref[:, h * hv:(h + 1) * hv] = (acc[:hv] * (1.0 / acc[hv:hv + 1])).T.astype(BF16)


def _attn_short_kernel(n_cached_seq, q_ref, kvn_ref, cc_ref, cr_ref, wuv_ref, o_ref):
    s_id = pl.program_id(0)
    hq = q_ref.shape[1] // N_HEADS
    n_cache = cc_ref.shape[2]
    qs = jnp.concatenate([q_ref[:, h * hq:(h + 1) * hq] for h in range(N_HEADS)], axis=0)
    q_lat = qs[:, :2 * LANE]
    q_pe = qs[:, 2 * LANE:2 * LANE + QK_ROPE]
    kvn = kvn_ref[...]
    s = _dot_nt(qs, kvn)
    m = jnp.max(s, axis=1, keepdims=True)
    p = jnp.exp2(s - m)
    l = jnp.sum(p, axis=1, keepdims=True)
    acc = _dot(p.astype(BF16), kvn[:, :2 * LANE])
    has_cache = s_id < n_cached_seq
    start = 0
    while start < n_cache:
        size = min(CACHE_BLOCK, n_cache - start)
        ck = cc_ref[0, 0, start:start + size, :].astype(BF16)
        rk = cr_ref[0, 0, start:start + size, :].astype(BF16)
        s = _dot_nt(q_lat, ck) + _dot_nt(q_pe, rk)
        s = jnp.where(has_cache, s, NEG_INF)
        m_new = jnp.maximum(m, jnp.max(s, axis=1, keepdims=True))
        alpha = jnp.exp2(m - m_new)
        p = jnp.exp2(s - m_new)
        l = alpha * l + jnp.sum(p, axis=1, keepdims=True)
        acc = alpha * acc + _dot(p.astype(BF16), ck)
        m = m_new
        start += size
    o = (acc / l).astype(BF16)
    hv = wuv_ref.shape[2]
    for h in range(N_HEADS):
        o_ref[:, h * hv:(h + 1) * hv] = _dot(o[h * SEQ_S:(h + 1) * SEQ_S, :], wuv_ref[h]).astype(BF16)


def _proj_ffn_kernel(n_fb, xp_ref, xs_ref, op_ref, os_ref, wo_ref, g_ref, wg_ref, wu_ref, wd_ref, *cast_refs):
    n_cast = (len(cast_refs) - 1) // 2
    h_ref = cast_refs[n_cast]
    for src, dst in zip(cast_refs[:n_cast], cast_refs[n_cast + 1:]):
        dst[...] = src[...].astype(BF16)
    i = pl.program_id(0)
    x = jnp.where(i < n_fb, xp_ref[...], xs_ref[...])
    o = jnp.where(i < n_fb, op_ref[...], os_ref[...])
    h1 = x + _dot(o, wo_ref[...])
    u = _rms(h1, g_ref[...]).astype(BF16)
    act = jax.nn.silu(_dot(u, wg_ref[...])) * _dot(u, wu_ref[...])
    h_ref[...] = h1 + _dot(act.astype(BF16), wd_ref[...])


def _pool_route(h1, ext_ref, cnt_rows, valid, base, gm_ref, pw_ref, ps_ref, gf_ref, wr_ref, br_ref, tri_ref):
    t = h1.shape[0]
    grp = h1.shape[1] // len(POOL_WINDOWS)
    u1 = _rms(h1, gm_ref[...])
    ext_ref[HIST_ROWS:HIST_ROWS + t, :] = u1
    ys = []
    for g, w in enumerate(POOL_WINDOWS):
        cols = slice(g * grp, (g + 1) * grp)
        acc = u1[:, cols]
        for k in range(1, w):
            acc = acc + ext_ref[HIST_ROWS - k:HIST_ROWS - k + t, cols]
        cnt = float(w) if cnt_rows is None else jnp.minimum(float(w), cnt_rows)
        mean = acc / cnt - u1[:, cols]
        ys.append(_dot(mean.astype(BF16), pw_ref[g]))
    h2 = h1 + jnp.concatenate(ys, axis=1) * ps_ref[...]
    u2 = _rms(h2, gf_ref[...])

    n_e = wr_ref.shape[1] // 2
    u2_hi = u2.astype(BF16)
    u2_lo = (u2 - u2_hi.astype(F32)).astype(BF16)
    parts = _dot(u2_hi, wr_ref[...]) + _dot(u2_lo, wr_ref[...])
    logits = parts[:, :n_e] + parts[:, n_e:] + br_ref[...]
    e_iota = lax.broadcasted_iota(jnp.int32, logits.shape, 1).astype(F32)
    v0 = jnp.max(logits, axis=1, keepdims=True)
    e0 = jnp.min(jnp.where(logits == v0, e_iota, float(n_e)), axis=1, keepdims=True)
    rest = jnp.where(e_iota == e0, NEG_INF, logits)
    v1 = jnp.max(rest, axis=1, keepdims=True)
    e1 = jnp.min(jnp.where(rest == v1, e_iota, float(n_e)), axis=1, keepdims=True)
    tt = jnp.exp(v1 - v0)
    g0 = 1.0 / (1.0 + tt)
    g1 = tt / (1.0 + tt)
    hit0 = e_iota == e0
    hit1 = e_iota == e1
    onehot = jnp.where(valid & (hit0 | hit1), 1.0, 0.0)
    rank = _dot(tri_ref[...], onehot.astype(BF16)) + base
    r0 = jnp.sum(jnp.where(hit0, rank, 0.0), axis=1, keepdims=True)
    r1 = jnp.sum(jnp.where(hit1, rank, 0.0), axis=1, keepdims=True)
    cols = (jnp.where(valid, e0, -1).astype(F32), jnp.where(valid, e1, -1).astype(F32), r0, r1, g0, g1)
    info = jnp.zeros(logits.shape, F32)
    for k, col in enumerate(cols):
        info = jnp.where(e_iota == k, col, info)
    new_base = base + jnp.sum(onehot, axis=0, keepdims=True)
    return h2, u2, info, new_base, u1


def _pool_frames_kernel(n_fb, h_ref, hist_ref, gm_ref, pw_ref, ps_ref, gf_ref, wr_ref, br_ref, tri_ref,
                        h2_ref, u2_ref, info_ref, cum_ref, tot_ref, state_ref, ext_ref, base_ref):
    i = pl.program_id(0)

    @pl.when(i == 0)
    def _():
        base_ref[...] = jnp.zeros_like(base_ref)

    @pl.when(i >= n_fb)
    def _():
        h2_ref[...] = jnp.zeros_like(h2_ref)
        u2_ref[...] = jnp.zeros_like(u2_ref)
        info_ref[...] = jnp.zeros_like(info_ref)

    @pl.when(i < n_fb)
    def _():
        ext_ref[:HIST_ROWS, :] = _rms(hist_ref[...], gm_ref[...])
        base = base_ref[...]
        cum_ref[0] = base
        h2, u2, info, new_base, u1 = _pool_route(h_ref[...], ext_ref, None, True, base, gm_ref, pw_ref, ps_ref,
                                                 gf_ref, wr_ref, br_ref, tri_ref)
        h2_ref[...] = h2
        u2_ref[...] = u2.astype(BF16)
        info_ref[...] = info
        base_ref[...] = new_base
        tot_ref[...] = new_base
        state_ref[0] = u1[u1.shape[0] - HIST_ROWS:, :]


def _pool_short_kernel(n_sample, n_valid, h2_hbm, u2_hbm, info_hbm, h_ref, hist_ref, base0_ref, gm_ref, pw_ref, ps_ref,
                       gf_ref, wr_ref, br_ref, tri_ref, h2_ref, u2_ref, info_ref, cum_ref, tot_ref, state_ref,
                       ext_ref, base_ref):
    del h2_hbm, u2_hbm, info_hbm
    s_id = pl.program_id(0)

    @pl.when(s_id == 0)
    def _():
        base_ref[...] = base0_ref[...]

    ext_ref[:HIST_ROWS, :] = hist_ref[0]
    base = base_ref[...]
    cum_ref[0] = base
    pos1 = (lax.broadcasted_iota(jnp.int32, (SEQ_S, 1), 0) + 1).astype(F32)
    cnt_rows = jnp.where(s_id < n_sample, float(max(POOL_WINDOWS)), pos1)
    h2, u2, info, new_base, u1 = _pool_route(h_ref[...], ext_ref, cnt_rows, s_id < n_valid, base, gm_ref, pw_ref,
                                             ps_ref, gf_ref, wr_ref, br_ref, tri_ref)
    h2_ref[...] = h2
    u2_ref[...] = u2.astype(BF16)
    info_ref[...] = info
    base_ref[...] = new_base
    tot_ref[...] = new_base
    state_ref[0] = u1


def _expert_kernel(n_blocks, be_ref, lo_ref, hi_ref, ok_ref, u_hbm, pos_ref, gate_ref, wg_ref, wu_ref, wd_ref, ys_ref,
                   buf_ref, sel_ref, gsel_ref, acc_ref, gacc_ref, sem):
    del be_ref
    j = pl.program_id(0)
    rb = ys_ref.shape[0]
    n_slots, tc, _ = buf_ref.shape
    last_chunk = pos_ref.shape[0] - 1

    def n_chunks(jj):
        return jnp.where(ok_ref[jj] != 0, hi_ref[jj] - lo_ref[jj] + 1, 0)

    def chunk_copy(c, slot):
        return pltpu.make_async_copy(u_hbm.at[pl.ds(pl.multiple_of(c * tc, tc), tc), :], buf_ref.at[slot],
                                     sem.at[slot])

    def request(jj):
        n_req = jnp.minimum(n_chunks(jj), n_slots)
        for k in range(n_slots):
            @pl.when(k < n_req)
            def _():
                chunk_copy(lo_ref[jj] + k, k).start()

    def select(jj, c, live):
        rows = jj * rb + lax.broadcasted_iota(jnp.int32, (rb, tc), 0)
        pos = pos_ref[c]
        gate = gate_ref[c]
        m0 = (rows == pos[0:1, :]) & live
        m1 = (rows == pos[1:2, :]) & live
        sel = jnp.where(m0 | m1, 1.0, 0.0).astype(BF16)
        row_gate = jnp.sum(jnp.where(m0, gate[0:1, :], 0.0) + jnp.where(m1, gate[1:2, :], 0.0), axis=1, keepdims=True)
        return sel, row_gate

    def build_selection(jj):
        n_req = jnp.minimum(n_chunks(jj), n_slots)
        row_gate = jnp.zeros(gsel_ref.shape, F32)
        for k in range(n_slots):
            sel, g = select(jj, jnp.minimum(lo_ref[jj] + k, last_chunk), k < n_req)
            sel_ref[:, k * tc:(k + 1) * tc] = sel
            row_gate = row_gate + g
        gsel_ref[...] = row_gate

    @pl.when(j == 0)
    def _():
        buf_ref[...] = jnp.zeros_like(buf_ref)
        request(0)
        build_selection(0)

    n = n_chunks(j)
    lo = lo_ref[j]

    @pl.when(n > 0)
    def _():
        for k in range(n_slots):
            @pl.when(k < jnp.minimum(n, n_slots))
            def _():
                chunk_copy(lo + k, k).wait()

        acc_ref[...] = _dot(sel_ref[...], buf_ref[...].reshape(n_slots * tc, -1))
        gacc_ref[...] = gsel_ref[...]

        def overflow(k, carry):
            copy = chunk_copy(lo + k, 0)
            copy.start()
            copy.wait()
            sel, row_gate = select(j, lo + k, True)
            acc_ref[...] += _dot(sel, buf_ref[0])
            gacc_ref[...] += row_gate
            return carry

        lax.fori_loop(n_slots, jnp.maximum(n, n_slots), overflow, 0)

    @pl.when(j + 1 < n_blocks)
    def _():
        request(jnp.minimum(j + 1, n_blocks - 1))

    @pl.when(n > 0)
    def _():
        xg = acc_ref[...].astype(BF16)
        act = jax.nn.silu(_dot(xg, wg_ref[0])) * _dot(xg, wu_ref[0])
        ys_ref[...] = (_dot(act.astype(BF16), wd_ref[0]) * gacc_ref[...]).astype(BF16)
        build_selection(jnp.minimum(j + 1, n_blocks - 1))

    @pl.when(n == 0)
    def _():
        ys_ref[...] = jnp.zeros_like(ys_ref)


def _combine_kernel(n_blocks, n_fb, n_ref, ids_ref, h2_ref, pos_ref, g_ref, ys_hbm, yp_ref, yshort_ref, buf_ref, sem):
    i = pl.program_id(0)
    _, n_slots, cr, _ = buf_ref.shape
    per_vreg = LANE // cr
    par = i % 2
    nxt = jnp.minimum(i + 1, n_blocks - 1)

    def chunk_copy(ii, k, which):
        ch = ids_ref[ii * n_slots + k]
        return pltpu.make_async_copy(ys_hbm.at[pl.ds(pl.multiple_of(ch * cr, cr), cr), :], buf_ref.at[which, k],
                                     sem.at[which * n_slots + k])

    def request(ii, which):
        for k in range(n_slots):
            @pl.when(k < n_ref[ii])
            def _():
                chunk_copy(ii, k, which).start()

    @pl.when(i == 0)
    def _():
        buf_ref[...] = jnp.zeros_like(buf_ref)
        request(0, 0)

    @pl.when(i + 1 < n_blocks)
    def _():
        request(nxt, 1 - par)

    lane = lax.broadcasted_iota(jnp.int32, (1, LANE), 1)
    pieces = []
    for v in range(n_slots // per_vreg):
        row = jnp.zeros((1, LANE), jnp.int32)
        for q in range(per_vreg):
            k = v * per_vreg + q
            row = jnp.where(lane // cr == q, ids_ref[i * n_slots + k] * cr + lane % cr, row)
        pieces.append(row)
    slot_rows = jnp.concatenate(pieces, axis=1)
    sel = jnp.where((slot_rows == pos_ref[:, 0:1]) | (slot_rows == pos_ref[:, 1:2]), 1.0, 0.0).astype(BF16)

    for k in range(n_slots):
        @pl.when(k < n_ref[i])
        def _():
            chunk_copy(i, k, par).wait()

    moe = _dot(sel, buf_ref[par].reshape(n_slots * cr, -1))
    y = _rms(h2_ref[...] + moe, g_ref[...])

    @pl.when(i < n_fb)
    def _():
        yp_ref[...] = y

    @pl.when(i >= n_fb)
    def _():
        yshort_ref[...] = y


def _rope_table(pos):
    half = QK_ROPE // 2
    inv = ROPE_THETA ** (-jnp.arange(half, dtype=F32) / half)
    ang = pos.astype(F32)[:, None] * inv[None, :]
    cos = jnp.cos(ang)
    sin = jnp.sin(ang)
    zero = jnp.zeros((pos.shape[0], LANE - QK_ROPE), F32)
    return jnp.concatenate([cos, cos, zero, -sin, sin, zero], axis=1)


def _half_swap(w):
    half = QK_ROPE // 2
    return jnp.concatenate([w[..., half:], w[..., :half]], axis=-1)


def _pad_last(w, width):
    return jnp.pad(w, [(0, 0)] * (w.ndim - 1) + [(0, width - w.shape[-1])])


def kernel(x_prompt, x_sample, cache_kv_latent, cache_k_rope, state_pool, meta_tokens, norm_mix, norm_ffn, norm_final,
           mla_w_dq, mla_g_q, mla_w_uq, mla_w_dkv, mla_g_kv, mla_w_uk, mla_w_uv, mla_w_o, pool_w, pool_scale,
           ffn_w_gate, ffn_w_up, ffn_w_down, moe_w_router, moe_b_router, moe_w_gate, moe_w_up, moe_w_down):
    nb, seq, d = x_prompt.shape
    db, dseq, _ = x_sample.shape
    n_cache = cache_kv_latent.shape[2]
    q_rank = mla_w_dq.shape[2]
    kv_rank = mla_g_kv.shape[1]
    v_dim = mla_w_uv.shape[3]
    d_ff = ffn_w_gate.shape[2]
    n_e = moe_w_router.shape[2]
    d_e = moe_w_gate.shape[3]
    tb, rb, tq = TOKEN_BLOCK, ROW_BLOCK, ATTN_BLOCK
    assert norm_mix.shape[0] == 2 and cache_kv_latent.shape[0] == 1 and state_pool.shape[0] == 1
    assert dseq == SEQ_S and N_META == SEQ_S and meta_tokens.shape[0] == N_META
    assert kv_rank == 2 * LANE and QK_NOPE == LANE and QK_ROPE <= LANE and HIST_ROWS >= POOL_HIST
    assert seq % tb == 0 and tb == tq and tq % CHUNK == 0 and tb % SEQ_S == 0 and d % LANE == 0
    assert (n_cache - N_META) % CHUNK == 0 and dseq <= CHUNK

    nf = nb * seq
    n_valid_seq = db + 1
    ns = -(-(n_valid_seq * SEQ_S) // tb) * tb
    n_seq = ns // SEQ_S
    nt = nf + ns
    n_fb, n_sb, n_tb = nf // tb, ns // tb, nt // tb
    sb = seq // tb
    meta_row = nf + db * SEQ_S
    scale = float((QK_NOPE + QK_ROPE) ** -0.5)
    hq = 3 * LANE

    xp = x_prompt.reshape(nf, d)
    xs = jnp.concatenate([x_sample.reshape(db * SEQ_S, d), meta_tokens.astype(x_prompt.dtype),
                          jnp.zeros((ns - n_valid_seq * SEQ_S, d), x_prompt.dtype)], axis=0)

    t_s = jnp.arange(SEQ_S)
    pos_short = jnp.concatenate([jnp.tile(n_cache + t_s, db), jnp.tile(t_s, n_seq - db)])
    tab_frames = _rope_table(N_META + jnp.arange(seq))
    tab = jnp.concatenate([tab_frames, _rope_table(pos_short)], axis=0)

    wuq = mla_w_uq[0].reshape(q_rank, N_HEADS, QK_NOPE + QK_ROPE)
    wuq_pe = wuq[:, :, QK_NOPE:]
    wuq2 = jnp.concatenate([wuq[:, :, :QK_NOPE].reshape(q_rank, -1),
                            _pad_last(wuq_pe, LANE).reshape(q_rank, -1),
                            _pad_last(_half_swap(wuq_pe), LANE).reshape(q_rank, -1)], axis=1).astype(BF16)
    wuq_t = jnp.concatenate([wuq[:, :, :QK_NOPE].reshape(q_rank, -1), wuq_pe.reshape(q_rank, -1),
                             _half_swap(wuq_pe).reshape(q_rank, -1)], axis=1).T.astype(BF16)
    wdkv_r =mla_w_dkv[0][:, kv_rank:]
    wdkv2 = jnp.concatenate([mla_w_dkv[0][:, :kv_rank], _pad_last(wdkv_r, LANE),
                             _pad_last(_half_swap(wdkv_r), LANE)], axis=1).astype(BF16)
    wuk_t = jnp.transpose(mla_w_uk[0], (1, 2, 0)).astype(BF16)
    wuk_flat = mla_w_uk[0].reshape(kv_rank, N_HEADS * QK_NOPE).astype(BF16)
    wuv = jnp.transpose(mla_w_uv[0], (1, 0, 2)).astype(BF16)
    wuv_t = mla_w_uv[0].reshape(kv_rank, N_HEADS * v_dim).T.astype(BF16)
    row = lambda v: v.reshape(1, -1)
    hk = 2 * LANE
    hx = v_dim + BF16_SUBLANES

    tok_p = pl.BlockSpec((tb, d), lambda i: (jnp.minimum(i, n_fb - 1), 0))
    tok_s = pl.BlockSpec((tb, d), lambda i: (jnp.maximum(i - n_fb, 0), 0))
    short_tok = lambda i: (jnp.maximum(i - n_fb, 0), 0)
    q_short, qt_frames, kvb_short, kn_all, kr_all, vt_all, c_all, r_all = pl.pallas_call(
        functools.partial(_qkv_kernel, n_fb, scale * LOG2_E),
        grid=(n_tb,),
        in_specs=[tok_p, tok_s,
                  pl.BlockSpec((tb, 2 * LANE), lambda i: (jnp.where(i < n_fb, i % sb, sb + i - n_fb), 0)),
                  pl.BlockSpec((2 * LANE, tb), lambda i: (0, i % sb)),
                  _const_spec((1, d)), _const_spec((d, q_rank)), _const_spec((1, q_rank)),
                  _const_spec(wuq2.shape), _const_spec(wuq_t.shape), _const_spec(wuk_t.shape),
                  _const_spec(wdkv2.shape), _const_spec((1, kv_rank)), _const_spec(wuk_flat.shape),
                  _const_spec(wuv_t.shape)],
        out_specs=[pl.BlockSpec((tb, N_HEADS * hq), short_tok),
                   pl.BlockSpec((1, N_HEADS, hk, tb), lambda i: (jnp.minimum(i, n_fb - 1), 0, 0, 0)),
                   pl.BlockSpec((tb, hq), short_tok),
                   pl.BlockSpec((tb, N_HEADS * QK_NOPE), lambda i: (i, 0)),
                   pl.BlockSpec((tb, LANE), lambda i: (i, 0)),
                   pl.BlockSpec((1, N_HEADS * hx, tb), lambda i: (i, 0, 0)),
                   pl.BlockSpec((tb, kv_rank), lambda i: (i, 0)),
                   pl.BlockSpec((tb, QK_ROPE), lambda i: (i, 0))],
        out_shape=[jax.ShapeDtypeStruct((ns, N_HEADS * hq), BF16),
                   jax.ShapeDtypeStruct((n_fb, N_HEADS, hk, tb), BF16),
                   jax.ShapeDtypeStruct((ns, hq), BF16),
                   jax.ShapeDtypeStruct((nt, N_HEADS * QK_NOPE), BF16), jax.ShapeDtypeStruct((nt, LANE), BF16),
                   jax.ShapeDtypeStruct((n_tb, N_HEADS * hx, tb), BF16),
                   jax.ShapeDtypeStruct((nt, kv_rank), F32), jax.ShapeDtypeStruct((nt, QK_ROPE), F32)],
        compiler_params=_params(("arbitrary",)),
        name="qkv",
    )(xp, xs, tab, tab_frames.T, row(norm_mix[0]), mla_w_dq[0].astype(BF16), row(mla_g_q[0]), wuq2, wuq_t, wuk_t,
      wdkv2, row(mla_g_kv[0]), wuk_flat, wuv_t)

    qb = seq // tq
    once = pl.Buffered(1)
    o_frames = pl.pallas_call(
        functools.partial(_attn_frames_kernel, qb, meta_row % tb),
        grid=(n_fb,),
        in_specs=[pl.BlockSpec((1, N_HEADS, hk, tq), lambda i: (i, 0, 0, 0)),
                  pl.BlockSpec((seq, N_HEADS * QK_NOPE), lambda i: (i // qb, 0)),
                  pl.BlockSpec((seq, LANE), lambda i: (i // qb, 0)),
                  pl.BlockSpec((qb, N_HEADS * hx, tq), lambda i: (i // qb, 0, 0)),
                  pl.BlockSpec((N_META, N_HEADS * QK_NOPE), lambda i: (meta_row // N_META, 0), pipeline_mode=once),
                  pl.BlockSpec((N_META, LANE), lambda i: (meta_row // N_META, 0), pipeline_mode=once),
                  pl.BlockSpec((1, N_HEADS * hx, tb), lambda i: (meta_row // tb, 0, 0), pipeline_mode=once)],
        out_specs=pl.BlockSpec((tq, N_HEADS * v_dim), lambda i: (i, 0)),
        out_shape=jax.ShapeDtypeStruct((nf, N_HEADS * v_dim), BF16),
        scratch_shapes=[pltpu.VMEM((N_HEADS, 1, tq), F32), pltpu.VMEM((N_HEADS, hx, tq), F32)],
        compiler_params=_params(("parallel",)),
        name="attn_frames",
    )(qt_frames, kn_all, kr_all, vt_all, kn_all, kr_all, vt_all)

    o_short = pl.pallas_call(
        functools.partial(_attn_short_kernel, db),
        grid=(n_seq,),
        in_specs=[pl.BlockSpec((SEQ_S, N_HEADS * hq), lambda s: (s, 0)),
                  pl.BlockSpec((SEQ_S, hq), lambda s: (s, 0)),
                  pl.BlockSpec((1, 1, n_cache, kv_rank), lambda s: (0, jnp.minimum(s, db - 1), 0, 0)),
                  pl.BlockSpec((1, 1, n_cache, QK_ROPE), lambda s: (0, jnp.minimum(s, db - 1), 0, 0)),
                  _const_spec(wuv.shape)],
        out_specs=pl.BlockSpec((SEQ_S, N_HEADS * v_dim), lambda s: (s, 0)),
        out_shape=jax.ShapeDtypeStruct((ns, N_HEADS * v_dim), BF16),
        compiler_params=_params(("parallel",)),
        name="attn_short",
    )(q_short, kvb_short, cache_kv_latent, cache_k_rope, wuv)

    moe_f32 = [moe_w_gate[0].reshape(n_e * d, d_e), moe_w_up[0].reshape(n_e * d, d_e),
               moe_w_down[0].reshape(n_e * d_e, d)]
    cast_in, cast_out, cast_shapes = [], [], []
    for w in moe_f32:
        steps = max(s for s in range(1, n_tb + 1) if w.shape[0] % s == 0 and (w.shape[0] // s) % BF16_SUBLANES == 0)
        spec = pl.BlockSpec((w.shape[0] // steps, w.shape[1]), lambda i, last=steps - 1: (jnp.minimum(i, last), 0))
        cast_in.append(spec)
        cast_out.append(spec)
        cast_shapes.append(jax.ShapeDtypeStruct(w.shape, BF16))
    h1, wg_e, wu_e, wd_e = pl.pallas_call(
        functools.partial(_proj_ffn_kernel, n_fb),
        grid=(n_tb,),
        in_specs=[tok_p, tok_s,
                  pl.BlockSpec((tb, N_HEADS * v_dim), lambda i: (jnp.minimum(i, n_fb - 1), 0)),
                  pl.BlockSpec((tb, N_HEADS * v_dim), short_tok),
                  _const_spec((N_HEADS * v_dim, d)), _const_spec((1, d)),
                  _const_spec((d, d_ff)), _const_spec((d, d_ff)), _const_spec((d_ff, d))] + cast_in,
        out_specs=[pl.BlockSpec((tb, d), lambda i: (i, 0))] + cast_out,
        out_shape=[jax.ShapeDtypeStruct((nt, d), F32)] + cast_shapes,
        compiler_params=_params(("arbitrary",)),
        name="proj_ffn",
    )(xp, xs, o_frames, o_short, mla_w_o[0].astype(BF16), row(norm_ffn[0]),
      ffn_w_gate[0].astype(BF16), ffn_w_up[0].astype(BF16), ffn_w_down[0].astype(BF16), *moe_f32)
    wg_e = wg_e.reshape(n_e, d, d_e)
    wu_e = wu_e.reshape(n_e, d, d_e)
    wd_e = wd_e.reshape(n_e, d_e, d)

    route_w = [_const_spec((1, d)), _const_spec(pool_w.shape[1:]), _const_spec((1, d)), _const_spec((1, d)),
               _const_spec((d, 2 * n_e)), _const_spec((1, n_e))]
    wr_hi = moe_w_router[0].astype(BF16)
    wr_lo = (moe_w_router[0].astype(F32) - wr_hi.astype(F32)).astype(BF16)
    route_args = (row(norm_mix[1]), pool_w[0].astype(BF16), row(pool_scale[0]), row(norm_ffn[1]),
                  jnp.concatenate([wr_hi, wr_lo], axis=1), row(moe_b_router[0]))
    tri = lambda n: (jnp.arange(n)[:, None] > jnp.arange(n)[None, :]).astype(BF16)
    hist_blk = lambda i: jnp.where(i % sb == 0, meta_row // HIST_ROWS, i * (tb // HIST_ROWS) - 1)
    h2_f, u2_f, info_f, cum_f, tot_f, state_f = pl.pallas_call(
        functools.partial(_pool_frames_kernel, n_fb),
        grid=(n_tb,),
        in_specs=[pl.BlockSpec((tb, d), lambda i: (i, 0)),
                  pl.BlockSpec((HIST_ROWS, d), lambda i: (hist_blk(i), 0))] + route_w + [_const_spec((tb, tb))],
        out_specs=[pl.BlockSpec((tb, d), lambda i: (i, 0)), pl.BlockSpec((tb, d), lambda i: (i, 0)),
                   pl.BlockSpec((tb, n_e), lambda i: (i, 0)),
                   pl.BlockSpec((1, 1, n_e), lambda i: (jnp.minimum(i, n_fb - 1), 0, 0)),
                   pl.BlockSpec((1, n_e), lambda i: (0, 0)),
                   pl.BlockSpec((1, HIST_ROWS, d), lambda i: (jnp.minimum(i // sb, nb - 1), 0, 0))],
        out_shape=[jax.ShapeDtypeStruct((nt, d), F32), jax.ShapeDtypeStruct((nt, d), BF16),
                   jax.ShapeDtypeStruct((nt, n_e), F32), jax.ShapeDtypeStruct((n_fb, 1, n_e), F32),
                   jax.ShapeDtypeStruct((1, n_e), F32), jax.ShapeDtypeStruct((nb, HIST_ROWS, d), F32)],
        scratch_shapes=[pltpu.VMEM((HIST_ROWS + tb, d), F32), pltpu.VMEM((1, n_e), F32)],
        compiler_params=_params(("arbitrary",)),
        name="pool_route_frames",
    )(h1, h1, *route_args, tri(tb))

    hist_s = jnp.concatenate([
        jnp.pad(state_pool[0].astype(F32), [(0, 0), (HIST_ROWS - POOL_HIST, 0), (0, 0)]),
        jnp.zeros((n_seq - db, HIST_ROWS, d), F32)], axis=0)
    any_spec = pl.BlockSpec(memory_space=pl.ANY)
    short_blk = lambda s: (nf // SEQ_S + s, 0)
    h2, u2, info, cum_s, tot_s, state_s = pl.pallas_call(
        functools.partial(_pool_short_kernel, db, n_valid_seq),
        grid=(n_seq,),
        in_specs=[any_spec, any_spec, any_spec,
                  pl.BlockSpec((SEQ_S, d), short_blk),
                  pl.BlockSpec((1, HIST_ROWS, d), lambda s: (s, 0, 0)),
                  _const_spec((1, n_e))] + route_w + [_const_spec((SEQ_S, SEQ_S))],
        out_specs=[pl.BlockSpec((SEQ_S, d), short_blk), pl.BlockSpec((SEQ_S, d), short_blk),
                   pl.BlockSpec((SEQ_S, n_e), short_blk), pl.BlockSpec((1, 1, n_e), lambda s: (s, 0, 0)),
                   pl.BlockSpec((1, n_e), lambda s: (0, 0)),
                   pl.BlockSpec((1, SEQ_S, d), lambda s: (s, 0, 0))],
        out_shape=[jax.ShapeDtypeStruct((nt, d), F32), jax.ShapeDtypeStruct((nt, d), BF16),
                   jax.ShapeDtypeStruct((nt, n_e), F32), jax.ShapeDtypeStruct((n_seq, 1, n_e), F32),
                   jax.ShapeDtypeStruct((1, n_e), F32), jax.ShapeDtypeStruct((n_seq, SEQ_S, d), F32)],
        scratch_shapes=[pltpu.VMEM((HIST_ROWS + SEQ_S, d), F32), pltpu.VMEM((1, n_e), F32)],
        input_output_aliases={0: 0, 1: 1, 2: 2},
        compiler_params=_params(("arbitrary",)),
        name="pool_route_short",
    )(h2_f, u2_f, info_f, h1, hist_s, tot_f, *route_args, tri(SEQ_S))

    counts = tot_s[0].astype(jnp.int32)
    cum = jnp.concatenate([cum_f[:, 0], cum_s[::tb // SEQ_S, 0], tot_s], axis=0).astype(jnp.int32)
    padded = (counts + rb - 1) // rb * rb
    pend = jnp.cumsum(padded)
    pstart = pend - padded
    n_rows_max = -(-(TOP_K * (nf + n_valid_seq * SEQ_S)) // rb) * rb + n_e * rb
    n_rb = n_rows_max // rb
    e_tok = info[:, 0:TOP_K].astype(jnp.int32)
    pos = jnp.where(e_tok >= 0, pstart[jnp.maximum(e_tok, 0)] + info[:, 2:2 + TOP_K].astype(jnp.int32), -1)
    gates = info[:, 4:4 + TOP_K]
    pos_l = pos.reshape(n_tb, tb, TOP_K).transpose(0, 2, 1)
    gate_l = gates.reshape(n_tb, tb, TOP_K).transpose(0, 2, 1)
    pos_c = jnp.pad(pos, [(0, 0), (0, n_e - TOP_K)], constant_values=-1)

    blk_row = jnp.arange(n_rb, dtype=jnp.int32) * rb
    blk_e = jnp.minimum(jnp.sum(pend[None, :] <= blk_row[:, None], axis=1), n_e - 1).astype(jnp.int32)
    blk_ok = (blk_row < pend[-1]).astype(jnp.int32)
    r_lo = blk_row - pstart[blk_e]
    r_hi = jnp.minimum(r_lo + rb, counts[blk_e])
    cum_b = cum.T[blk_e]
    first = jnp.sum(cum_b[:, 1:] <= r_lo[:, None], axis=1)
    last = jnp.sum(cum_b[:, :-1] < r_hi[:, None], axis=1) - 1
    blk_lo = jnp.clip(first, 0, n_tb - 1).astype(jnp.int32)
    blk_hi = jnp.clip(last, blk_lo, n_tb - 1).astype(jnp.int32)

    ys = pl.pallas_call(
        functools.partial(_expert_kernel, n_rb),
        grid_spec=pltpu.PrefetchScalarGridSpec(
            num_scalar_prefetch=4,
            grid=(n_rb,),
            in_specs=[pl.BlockSpec(memory_space=pl.ANY),
                      _const_spec((n_tb, TOP_K, tb)), _const_spec((n_tb, TOP_K, tb)),
                      pl.BlockSpec((1, d, d_e), lambda j, be, lo, hi, ok: (be[j], 0, 0)),
                      pl.BlockSpec((1, d, d_e), lambda j, be, lo, hi, ok: (be[j], 0, 0)),
                      pl.BlockSpec((1, d_e, d), lambda j, be, lo, hi, ok: (be[j], 0, 0))],
            out_specs=pl.BlockSpec((rb, d), lambda j, be, lo, hi, ok: (j, 0)),
            scratch_shapes=[pltpu.VMEM((GATHER_SLOTS, tb, d), BF16), pltpu.VMEM((rb, GATHER_SLOTS * tb), BF16),
                            pltpu.VMEM((rb, 1), F32), pltpu.VMEM((rb, d), F32), pltpu.VMEM((rb, 1), F32),
                            pltpu.SemaphoreType.DMA((GATHER_SLOTS,))]),
        out_shape=jax.ShapeDtypeStruct((n_rows_max, d), BF16),
        compiler_params=_params(("arbitrary",)),
        name="experts",
    )(blk_e, blk_lo, blk_hi, blk_ok, u2, pos_l, gate_l, wg_e, wu_e, wd_e)

    cr = COMBINE_ROWS
    per_expert = tb // cr + 1
    n_slots = TOP_K * tb // cr + 2 * n_e
    assert rb % cr == 0 and LANE % cr == 0 and n_slots % (LANE // cr) == 0
    w_lo = pstart[None, :] + cum[:-1]
    w_hi = pstart[None, :] + cum[1:]
    c_lo = w_lo // cr
    n_chunks = jnp.where(w_hi > w_lo, (w_hi - 1) // cr - c_lo + 1, 0)
    q = jnp.arange(per_expert)
    cand = (c_lo[:, :, None] + q).reshape(n_tb, n_e * per_expert)
    keep = (q < n_chunks[:, :, None]).reshape(n_tb, n_e * per_expert)
    order = jnp.argsort(~keep, axis=1, stable=True)[:, :n_slots]
    ids = jnp.where(jnp.take_along_axis(keep, order, axis=1), jnp.take_along_axis(cand, order, axis=1),
                    n_rows_max // cr)
    n_ids = jnp.sum(keep, axis=1).astype(jnp.int32)
    y_frames, y_short = pl.pallas_call(
        functools.partial(_combine_kernel, n_tb, n_fb),
        grid_spec=pltpu.PrefetchScalarGridSpec(
            num_scalar_prefetch=2,
            grid=(n_tb,),
            in_specs=[pl.BlockSpec((tb, d), lambda i, n, ids: (i, 0)),
                      pl.BlockSpec((tb, n_e), lambda i, n, ids: (i, 0)),
                      pl.BlockSpec((1, d), lambda i, n, ids: (0, 0)),
                      pl.BlockSpec(memory_space=pl.ANY)],
            out_specs=[pl.BlockSpec((tb, d), lambda i, n, ids: (jnp.minimum(i, n_fb - 1), 0)),
                       pl.BlockSpec((tb, d), lambda i, n, ids: (jnp.maximum(i - n_fb, 0), 0))],
            scratch_shapes=[pltpu.VMEM((2, n_slots, cr, d), BF16), pltpu.SemaphoreType.DMA((2 * n_slots,))]),
        out_shape=[jax.ShapeDtypeStruct((nf, d), F32), jax.ShapeDtypeStruct((ns, d), F32)],
        compiler_params=_params(("arbitrary",)),
        name="combine",
    )(n_ids, ids.reshape(-1).astype(jnp.int32), h2, pos_c, row(norm_final), ys)

    y_prompt = y_frames.reshape(nb, seq, d)
    y_sample = y_short[:db * SEQ_S].reshape(db, SEQ_S, d)

    def with_meta(a, width):
        meta = jnp.broadcast_to(a[meta_row:meta_row + N_META][None], (nb, N_META, width))
        return jnp.concatenate([meta, a[:nf].reshape(nb, seq, width)], axis=1)[None]

    c_p = with_meta(c_all, kv_rank)
    r_p = with_meta(r_all, QK_ROPE)
    c_s = c_all[nf:nf + db * SEQ_S].reshape(1, db, SEQ_S, kv_rank)
    r_s = r_all[nf:nf + db * SEQ_S].reshape(1, db, SEQ_S, QK_ROPE)
    s_p = state_f[:, HIST_ROWS - POOL_HIST:][None]
    s_s = state_s[:db, SEQ_S - POOL_HIST:][None]
    return (y_prompt, y_sample, c_p, r_p, s_p, c_s, r_s, s_s)
```

```python
import functools

import jax
import jax.numpy as jnp
from jax import lax
from jax.experimental import pallas as pl
from jax.experimental.pallas import tpu as pltpu

CHUNK = 64
N_META = 16
N_HEADS = 8
QK_NOPE = 128
QK_ROPE = 64
ROPE_THETA = 10000.0
POOL_WINDOWS = (2, 4, 8, 16)
POOL_HIST = max(POOL_WINDOWS) - 1
TOP_K = 2
RMS_EPS = 1e-6

LANE = 128
BF16_SUBLANES = 16
SEQ_S = 16
F32_SUBLANES = 8
HIST_ROWS = 16
HIST_PAD = F32_SUBLANES * len(POOL_WINDOWS)
TOKEN_BLOCK = 256
ROW_BLOCK = 256
ATTN_BLOCK = 256
CACHE_BLOCK = 512
KEY_BLOCKS_PER_ITER = 4
SCORES_AHEAD = 6
COMBINE_ROWS = 64
GATHER_SLOTS = 6
VMEM_LIMIT = 56 * 1024 * 1024

F32 = jnp.float32
BF16 = jnp.bfloat16
NEG_INF = float("-inf")
LOG2_E = 1.4426950408889634


def _dot(a, b):
    return jnp.dot(a, b, preferred_element_type=F32)


def _dot_nt(a, b):
    return lax.dot_general(a, b, (((1,), (1,)), ((), ())), preferred_element_type=F32)


def _rms(x, g):
    return x * lax.rsqrt(jnp.mean(x * x, axis=-1, keepdims=True) + RMS_EPS) * g


def _const_spec(shape):
    nd = len(shape)
    return pl.BlockSpec(shape, lambda *_: (0,) * nd, pipeline_mode=pl.Buffered(1))


def _params(sem):
    return pltpu.CompilerParams(dimension_semantics=sem, vmem_limit_bytes=VMEM_LIMIT)


def _qkv_kernel(n_fb, scale, xp_ref, xs_ref, tab_ref, tabt_ref, g_ref, wdq_ref, gq_ref, wuq_ref, wuqt_ref, wukt_ref,
                wdkv_ref, gkv_ref, wukf_ref, wuvt_ref, q_ref, qt_ref, kvb_ref, kn_ref, kr_ref, vt_ref, c_ref, r_ref):
    i = pl.program_id(0)
    x = jnp.where(i < n_fb, xp_ref[...], xs_ref[...])
    u = _rms(x, g_ref[...]).astype(BF16)
    cq = _rms(_dot(u, wdq_ref[...]), gq_ref[...]).astype(BF16)
    cos = tab_ref[:, :LANE]
    sin = tab_ref[:, LANE:]
    hq = QK_NOPE + 2 * LANE

    kv = _dot(u, wdkv_ref[...])
    c = _rms(kv[:, :2 * LANE], gkv_ref[...])
    r = kv[:, 2 * LANE:3 * LANE] * cos + kv[:, 3 * LANE:] * sin
    c_ref[...] = c
    r_ref[...] = r[:, :QK_ROPE]
    c_bf = c.astype(BF16)
    r_bf = r.astype(BF16)

    kn_ref[...] = _dot(c_bf, wukf_ref[...]).astype(BF16)
    kr_ref[...] = r_bf
    vt = _dot_nt(wuvt_ref[...], c_bf).astype(BF16)
    hv = vt.shape[0] // N_HEADS
    hv_ext = vt_ref.shape[1] // N_HEADS
    for h in range(N_HEADS):
        vt_ref[0, h * hv_ext:h * hv_ext + hv, :] = vt[h * hv:(h + 1) * hv, :]
        vt_ref[0, h * hv_ext + hv:(h + 1) * hv_ext, :] = jnp.ones((hv_ext - hv, vt.shape[1]), BF16)

    @pl.when(i < n_fb)
    def _():
        qat = _dot_nt(wuqt_ref[...], cq)
        cos_t = tabt_ref[:QK_ROPE, :]
        sin_t = tabt_ref[LANE:LANE + QK_ROPE, :]
        pe0 = N_HEADS * QK_NOPE
        sw0 = pe0 + N_HEADS * QK_ROPE
        for h in range(N_HEADS):
            qt_ref[0, h, :QK_NOPE, :] = (qat[h * QK_NOPE:(h + 1) * QK_NOPE, :] * scale).astype(BF16)
            a = qat[pe0 + h * QK_ROPE:pe0 + (h + 1) * QK_ROPE, :]
            b = qat[sw0 + h * QK_ROPE:sw0 + (h + 1) * QK_ROPE, :]
            qt_ref[0, h, QK_NOPE:QK_NOPE + QK_ROPE, :] = ((a * cos_t + b * sin_t) * scale).astype(BF16)
            qt_ref[0, h, QK_NOPE + QK_ROPE:, :] = jnp.zeros((LANE - QK_ROPE, qt_ref.shape[3]), BF16)

    @pl.when(i >= n_fb)
    def _():
        qa = _dot(cq, wuq_ref[...])
        for h in range(N_HEADS):
            qn = qa[:, h * LANE:(h + 1) * LANE].astype(BF16)
            q_ref[:, h * hq:h * hq + 2 * LANE] = (_dot(qn, wukt_ref[h]) * scale).astype(BF16)
            a = qa[:, (N_HEADS + h) * LANE:(N_HEADS + h + 1) * LANE]
            b = qa[:, (2 * N_HEADS + h) * LANE:(2 * N_HEADS + h + 1) * LANE]
            q_ref[:, h * hq + 2 * LANE:(h + 1) * hq] = ((a * cos + b * sin) * scale).astype(BF16)
        kvb_ref[:, :2 * LANE] = c_bf
        kvb_ref[:, 2 * LANE:] = r_bf


def _attn_frames_kernel(qb, meta_col, qt_ref, kn_ref, kr_ref, vt_ref, knm_ref, krm_ref, vtm_ref, o_ref, m_ref, acc_ref):
    j = pl.program_id(0) % qb
    tq = o_ref.shape[0]
    hn = kn_ref.shape[1] // N_HEADS
    hv = o_ref.shape[1] // N_HEADS
    hx = vt_ref.shape[1] // N_HEADS

    def head_keys(kn, kr, rows, h):
        return jnp.concatenate([kn[rows, h * hn:(h + 1) * hn], kr[rows, :]], axis=1)

    def scores(h, rows):
        return _dot(head_keys(kn_ref, kr_ref, rows, h), qt_ref[0, h])

    m_ref[...] = jnp.full(m_ref.shape, NEG_INF, F32)
    acc_ref[...] = jnp.zeros_like(acc_ref)

    def unmasked(kbs):
        steps = [(kb, h) for kb in kbs for h in range(N_HEADS)]
        block_scores = lambda kb, h: scores(h, pl.ds(pl.multiple_of(kb * tq, tq), tq))
        ahead = [block_scores(*step) for step in steps[:SCORES_AHEAD]]
        for n, (kb, h) in enumerate(steps):
            s = ahead.pop(0)
            if n + SCORES_AHEAD < len(steps):
                ahead.append(block_scores(*steps[n + SCORES_AHEAD]))
            m_old = m_ref[h]
            m_new = jnp.maximum(m_old, jnp.max(s, axis=0, keepdims=True))
            alpha = jnp.exp2(m_old - m_new)
            p = jnp.exp2(s - m_new)
            m_ref[h] = m_new
            acc_ref[h] = alpha * acc_ref[h] + _dot(vt_ref[kb, h * hx:(h + 1) * hx, :], p.astype(BF16))

    def body(t, carry):
        unmasked([KEY_BLOCKS_PER_ITER * t + u for u in range(KEY_BLOCKS_PER_ITER)])
        return carry

    lax.fori_loop(0, j // KEY_BLOCKS_PER_ITER, body, 0)
    for left in range(1, KEY_BLOCKS_PER_ITER):
        @pl.when(j % KEY_BLOCKS_PER_ITER == left)
        def _():
            unmasked([j - left + u for u in range(left)])

    rows = pl.ds(pl.multiple_of(j * tq, tq), tq)
    visible = (lax.broadcasted_iota(jnp.int32, (tq, tq), 0) // CHUNK
               <= lax.broadcasted_iota(jnp.int32, (tq, tq), 1) // CHUNK)

    def last_scores(h):
        return scores(h, rows), _dot(head_keys(knm_ref, krm_ref, slice(None), h), qt_ref[0, h])

    ahead = [last_scores(h) for h in range(SCORES_AHEAD)]
    for h in range(N_HEADS):
        s, s_meta = ahead.pop(0)
        s = jnp.where(visible, s, NEG_INF)
        if h + SCORES_AHEAD < N_HEADS:
            ahead.append(last_scores(h + SCORES_AHEAD))
        m_old = m_ref[h]
        m_new = jnp.maximum(m_old, jnp.maximum(jnp.max(s, axis=0, keepdims=True),
                                               jnp.max(s_meta, axis=0, keepdims=True)))
        alpha = jnp.exp2(m_old - m_new)
        p = jnp.exp2(s - m_new)
        p_meta = jnp.exp2(s_meta - m_new)
        acc = (alpha * acc_ref[h] + _dot(vt_ref[j, h * hx:(h + 1) * hx, :], p.astype(BF16))
               + _dot(vtm_ref[0, h * hx:(h + 1) * hx, meta_col:meta_col + N_META], p_meta.astype(BF16)))
        o_ref[:, h * hv:(h + 1) * hv] = (acc[:hv] * (1.0 / acc[hv:hv + 1])).T.astype(BF16)


def _attn_short_kernel(n_cached_seq, q_ref, kvn_ref, cc_ref, cr_ref, wuv_ref, o_ref):
    s_id = pl.program_id(0)
    hq = q_ref.shape[1] // N_HEADS
    n_cache = cc_ref.shape[2]
    qs = jnp.concatenate([q_ref[:, h * hq:(h + 1) * hq] for h in range(N_HEADS)], axis=0)
    q_lat = qs[:, :2 * LANE]
    q_pe = qs[:, 2 * LANE:2 * LANE + QK_ROPE]
    kvn = kvn_ref[...]
    s = _dot_nt(qs, kvn)
    m = jnp.max(s, axis=1, keepdims=True)
    p = jnp.exp2(s - m)
    l = jnp.sum(p, axis=1, keepdims=True)
    acc = _dot(p.astype(BF16), kvn[:, :2 * LANE])
    has_cache = s_id < n_cached_seq
    start = 0
    while start < n_cache:
        size = min(CACHE_BLOCK, n_cache - start)
        ck = cc_ref[0, 0, start:start + size, :].astype(BF16)
        rk = cr_ref[0, 0, start:start + size, :].astype(BF16)
        s = _dot_nt(q_lat, ck) + _dot_nt(q_pe, rk)
        s = jnp.where(has_cache, s, NEG_INF)
        m_new = jnp.maximum(m, jnp.max(s, axis=1, keepdims=True))
        alpha = jnp.exp2(m - m_new)
        p = jnp.exp2(s - m_new)
        l = alpha * l + jnp.sum(p, axis=1, keepdims=True)
        acc = alpha * acc + _dot(p.astype(BF16), ck)
        m = m_new
        start += size
    o = (acc / l).astype(BF16)
    hv = wuv_ref.shape[2]
    for h in range(N_HEADS):
        o_ref[:, h * hv:(h + 1) * hv] = _dot(o[h * SEQ_S:(h + 1) * SEQ_S, :], wuv_ref[h]).astype(BF16)


def _proj_ffn_kernel(n_fb, xp_ref, xs_ref, op_ref, os_ref, wo_ref, g_ref, wg_ref, wu_ref, wd_ref, *cast_refs):
    n_cast = (len(cast_refs) - 1) // 2
    h_ref = cast_refs[n_cast]
    for src, dst in zip(cast_refs[:n_cast], cast_refs[n_cast + 1:]):
        dst[...] = src[...].astype(BF16)
    i = pl.program_id(0)
    x = jnp.where(i < n_fb, xp_ref[...], xs_ref[...])
    o = jnp.where(i < n_fb, op_ref[...], os_ref[...])
    h1 = x + _dot(o, wo_ref[...])
    u = _rms(h1, g_ref[...]).astype(BF16)
    act = jax.nn.silu(_dot(u, wg_ref[...])) * _dot(u, wu_ref[...])
    h_ref[...] = h1 + _dot(act.astype(BF16), wd_ref[...])


def _set_history(ext_ref, hist):
    for start in range(0, HIST_PAD, HIST_ROWS):
        ext_ref[start:start + HIST_ROWS, :] = hist


def _window_sums(ext_ref, stage_refs, t):
    grp = ext_ref.shape[1] // len(POOL_WINDOWS)
    end = HIST_PAD + t
    first = F32_SUBLANES
    s = ext_ref[first:end, :] + ext_ref[first - 1:end - 1, :]
    sums = [s[HIST_PAD - first:, :grp]]
    for ref, shift in zip(stage_refs, POOL_WINDOWS):
        rest = s[:, grp:]
        ref[first:end, :] = rest
        first += F32_SUBLANES
        s = rest[F32_SUBLANES:, :] + ref[first - shift:end - shift, :]
        sums.append(s[HIST_PAD - first:, :grp])
    return sums


def _pool_route(h1, ext_ref, stage_refs, cnt_rows, valid, base, gm_ref, pw_ref, ps_ref, gf_ref, wr_ref, br_ref,
                tri_ref):
    t = h1.shape[0]
    grp = h1.shape[1] // len(POOL_WINDOWS)
    u1 = _rms(h1, gm_ref[...])
    ext_ref[HIST_PAD:HIST_PAD + t, :] = u1
    sums = _window_sums(ext_ref, stage_refs, t)
    ys = []
    for g, w in enumerate(POOL_WINDOWS):
        cnt = float(w) if cnt_rows is None else jnp.minimum(float(w), cnt_rows)
        mean = sums[g] / cnt - u1[:, g * grp:(g + 1) * grp]
        ys.append(_dot(mean.astype(BF16), pw_ref[g]))
    h2 = h1 + jnp.concatenate(ys, axis=1) * ps_ref[...]
    u2 = _rms(h2, gf_ref[...])

    n_e = wr_ref.shape[1] // 2
    u2_hi = u2.astype(BF16)
    u2_lo = (u2 - u2_hi.astype(F32)).astype(BF16)
    parts = _dot(u2_hi, wr_ref[...]) + _dot(u2_lo, wr_ref[...])
    logits = parts[:, :n_e] + parts[:, n_e:] + br_ref[...]
    e_iota = lax.broadcasted_iota(jnp.int32, logits.shape, 1).astype(F32)
    v0 = jnp.max(logits, axis=1, keepdims=True)
    e0 = jnp.min(jnp.where(logits == v0, e_iota, float(n_e)), axis=1, keepdims=True)
    rest = jnp.where(e_iota == e0, NEG_INF, logits)
    v1 = jnp.max(rest, axis=1, keepdims=True)
    e1 = jnp.min(jnp.where(rest == v1, e_iota, float(n_e)), axis=1, keepdims=True)
    tt = jnp.exp(v1 - v0)
    g0 = 1.0 / (1.0 + tt)
    g1 = tt / (1.0 + tt)
    hit0 = e_iota == e0
    hit1 = e_iota == e1
    onehot = jnp.where(valid & (hit0 | hit1), 1.0, 0.0)
    rank = _dot(tri_ref[...], onehot.astype(BF16)) + base
    r0 = jnp.sum(jnp.where(hit0, rank, 0.0), axis=1, keepdims=True)
    r1 = jnp.sum(jnp.where(hit1, rank, 0.0), axis=1, keepdims=True)
    cols = (jnp.where(valid, e0, -1).astype(F32), jnp.where(valid, e1, -1).astype(F32), r0, r1, g0, g1)
    info = jnp.zeros(logits.shape, F32)
    for k, col in enumerate(cols):
        info = jnp.where(e_iota == k, col, info)
    new_base = base + jnp.sum(onehot, axis=0, keepdims=True)
    return h2, u2, info, new_base, u1


def _pool_frames_kernel(n_fb, h_ref, hist_ref, gm_ref, pw_ref, ps_ref, gf_ref, wr_ref, br_ref, tri_ref,
                        h2_ref, u2_ref, info_ref, cum_ref, tot_ref, state_ref, ext_ref, base_ref, *stage_refs):
    i = pl.program_id(0)

    @pl.when(i == 0)
    def _():
        base_ref[...] = jnp.zeros_like(base_ref)

    @pl.when(i >= n_fb)
    def _():
        h2_ref[...] = jnp.zeros_like(h2_ref)
        u2_ref[...] = jnp.zeros_like(u2_ref)
        info_ref[...] = jnp.zeros_like(info_ref)

    @pl.when(i < n_fb)
    def _():
        _set_history(ext_ref, _rms(hist_ref[...], gm_ref[...]))
        base = base_ref[...]
        cum_ref[0] = base
        h2, u2, info, new_base, u1 = _pool_route(h_ref[...], ext_ref, stage_refs, None, True, base, gm_ref, pw_ref,
                                                 ps_ref, gf_ref, wr_ref, br_ref, tri_ref)
        h2_ref[...] = h2
        u2_ref[...] = u2.astype(BF16)
        info_ref[...] = info
        base_ref[...] = new_base
        tot_ref[...] = new_base
        state_ref[0] = u1[u1.shape[0] - HIST_ROWS:, :]


def _pool_short_kernel(n_sample, n_valid, h2_hbm, u2_hbm, info_hbm, h_ref, hist_ref, base0_ref, gm_ref, pw_ref, ps_ref,
                       gf_ref, wr_ref, br_ref, tri_ref, h2_ref, u2_ref, info_ref, cum_ref, tot_ref, state_ref,
                       ext_ref, base_ref, *stage_refs):
    del h2_hbm, u2_hbm, info_hbm
    s_id = pl.program_id(0)

    @pl.when(s_id == 0)
    def _():
        base_ref[...] = base0_ref[...]

    _set_history(ext_ref, hist_ref[0])
    base = base_ref[...]
    cum_ref[0] = base
    pos1 = (lax.broadcasted_iota(jnp.int32, (SEQ_S, 1), 0) + 1).astype(F32)
    cnt_rows = jnp.where(s_id < n_sample, float(max(POOL_WINDOWS)), pos1)
    h2, u2, info, new_base, u1 = _pool_route(h_ref[...], ext_ref, stage_refs, cnt_rows, s_id < n_valid, base, gm_ref,
                                             pw_ref, ps_ref, gf_ref, wr_ref, br_ref, tri_ref)
    h2_ref[...] = h2
    u2_ref[...] = u2.astype(BF16)
    info_ref[...] = info
    base_ref[...] = new_base
    tot_ref[...] = new_base
    state_ref[0] = u1


def _expert_kernel(n_blocks, be_ref, lo_ref, hi_ref, ok_ref, u_hbm, pos_ref, gate_ref, wg_ref, wu_ref, wd_ref, ys_ref,
                   buf_ref, sel_ref, gsel_ref, acc_ref, gacc_ref, sem):
    del be_ref
    j = pl.program_id(0)
    rb = ys_ref.shape[0]
    n_slots, tc, _ = buf_ref.shape
    last_chunk = pos_ref.shape[0] - 1

    def n_chunks(jj):
        return jnp.where(ok_ref[jj] != 0, hi_ref[jj] - lo_ref[jj] + 1, 0)

    def chunk_copy(c, slot):
        return pltpu.make_async_copy(u_hbm.at[pl.ds(pl.multiple_of(c * tc, tc), tc), :], buf_ref.at[slot],
                                     sem.at[slot])

    def request(jj):
        n_req = jnp.minimum(n_chunks(jj), n_slots)
        for k in range(n_slots):
            @pl.when(k < n_req)
            def _():
                chunk_copy(lo_ref[jj] + k, k).start()

    def select(jj, c, live):
        rows = jj * rb + lax.broadcasted_iota(jnp.int32, (rb, tc), 0)
        pos = pos_ref[c]
        gate = gate_ref[c]
        m0 = (rows == pos[0:1, :]) & live
        m1 = (rows == pos[1:2, :]) & live
        sel = jnp.where(m0 | m1, 1.0, 0.0).astype(BF16)
        row_gate = jnp.sum(jnp.where(m0, gate[0:1, :], 0.0) + jnp.where(m1, gate[1:2, :], 0.0), axis=1, keepdims=True)
        return sel, row_gate

    def build_selection(jj):
        n_req = jnp.minimum(n_chunks(jj), n_slots)
        row_gate = jnp.zeros(gsel_ref.shape, F32)
        for k in range(n_slots):
            sel, g = select(jj, jnp.minimum(lo_ref[jj] + k, last_chunk), k < n_req)
            sel_ref[:, k * tc:(k + 1) * tc] = sel
            row_gate = row_gate + g
        gsel_ref[...] = row_gate

    @pl.when(j == 0)
    def _():
        buf_ref[...] = jnp.zeros_like(buf_ref)
        request(0)
        build_selection(0)

    n = n_chunks(j)
    lo = lo_ref[j]

    @pl.when(n > 0)
    def _():
        for k in range(n_slots):
            @pl.when(k < jnp.minimum(n, n_slots))
            def _():
                chunk_copy(lo + k, k).wait()

        acc_ref[...] = _dot(sel_ref[...], buf_ref[...].reshape(n_slots * tc, -1))
        gacc_ref[...] = gsel_ref[...]

        def overflow(k, carry):
            copy = chunk_copy(lo + k, 0)
            copy.start()
            copy.wait()
            sel, row_gate = select(j, lo + k, True)
            acc_ref[...] += _dot(sel, buf_ref[0])
            gacc_ref[...] += row_gate
            return carry

        lax.fori_loop(n_slots, jnp.maximum(n, n_slots), overflow, 0)

    @pl.when(j + 1 < n_blocks)
    def _():
        request(jnp.minimum(j + 1, n_blocks - 1))

    @pl.when(n > 0)
    def _():
        xg = acc_ref[...].astype(BF16)
        act = jax.nn.silu(_dot(xg, wg_ref[0])) * _dot(xg, wu_ref[0])
        ys_ref[...] = (_dot(act.astype(BF16), wd_ref[0]) * gacc_ref[...]).astype(BF16)
        build_selection(jnp.minimum(j + 1, n_blocks - 1))

    @pl.when(n == 0)
    def _():
        ys_ref[...] = jnp.zeros_like(ys_ref)


def _combine_kernel(n_blocks, n_fb, n_ref, ids_ref, h2_ref, pos_ref, g_ref, ys_hbm, yp_ref, yshort_ref, buf_ref, sem):
    i = pl.program_id(0)
    _, n_slots, cr, _ = buf_ref.shape
    per_vreg = LANE // cr
    par = i % 2
    nxt = jnp.minimum(i + 1, n_blocks - 1)

    def chunk_copy(ii, k, which):
        ch = ids_ref[ii * n_slots + k]
        return pltpu.make_async_copy(ys_hbm.at[pl.ds(pl.multiple_of(ch * cr, cr), cr), :], buf_ref.at[which, k],
                                     sem.at[which * n_slots + k])

    def request(ii, which):
        for k in range(n_slots):
            @pl.when(k < n_ref[ii])
            def _():
                chunk_copy(ii, k, which).start()

    @pl.when(i == 0)
    def _():
        buf_ref[...] = jnp.zeros_like(buf_ref)
        request(0, 0)

    @pl.when(i + 1 < n_blocks)
    def _():
        request(nxt, 1 - par)

    lane = lax.broadcasted_iota(jnp.int32, (1, LANE), 1)
    pieces = []
    for v in range(n_slots // per_vreg):
        row = jnp.zeros((1, LANE), jnp.int32)
        for q in range(per_vreg):
            k = v * per_vreg + q
            row = jnp.where(lane // cr == q, ids_ref[i * n_slots + k] * cr + lane % cr, row)
        pieces.append(row)
    slot_rows = jnp.concatenate(pieces, axis=1)
    sel = jnp.where((slot_rows == pos_ref[:, 0:1]) | (slot_rows == pos_ref[:, 1:2]), 1.0, 0.0).astype(BF16)

    for k in range(n_slots):
        @pl.when(k < n_ref[i])
        def _():
            chunk_copy(i, k, par).wait()

    moe = _dot(sel, buf_ref[par].reshape(n_slots * cr, -1))
    y = _rms(h2_ref[...] + moe, g_ref[...])

    @pl.when(i < n_fb)
    def _():
        yp_ref[...] = y

    @pl.when(i >= n_fb)
    def _():
        yshort_ref[...] = y


def _rope_table(pos):
    half = QK_ROPE // 2
    inv = ROPE_THETA ** (-jnp.arange(half, dtype=F32) / half)
    ang = pos.astype(F32)[:, None] * inv[None, :]
    cos = jnp.cos(ang)
    sin = jnp.sin(ang)
    zero = jnp.zeros((pos.shape[0], LANE - QK_ROPE), F32)
    return jnp.concatenate([cos, cos, zero, -sin, sin, zero], axis=1)


def _half_swap(w):
    half = QK_ROPE // 2
    return jnp.concatenate([w[..., half:], w[..., :half]], axis=-1)


def _pad_last(w, width):
    return jnp.pad(w, [(0, 0)] * (w.ndim - 1) + [(0, width - w.shape[-1])])


def kernel(x_prompt, x_sample, cache_kv_latent, cache_k_rope, state_pool, meta_tokens, norm_mix, norm_ffn, norm_final,
           mla_w_dq, mla_g_q, mla_w_uq, mla_w_dkv, mla_g_kv, mla_w_uk, mla_w_uv, mla_w_o, pool_w, pool_scale,
           ffn_w_gate, ffn_w_up, ffn_w_down, moe_w_router, moe_b_router, moe_w_gate, moe_w_up, moe_w_down):
    nb, seq, d = x_prompt.shape
    db, dseq, _ = x_sample.shape
    n_cache = cache_kv_latent.shape[2]
    q_rank = mla_w_dq.shape[2]
    kv_rank = mla_g_kv.shape[1]
    v_dim = mla_w_uv.shape[3]
    d_ff = ffn_w_gate.shape[2]
    n_e = moe_w_router.shape[2]
    d_e = moe_w_gate.shape[3]
    tb, rb, tq = TOKEN_BLOCK, ROW_BLOCK, ATTN_BLOCK
    assert norm_mix.shape[0] == 2 and cache_kv_latent.shape[0] == 1 and state_pool.shape[0] == 1
    assert dseq == SEQ_S and N_META == SEQ_S and meta_tokens.shape[0] == N_META
    assert POOL_WINDOWS == tuple(2 ** (k + 1) for k in range(len(POOL_WINDOWS))) and HIST_PAD % HIST_ROWS == 0
    assert kv_rank == 2 * LANE and QK_NOPE == LANE and QK_ROPE <= LANE and HIST_ROWS >= POOL_HIST
    assert seq % tb == 0 and tb == tq and tq % CHUNK == 0 and tb % SEQ_S == 0 and d % LANE == 0
    assert (n_cache - N_META) % CHUNK == 0 and dseq <= CHUNK

    nf = nb * seq
    n_valid_seq = db + 1
    ns = -(-(n_valid_seq * SEQ_S) // tb) * tb
    n_seq = ns // SEQ_S
    nt = nf + ns
    n_fb, n_sb, n_tb = nf // tb, ns // tb, nt // tb
    sb = seq // tb
    meta_row = nf + db * SEQ_S
    scale = float((QK_NOPE + QK_ROPE) ** -0.5)
    hq = 3 * LANE

    xp = x_prompt.reshape(nf, d)
    xs = jnp.concatenate([x_sample.reshape(db * SEQ_S, d), meta_tokens.astype(x_prompt.dtype),
                          jnp.zeros((ns - n_valid_seq * SEQ_S, d), x_prompt.dtype)], axis=0)

    t_s = jnp.arange(SEQ_S)
    pos_short = jnp.concatenate([jnp.tile(n_cache + t_s, db), jnp.tile(t_s, n_seq - db)])
    tab_frames = _rope_table(N_META + jnp.arange(seq))
    tab = jnp.concatenate([tab_frames, _rope_table(pos_short)], axis=0)

    wuq = mla_w_uq[0].reshape(q_rank, N_HEADS, QK_NOPE + QK_ROPE)
    wuq_pe = wuq[:, :, QK_NOPE:]
    wuq2 = jnp.concatenate([wuq[:, :, :QK_NOPE].reshape(q_rank, -1),
                            _pad_last(wuq_pe, LANE).reshape(q_rank, -1),
                            _pad_last(_half_swap(wuq_pe), LANE).reshape(q_rank, -1)], axis=1).astype(BF16)
    wuq_t = jnp.concatenate([wuq[:, :, :QK_NOPE].reshape(q_rank, -1), wuq_pe.reshape(q_rank, -1),
                             _half_swap(wuq_pe).reshape(q_rank, -1)], axis=1).T.astype(BF16)
    wdkv_r =mla_w_dkv[0][:, kv_rank:]
    wdkv2 = jnp.concatenate([mla_w_dkv[0][:, :kv_rank], _pad_last(wdkv_r, LANE),
                             _pad_last(_half_swap(wdkv_r), LANE)], axis=1).astype(BF16)
    wuk_t = jnp.transpose(mla_w_uk[0], (1, 2, 0)).astype(BF16)
    wuk_flat = mla_w_uk[0].reshape(kv_rank, N_HEADS * QK_NOPE).astype(BF16)
    wuv = jnp.transpose(mla_w_uv[0], (1, 0, 2)).astype(BF16)
    wuv_t = mla_w_uv[0].reshape(kv_rank, N_HEADS * v_dim).T.astype(BF16)
    row = lambda v: v.reshape(1, -1)
    hk = 2 * LANE
    hx = v_dim + BF16_SUBLANES

    tok_p = pl.BlockSpec((tb, d), lambda i: (jnp.minimum(i, n_fb - 1), 0))
    tok_s = pl.BlockSpec((tb, d), lambda i: (jnp.maximum(i - n_fb, 0), 0))
    short_tok = lambda i: (jnp.maximum(i - n_fb, 0), 0)
    q_short, qt_frames, kvb_short, kn_all, kr_all, vt_all, c_all, r_all = pl.pallas_call(
        functools.partial(_qkv_kernel, n_fb, scale * LOG2_E),
        grid=(n_tb,),
        in_specs=[tok_p, tok_s,
                  pl.BlockSpec((tb, 2 * LANE), lambda i: (jnp.where(i < n_fb, i % sb, sb + i - n_fb), 0)),
                  pl.BlockSpec((2 * LANE, tb), lambda i: (0, i % sb)),
                  _const_spec((1, d)), _const_spec((d, q_rank)), _const_spec((1, q_rank)),
                  _const_spec(wuq2.shape), _const_spec(wuq_t.shape), _const_spec(wuk_t.shape),
                  _const_spec(wdkv2.shape), _const_spec((1, kv_rank)), _const_spec(wuk_flat.shape),
                  _const_spec(wuv_t.shape)],
        out_specs=[pl.BlockSpec((tb, N_HEADS * hq), short_tok),
                   pl.BlockSpec((1, N_HEADS, hk, tb), lambda i: (jnp.minimum(i, n_fb - 1), 0, 0, 0)),
                   pl.BlockSpec((tb, hq), short_tok),
                   pl.BlockSpec((tb, N_HEADS * QK_NOPE), lambda i: (i, 0)),
                   pl.BlockSpec((tb, LANE), lambda i: (i, 0)),
                   pl.BlockSpec((1, N_HEADS * hx, tb), lambda i: (i, 0, 0)),
                   pl.BlockSpec((tb, kv_rank), lambda i: (i, 0)),
                   pl.BlockSpec((tb, QK_ROPE), lambda i: (i, 0))],
        out_shape=[jax.ShapeDtypeStruct((ns, N_HEADS * hq), BF16),
                   jax.ShapeDtypeStruct((n_fb, N_HEADS, hk, tb), BF16),
                   jax.ShapeDtypeStruct((ns, hq), BF16),
                   jax.ShapeDtypeStruct((nt, N_HEADS * QK_NOPE), BF16), jax.ShapeDtypeStruct((nt, LANE), BF16),
                   jax.ShapeDtypeStruct((n_tb, N_HEADS * hx, tb), BF16),
                   jax.ShapeDtypeStruct((nt, kv_rank), F32), jax.ShapeDtypeStruct((nt, QK_ROPE), F32)],
        compiler_params=_params(("arbitrary",)),
        name="qkv",
    )(xp, xs, tab, tab_frames.T, row(norm_mix[0]), mla_w_dq[0].astype(BF16), row(mla_g_q[0]), wuq2, wuq_t, wuk_t,
      wdkv2, row(mla_g_kv[0]), wuk_flat, wuv_t)

    qb = seq // tq
    once = pl.Buffered(1)
    o_frames = pl.pallas_call(
        functools.partial(_attn_frames_kernel, qb, meta_row % tb),
        grid=(n_fb,),
        in_specs=[pl.BlockSpec((1, N_HEADS, hk, tq), lambda i: (i, 0, 0, 0)),
                  pl.BlockSpec((seq, N_HEADS * QK_NOPE), lambda i: (i // qb, 0)),
                  pl.BlockSpec((seq, LANE), lambda i: (i // qb, 0)),
                  pl.BlockSpec((qb, N_HEADS * hx, tq), lambda i: (i // qb, 0, 0)),
                  pl.BlockSpec((N_META, N_HEADS * QK_NOPE), lambda i: (meta_row // N_META, 0), pipeline_mode=once),
                  pl.BlockSpec((N_META, LANE), lambda i: (meta_row // N_META, 0), pipeline_mode=once),
                  pl.BlockSpec((1, N_HEADS * hx, tb), lambda i: (meta_row // tb, 0, 0), pipeline_mode=once)],
        out_specs=pl.BlockSpec((tq, N_HEADS * v_dim), lambda i: (i, 0)),
        out_shape=jax.ShapeDtypeStruct((nf, N_HEADS * v_dim), BF16),
        scratch_shapes=[pltpu.VMEM((N_HEADS, 1, tq), F32), pltpu.VMEM((N_HEADS, hx, tq), F32)],
        compiler_params=_params(("parallel",)),
        name="attn_frames",
    )(qt_frames, kn_all, kr_all, vt_all, kn_all, kr_all, vt_all)

    o_short = pl.pallas_call(
        functools.partial(_attn_short_kernel, db),
        grid=(n_seq,),
        in_specs=[pl.BlockSpec((SEQ_S, N_HEADS * hq), lambda s: (s, 0)),
                  pl.BlockSpec((SEQ_S, hq), lambda s: (s, 0)),
                  pl.BlockSpec((1, 1, n_cache, kv_rank), lambda s: (0, jnp.minimum(s, db - 1), 0, 0)),
                  pl.BlockSpec((1, 1, n_cache, QK_ROPE), lambda s: (0, jnp.minimum(s, db - 1), 0, 0)),
                  _const_spec(wuv.shape)],
        out_specs=pl.BlockSpec((SEQ_S, N_HEADS * v_dim), lambda s: (s, 0)),
        out_shape=jax.ShapeDtypeStruct((ns, N_HEADS * v_dim), BF16),
        compiler_params=_params(("parallel",)),
        name="attn_short",
    )(q_short, kvb_short, cache_kv_latent, cache_k_rope, wuv)

    moe_f32 = [moe_w_gate[0].reshape(n_e * d, d_e), moe_w_up[0].reshape(n_e * d, d_e),
               moe_w_down[0].reshape(n_e * d_e, d)]
    cast_in, cast_out, cast_shapes = [], [], []
    for w in moe_f32:
        steps = max(s for s in range(1, n_tb + 1) if w.shape[0] % s == 0 and (w.shape[0] // s) % BF16_SUBLANES == 0)
        spec = pl.BlockSpec((w.shape[0] // steps, w.shape[1]), lambda i, last=steps - 1: (jnp.minimum(i, last), 0))
        cast_in.append(spec)
        cast_out.append(spec)
        cast_shapes.append(jax.ShapeDtypeStruct(w.shape, BF16))
    h1, wg_e, wu_e, wd_e = pl.pallas_call(
        functools.partial(_proj_ffn_kernel, n_fb),
        grid=(n_tb,),
        in_specs=[tok_p, tok_s,
                  pl.BlockSpec((tb, N_HEADS * v_dim), lambda i: (jnp.minimum(i, n_fb - 1), 0)),
                  pl.BlockSpec((tb, N_HEADS * v_dim), short_tok),
                  _const_spec((N_HEADS * v_dim, d)), _const_spec((1, d)),
                  _const_spec((d, d_ff)), _const_spec((d, d_ff)), _const_spec((d_ff, d))] + cast_in,
        out_specs=[pl.BlockSpec((tb, d), lambda i: (i, 0))] + cast_out,
        out_shape=[jax.ShapeDtypeStruct((nt, d), F32)] + cast_shapes,
        compiler_params=_params(("arbitrary",)),
        name="proj_ffn",
    )(xp, xs, o_frames, o_short, mla_w_o[0].astype(BF16), row(norm_ffn[0]),
      ffn_w_gate[0].astype(BF16), ffn_w_up[0].astype(BF16), ffn_w_down[0].astype(BF16), *moe_f32)
    wg_e = wg_e.reshape(n_e, d, d_e)
    wu_e = wu_e.reshape(n_e, d, d_e)
    wd_e = wd_e.reshape(n_e, d_e, d)

    route_w = [_const_spec((1, d)), _const_spec(pool_w.shape[1:]), _const_spec((1, d)), _const_spec((1, d)),
               _const_spec((d, 2 * n_e)), _const_spec((1, n_e))]
    wr_hi = moe_w_router[0].astype(BF16)
    wr_lo = (moe_w_router[0].astype(F32) - wr_hi.astype(F32)).astype(BF16)
    route_args = (row(norm_mix[1]), pool_w[0].astype(BF16), row(pool_scale[0]), row(norm_ffn[1]),
                  jnp.concatenate([wr_hi, wr_lo], axis=1), row(moe_b_router[0]))
    tri = lambda n: (jnp.arange(n)[:, None] > jnp.arange(n)[None, :]).astype(BF16)
    n_win = len(POOL_WINDOWS)
    stage_scratch = lambda t: [pltpu.VMEM((HIST_PAD + t, d // n_win * (n_win - 1 - k)), F32) for k in range(n_win - 1)]
    hist_blk = lambda i: jnp.where(i % sb == 0, meta_row // HIST_ROWS, i * (tb // HIST_ROWS) - 1)
    h2_f, u2_f, info_f, cum_f, tot_f, state_f = pl.pallas_call(
        functools.partial(_pool_frames_kernel, n_fb),
        grid=(n_tb,),
        in_specs=[pl.BlockSpec((tb, d), lambda i: (i, 0)),
                  pl.BlockSpec((HIST_ROWS, d), lambda i: (hist_blk(i), 0))] + route_w + [_const_spec((tb, tb))],
        out_specs=[pl.BlockSpec((tb, d), lambda i: (i, 0)), pl.BlockSpec((tb, d), lambda i: (i, 0)),
                   pl.BlockSpec((tb, n_e), lambda i: (i, 0)),
                   pl.BlockSpec((1, 1, n_e), lambda i: (jnp.minimum(i, n_fb - 1), 0, 0)),
                   pl.BlockSpec((1, n_e), lambda i: (0, 0)),
                   pl.BlockSpec((1, HIST_ROWS, d), lambda i: (jnp.minimum(i // sb, nb - 1), 0, 0))],
        out_shape=[jax.ShapeDtypeStruct((nt, d), F32), jax.ShapeDtypeStruct((nt, d), BF16),
                   jax.ShapeDtypeStruct((nt, n_e), F32), jax.ShapeDtypeStruct((n_fb, 1, n_e), F32),
                   jax.ShapeDtypeStruct((1, n_e), F32), jax.ShapeDtypeStruct((nb, HIST_ROWS, d), F32)],
        scratch_shapes=[pltpu.VMEM((HIST_PAD + tb, d), F32), pltpu.VMEM((1, n_e), F32)] + stage_scratch(tb),
        compiler_params=_params(("arbitrary",)),
        name="pool_route_frames",
    )(h1, h1, *route_args, tri(tb))

    hist_s = jnp.concatenate([
        jnp.pad(state_pool[0].astype(F32), [(0, 0), (HIST_ROWS - POOL_HIST, 0), (0, 0)]),
        jnp.zeros((n_seq - db, HIST_ROWS, d), F32)], axis=0)
    any_spec = pl.BlockSpec(memory_space=pl.ANY)
    short_blk = lambda s: (nf // SEQ_S + s, 0)
    h2, u2, info, cum_s, tot_s, state_s = pl.pallas_call(
        functools.partial(_pool_short_kernel, db, n_valid_seq),
        grid=(n_seq,),
        in_specs=[any_spec, any_spec, any_spec,
                  pl.BlockSpec((SEQ_S, d), short_blk),
                  pl.BlockSpec((1, HIST_ROWS, d), lambda s: (s, 0, 0)),
                  _const_spec((1, n_e))] + route_w + [_const_spec((SEQ_S, SEQ_S))],
        out_specs=[pl.BlockSpec((SEQ_S, d), short_blk), pl.BlockSpec((SEQ_S, d), short_blk),
                   pl.BlockSpec((SEQ_S, n_e), short_blk), pl.BlockSpec((1, 1, n_e), lambda s: (s, 0, 0)),
                   pl.BlockSpec((1, n_e), lambda s: (0, 0)),
                   pl.BlockSpec((1, SEQ_S, d), lambda s: (s, 0, 0))],
        out_shape=[jax.ShapeDtypeStruct((nt, d), F32), jax.ShapeDtypeStruct((nt, d), BF16),
                   jax.ShapeDtypeStruct((nt, n_e), F32), jax.ShapeDtypeStruct((n_seq, 1, n_e), F32),
                   jax.ShapeDtypeStruct((1, n_e), F32), jax.ShapeDtypeStruct((n_seq, SEQ_S, d), F32)],
        scratch_shapes=[pltpu.VMEM((HIST_PAD + SEQ_S, d), F32), pltpu.VMEM((1, n_e), F32)] + stage_scratch(SEQ_S),
        input_output_aliases={0: 0, 1: 1, 2: 2},
        compiler_params=_params(("arbitrary",)),
        name="pool_route_short",
    )(h2_f, u2_f, info_f, h1, hist_s, tot_f, *route_args, tri(SEQ_S))

    counts = tot_s[0].astype(jnp.int32)
    cum = jnp.concatenate([cum_f[:, 0], cum_s[::tb // SEQ_S, 0], tot_s], axis=0).astype(jnp.int32)
    padded = (counts + rb - 1) // rb * rb
    pend = jnp.cumsum(padded)
    pstart = pend - padded
    n_rows_max = -(-(TOP_K * (nf + n_valid_seq * SEQ_S)) // rb) * rb + n_e * rb
    n_rb = n_rows_max // rb
    e_tok = info[:, 0:TOP_K].astype(jnp.int32)
    pos = jnp.where(e_tok >= 0, pstart[jnp.maximum(e_tok, 0)] + info[:, 2:2 + TOP_K].astype(jnp.int32), -1)
    gates = info[:, 4:4 + TOP_K]
    pos_l = pos.reshape(n_tb, tb, TOP_K).transpose(0, 2, 1)
    gate_l = gates.reshape(n_tb, tb, TOP_K).transpose(0, 2, 1)
    pos_c = jnp.pad(pos, [(0, 0), (0, n_e - TOP_K)], constant_values=-1)

    blk_row = jnp.arange(n_rb, dtype=jnp.int32) * rb
    blk_e = jnp.minimum(jnp.sum(pend[None, :] <= blk_row[:, None], axis=1), n_e - 1).astype(jnp.int32)
    blk_ok = (blk_row < pend[-1]).astype(jnp.int32)
    r_lo = blk_row - pstart[blk_e]
    r_hi = jnp.minimum(r_lo + rb, counts[blk_e])
    cum_b = cum.T[blk_e]
    first = jnp.sum(cum_b[:, 1:] <= r_lo[:, None], axis=1)
    last = jnp.sum(cum_b[:, :-1] < r_hi[:, None], axis=1) - 1
    blk_lo = jnp.clip(first, 0, n_tb - 1).astype(jnp.int32)
    blk_hi = jnp.clip(last, blk_lo, n_tb - 1).astype(jnp.int32)

    ys = pl.pallas_call(
        functools.partial(_expert_kernel, n_rb),
        grid_spec=pltpu.PrefetchScalarGridSpec(
            num_scalar_prefetch=4,
            grid=(n_rb,),
            in_specs=[pl.BlockSpec(memory_space=pl.ANY),
                      _const_spec((n_tb, TOP_K, tb)), _const_spec((n_tb, TOP_K, tb)),
                      pl.BlockSpec((1, d, d_e), lambda j, be, lo, hi, ok: (be[j], 0, 0)),
                      pl.BlockSpec((1, d, d_e), lambda j, be, lo, hi, ok: (be[j], 0, 0)),
                      pl.BlockSpec((1, d_e, d), lambda j, be, lo, hi, ok: (be[j], 0, 0))],
            out_specs=pl.BlockSpec((rb, d), lambda j, be, lo, hi, ok: (j, 0)),
            scratch_shapes=[pltpu.VMEM((GATHER_SLOTS, tb, d), BF16), pltpu.VMEM((rb, GATHER_SLOTS * tb), BF16),
                            pltpu.VMEM((rb, 1), F32), pltpu.VMEM((rb, d), F32), pltpu.VMEM((rb, 1), F32),
                            pltpu.SemaphoreType.DMA((GATHER_SLOTS,))]),
        out_shape=jax.ShapeDtypeStruct((n_rows_max, d), BF16),
        compiler_params=_params(("arbitrary",)),
        name="experts",
    )(blk_e, blk_lo, blk_hi, blk_ok, u2, pos_l, gate_l, wg_e, wu_e, wd_e)

    cr = COMBINE_ROWS
    per_expert = tb // cr + 1
    n_slots = TOP_K * tb // cr + 2 * n_e
    assert rb % cr == 0 and LANE % cr == 0 and n_slots % (LANE // cr) == 0
    w_lo = pstart[None, :] + cum[:-1]
    w_hi = pstart[None, :] + cum[1:]
    c_lo = w_lo // cr
    n_chunks = jnp.where(w_hi > w_lo, (w_hi - 1) // cr - c_lo + 1, 0)
    q = jnp.arange(per_expert)
    cand = (c_lo[:, :, None] + q).reshape(n_tb, n_e * per_expert)
    keep = (q < n_chunks[:, :, None]).reshape(n_tb, n_e * per_expert)
    order = jnp.argsort(~keep, axis=1, stable=True)[:, :n_slots]
    ids = jnp.where(jnp.take_along_axis(keep, order, axis=1), jnp.take_along_axis(cand, order, axis=1),
                    n_rows_max // cr)
    n_ids = jnp.sum(keep, axis=1).astype(jnp.int32)
    y_frames, y_short = pl.pallas_call(
        functools.partial(_combine_kernel, n_tb, n_fb),
        grid_spec=pltpu.PrefetchScalarGridSpec(
            num_scalar_prefetch=2,
            grid=(n_tb,),
            in_specs=[pl.BlockSpec((tb, d), lambda i, n, ids: (i, 0)),
                      pl.BlockSpec((tb, n_e), lambda i, n, ids: (i, 0)),
                      pl.BlockSpec((1, d), lambda i, n, ids: (0, 0)),
                      pl.BlockSpec(memory_space=pl.ANY)],
            out_specs=[pl.BlockSpec((tb, d), lambda i, n, ids: (jnp.minimum(i, n_fb - 1), 0)),
                       pl.BlockSpec((tb, d), lambda i, n, ids: (jnp.maximum(i - n_fb, 0), 0))],
            scratch_shapes=[pltpu.VMEM((2, n_slots, cr, d), BF16), pltpu.SemaphoreType.DMA((2 * n_slots,))]),
        out_shape=[jax.ShapeDtypeStruct((nf, d), F32), jax.ShapeDtypeStruct((ns, d), F32)],
        compiler_params=_params(("arbitrary",)),
        name="combine",
    )(n_ids, ids.reshape(-1).astype(jnp.int32), h2, pos_c, row(norm_final), ys)

    y_prompt = y_frames.reshape(nb, seq, d)
    y_sample = y_short[:db * SEQ_S].reshape(db, SEQ_S, d)

    def with_meta(a, width):
        meta = jnp.broadcast_to(a[meta_row:meta_row + N_META][None], (nb, N_META, width))
        return jnp.concatenate([meta, a[:nf].reshape(nb, seq, width)], axis=1)[None]

    c_p = with_meta(c_all, kv_rank)
    r_p = with_meta(r_all, QK_ROPE)
    c_s = c_all[nf:nf + db * SEQ_S].reshape(1, db, SEQ_S, kv_rank)
    r_s = r_all[nf:nf + db * SEQ_S].reshape(1, db, SEQ_S, QK_ROPE)
    s_p = state_f[:, HIST_ROWS - POOL_HIST:][None]
    s_s = state_s[:db, SEQ_S - POOL_HIST:][None]
    return (y_prompt, y_sample, c_p, r_p, s_p, c_s, r_s, s_s)
```

```python
import functools

import jax
import jax.numpy as jnp
from jax import lax
from jax.experimental import pallas as pl
from jax.experimental.pallas import tpu as pltpu

CHUNK = 64
N_META = 16
N_HEADS = 8
QK_NOPE = 128
QK_ROPE = 64
ROPE_THETA = 10000.0
POOL_WINDOWS = (2, 4, 8, 16)
POOL_HIST = max(POOL_WINDOWS) - 1
TOP_K = 2
RMS_EPS = 1e-6

LANE = 128
BF16_SUBLANES = 16
SEQ_S = 16
F32_SUBLANES = 8
HIST_ROWS = 16
HIST_PAD = F32_SUBLANES * len(POOL_WINDOWS)
TOKEN_BLOCK = 256
ROW_BLOCK = 256
ATTN_BLOCK = 256
CACHE_BLOCK = 512
KEY_BLOCKS_PER_ITER = 4
SCORES_AHEAD = 8
COMBINE_ROWS = 64
GATHER_SLOTS = 6
V7X_VMEM_BYTES = 64 * 1024 * 1024
VMEM_LIMIT = V7X_VMEM_BYTES // 8 * 7

F32 = jnp.float32
BF16 = jnp.bfloat16
NEG_INF = float("-inf")
LOG2_E = 1.4426950408889634


def _dot(a, b):
    return jnp.dot(a, b, preferred_element_type=F32)


def _dot_nt(a, b):
    return lax.dot_general(a, b, (((1,), (1,)), ((), ())), preferred_element_type=F32)


def _rms(x, g):
    return x * lax.rsqrt(jnp.mean(x * x, axis=-1, keepdims=True) + RMS_EPS) * g


def _const_spec(shape):
    nd = len(shape)
    return pl.BlockSpec(shape, lambda *_: (0,) * nd, pipeline_mode=pl.Buffered(1))


def _params(sem):
    return pltpu.CompilerParams(dimension_semantics=sem, vmem_limit_bytes=VMEM_LIMIT)


def _qkv_kernel(n_fb, scale, xp_ref, xs_ref, tab_ref, tabt_ref, g_ref, wdq_ref, gq_ref, wuq_ref, wuqt_ref, wukt_ref,
                wdkv_ref, gkv_ref, wukf_ref, wuvt_ref, q_ref, qt_ref, kvb_ref, kn_ref, kr_ref, vt_ref, c_ref, r_ref):
    i = pl.program_id(0)
    x = jnp.where(i < n_fb, xp_ref[...], xs_ref[...])
    u = _rms(x, g_ref[...]).astype(BF16)
    cq = _rms(_dot(u, wdq_ref[...]), gq_ref[...]).astype(BF16)
    cos = tab_ref[:, :LANE]
    sin = tab_ref[:, LANE:]
    hq = QK_NOPE + 2 * LANE

    kv = _dot(u, wdkv_ref[...])
    c = _rms(kv[:, :2 * LANE], gkv_ref[...])
    r = kv[:, 2 * LANE:3 * LANE] * cos + kv[:, 3 * LANE:] * sin
    c_ref[...] = c
    r_ref[...] = r[:, :QK_ROPE]
    c_bf = c.astype(BF16)
    r_bf = r.astype(BF16)

    kn_ref[...] = _dot(c_bf, wukf_ref[...]).astype(BF16)
    kr_ref[...] = r_bf
    vt = _dot_nt(wuvt_ref[...], c_bf).astype(BF16)
    hv = vt.shape[0] // N_HEADS
    hv_ext = vt_ref.shape[1] // N_HEADS
    for h in range(N_HEADS):
        vt_ref[0, h * hv_ext:h * hv_ext + hv, :] = vt[h * hv:(h + 1) * hv, :]
        vt_ref[0, h * hv_ext + hv:(h + 1) * hv_ext, :] = jnp.ones((hv_ext - hv, vt.shape[1]), BF16)

    @pl.when(i < n_fb)
    def _():
        qat = _dot_nt(wuqt_ref[...], cq)
        cos_t = tabt_ref[:QK_ROPE, :]
        sin_t = tabt_ref[LANE:LANE + QK_ROPE, :]
        pe0 = N_HEADS * QK_NOPE
        sw0 = pe0 + N_HEADS * QK_ROPE
        for h in range(N_HEADS):
            qt_ref[0, h, :QK_NOPE, :] = (qat[h * QK_NOPE:(h + 1) * QK_NOPE, :] * scale).astype(BF16)
            a = qat[pe0 + h * QK_ROPE:pe0 + (h + 1) * QK_ROPE, :]
            b = qat[sw0 + h * QK_ROPE:sw0 + (h + 1) * QK_ROPE, :]
            qt_ref[0, h, QK_NOPE:QK_NOPE + QK_ROPE, :] = ((a * cos_t + b * sin_t) * scale).astype(BF16)
            qt_ref[0, h, QK_NOPE + QK_ROPE:, :] = jnp.zeros((LANE - QK_ROPE, qt_ref.shape[3]), BF16)

    @pl.when(i >= n_fb)
    def _():
        qa = _dot(cq, wuq_ref[...])
        for h in range(N_HEADS):
            qn = qa[:, h * LANE:(h + 1) * LANE].astype(BF16)
            q_ref[:, h * hq:h * hq + 2 * LANE] = (_dot(qn, wukt_ref[h]) * scale).astype(BF16)
            a = qa[:, (N_HEADS + h) * LANE:(N_HEADS + h + 1) * LANE]
            b = qa[:, (2 * N_HEADS + h) * LANE:(2 * N_HEADS + h + 1) * LANE]
            q_ref[:, h * hq + 2 * LANE:(h + 1) * hq] = ((a * cos + b * sin) * scale).astype(BF16)
        kvb_ref[:, :2 * LANE] = c_bf
        kvb_ref[:, 2 * LANE:] = r_bf


def _attn_frames_kernel(qb, meta_col, qt_ref, kn_ref, kr_ref, vt_ref, knm_ref, krm_ref, vtm_ref, o_ref, m_ref, acc_ref):
    j = pl.program_id(0) % qb
    tq = o_ref.shape[0]
    hn = kn_ref.shape[1] // N_HEADS
    hv = o_ref.shape[1] // N_HEADS
    hx = vt_ref.shape[1] // N_HEADS

    def head_keys(kn, kr, rows, h):
        return jnp.concatenate([kn[rows, h * hn:(h + 1) * hn], kr[rows, :]], axis=1)

    def scores(h, rows):
        return _dot(head_keys(kn_ref, kr_ref, rows, h), qt_ref[0, h])

    m_ref[...] = jnp.full(m_ref.shape, NEG_INF, F32)
    acc_ref[...] = jnp.zeros_like(acc_ref)

    def unmasked(kbs):
        steps = [(kb, h) for kb in kbs for h in range(N_HEADS)]
        block_scores = lambda kb, h: scores(h, pl.ds(pl.multiple_of(kb * tq, tq), tq))
        ahead = [block_scores(*step) for step in steps[:SCORES_AHEAD]]
        for n, (kb, h) in enumerate(steps):
            s = ahead.pop(0)
            if n + SCORES_AHEAD < len(steps):
                ahead.append(block_scores(*steps[n + SCORES_AHEAD]))
            m_old = m_ref[h]
            m_new = jnp.maximum(m_old, jnp.max(s, axis=0, keepdims=True))
            alpha = jnp.exp2(m_old - m_new)
            p = jnp.exp2(s - m_new)
            m_ref[h] = m_new
            acc_ref[h] = alpha * acc_ref[h] + _dot(vt_ref[kb, h * hx:(h + 1) * hx, :], p.astype(BF16))

    def body(t, carry):
        unmasked([KEY_BLOCKS_PER_ITER * t + u for u in range(KEY_BLOCKS_PER_ITER)])
        return carry

    lax.fori_loop(0, j // KEY_BLOCKS_PER_ITER, body, 0)
    for left in range(1, KEY_BLOCKS_PER_ITER):
        @pl.when(j % KEY_BLOCKS_PER_ITER == left)
        def _():
            unmasked([j - left + u for u in range(left)])

    rows = pl.ds(pl.multiple_of(j * tq, tq), tq)
    visible = (lax.broadcasted_iota(jnp.int32, (tq, tq), 0) // CHUNK
               <= lax.broadcasted_iota(jnp.int32, (tq, tq), 1) // CHUNK)

    def last_scores(h):
        return scores(h, rows), _dot(head_keys(knm_ref, krm_ref, slice(None), h), qt_ref[0, h])

    ahead = [last_scores(h) for h in range(SCORES_AHEAD)]
    for h in range(N_HEADS):
        s, s_meta = ahead.pop(0)
        s = jnp.where(visible, s, NEG_INF)
        if h + SCORES_AHEAD < N_HEADS:
            ahead.append(last_scores(h + SCORES_AHEAD))
        m_old = m_ref[h]
        m_new = jnp.maximum(m_old, jnp.maximum(jnp.max(s, axis=0, keepdims=True),
                                               jnp.max(s_meta, axis=0, keepdims=True)))
        alpha = jnp.exp2(m_old - m_new)
        p = jnp.exp2(s - m_new)
        p_meta = jnp.exp2(s_meta - m_new)
        acc = (alpha * acc_ref[h] + _dot(vt_ref[j, h * hx:(h + 1) * hx, :], p.astype(BF16))
               + _dot(vtm_ref[0, h * hx:(h + 1) * hx, meta_col:meta_col + N_META], p_meta.astype(BF16)))
        o_ref[:, h * hv:(h + 1) * hv] = (acc[:hv] * (1.0 / acc[hv:hv + 1])).T.astype(BF16)


def _attn_short_kernel(n_cached_seq, q_ref, kvn_ref, cc_ref, cr_ref, wuv_ref, o_ref):
    s_id = pl.program_id(0)
    hq = q_ref.shape[1] // N_HEADS
    n_cache = cc_ref.shape[2]
    qs = jnp.concatenate([q_ref[:, h * hq:(h + 1) * hq] for h in range(N_HEADS)], axis=0)
    q_lat = qs[:, :2 * LANE]
    q_pe = qs[:, 2 * LANE:2 * LANE + QK_ROPE]
    kvn = kvn_ref[...]
    s = _dot_nt(qs, kvn)
    m = jnp.max(s, axis=1, keepdims=True)
    p = jnp.exp2(s - m)
    l = jnp.sum(p, axis=1, keepdims=True)
    acc = _dot(p.astype(BF16), kvn[:, :2 * LANE])
    has_cache = s_id < n_cached_seq
    start = 0
    while start < n_cache:
        size = min(CACHE_BLOCK, n_cache - start)
        ck = cc_ref[0, 0, start:start + size, :].astype(BF16)
        rk = cr_ref[0, 0, start:start + size, :].astype(BF16)
        s = _dot_nt(q_lat, ck) + _dot_nt(q_pe, rk)
        s = jnp.where(has_cache, s, NEG_INF)
        m_new = jnp.maximum(m, jnp.max(s, axis=1, keepdims=True))
        alpha = jnp.exp2(m - m_new)
        p = jnp.exp2(s - m_new)
        l = alpha * l + jnp.sum(p, axis=1, keepdims=True)
        acc = alpha * acc + _dot(p.astype(BF16), ck)
        m = m_new
        start += size
    o = (acc / l).astype(BF16)
    hv = wuv_ref.shape[2]
    for h in range(N_HEADS):
        o_ref[:, h * hv:(h + 1) * hv] = _dot(o[h * SEQ_S:(h + 1) * SEQ_S, :], wuv_ref[h]).astype(BF16)


def _proj_ffn_kernel(n_fb, xp_ref, xs_ref, op_ref, os_ref, wo_ref, g_ref, wg_ref, wu_ref, wd_ref, *cast_refs):
    n_cast = (len(cast_refs) - 1) // 2
    h_ref = cast_refs[n_cast]
    for src, dst in zip(cast_refs[:n_cast], cast_refs[n_cast + 1:]):
        dst[...] = src[...].astype(BF16)
    i = pl.program_id(0)
    x = jnp.where(i < n_fb, xp_ref[...], xs_ref[...])
    o = jnp.where(i < n_fb, op_ref[...], os_ref[...])
    h1 = x + _dot(o, wo_ref[...])
    u = _rms(h1, g_ref[...]).astype(BF16)
    act = jax.nn.silu(_dot(u, wg_ref[...])) * _dot(u, wu_ref[...])
    h_ref[...] = h1 + _dot(act.astype(BF16), wd_ref[...])


def _set_history(ext_ref, hist):
    for start in range(0, HIST_PAD, HIST_ROWS):
        ext_ref[start:start + HIST_ROWS, :] = hist


def _window_sums(ext_ref, stage_refs, t):
    grp = ext_ref.shape[1] // len(POOL_WINDOWS)
    end = HIST_PAD + t
    first = F32_SUBLANES
    s = ext_ref[first:end, :] + ext_ref[first - 1:end - 1, :]
    sums = [s[HIST_PAD - first:, :grp]]
    for ref, shift in zip(stage_refs, POOL_WINDOWS):
        rest = s[:, grp:]
        ref[first:end, :] = rest
        first += F32_SUBLANES
        s = rest[F32_SUBLANES:, :] + ref[first - shift:end - shift, :]
        sums.append(s[HIST_PAD - first:, :grp])
    return sums


def _pool_route(h1, ext_ref, stage_refs, cnt_rows, valid, base, gm_ref, pw_ref, ps_ref, gf_ref, wr_ref, br_ref,
                tri_ref):
    t = h1.shape[0]
    grp = h1.shape[1] // len(POOL_WINDOWS)
    u1 = _rms(h1, gm_ref[...])
    ext_ref[HIST_PAD:HIST_PAD + t, :] = u1
    sums = _window_sums(ext_ref, stage_refs, t)
    ys = []
    for g, w in enumerate(POOL_WINDOWS):
        cnt = float(w) if cnt_rows is None else jnp.minimum(float(w), cnt_rows)
        mean = sums[g] / cnt - u1[:, g * grp:(g + 1) * grp]
        ys.append(_dot(mean.astype(BF16), pw_ref[g]))
    h2 = h1 + jnp.concatenate(ys, axis=1) * ps_ref[...]
    u2 = _rms(h2, gf_ref[...])

    n_e = wr_ref.shape[1] // 2
    u2_hi = u2.astype(BF16)
    u2_lo = (u2 - u2_hi.astype(F32)).astype(BF16)
    parts = _dot(u2_hi, wr_ref[...]) + _dot(u2_lo, wr_ref[...])
    logits = parts[:, :n_e] + parts[:, n_e:] + br_ref[...]
    e_iota = lax.broadcasted_iota(jnp.int32, logits.shape, 1).astype(F32)
    v0 = jnp.max(logits, axis=1, keepdims=True)
    e0 = jnp.min(jnp.where(logits == v0, e_iota, float(n_e)), axis=1, keepdims=True)
    rest = jnp.where(e_iota == e0, NEG_INF, logits)
    v1 = jnp.max(rest, axis=1, keepdims=True)
    e1 = jnp.min(jnp.where(rest == v1, e_iota, float(n_e)), axis=1, keepdims=True)
    tt = jnp.exp(v1 - v0)
    g0 = 1.0 / (1.0 + tt)
    g1 = tt / (1.0 + tt)
    hit0 = e_iota == e0
    hit1 = e_iota == e1
    onehot = jnp.where(valid & (hit0 | hit1), 1.0, 0.0)
    rank = _dot(tri_ref[...], onehot.astype(BF16)) + base
    r0 = jnp.sum(jnp.where(hit0, rank, 0.0), axis=1, keepdims=True)
    r1 = jnp.sum(jnp.where(hit1, rank, 0.0), axis=1, keepdims=True)
    cols = (jnp.where(valid, e0, -1).astype(F32), jnp.where(valid, e1, -1).astype(F32), r0, r1, g0, g1)
    info = jnp.zeros(logits.shape, F32)
    for k, col in enumerate(cols):
        info = jnp.where(e_iota == k, col, info)
    new_base = base + jnp.sum(onehot, axis=0, keepdims=True)
    return h2, u2, info, new_base, u1


def _pool_frames_kernel(n_fb, h_ref, hist_ref, gm_ref, pw_ref, ps_ref, gf_ref, wr_ref, br_ref, tri_ref,
                        h2_ref, u2_ref, info_ref, cum_ref, tot_ref, state_ref, ext_ref, base_ref, *stage_refs):
    i = pl.program_id(0)

    @pl.when(i == 0)
    def _():
        base_ref[...] = jnp.zeros_like(base_ref)

    @pl.when(i >= n_fb)
    def _():
        h2_ref[...] = jnp.zeros_like(h2_ref)
        u2_ref[...] = jnp.zeros_like(u2_ref)
        info_ref[...] = jnp.zeros_like(info_ref)

    @pl.when(i < n_fb)
    def _():
        _set_history(ext_ref, _rms(hist_ref[...], gm_ref[...]))
        base = base_ref[...]
        cum_ref[0] = base
        h2, u2, info, new_base, u1 = _pool_route(h_ref[...], ext_ref, stage_refs, None, True, base, gm_ref, pw_ref,
                                                 ps_ref, gf_ref, wr_ref, br_ref, tri_ref)
        h2_ref[...] = h2
        u2_ref[...] = u2.astype(BF16)
        info_ref[...] = info
        base_ref[...] = new_base
        tot_ref[...] = new_base
        state_ref[0] = u1[u1.shape[0] - HIST_ROWS:, :]


def _pool_short_kernel(n_sample, n_valid, h2_hbm, u2_hbm, info_hbm, h_ref, hist_ref, base0_ref, gm_ref, pw_ref, ps_ref,
                       gf_ref, wr_ref, br_ref, tri_ref, h2_ref, u2_ref, info_ref, cum_ref, tot_ref, state_ref,
                       ext_ref, base_ref, *stage_refs):
    del h2_hbm, u2_hbm, info_hbm
    s_id = pl.program_id(0)

    @pl.when(s_id == 0)
    def _():
        base_ref[...] = base0_ref[...]

    _set_history(ext_ref, hist_ref[0])
    base = base_ref[...]
    cum_ref[0] = base
    pos1 = (lax.broadcasted_iota(jnp.int32, (SEQ_S, 1), 0) + 1).astype(F32)
    cnt_rows = jnp.where(s_id < n_sample, float(max(POOL_WINDOWS)), pos1)
    h2, u2, info, new_base, u1 = _pool_route(h_ref[...], ext_ref, stage_refs, cnt_rows, s_id < n_valid, base, gm_ref,
                                             pw_ref, ps_ref, gf_ref, wr_ref, br_ref, tri_ref)
    h2_ref[...] = h2
    u2_ref[...] = u2.astype(BF16)
    info_ref[...] = info
    base_ref[...] = new_base
    tot_ref[...] = new_base
    state_ref[0] = u1


def _expert_kernel(n_blocks, be_ref, lo_ref, hi_ref, ok_ref, u_hbm, pos_ref, gate_ref, wg_ref, wu_ref, wd_ref, ys_ref,
                   buf_ref, sel_ref, gsel_ref, acc_ref, gacc_ref, sem):
    del be_ref
    j = pl.program_id(0)
    rb = ys_ref.shape[0]
    n_slots, tc, _ = buf_ref.shape
    last_chunk = pos_ref.shape[0] - 1

    def n_chunks(jj):
        return jnp.where(ok_ref[jj] != 0, hi_ref[jj] - lo_ref[jj] + 1, 0)

    def chunk_copy(c, slot):
        return pltpu.make_async_copy(u_hbm.at[pl.ds(pl.multiple_of(c * tc, tc), tc), :], buf_ref.at[slot],
                                     sem.at[slot])

    def request(jj):
        n_req = jnp.minimum(n_chunks(jj), n_slots)
        for k in range(n_slots):
            @pl.when(k < n_req)
            def _():
                chunk_copy(lo_ref[jj] + k, k).start()

    def select(jj, c, live):
        rows = jj * rb + lax.broadcasted_iota(jnp.int32, (rb, tc), 0)
        pos = pos_ref[c]
        gate = gate_ref[c]
        m0 = (rows == pos[0:1, :]) & live
        m1 = (rows == pos[1:2, :]) & live
        sel = jnp.where(m0 | m1, 1.0, 0.0).astype(BF16)
        row_gate = jnp.sum(jnp.where(m0, gate[0:1, :], 0.0) + jnp.where(m1, gate[1:2, :], 0.0), axis=1, keepdims=True)
        return sel, row_gate

    def build_selection(jj):
        n_req = jnp.minimum(n_chunks(jj), n_slots)
        row_gate = jnp.zeros(gsel_ref.shape, F32)
        for k in range(n_slots):
            sel, g = select(jj, jnp.minimum(lo_ref[jj] + k, last_chunk), k < n_req)
            sel_ref[:, k * tc:(k + 1) * tc] = sel
            row_gate = row_gate + g
        gsel_ref[...] = row_gate

    @pl.when(j == 0)
    def _():
        buf_ref[...] = jnp.zeros_like(buf_ref)
        request(0)
        build_selection(0)

    n = n_chunks(j)
    lo = lo_ref[j]

    @pl.when(n > 0)
    def _():
        for k in range(n_slots):
            @pl.when(k < jnp.minimum(n, n_slots))
            def _():
                chunk_copy(lo + k, k).wait()

        acc_ref[...] = _dot(sel_ref[...], buf_ref[...].reshape(n_slots * tc, -1))
        gacc_ref[...] = gsel_ref[...]

        def overflow(k, carry):
            copy = chunk_copy(lo + k, 0)
            copy.start()
            copy.wait()
            sel, row_gate = select(j, lo + k, True)
            acc_ref[...] += _dot(sel, buf_ref[0])
            gacc_ref[...] += row_gate
            return carry

        lax.fori_loop(n_slots, jnp.maximum(n, n_slots), overflow, 0)

    @pl.when(j + 1 < n_blocks)
    def _():
        request(jnp.minimum(j + 1, n_blocks - 1))

    @pl.when(n > 0)
    def _():
        xg = acc_ref[...].astype(BF16)
        act = jax.nn.silu(_dot(xg, wg_ref[0])) * _dot(xg, wu_ref[0])
        ys_ref[...] = (_dot(act.astype(BF16), wd_ref[0]) * gacc_ref[...]).astype(BF16)
        build_selection(jnp.minimum(j + 1, n_blocks - 1))

    @pl.when(n == 0)
    def _():
        ys_ref[...] = jnp.zeros_like(ys_ref)


def _combine_kernel(n_blocks, n_fb, n_ref, ids_ref, h2_ref, pos_ref, g_ref, ys_hbm, yp_ref, yshort_ref, buf_ref, sem):
    i = pl.program_id(0)
    _, n_slots, cr, _ = buf_ref.shape
    per_vreg = LANE // cr
    par = i % 2
    nxt = jnp.minimum(i + 1, n_blocks - 1)

    def chunk_copy(ii, k, which):
        ch = ids_ref[ii * n_slots + k]
        return pltpu.make_async_copy(ys_hbm.at[pl.ds(pl.multiple_of(ch * cr, cr), cr), :], buf_ref.at[which, k],
                                     sem.at[which * n_slots + k])

    def request(ii, which):
        for k in range(n_slots):
            @pl.when(k < n_ref[ii])
            def _():
                chunk_copy(ii, k, which).start()

    @pl.when(i == 0)
    def _():
        buf_ref[...] = jnp.zeros_like(buf_ref)
        request(0, 0)

    @pl.when(i + 1 < n_blocks)
    def _():
        request(nxt, 1 - par)

    lane = lax.broadcasted_iota(jnp.int32, (1, LANE), 1)
    pieces = []
    for v in range(n_slots // per_vreg):
        row = jnp.zeros((1, LANE), jnp.int32)
        for q in range(per_vreg):
            k = v * per_vreg + q
            row = jnp.where(lane // cr == q, ids_ref[i * n_slots + k] * cr + lane % cr, row)
        pieces.append(row)
    slot_rows = jnp.concatenate(pieces, axis=1)
    sel = jnp.where((slot_rows == pos_ref[:, 0:1]) | (slot_rows == pos_ref[:, 1:2]), 1.0, 0.0).astype(BF16)

    for k in range(n_slots):
        @pl.when(k < n_ref[i])
        def _():
            chunk_copy(i, k, par).wait()

    moe = _dot(sel, buf_ref[par].reshape(n_slots * cr, -1))
    y = _rms(h2_ref[...] + moe, g_ref[...])

    @pl.when(i < n_fb)
    def _():
        yp_ref[...] = y

    @pl.when(i >= n_fb)
    def _():
        yshort_ref[...] = y


def _rope_table(pos):
    half = QK_ROPE // 2
    inv = ROPE_THETA ** (-jnp.arange(half, dtype=F32) / half)
    ang = pos.astype(F32)[:, None] * inv[None, :]
    cos = jnp.cos(ang)
    sin = jnp.sin(ang)
    zero = jnp.zeros((pos.shape[0], LANE - QK_ROPE), F32)
    return jnp.concatenate([cos, cos, zero, -sin, sin, zero], axis=1)


def _half_swap(w):
    half = QK_ROPE // 2
    return jnp.concatenate([w[..., half:], w[..., :half]], axis=-1)


def _pad_last(w, width):
    return jnp.pad(w, [(0, 0)] * (w.ndim - 1) + [(0, width - w.shape[-1])])


def kernel(x_prompt, x_sample, cache_kv_latent, cache_k_rope, state_pool, meta_tokens, norm_mix, norm_ffn, norm_final,
           mla_w_dq, mla_g_q, mla_w_uq, mla_w_dkv, mla_g_kv, mla_w_uk, mla_w_uv, mla_w_o, pool_w, pool_scale,
           ffn_w_gate, ffn_w_up, ffn_w_down, moe_w_router, moe_b_router, moe_w_gate, moe_w_up, moe_w_down):
    nb, seq, d = x_prompt.shape
    db, dseq, _ = x_sample.shape
    n_cache = cache_kv_latent.shape[2]
    q_rank = mla_w_dq.shape[2]
    kv_rank = mla_g_kv.shape[1]
    v_dim = mla_w_uv.shape[3]
    d_ff = ffn_w_gate.shape[2]
    n_e = moe_w_router.shape[2]
    d_e = moe_w_gate.shape[3]
    tb, rb, tq = TOKEN_BLOCK, ROW_BLOCK, ATTN_BLOCK
    assert norm_mix.shape[0] == 2 and cache_kv_latent.shape[0] == 1 and state_pool.shape[0] == 1
    assert dseq == SEQ_S and N_META == SEQ_S and meta_tokens.shape[0] == N_META
    assert POOL_WINDOWS == tuple(2 ** (k + 1) for k in range(len(POOL_WINDOWS))) and HIST_PAD % HIST_ROWS == 0
    assert kv_rank == 2 * LANE and QK_NOPE == LANE and QK_ROPE <= LANE and HIST_ROWS >= POOL_HIST
    assert seq % tb == 0 and tb == tq and tq % CHUNK == 0 and tb % SEQ_S == 0 and d % LANE == 0
    assert (n_cache - N_META) % CHUNK == 0 and dseq <= CHUNK

    nf = nb * seq
    n_valid_seq = db + 1
    ns = -(-(n_valid_seq * SEQ_S) // tb) * tb
    n_seq = ns // SEQ_S
    nt = nf + ns
    n_fb, n_sb, n_tb = nf // tb, ns // tb, nt // tb
    sb = seq // tb
    meta_row = nf + db * SEQ_S
    scale = float((QK_NOPE + QK_ROPE) ** -0.5)
    hq = 3 * LANE

    xp = x_prompt.reshape(nf, d)
    xs = jnp.concatenate([x_sample.reshape(db * SEQ_S, d), meta_tokens.astype(x_prompt.dtype),
                          jnp.zeros((ns - n_valid_seq * SEQ_S, d), x_prompt.dtype)], axis=0)

    t_s = jnp.arange(SEQ_S)
    pos_short = jnp.concatenate([jnp.tile(n_cache + t_s, db), jnp.tile(t_s, n_seq - db)])
    tab_frames = _rope_table(N_META + jnp.arange(seq))
    tab = jnp.concatenate([tab_frames, _rope_table(pos_short)], axis=0)

    wuq = mla_w_uq[0].reshape(q_rank, N_HEADS, QK_NOPE + QK_ROPE)
    wuq_pe = wuq[:, :, QK_NOPE:]
    wuq2 = jnp.concatenate([wuq[:, :, :QK_NOPE].reshape(q_rank, -1),
                            _pad_last(wuq_pe, LANE).reshape(q_rank, -1),
                            _pad_last(_half_swap(wuq_pe), LANE).reshape(q_rank, -1)], axis=1).astype(BF16)
    wuq_t = jnp.concatenate([wuq[:, :, :QK_NOPE].reshape(q_rank, -1), wuq_pe.reshape(q_rank, -1),
                             _half_swap(wuq_pe).reshape(q_rank, -1)], axis=1).T.astype(BF16)
    wdkv_r =mla_w_dkv[0][:, kv_rank:]
    wdkv2 = jnp.concatenate([mla_w_dkv[0][:, :kv_rank], _pad_last(wdkv_r, LANE),
                             _pad_last(_half_swap(wdkv_r), LANE)], axis=1).astype(BF16)
    wuk_t = jnp.transpose(mla_w_uk[0], (1, 2, 0)).astype(BF16)
    wuk_flat = mla_w_uk[0].reshape(kv_rank, N_HEADS * QK_NOPE).astype(BF16)
    wuv = jnp.transpose(mla_w_uv[0], (1, 0, 2)).astype(BF16)
    wuv_t = mla_w_uv[0].reshape(kv_rank, N_HEADS * v_dim).T.astype(BF16)
    row = lambda v: v.reshape(1, -1)
    hk = 2 * LANE
    hx = v_dim + BF16_SUBLANES

    tok_p = pl.BlockSpec((tb, d), lambda i: (jnp.minimum(i, n_fb - 1), 0))
    tok_s = pl.BlockSpec((tb, d), lambda i: (jnp.maximum(i - n_fb, 0), 0))
    short_tok = lambda i: (jnp.maximum(i - n_fb, 0), 0)
    q_short, qt_frames, kvb_short, kn_all, kr_all, vt_all, c_all, r_all = pl.pallas_call(
        functools.partial(_qkv_kernel, n_fb, scale * LOG2_E),
        grid=(n_tb,),
        in_specs=[tok_p, tok_s,
                  pl.BlockSpec((tb, 2 * LANE), lambda i: (jnp.where(i < n_fb, i % sb, sb + i - n_fb), 0)),
                  pl.BlockSpec((2 * LANE, tb), lambda i: (0, i % sb)),
                  _const_spec((1, d)), _const_spec((d, q_rank)), _const_spec((1, q_rank)),
                  _const_spec(wuq2.shape), _const_spec(wuq_t.shape), _const_spec(wuk_t.shape),
                  _const_spec(wdkv2.shape), _const_spec((1, kv_rank)), _const_spec(wuk_flat.shape),
                  _const_spec(wuv_t.shape)],
        out_specs=[pl.BlockSpec((tb, N_HEADS * hq), short_tok),
                   pl.BlockSpec((1, N_HEADS, hk, tb), lambda i: (jnp.minimum(i, n_fb - 1), 0, 0, 0)),
                   pl.BlockSpec((tb, hq), short_tok),
                   pl.BlockSpec((tb, N_HEADS * QK_NOPE), lambda i: (i, 0)),
                   pl.BlockSpec((tb, LANE), lambda i: (i, 0)),
                   pl.BlockSpec((1, N_HEADS * hx, tb), lambda i: (i, 0, 0)),
                   pl.BlockSpec((tb, kv_rank), lambda i: (i, 0)),
                   pl.BlockSpec((tb, QK_ROPE), lambda i: (i, 0))],
        out_shape=[jax.ShapeDtypeStruct((ns, N_HEADS * hq), BF16),
                   jax.ShapeDtypeStruct((n_fb, N_HEADS, hk, tb), BF16),
                   jax.ShapeDtypeStruct((ns, hq), BF16),
                   jax.ShapeDtypeStruct((nt, N_HEADS * QK_NOPE), BF16), jax.ShapeDtypeStruct((nt, LANE), BF16),
                   jax.ShapeDtypeStruct((n_tb, N_HEADS * hx, tb), BF16),
                   jax.ShapeDtypeStruct((nt, kv_rank), F32), jax.ShapeDtypeStruct((nt, QK_ROPE), F32)],
        compiler_params=_params(("arbitrary",)),
        name="qkv",
    )(xp, xs, tab, tab_frames.T, row(norm_mix[0]), mla_w_dq[0].astype(BF16), row(mla_g_q[0]), wuq2, wuq_t, wuk_t,
      wdkv2, row(mla_g_kv[0]), wuk_flat, wuv_t)

    qb = seq // tq
    once = pl.Buffered(1)
    o_frames = pl.pallas_call(
        functools.partial(_attn_frames_kernel, qb, meta_row % tb),
        grid=(n_fb,),
        in_specs=[pl.BlockSpec((1, N_HEADS, hk, tq), lambda i: (i, 0, 0, 0)),
                  pl.BlockSpec((seq, N_HEADS * QK_NOPE), lambda i: (i // qb, 0)),
                  pl.BlockSpec((seq, LANE), lambda i: (i // qb, 0)),
                  pl.BlockSpec((qb, N_HEADS * hx, tq), lambda i: (i // qb, 0, 0)),
                  pl.BlockSpec((N_META, N_HEADS * QK_NOPE), lambda i: (meta_row // N_META, 0), pipeline_mode=once),
                  pl.BlockSpec((N_META, LANE), lambda i: (meta_row // N_META, 0), pipeline_mode=once),
                  pl.BlockSpec((1, N_HEADS * hx, tb), lambda i: (meta_row // tb, 0, 0), pipeline_mode=once)],
        out_specs=pl.BlockSpec((tq, N_HEADS * v_dim), lambda i: (i, 0)),
        out_shape=jax.ShapeDtypeStruct((nf, N_HEADS * v_dim), BF16),
        scratch_shapes=[pltpu.VMEM((N_HEADS, 1, tq), F32), pltpu.VMEM((N_HEADS, hx, tq), F32)],
        compiler_params=_params(("parallel",)),
        name="attn_frames",
    )(qt_frames, kn_all, kr_all, vt_all, kn_all, kr_all, vt_all)

    o_short = pl.pallas_call(
        functools.partial(_attn_short_kernel, db),
        grid=(n_seq,),
        in_specs=[pl.BlockSpec((SEQ_S, N_HEADS * hq), lambda s: (s, 0)),
                  pl.BlockSpec((SEQ_S, hq), lambda s: (s, 0)),
                  pl.BlockSpec((1, 1, n_cache, kv_rank), lambda s: (0, jnp.minimum(s, db - 1), 0, 0)),
                  pl.BlockSpec((1, 1, n_cache, QK_ROPE), lambda s: (0, jnp.minimum(s, db - 1), 0, 0)),
                  _const_spec(wuv.shape)],
        out_specs=pl.BlockSpec((SEQ_S, N_HEADS * v_dim), lambda s: (s, 0)),
        out_shape=jax.ShapeDtypeStruct((ns, N_HEADS * v_dim), BF16),
        compiler_params=_params(("parallel",)),
        name="attn_short",
    )(q_short, kvb_short, cache_kv_latent, cache_k_rope, wuv)

    moe_f32 = [moe_w_gate[0].reshape(n_e * d, d_e), moe_w_up[0].reshape(n_e * d, d_e),
               moe_w_down[0].reshape(n_e * d_e, d)]
    cast_in, cast_out, cast_shapes = [], [], []
    for w in moe_f32:
        steps = max(s for s in range(1, n_tb + 1) if w.shape[0] % s == 0 and (w.shape[0] // s) % BF16_SUBLANES == 0)
        spec = pl.BlockSpec((w.shape[0] // steps, w.shape[1]), lambda i, last=steps - 1: (jnp.minimum(i, last), 0))
        cast_in.append(spec)
        cast_out.append(spec)
        cast_shapes.append(jax.ShapeDtypeStruct(w.shape, BF16))
    h1, wg_e, wu_e, wd_e = pl.pallas_call(
        functools.partial(_proj_ffn_kernel, n_fb),
        grid=(n_tb,),
        in_specs=[tok_p, tok_s,
                  pl.BlockSpec((tb, N_HEADS * v_dim), lambda i: (jnp.minimum(i, n_fb - 1), 0)),
                  pl.BlockSpec((tb, N_HEADS * v_dim), short_tok),
                  _const_spec((N_HEADS * v_dim, d)), _const_spec((1, d)),
                  _const_spec((d, d_ff)), _const_spec((d, d_ff)), _const_spec((d_ff, d))] + cast_in,
        out_specs=[pl.BlockSpec((tb, d), lambda i: (i, 0))] + cast_out,
        out_shape=[jax.ShapeDtypeStruct((nt, d), F32)] + cast_shapes,
        compiler_params=_params(("arbitrary",)),
        name="proj_ffn",
    )(xp, xs, o_frames, o_short, mla_w_o[0].astype(BF16), row(norm_ffn[0]),
      ffn_w_gate[0].astype(BF16), ffn_w_up[0].astype(BF16), ffn_w_down[0].astype(BF16), *moe_f32)
    wg_e = wg_e.reshape(n_e, d, d_e)
    wu_e = wu_e.reshape(n_e, d, d_e)
    wd_e = wd_e.reshape(n_e, d_e, d)

    route_w = [_const_spec((1, d)), _const_spec(pool_w.shape[1:]), _const_spec((1, d)), _const_spec((1, d)),
               _const_spec((d, 2 * n_e)), _const_spec((1, n_e))]
    wr_hi = moe_w_router[0].astype(BF16)
    wr_lo = (moe_w_router[0].astype(F32) - wr_hi.astype(F32)).astype(BF16)
    route_args = (row(norm_mix[1]), pool_w[0].astype(BF16), row(pool_scale[0]), row(norm_ffn[1]),
                  jnp.concatenate([wr_hi, wr_lo], axis=1), row(moe_b_router[0]))
    tri = lambda n: (jnp.arange(n)[:, None] > jnp.arange(n)[None, :]).astype(BF16)
    n_win = len(POOL_WINDOWS)
    stage_scratch = lambda t: [pltpu.VMEM((HIST_PAD + t, d // n_win * (n_win - 1 - k)), F32) for k in range(n_win - 1)]
    hist_blk = lambda i: jnp.where(i % sb == 0, meta_row // HIST_ROWS, i * (tb // HIST_ROWS) - 1)
    h2_f, u2_f, info_f, cum_f, tot_f, state_f = pl.pallas_call(
        functools.partial(_pool_frames_kernel, n_fb),
        grid=(n_tb,),
        in_specs=[pl.BlockSpec((tb, d), lambda i: (i, 0)),
                  pl.BlockSpec((HIST_ROWS, d), lambda i: (hist_blk(i), 0))] + route_w + [_const_spec((tb, tb))],
        out_specs=[pl.BlockSpec((tb, d), lambda i: (i, 0)), pl.BlockSpec((tb, d), lambda i: (i, 0)),
                   pl.BlockSpec((tb, n_e), lambda i: (i, 0)),
                   pl.BlockSpec((1, 1, n_e), lambda i: (jnp.minimum(i, n_fb - 1), 0, 0)),
                   pl.BlockSpec((1, n_e), lambda i: (0, 0)),
                   pl.BlockSpec((1, HIST_ROWS, d), lambda i: (jnp.minimum(i // sb, nb - 1), 0, 0))],
        out_shape=[jax.ShapeDtypeStruct((nt, d), F32), jax.ShapeDtypeStruct((nt, d), BF16),
                   jax.ShapeDtypeStruct((nt, n_e), F32), jax.ShapeDtypeStruct((n_fb, 1, n_e), F32),
                   jax.ShapeDtypeStruct((1, n_e), F32), jax.ShapeDtypeStruct((nb, HIST_ROWS, d), F32)],
        scratch_shapes=[pltpu.VMEM((HIST_PAD + tb, d), F32), pltpu.VMEM((1, n_e), F32)] + stage_scratch(tb),
        compiler_params=_params(("arbitrary",)),
        name="pool_route_frames",
    )(h1, h1, *route_args, tri(tb))

    hist_s = jnp.concatenate([
        jnp.pad(state_pool[0].astype(F32), [(0, 0), (HIST_ROWS - POOL_HIST, 0), (0, 0)]),
        jnp.zeros((n_seq - db, HIST_ROWS, d), F32)], axis=0)
    any_spec = pl.BlockSpec(memory_space=pl.ANY)
    short_blk = lambda s: (nf // SEQ_S + s, 0)
    h2, u2, info, cum_s, tot_s, state_s = pl.pallas_call(
        functools.partial(_pool_short_kernel, db, n_valid_seq),
        grid=(n_seq,),
        in_specs=[any_spec, any_spec, any_spec,
                  pl.BlockSpec((SEQ_S, d), short_blk),
                  pl.BlockSpec((1, HIST_ROWS, d), lambda s: (s, 0, 0)),
                  _const_spec((1, n_e))] + route_w + [_const_spec((SEQ_S, SEQ_S))],
        out_specs=[pl.BlockSpec((SEQ_S, d), short_blk), pl.BlockSpec((SEQ_S, d), short_blk),
                   pl.BlockSpec((SEQ_S, n_e), short_blk), pl.BlockSpec((1, 1, n_e), lambda s: (s, 0, 0)),
                   pl.BlockSpec((1, n_e), lambda s: (0, 0)),
                   pl.BlockSpec((1, SEQ_S, d), lambda s: (s, 0, 0))],
        out_shape=[jax.ShapeDtypeStruct((nt, d), F32), jax.ShapeDtypeStruct((nt, d), BF16),
                   jax.ShapeDtypeStruct((nt, n_e), F32), jax.ShapeDtypeStruct((n_seq, 1, n_e), F32),
                   jax.ShapeDtypeStruct((1, n_e), F32), jax.ShapeDtypeStruct((n_seq, SEQ_S, d), F32)],
        scratch_shapes=[pltpu.VMEM((HIST_PAD + SEQ_S, d), F32), pltpu.VMEM((1, n_e), F32)] + stage_scratch(SEQ_S),
        input_output_aliases={0: 0, 1: 1, 2: 2},
        compiler_params=_params(("arbitrary",)),
        name="pool_route_short",
    )(h2_f, u2_f, info_f, h1, hist_s, tot_f, *route_args, tri(SEQ_S))

    counts = tot_s[0].astype(jnp.int32)
    cum = jnp.concatenate([cum_f[:, 0], cum_s[::tb // SEQ_S, 0], tot_s], axis=0).astype(jnp.int32)
    padded = (counts + rb - 1) // rb * rb
    pend = jnp.cumsum(padded)
    pstart = pend - padded
    n_rows_max = -(-(TOP_K * (nf + n_valid_seq * SEQ_S)) // rb) * rb + n_e * rb
    n_rb = n_rows_max // rb
    info_t = info.T
    e_tok = info_t[0:TOP_K].astype(jnp.int32)
    rank = info_t[TOP_K:2 * TOP_K].astype(jnp.int32)
    first_row = jnp.sum(jnp.where(e_tok[None] == jnp.arange(n_e)[:, None, None], pstart[:, None, None], 0), axis=0)
    pos = jnp.where(e_tok >= 0, first_row + rank, -1)
    pos_l = pos.reshape(TOP_K, n_tb, tb).transpose(1, 0, 2)
    gate_l = info_t[2 * TOP_K:3 * TOP_K].reshape(TOP_K, n_tb, tb).transpose(1, 0, 2)
    pos_c = jnp.pad(pos, [(0, n_e - TOP_K), (0, 0)], constant_values=-1).T

    blk_row = jnp.arange(n_rb, dtype=jnp.int32) * rb
    blk_e = jnp.minimum(jnp.sum(pend[None, :] <= blk_row[:, None], axis=1), n_e - 1).astype(jnp.int32)
    blk_ok = (blk_row < pend[-1]).astype(jnp.int32)
    r_lo = blk_row - pstart[blk_e]
    r_hi = jnp.minimum(r_lo + rb, counts[blk_e])
    cum_b = cum.T[blk_e]
    first = jnp.sum(cum_b[:, 1:] <= r_lo[:, None], axis=1)
    last = jnp.sum(cum_b[:, :-1] < r_hi[:, None], axis=1) - 1
    blk_lo = jnp.clip(first, 0, n_tb - 1).astype(jnp.int32)
    blk_hi = jnp.clip(last, blk_lo, n_tb - 1).astype(jnp.int32)

    ys = pl.pallas_call(
        functools.partial(_expert_kernel, n_rb),
        grid_spec=pltpu.PrefetchScalarGridSpec(
            num_scalar_prefetch=4,
            grid=(n_rb,),
            in_specs=[pl.BlockSpec(memory_space=pl.ANY),
                      _const_spec((n_tb, TOP_K, tb)), _const_spec((n_tb, TOP_K, tb)),
                      pl.BlockSpec((1, d, d_e), lambda j, be, lo, hi, ok: (be[j], 0, 0)),
                      pl.BlockSpec((1, d, d_e), lambda j, be, lo, hi, ok: (be[j], 0, 0)),
                      pl.BlockSpec((1, d_e, d), lambda j, be, lo, hi, ok: (be[j], 0, 0))],
            out_specs=pl.BlockSpec((rb, d), lambda j, be, lo, hi, ok: (j, 0)),
            scratch_shapes=[pltpu.VMEM((GATHER_SLOTS, tb, d), BF16), pltpu.VMEM((rb, GATHER_SLOTS * tb), BF16),
                            pltpu.VMEM((rb, 1), F32), pltpu.VMEM((rb, d), F32), pltpu.VMEM((rb, 1), F32),
                            pltpu.SemaphoreType.DMA((GATHER_SLOTS,))]),
        out_shape=jax.ShapeDtypeStruct((n_rows_max, d), BF16),
        compiler_params=_params(("arbitrary",)),
        name="experts",
    )(blk_e, blk_lo, blk_hi, blk_ok, u2, pos_l, gate_l, wg_e, wu_e, wd_e)

    cr = COMBINE_ROWS
    per_expert = tb // cr + 1
    n_slots = TOP_K * tb // cr + 2 * n_e
    assert rb % cr == 0 and LANE % cr == 0 and n_slots % (LANE // cr) == 0
    w_lo = pstart[None, :] + cum[:-1]
    w_hi = pstart[None, :] + cum[1:]
    c_lo = w_lo // cr
    n_chunks = jnp.where(w_hi > w_lo, (w_hi - 1) // cr - c_lo + 1, 0)
    q = jnp.arange(per_expert)
    cand = (c_lo[:, :, None] + q).reshape(n_tb, n_e * per_expert)
    keep = (q < n_chunks[:, :, None]).reshape(n_tb, n_e * per_expert)
    order = jnp.argsort(~keep, axis=1, stable=True)[:, :n_slots]
    ids = jnp.where(jnp.take_along_axis(keep, order, axis=1), jnp.take_along_axis(cand, order, axis=1),
                    n_rows_max // cr)
    n_ids = jnp.sum(keep, axis=1).astype(jnp.int32)
    y_frames, y_short = pl.pallas_call(
        functools.partial(_combine_kernel, n_tb, n_fb),
        grid_spec=pltpu.PrefetchScalarGridSpec(
            num_scalar_prefetch=2,
            grid=(n_tb,),
            in_specs=[pl.BlockSpec((tb, d), lambda i, n, ids: (i, 0)),
                      pl.BlockSpec((tb, n_e), lambda i, n, ids: (i, 0)),
                      pl.BlockSpec((1, d), lambda i, n, ids: (0, 0)),
                      pl.BlockSpec(memory_space=pl.ANY)],
            out_specs=[pl.BlockSpec((tb, d), lambda i, n, ids: (jnp.minimum(i, n_fb - 1), 0)),
                       pl.BlockSpec((tb, d), lambda i, n, ids: (jnp.maximum(i - n_fb, 0), 0))],
            scratch_shapes=[pltpu.VMEM((2, n_slots, cr, d), BF16), pltpu.SemaphoreType.DMA((2 * n_slots,))]),
        out_shape=[jax.ShapeDtypeStruct((nf, d), F32), jax.ShapeDtypeStruct((ns, d), F32)],
        compiler_params=_params(("arbitrary",)),
        name="combine",
    )(n_ids, ids.reshape(-1).astype(jnp.int32), h2, pos_c, row(norm_final), ys)

    y_prompt = y_frames.reshape(nb, seq, d)
    y_sample = y_short[:db * SEQ_S].reshape(db, SEQ_S, d)

    def with_meta(a, width):
        meta = jnp.broadcast_to(a[meta_row:meta_row + N_META][None], (nb, N_META, width))
        return jnp.concatenate([meta, a[:nf].reshape(nb, seq, width)], axis=1)[None]

    c_p = with_meta(c_all, kv_rank)
    r_p = with_meta(r_all, QK_ROPE)
    c_s = c_all[nf:nf + db * SEQ_S].reshape(1, db, SEQ_S, kv_rank)
    r_s = r_all[nf:nf + db * SEQ_S].reshape(1, db, SEQ_S, QK_ROPE)
    s_p = state_f[:, HIST_ROWS - POOL_HIST:][None]
    s_s = state_s[:db, SEQ_S - POOL_HIST:][None]
    return (y_prompt, y_sample, c_p, r_p, s_p, c_s, r_s, s_s)
```

```python
import functools

import jax
import jax.numpy as jnp
from jax import lax
from jax.experimental import pallas as pl
from jax.experimental.pallas import tpu as pltpu

CHUNK = 64
N_META = 16
N_HEADS = 8
QK_NOPE = 128
QK_ROPE = 64
ROPE_THETA = 10000.0
POOL_WINDOWS = (2, 4, 8, 16)
POOL_HIST = max(POOL_WINDOWS) - 1
TOP_K = 2
RMS_EPS = 1e-6

LANE = 128
BF16_SUBLANES = 16
SEQ_S = 16
F32_SUBLANES = 8
HIST_ROWS = 16
HIST_PAD = F32_SUBLANES * len(POOL_WINDOWS)
TOKEN_BLOCK = 256
ROW_BLOCK = 256
ATTN_BLOCK = 256
CACHE_BLOCK = 512
KEY_BLOCKS_PER_ITER = 4
SCORES_AHEAD = 8
COMBINE_ROWS = 64
GATHER_SLOTS = 6
V7X_VMEM_BYTES = 64 * 1024 * 1024
VMEM_LIMIT = V7X_VMEM_BYTES // 8 * 7

F32 = jnp.float32
BF16 = jnp.bfloat16
NEG_INF = float("-inf")
LOG2_E = 1.4426950408889634


def _dot(a, b):
    return jnp.dot(a, b, preferred_element_type=F32)


def _dot_nt(a, b):
    return lax.dot_general(a, b, (((1,), (1,)), ((), ())), preferred_element_type=F32)


def _rms(x, g):
    return x * lax.rsqrt(jnp.mean(x * x, axis=-1, keepdims=True) + RMS_EPS) * g


def _const_spec(shape):
    nd = len(shape)
    return pl.BlockSpec(shape, lambda *_: (0,) * nd, pipeline_mode=pl.Buffered(1))


def _params(sem):
    return pltpu.CompilerParams(dimension_semantics=sem, vmem_limit_bytes=VMEM_LIMIT)


def _qkv_kernel(n_fb, scale, xp_ref, xs_ref, tab_ref, tabt_ref, g_ref, wdq_ref, gq_ref, wuq_ref, wuqt_ref, wukt_ref,
                wdkv_ref, gkv_ref, wukf_ref, wuvt_ref, q_ref, qt_ref, kvb_ref, kn_ref, kr_ref, vt_ref, cp_ref, rp_ref,
                cs_ref, rs_ref):
    i = pl.program_id(0)
    x = jnp.where(i < n_fb, xp_ref[...], xs_ref[...])
    u = _rms(x, g_ref[...]).astype(BF16)
    cq = _rms(_dot(u, wdq_ref[...]), gq_ref[...]).astype(BF16)
    cos = tab_ref[:, :LANE]
    sin = tab_ref[:, LANE:]
    hq = QK_NOPE + 2 * LANE

    kv = _dot(u, wdkv_ref[...])
    c = _rms(kv[:, :2 * LANE], gkv_ref[...])
    r = kv[:, 2 * LANE:3 * LANE] * cos + kv[:, 3 * LANE:] * sin
    c_bf = c.astype(BF16)
    r_bf = r.astype(BF16)

    kn_ref[...] = _dot(c_bf, wukf_ref[...]).astype(BF16)
    kr_ref[...] = r_bf
    vt = _dot_nt(wuvt_ref[...], c_bf).astype(BF16)
    hv = vt.shape[0] // N_HEADS
    hv_ext = vt_ref.shape[1] // N_HEADS
    for h in range(N_HEADS):
        vt_ref[0, h * hv_ext:h * hv_ext + hv, :] = vt[h * hv:(h + 1) * hv, :]
        vt_ref[0, h * hv_ext + hv:(h + 1) * hv_ext, :] = jnp.ones((hv_ext - hv, vt.shape[1]), BF16)

    @pl.when(i < n_fb)
    def _():
        qat = _dot_nt(wuqt_ref[...], cq)
        cos_t = tabt_ref[:QK_ROPE, :]
        sin_t = tabt_ref[LANE:LANE + QK_ROPE, :]
        pe0 = N_HEADS * QK_NOPE
        sw0 = pe0 + N_HEADS * QK_ROPE
        for h in range(N_HEADS):
            qt_ref[0, h, :QK_NOPE, :] = (qat[h * QK_NOPE:(h + 1) * QK_NOPE, :] * scale).astype(BF16)
            a = qat[pe0 + h * QK_ROPE:pe0 + (h + 1) * QK_ROPE, :]
            b = qat[sw0 + h * QK_ROPE:sw0 + (h + 1) * QK_ROPE, :]
            qt_ref[0, h, QK_NOPE:QK_NOPE + QK_ROPE, :] = ((a * cos_t + b * sin_t) * scale).astype(BF16)
            qt_ref[0, h, QK_NOPE + QK_ROPE:, :] = jnp.zeros((LANE - QK_ROPE, qt_ref.shape[3]), BF16)
        cp_ref[...] = c
        rp_ref[...] = r[:, :QK_ROPE]

    @pl.when(i >= n_fb)
    def _():
        qa = _dot(cq, wuq_ref[...])
        for h in range(N_HEADS):
            qn = qa[:, h * LANE:(h + 1) * LANE].astype(BF16)
            q_ref[:, h * hq:h * hq + 2 * LANE] = (_dot(qn, wukt_ref[h]) * scale).astype(BF16)
            a = qa[:, (N_HEADS + h) * LANE:(N_HEADS + h + 1) * LANE]
            b = qa[:, (2 * N_HEADS + h) * LANE:(2 * N_HEADS + h + 1) * LANE]
            q_ref[:, h * hq + 2 * LANE:(h + 1) * hq] = ((a * cos + b * sin) * scale).astype(BF16)
        kvb_ref[:, :2 * LANE] = c_bf
        kvb_ref[:, 2 * LANE:] = r_bf
        cs_ref[...] = c
        rs_ref[...] = r[:, :QK_ROPE]


def _attn_frames_kernel(qb, meta_col, qt_ref, kn_ref, kr_ref, vt_ref, knm_ref, krm_ref, vtm_ref, o_ref, m_ref, acc_ref):
    j = pl.program_id(0) % qb
    tq = o_ref.shape[0]
    hn = kn_ref.shape[1] // N_HEADS
    hv = o_ref.shape[1] // N_HEADS
    hx = vt_ref.shape[1] // N_HEADS

    def head_keys(kn, kr, rows, h):
        return jnp.concatenate([kn[rows, h * hn:(h + 1) * hn], kr[rows, :]], axis=1)

    def scores(h, rows):
        return _dot(head_keys(kn_ref, kr_ref, rows, h), qt_ref[0, h])

    m_ref[...] = jnp.full(m_ref.shape, NEG_INF, F32)
    acc_ref[...] = jnp.zeros_like(acc_ref)

    def unmasked(kbs):
        steps = [(kb, h) for kb in kbs for h in range(N_HEADS)]
        block_scores = lambda kb, h: scores(h, pl.ds(pl.multiple_of(kb * tq, tq), tq))
        ahead = [block_scores(*step) for step in steps[:SCORES_AHEAD]]
        for n, (kb, h) in enumerate(steps):
            s = ahead.pop(0)
            if n + SCORES_AHEAD < len(steps):
                ahead.append(block_scores(*steps[n + SCORES_AHEAD]))
            m_old = m_ref[h]
            m_new = jnp.maximum(m_old, jnp.max(s, axis=0, keepdims=True))
            alpha = jnp.exp2(m_old - m_new)
            p = jnp.exp2(s - m_new)
            m_ref[h] = m_new
            acc_ref[h] = alpha * acc_ref[h] + _dot(vt_ref[kb, h * hx:(h + 1) * hx, :], p.astype(BF16))

    def body(t, carry):
        unmasked([KEY_BLOCKS_PER_ITER * t + u for u in range(KEY_BLOCKS_PER_ITER)])
        return carry

    lax.fori_loop(0, j // KEY_BLOCKS_PER_ITER, body, 0)
    for left in range(1, KEY_BLOCKS_PER_ITER):
        @pl.when(j % KEY_BLOCKS_PER_ITER == left)
        def _():
            unmasked([j - left + u for u in range(left)])

    rows = pl.ds(pl.multiple_of(j * tq, tq), tq)
    visible = (lax.broadcasted_iota(jnp.int32, (tq, tq), 0) // CHUNK
               <= lax.broadcasted_iota(jnp.int32, (tq, tq), 1) // CHUNK)

    def last_scores(h):
        return scores(h, rows), _dot(head_keys(knm_ref, krm_ref, slice(None), h), qt_ref[0, h])

    ahead = [last_scores(h) for h in range(SCORES_AHEAD)]
    for h in range(N_HEADS):
        s, s_meta = ahead.pop(0)
        s = jnp.where(visible, s, NEG_INF)
        if h + SCORES_AHEAD < N_HEADS:
            ahead.append(last_scores(h + SCORES_AHEAD))
        m_old = m_ref[h]
        m_new = jnp.maximum(m_old, jnp.maximum(jnp.max(s, axis=0, keepdims=True),
                                               jnp.max(s_meta, axis=0, keepdims=True)))
        alpha = jnp.exp2(m_old - m_new)
        p = jnp.exp2(s - m_new)
        p_meta = jnp.exp2(s_meta - m_new)
        acc = (alpha * acc_ref[h] + _dot(vt_ref[j, h * hx:(h + 1) * hx, :], p.astype(BF16))
               + _dot(vtm_ref[0, h * hx:(h + 1) * hx, meta_col:meta_col + N_META], p_meta.astype(BF16)))
        o_ref[:, h * hv:(h + 1) * hv] = (acc[:hv] * (1.0 / acc[hv:hv + 1])).T.astype(BF16)


def _attn_short_kernel(n_cached_seq, q_ref, kvn_ref, cc_ref, cr_ref, wuv_ref, o_ref):
    s_id = pl.program_id(0)
    hq = q_ref.shape[1] // N_HEADS
    n_cache = cc_ref.shape[2]
    qs = jnp.concatenate([q_ref[:, h * hq:(h + 1) * hq] for h in range(N_HEADS)], axis=0)
    q_lat = qs[:, :2 * LANE]
    q_pe = qs[:, 2 * LANE:2 * LANE + QK_ROPE]
    kvn = kvn_ref[...]
    s = _dot_nt(qs, kvn)
    m = jnp.max(s, axis=1, keepdims=True)
    p = jnp.exp2(s - m)
    l = jnp.sum(p, axis=1, keepdims=True)
    acc = _dot(p.astype(BF16), kvn[:, :2 * LANE])
    has_cache = s_id < n_cached_seq
    start = 0
    while start < n_cache:
        size = min(CACHE_BLOCK, n_cache - start)
        ck = cc_ref[0, 0, start:start + size, :].astype(BF16)
        rk = cr_ref[0, 0, start:start + size, :].astype(BF16)
        s = _dot_nt(q_lat, ck) + _dot_nt(q_pe, rk)
        s = jnp.where(has_cache, s, NEG_INF)
        m_new = jnp.maximum(m, jnp.max(s, axis=1, keepdims=True))
        alpha = jnp.exp2(m - m_new)
        p = jnp.exp2(s - m_new)
        l = alpha * l + jnp.sum(p, axis=1, keepdims=True)
        acc = alpha * acc + _dot(p.astype(BF16), ck)
        m = m_new
        start += size
    o = (acc / l).astype(BF16)
    hv = wuv_ref.shape[2]
    for h in range(N_HEADS):
        o_ref[:, h * hv:(h + 1) * hv] = _dot(o[h * SEQ_S:(h + 1) * SEQ_S, :], wuv_ref[h]).astype(BF16)


def _proj_ffn_kernel(n_fb, xp_ref, xs_ref, op_ref, os_ref, wo_ref, g_ref, wg_ref, wu_ref, wd_ref, *cast_refs):
    n_cast = (len(cast_refs) - 1) // 2
    h_ref = cast_refs[n_cast]
    for src, dst in zip(cast_refs[:n_cast], cast_refs[n_cast + 1:]):
        dst[...] = src[...].astype(BF16)
    i = pl.program_id(0)
    x = jnp.where(i < n_fb, xp_ref[...], xs_ref[...])
    o = jnp.where(i < n_fb, op_ref[...], os_ref[...])
    h1 = x + _dot(o, wo_ref[...])
    u = _rms(h1, g_ref[...]).astype(BF16)
    act = jax.nn.silu(_dot(u, wg_ref[...])) * _dot(u, wu_ref[...])
    h_ref[...] = h1 + _dot(act.astype(BF16), wd_ref[...])


def _set_history(ext_ref, hist):
    for start in range(0, HIST_PAD, HIST_ROWS):
        ext_ref[start:start + HIST_ROWS, :] = hist


def _window_sums(ext_ref, stage_refs, t):
    grp = ext_ref.shape[1] // len(POOL_WINDOWS)
    end = HIST_PAD + t
    first = F32_SUBLANES
    s = ext_ref[first:end, :] + ext_ref[first - 1:end - 1, :]
    sums = [s[HIST_PAD - first:, :grp]]
    for ref, shift in zip(stage_refs, POOL_WINDOWS):
        rest = s[:, grp:]
        ref[first:end, :] = rest
        first += F32_SUBLANES
        s = rest[F32_SUBLANES:, :] + ref[first - shift:end - shift, :]
        sums.append(s[HIST_PAD - first:, :grp])
    return sums


def _pool_route(h1, ext_ref, stage_refs, cnt_rows, valid, base, gm_ref, pw_ref, ps_ref, gf_ref, wr_ref, br_ref,
                tri_ref):
    t = h1.shape[0]
    grp = h1.shape[1] // len(POOL_WINDOWS)
    u1 = _rms(h1, gm_ref[...])
    ext_ref[HIST_PAD:HIST_PAD + t, :] = u1
    sums = _window_sums(ext_ref, stage_refs, t)
    ys = []
    for g, w in enumerate(POOL_WINDOWS):
        cnt = float(w) if cnt_rows is None else jnp.minimum(float(w), cnt_rows)
        mean = sums[g] / cnt - u1[:, g * grp:(g + 1) * grp]
        ys.append(_dot(mean.astype(BF16), pw_ref[g]))
    h2 = h1 + jnp.concatenate(ys, axis=1) * ps_ref[...]
    u2 = _rms(h2, gf_ref[...])

    n_e = wr_ref.shape[1] // 2
    u2_hi = u2.astype(BF16)
    u2_lo = (u2 - u2_hi.astype(F32)).astype(BF16)
    parts = _dot(u2_hi, wr_ref[...]) + _dot(u2_lo, wr_ref[...])
    logits = parts[:, :n_e] + parts[:, n_e:] + br_ref[...]
    e_iota = lax.broadcasted_iota(jnp.int32, logits.shape, 1).astype(F32)
    v0 = jnp.max(logits, axis=1, keepdims=True)
    e0 = jnp.min(jnp.where(logits == v0, e_iota, float(n_e)), axis=1, keepdims=True)
    rest = jnp.where(e_iota == e0, NEG_INF, logits)
    v1 = jnp.max(rest, axis=1, keepdims=True)
    e1 = jnp.min(jnp.where(rest == v1, e_iota, float(n_e)), axis=1, keepdims=True)
    tt = jnp.exp(v1 - v0)
    g0 = 1.0 / (1.0 + tt)
    g1 = tt / (1.0 + tt)
    hit0 = e_iota == e0
    hit1 = e_iota == e1
    onehot = jnp.where(valid & (hit0 | hit1), 1.0, 0.0)
    rank = _dot(tri_ref[...], onehot.astype(BF16)) + base
    r0 = jnp.sum(jnp.where(hit0, rank, 0.0), axis=1, keepdims=True)
    r1 = jnp.sum(jnp.where(hit1, rank, 0.0), axis=1, keepdims=True)
    cols = (jnp.where(valid, e0, -1).astype(F32), jnp.where(valid, e1, -1).astype(F32), r0, r1, g0, g1)
    info = jnp.zeros(logits.shape, F32)
    for k, col in enumerate(cols):
        info = jnp.where(e_iota == k, col, info)
    new_base = base + jnp.sum(onehot, axis=0, keepdims=True)
    return h2, u2, info, new_base, u1


def _pool_frames_kernel(n_fb, h_ref, hist_ref, gm_ref, pw_ref, ps_ref, gf_ref, wr_ref, br_ref, tri_ref,
                        h2_ref, u2_ref, info_ref, cum_ref, tot_ref, state_ref, ext_ref, base_ref, *stage_refs):
    i = pl.program_id(0)

    @pl.when(i == 0)
    def _():
        base_ref[...] = jnp.zeros_like(base_ref)

    @pl.when(i >= n_fb)
    def _():
        h2_ref[...] = jnp.zeros_like(h2_ref)
        u2_ref[...] = jnp.zeros_like(u2_ref)
        info_ref[...] = jnp.zeros_like(info_ref)

    @pl.when(i < n_fb)
    def _():
        _set_history(ext_ref, _rms(hist_ref[...], gm_ref[...]))
        base = base_ref[...]
        cum_ref[0] = base
        h2, u2, info, new_base, u1 = _pool_route(h_ref[...], ext_ref, stage_refs, None, True, base, gm_ref, pw_ref,
                                                 ps_ref, gf_ref, wr_ref, br_ref, tri_ref)
        h2_ref[...] = h2
        u2_ref[...] = u2.astype(BF16)
        info_ref[...] = info
        base_ref[...] = new_base
        tot_ref[...] = new_base
        state_ref[0] = u1[u1.shape[0] - HIST_ROWS:, :]


def _pool_short_kernel(n_sample, n_valid, h2_hbm, u2_hbm, info_hbm, h_ref, hist_ref, base0_ref, gm_ref, pw_ref, ps_ref,
                       gf_ref, wr_ref, br_ref, tri_ref, h2_ref, u2_ref, info_ref, cum_ref, tot_ref, state_ref,
                       ext_ref, base_ref, *stage_refs):
    del h2_hbm, u2_hbm, info_hbm
    s_id = pl.program_id(0)

    @pl.when(s_id == 0)
    def _():
        base_ref[...] = base0_ref[...]

    _set_history(ext_ref, hist_ref[0])
    base = base_ref[...]
    cum_ref[0] = base
    pos1 = (lax.broadcasted_iota(jnp.int32, (SEQ_S, 1), 0) + 1).astype(F32)
    cnt_rows = jnp.where(s_id < n_sample, float(max(POOL_WINDOWS)), pos1)
    h2, u2, info, new_base, u1 = _pool_route(h_ref[...], ext_ref, stage_refs, cnt_rows, s_id < n_valid, base, gm_ref,
                                             pw_ref, ps_ref, gf_ref, wr_ref, br_ref, tri_ref)
    h2_ref[...] = h2
    u2_ref[...] = u2.astype(BF16)
    info_ref[...] = info
    base_ref[...] = new_base
    tot_ref[...] = new_base
    state_ref[0] = u1


def _expert_kernel(n_blocks, be_ref, lo_ref, hi_ref, ok_ref, u_hbm, pos_ref, gate_ref, wg_ref, wu_ref, wd_ref, ys_ref,
                   buf_ref, sel_ref, gsel_ref, acc_ref, gacc_ref, sem):
    del be_ref
    j = pl.program_id(0)
    rb = ys_ref.shape[0]
    n_slots, tc, _ = buf_ref.shape
    last_chunk = pos_ref.shape[0] - 1

    def n_chunks(jj):
        return jnp.where(ok_ref[jj] != 0, hi_ref[jj] - lo_ref[jj] + 1, 0)

    def chunk_copy(c, slot):
        return pltpu.make_async_copy(u_hbm.at[pl.ds(pl.multiple_of(c * tc, tc), tc), :], buf_ref.at[slot],
                                     sem.at[slot])

    def request(jj):
        n_req = jnp.minimum(n_chunks(jj), n_slots)
        for k in range(n_slots):
            @pl.when(k < n_req)
            def _():
                chunk_copy(lo_ref[jj] + k, k).start()

    def select(jj, c, live):
        rows = jj * rb + lax.broadcasted_iota(jnp.int32, (rb, tc), 0)
        pos = pos_ref[c]
        gate = gate_ref[c]
        m0 = (rows == pos[0:1, :]) & live
        m1 = (rows == pos[1:2, :]) & live
        sel = jnp.where(m0 | m1, 1.0, 0.0).astype(BF16)
        row_gate = jnp.sum(jnp.where(m0, gate[0:1, :], 0.0) + jnp.where(m1, gate[1:2, :], 0.0), axis=1, keepdims=True)
        return sel, row_gate

    def build_selection(jj):
        n_req = jnp.minimum(n_chunks(jj), n_slots)
        row_gate = jnp.zeros(gsel_ref.shape, F32)
        for k in range(n_slots):
            sel, g = select(jj, jnp.minimum(lo_ref[jj] + k, last_chunk), k < n_req)
            sel_ref[:, k * tc:(k + 1) * tc] = sel
            row_gate = row_gate + g
        gsel_ref[...] = row_gate

    @pl.when(j == 0)
    def _():
        buf_ref[...] = jnp.zeros_like(buf_ref)
        request(0)
        build_selection(0)

    n = n_chunks(j)
    lo = lo_ref[j]

    @pl.when(n > 0)
    def _():
        for k in range(n_slots):
            @pl.when(k < jnp.minimum(n, n_slots))
            def _():
                chunk_copy(lo + k, k).wait()

        acc_ref[...] = _dot(sel_ref[...], buf_ref[...].reshape(n_slots * tc, -1))
        gacc_ref[...] = gsel_ref[...]

        def overflow(k, carry):
            copy = chunk_copy(lo + k, 0)
            copy.start()
            copy.wait()
            sel, row_gate = select(j, lo + k, True)
            acc_ref[...] += _dot(sel, buf_ref[0])
            gacc_ref[...] += row_gate
            return carry

        lax.fori_loop(n_slots, jnp.maximum(n, n_slots), overflow, 0)

    @pl.when(j + 1 < n_blocks)
    def _():
        request(jnp.minimum(j + 1, n_blocks - 1))

    @pl.when(n > 0)
    def _():
        xg = acc_ref[...].astype(BF16)
        act = jax.nn.silu(_dot(xg, wg_ref[0])) * _dot(xg, wu_ref[0])
        ys_ref[...] = (_dot(act.astype(BF16), wd_ref[0]) * gacc_ref[...]).astype(BF16)
        build_selection(jnp.minimum(j + 1, n_blocks - 1))

    @pl.when(n == 0)
    def _():
        ys_ref[...] = jnp.zeros_like(ys_ref)


def _combine_kernel(n_blocks, n_fb, n_ref, ids_ref, h2_ref, pos_ref, g_ref, ys_hbm, yp_ref, yshort_ref, buf_ref, sem):
    i = pl.program_id(0)
    _, n_slots, cr, _ = buf_ref.shape
    per_vreg = LANE // cr
    par = i % 2
    nxt = jnp.minimum(i + 1, n_blocks - 1)

    def chunk_copy(ii, k, which):
        ch = ids_ref[ii * n_slots + k]
        return pltpu.make_async_copy(ys_hbm.at[pl.ds(pl.multiple_of(ch * cr, cr), cr), :], buf_ref.at[which, k],
                                     sem.at[which * n_slots + k])

    def request(ii, which):
        for k in range(n_slots):
            @pl.when(k < n_ref[ii])
            def _():
                chunk_copy(ii, k, which).start()

    @pl.when(i == 0)
    def _():
        buf_ref[...] = jnp.zeros_like(buf_ref)
        request(0, 0)

    @pl.when(i + 1 < n_blocks)
    def _():
        request(nxt, 1 - par)

    lane = lax.broadcasted_iota(jnp.int32, (1, LANE), 1)
    pieces = []
    for v in range(n_slots // per_vreg):
        row = jnp.zeros((1, LANE), jnp.int32)
        for q in range(per_vreg):
            k = v * per_vreg + q
            row = jnp.where(lane // cr == q, ids_ref[i * n_slots + k] * cr + lane % cr, row)
        pieces.append(row)
    slot_rows = jnp.concatenate(pieces, axis=1)
    sel = jnp.where((slot_rows == pos_ref[:, 0:1]) | (slot_rows == pos_ref[:, 1:2]), 1.0, 0.0).astype(BF16)

    for k in range(n_slots):
        @pl.when(k < n_ref[i])
        def _():
            chunk_copy(i, k, par).wait()

    moe = _dot(sel, buf_ref[par].reshape(n_slots * cr, -1))
    y = _rms(h2_ref[...] + moe, g_ref[...])

    @pl.when(i < n_fb)
    def _():
        yp_ref[...] = y

    @pl.when(i >= n_fb)
    def _():
        yshort_ref[...] = y


def _rope_table(pos):
    half = QK_ROPE // 2
    inv = ROPE_THETA ** (-jnp.arange(half, dtype=F32) / half)
    ang = pos.astype(F32)[:, None] * inv[None, :]
    cos = jnp.cos(ang)
    sin = jnp.sin(ang)
    zero = jnp.zeros((pos.shape[0], LANE - QK_ROPE), F32)
    return jnp.concatenate([cos, cos, zero, -sin, sin, zero], axis=1)


def _half_swap(w):
    half = QK_ROPE // 2
    return jnp.concatenate([w[..., half:], w[..., :half]], axis=-1)


def _pad_last(w, width):
    return jnp.pad(w, [(0, 0)] * (w.ndim - 1) + [(0, width - w.shape[-1])])


def kernel(x_prompt, x_sample, cache_kv_latent, cache_k_rope, state_pool, meta_tokens, norm_mix, norm_ffn, norm_final,
           mla_w_dq, mla_g_q, mla_w_uq, mla_w_dkv, mla_g_kv, mla_w_uk, mla_w_uv, mla_w_o, pool_w, pool_scale,
           ffn_w_gate, ffn_w_up, ffn_w_down, moe_w_router, moe_b_router, moe_w_gate, moe_w_up, moe_w_down):
    nb, seq, d = x_prompt.shape
    db, dseq, _ = x_sample.shape
    n_cache = cache_kv_latent.shape[2]
    q_rank = mla_w_dq.shape[2]
    kv_rank = mla_g_kv.shape[1]
    v_dim = mla_w_uv.shape[3]
    d_ff = ffn_w_gate.shape[2]
    n_e = moe_w_router.shape[2]
    d_e = moe_w_gate.shape[3]
    tb, rb, tq = TOKEN_BLOCK, ROW_BLOCK, ATTN_BLOCK
    assert norm_mix.shape[0] == 2 and cache_kv_latent.shape[0] == 1 and state_pool.shape[0] == 1
    assert dseq == SEQ_S and N_META == SEQ_S and meta_tokens.shape[0] == N_META
    assert POOL_WINDOWS == tuple(2 ** (k + 1) for k in range(len(POOL_WINDOWS))) and HIST_PAD % HIST_ROWS == 0
    assert kv_rank == 2 * LANE and QK_NOPE == LANE and QK_ROPE <= LANE and HIST_ROWS >= POOL_HIST
    assert seq % tb == 0 and tb == tq and tq % CHUNK == 0 and tb % SEQ_S == 0 and d % LANE == 0
    assert (n_cache - N_META) % CHUNK == 0 and dseq <= CHUNK

    nf = nb * seq
    n_valid_seq = db + 1
    ns = -(-(n_valid_seq * SEQ_S) // tb) * tb
    n_seq = ns // SEQ_S
    nt = nf + ns
    n_fb, n_sb, n_tb = nf // tb, ns // tb, nt // tb
    sb = seq // tb
    meta_row = nf + db * SEQ_S
    scale = float((QK_NOPE + QK_ROPE) ** -0.5)
    hq = 3 * LANE

    xp = x_prompt.reshape(nf, d)
    xs = jnp.concatenate([x_sample.reshape(db * SEQ_S, d), meta_tokens.astype(x_prompt.dtype),
                          jnp.zeros((ns - n_valid_seq * SEQ_S, d), x_prompt.dtype)], axis=0)

    t_s = jnp.arange(SEQ_S)
    pos_short = jnp.concatenate([jnp.tile(n_cache + t_s, db), jnp.tile(t_s, n_seq - db)])
    tab_frames = _rope_table(N_META + jnp.arange(seq))
    tab = jnp.concatenate([tab_frames, _rope_table(pos_short)], axis=0)

    wuq = mla_w_uq[0].reshape(q_rank, N_HEADS, QK_NOPE + QK_ROPE)
    wuq_pe = wuq[:, :, QK_NOPE:]
    wuq2 = jnp.concatenate([wuq[:, :, :QK_NOPE].reshape(q_rank, -1),
                            _pad_last(wuq_pe, LANE).reshape(q_rank, -1),
                            _pad_last(_half_swap(wuq_pe), LANE).reshape(q_rank, -1)], axis=1).astype(BF16)
    wuq_t = jnp.concatenate([wuq[:, :, :QK_NOPE].reshape(q_rank, -1), wuq_pe.reshape(q_rank, -1),
                             _half_swap(wuq_pe).reshape(q_rank, -1)], axis=1).T.astype(BF16)
    wdkv_r =mla_w_dkv[0][:, kv_rank:]
    wdkv2 = jnp.concatenate([mla_w_dkv[0][:, :kv_rank], _pad_last(wdkv_r, LANE),
                             _pad_last(_half_swap(wdkv_r), LANE)], axis=1).astype(BF16)
    wuk_t = jnp.transpose(mla_w_uk[0], (1, 2, 0)).astype(BF16)
    wuk_flat = mla_w_uk[0].reshape(kv_rank, N_HEADS * QK_NOPE).astype(BF16)
    wuv = jnp.transpose(mla_w_uv[0], (1, 0, 2)).astype(BF16)
    wuv_t = mla_w_uv[0].reshape(kv_rank, N_HEADS * v_dim).T.astype(BF16)
    row = lambda v: v.reshape(1, -1)
    hk = 2 * LANE
    hx = v_dim + BF16_SUBLANES

    tok_p = pl.BlockSpec((tb, d), lambda i: (jnp.minimum(i, n_fb - 1), 0))
    tok_s = pl.BlockSpec((tb, d), lambda i: (jnp.maximum(i - n_fb, 0), 0))
    short_tok = lambda i: (jnp.maximum(i - n_fb, 0), 0)
    frame_tok = lambda i: (jnp.minimum(i, n_fb - 1), 0)
    q_short, qt_frames, kvb_short, kn_all, kr_all, vt_all, c_frames, r_frames, c_short, r_short = pl.pallas_call(
        functools.partial(_qkv_kernel, n_fb, scale * LOG2_E),
        grid=(n_tb,),
        in_specs=[tok_p, tok_s,
                  pl.BlockSpec((tb, 2 * LANE), lambda i: (jnp.where(i < n_fb, i % sb, sb + i - n_fb), 0)),
                  pl.BlockSpec((2 * LANE, tb), lambda i: (0, i % sb)),
                  _const_spec((1, d)), _const_spec((d, q_rank)), _const_spec((1, q_rank)),
                  _const_spec(wuq2.shape), _const_spec(wuq_t.shape), _const_spec(wuk_t.shape),
                  _const_spec(wdkv2.shape), _const_spec((1, kv_rank)), _const_spec(wuk_flat.shape),
                  _const_spec(wuv_t.shape)],
        out_specs=[pl.BlockSpec((tb, N_HEADS * hq), short_tok),
                   pl.BlockSpec((1, N_HEADS, hk, tb), lambda i: (jnp.minimum(i, n_fb - 1), 0, 0, 0)),
                   pl.BlockSpec((tb, hq), short_tok),
                   pl.BlockSpec((tb, N_HEADS * QK_NOPE), lambda i: (i, 0)),
                   pl.BlockSpec((tb, LANE), lambda i: (i, 0)),
                   pl.BlockSpec((1, N_HEADS * hx, tb), lambda i: (i, 0, 0)),
                   pl.BlockSpec((tb, kv_rank), frame_tok), pl.BlockSpec((tb, QK_ROPE), frame_tok),
                   pl.BlockSpec((tb, kv_rank), short_tok), pl.BlockSpec((tb, QK_ROPE), short_tok)],
        out_shape=[jax.ShapeDtypeStruct((ns, N_HEADS * hq), BF16),
                   jax.ShapeDtypeStruct((n_fb, N_HEADS, hk, tb), BF16),
                   jax.ShapeDtypeStruct((ns, hq), BF16),
                   jax.ShapeDtypeStruct((nt, N_HEADS * QK_NOPE), BF16), jax.ShapeDtypeStruct((nt, LANE), BF16),
                   jax.ShapeDtypeStruct((n_tb, N_HEADS * hx, tb), BF16),
                   jax.ShapeDtypeStruct((nf, kv_rank), F32), jax.ShapeDtypeStruct((nf, QK_ROPE), F32),
                   jax.ShapeDtypeStruct((ns, kv_rank), F32), jax.ShapeDtypeStruct((ns, QK_ROPE), F32)],
        compiler_params=_params(("arbitrary",)),
        name="qkv",
    )(xp, xs, tab, tab_frames.T, row(norm_mix[0]), mla_w_dq[0].astype(BF16), row(mla_g_q[0]), wuq2, wuq_t, wuk_t,
      wdkv2, row(mla_g_kv[0]), wuk_flat, wuv_t)

    qb = seq // tq
    once = pl.Buffered(1)
    o_frames = pl.pallas_call(
        functools.partial(_attn_frames_kernel, qb, meta_row % tb),
        grid=(n_fb,),
        in_specs=[pl.BlockSpec((1, N_HEADS, hk, tq), lambda i: (i, 0, 0, 0)),
                  pl.BlockSpec((seq, N_HEADS * QK_NOPE), lambda i: (i // qb, 0)),
                  pl.BlockSpec((seq, LANE), lambda i: (i // qb, 0)),
                  pl.BlockSpec((qb, N_HEADS * hx, tq), lambda i: (i // qb, 0, 0)),
                  pl.BlockSpec((N_META, N_HEADS * QK_NOPE), lambda i: (meta_row // N_META, 0), pipeline_mode=once),
                  pl.BlockSpec((N_META, LANE), lambda i: (meta_row // N_META, 0), pipeline_mode=once),
                  pl.BlockSpec((1, N_HEADS * hx, tb), lambda i: (meta_row // tb, 0, 0), pipeline_mode=once)],
        out_specs=pl.BlockSpec((tq, N_HEADS * v_dim), lambda i: (i, 0)),
        out_shape=jax.ShapeDtypeStruct((nf, N_HEADS * v_dim), BF16),
        scratch_shapes=[pltpu.VMEM((N_HEADS, 1, tq), F32), pltpu.VMEM((N_HEADS, hx, tq), F32)],
        compiler_params=_params(("parallel",)),
        name="attn_frames",
    )(qt_frames, kn_all, kr_all, vt_all, kn_all, kr_all, vt_all)

    o_short = pl.pallas_call(
        functools.partial(_attn_short_kernel, db),
        grid=(n_seq,),
        in_specs=[pl.BlockSpec((SEQ_S, N_HEADS * hq), lambda s: (s, 0)),
                  pl.BlockSpec((SEQ_S, hq), lambda s: (s, 0)),
                  pl.BlockSpec((1, 1, n_cache, kv_rank), lambda s: (0, jnp.minimum(s, db - 1), 0, 0)),
                  pl.BlockSpec((1, 1, n_cache, QK_ROPE), lambda s: (0, jnp.minimum(s, db - 1), 0, 0)),
                  _const_spec(wuv.shape)],
        out_specs=pl.BlockSpec((SEQ_S, N_HEADS * v_dim), lambda s: (s, 0)),
        out_shape=jax.ShapeDtypeStruct((ns, N_HEADS * v_dim), BF16),
        compiler_params=_params(("parallel",)),
        name="attn_short",
    )(q_short, kvb_short, cache_kv_latent, cache_k_rope, wuv)

    moe_f32 = [moe_w_gate[0].reshape(n_e * d, d_e), moe_w_up[0].reshape(n_e * d, d_e),
               moe_w_down[0].reshape(n_e * d_e, d)]
    cast_in, cast_out, cast_shapes = [], [], []
    for w in moe_f32:
        steps = max(s for s in range(1, n_tb + 1) if w.shape[0] % s == 0 and (w.shape[0] // s) % BF16_SUBLANES == 0)
        spec = pl.BlockSpec((w.shape[0] // steps, w.shape[1]), lambda i, last=steps - 1: (jnp.minimum(i, last), 0))
        cast_in.append(spec)
        cast_out.append(spec)
        cast_shapes.append(jax.ShapeDtypeStruct(w.shape, BF16))
    h1, wg_e, wu_e, wd_e = pl.pallas_call(
        functools.partial(_proj_ffn_kernel, n_fb),
        grid=(n_tb,),
        in_specs=[tok_p, tok_s,
                  pl.BlockSpec((tb, N_HEADS * v_dim), lambda i: (jnp.minimum(i, n_fb - 1), 0)),
                  pl.BlockSpec((tb, N_HEADS * v_dim), short_tok),
                  _const_spec((N_HEADS * v_dim, d)), _const_spec((1, d)),
                  _const_spec((d, d_ff)), _const_spec((d, d_ff)), _const_spec((d_ff, d))] + cast_in,
        out_specs=[pl.BlockSpec((tb, d), lambda i: (i, 0))] + cast_out,
        out_shape=[jax.ShapeDtypeStruct((nt, d), F32)] + cast_shapes,
        compiler_params=_params(("arbitrary",)),
        name="proj_ffn",
    )(xp, xs, o_frames, o_short, mla_w_o[0].astype(BF16), row(norm_ffn[0]),
      ffn_w_gate[0].astype(BF16), ffn_w_up[0].astype(BF16), ffn_w_down[0].astype(BF16), *moe_f32)
    wg_e = wg_e.reshape(n_e, d, d_e)
    wu_e = wu_e.reshape(n_e, d, d_e)
    wd_e = wd_e.reshape(n_e, d_e, d)

    route_w = [_const_spec((1, d)), _const_spec(pool_w.shape[1:]), _const_spec((1, d)), _const_spec((1, d)),
               _const_spec((d, 2 * n_e)), _const_spec((1, n_e))]
    wr_hi = moe_w_router[0].astype(BF16)
    wr_lo = (moe_w_router[0].astype(F32) - wr_hi.astype(F32)).astype(BF16)
    route_args = (row(norm_mix[1]), pool_w[0].astype(BF16), row(pool_scale[0]), row(norm_ffn[1]),
                  jnp.concatenate([wr_hi, wr_lo], axis=1), row(moe_b_router[0]))
    tri = lambda n: (jnp.arange(n)[:, None] > jnp.arange(n)[None, :]).astype(BF16)
    n_win = len(POOL_WINDOWS)
    stage_scratch = lambda t: [pltpu.VMEM((HIST_PAD + t, d // n_win * (n_win - 1 - k)), F32) for k in range(n_win - 1)]
    hist_blk = lambda i: jnp.where(i % sb == 0, meta_row // HIST_ROWS, i * (tb // HIST_ROWS) - 1)
    h2_f, u2_f, info_f, cum_f, tot_f, state_f = pl.pallas_call(
        functools.partial(_pool_frames_kernel, n_fb),
        grid=(n_tb,),
        in_specs=[pl.BlockSpec((tb, d), lambda i: (i, 0)),
                  pl.BlockSpec((HIST_ROWS, d), lambda i: (hist_blk(i), 0))] + route_w + [_const_spec((tb, tb))],
        out_specs=[pl.BlockSpec((tb, d), lambda i: (i, 0)), pl.BlockSpec((tb, d), lambda i: (i, 0)),
                   pl.BlockSpec((tb, n_e), lambda i: (i, 0)),
                   pl.BlockSpec((1, 1, n_e), lambda i: (jnp.minimum(i, n_fb - 1), 0, 0)),
                   pl.BlockSpec((1, n_e), lambda i: (0, 0)),
                   pl.BlockSpec((1, HIST_ROWS, d), lambda i: (jnp.minimum(i // sb, nb - 1), 0, 0))],
        out_shape=[jax.ShapeDtypeStruct((nt, d), F32), jax.ShapeDtypeStruct((nt, d), BF16),
                   jax.ShapeDtypeStruct((nt, n_e), F32), jax.ShapeDtypeStruct((n_fb, 1, n_e), F32),
                   jax.ShapeDtypeStruct((1, n_e), F32), jax.ShapeDtypeStruct((nb, HIST_ROWS, d), F32)],
        scratch_shapes=[pltpu.VMEM((HIST_PAD + tb, d), F32), pltpu.VMEM((1, n_e), F32)] + stage_scratch(tb),
        compiler_params=_params(("arbitrary",)),
        name="pool_route_frames",
    )(h1, h1, *route_args, tri(tb))

    hist_s = jnp.concatenate([
        jnp.pad(state_pool[0].astype(F32), [(0, 0), (HIST_ROWS - POOL_HIST, 0), (0, 0)]),
        jnp.zeros((n_seq - db, HIST_ROWS, d), F32)], axis=0)
    any_spec = pl.BlockSpec(memory_space=pl.ANY)
    short_blk = lambda s: (nf // SEQ_S + s, 0)
    h2, u2, info, cum_s, tot_s, state_s = pl.pallas_call(
        functools.partial(_pool_short_kernel, db, n_valid_seq),
        grid=(n_seq,),
        in_specs=[any_spec, any_spec, any_spec,
                  pl.BlockSpec((SEQ_S, d), short_blk),
                  pl.BlockSpec((1, HIST_ROWS, d), lambda s: (s, 0, 0)),
                  _const_spec((1, n_e))] + route_w + [_const_spec((SEQ_S, SEQ_S))],
        out_specs=[pl.BlockSpec((SEQ_S, d), short_blk), pl.BlockSpec((SEQ_S, d), short_blk),
                   pl.BlockSpec((SEQ_S, n_e), short_blk), pl.BlockSpec((1, 1, n_e), lambda s: (s, 0, 0)),
                   pl.BlockSpec((1, n_e), lambda s: (0, 0)),
                   pl.BlockSpec((1, SEQ_S, d), lambda s: (s, 0, 0))],
        out_shape=[jax.ShapeDtypeStruct((nt, d), F32), jax.ShapeDtypeStruct((nt, d), BF16),
                   jax.ShapeDtypeStruct((nt, n_e), F32), jax.ShapeDtypeStruct((n_seq, 1, n_e), F32),
                   jax.ShapeDtypeStruct((1, n_e), F32), jax.ShapeDtypeStruct((n_seq, SEQ_S, d), F32)],
        scratch_shapes=[pltpu.VMEM((HIST_PAD + SEQ_S, d), F32), pltpu.VMEM((1, n_e), F32)] + stage_scratch(SEQ_S),
        input_output_aliases={0: 0, 1: 1, 2: 2},
        compiler_params=_params(("arbitrary",)),
        name="pool_route_short",
    )(h2_f, u2_f, info_f, h1, hist_s, tot_f, *route_args, tri(SEQ_S))

    counts = tot_s[0].astype(jnp.int32)
    cum = jnp.concatenate([cum_f[:, 0], cum_s[::tb // SEQ_S, 0], tot_s], axis=0).astype(jnp.int32)
    padded = (counts + rb - 1) // rb * rb
    pend = jnp.cumsum(padded)
    pstart = pend - padded
    n_rows_max = -(-(TOP_K * (nf + n_valid_seq * SEQ_S)) // rb) * rb + n_e * rb
    n_rb = n_rows_max // rb
    e_tok = info[:, 0:TOP_K].astype(jnp.int32)
    pos = jnp.where(e_tok >= 0, pstart[jnp.maximum(e_tok, 0)] + info[:, 2:2 + TOP_K].astype(jnp.int32), -1)
    gates = info[:, 4:4 + TOP_K]
    pos_l = pos.reshape(n_tb, tb, TOP_K).transpose(0, 2, 1)
    gate_l = gates.reshape(n_tb, tb, TOP_K).transpose(0, 2, 1)
    pos_c = jnp.pad(pos, [(0, 0), (0, n_e - TOP_K)], constant_values=-1)

    blk_row = jnp.arange(n_rb, dtype=jnp.int32) * rb
    blk_e = jnp.minimum(jnp.sum(pend[None, :] <= blk_row[:, None], axis=1), n_e - 1).astype(jnp.int32)
    blk_ok = (blk_row < pend[-1]).astype(jnp.int32)
    r_lo = blk_row - pstart[blk_e]
    r_hi = jnp.minimum(r_lo + rb, counts[blk_e])
    cum_b = cum.T[blk_e]
    first = jnp.sum(cum_b[:, 1:] <= r_lo[:, None], axis=1)
    last = jnp.sum(cum_b[:, :-1] < r_hi[:, None], axis=1) - 1
    blk_lo = jnp.clip(first, 0, n_tb - 1).astype(jnp.int32)
    blk_hi = jnp.clip(last, blk_lo, n_tb - 1).astype(jnp.int32)

    ys = pl.pallas_call(
        functools.partial(_expert_kernel, n_rb),
        grid_spec=pltpu.PrefetchScalarGridSpec(
            num_scalar_prefetch=4,
            grid=(n_rb,),
            in_specs=[pl.BlockSpec(memory_space=pl.ANY),
                      _const_spec((n_tb, TOP_K, tb)), _const_spec((n_tb, TOP_K, tb)),
                      pl.BlockSpec((1, d, d_e), lambda j, be, lo, hi, ok: (be[j], 0, 0)),
                      pl.BlockSpec((1, d, d_e), lambda j, be, lo, hi, ok: (be[j], 0, 0)),
                      pl.BlockSpec((1, d_e, d), lambda j, be, lo, hi, ok: (be[j], 0, 0))],
            out_specs=pl.BlockSpec((rb, d), lambda j, be, lo, hi, ok: (j, 0)),
            scratch_shapes=[pltpu.VMEM((GATHER_SLOTS, tb, d), BF16), pltpu.VMEM((rb, GATHER_SLOTS * tb), BF16),
                            pltpu.VMEM((rb, 1), F32), pltpu.VMEM((rb, d), F32), pltpu.VMEM((rb, 1), F32),
                            pltpu.SemaphoreType.DMA((GATHER_SLOTS,))]),
        out_shape=jax.ShapeDtypeStruct((n_rows_max, d), BF16),
        compiler_params=_params(("arbitrary",)),
        name="experts",
    )(blk_e, blk_lo, blk_hi, blk_ok, u2, pos_l, gate_l, wg_e, wu_e, wd_e)

    cr = COMBINE_ROWS
    per_expert = tb // cr + 1
    n_slots = TOP_K * tb // cr + 2 * n_e
    assert rb % cr == 0 and LANE % cr == 0 and n_slots % (LANE // cr) == 0
    w_lo = pstart[None, :] + cum[:-1]
    w_hi = pstart[None, :] + cum[1:]
    c_lo = w_lo // cr
    n_chunks = jnp.where(w_hi > w_lo, (w_hi - 1) // cr - c_lo + 1, 0)
    q = jnp.arange(per_expert)
    cand = (c_lo[:, :, None] + q).reshape(n_tb, n_e * per_expert)
    keep = (q < n_chunks[:, :, None]).reshape(n_tb, n_e * per_expert)
    order = jnp.argsort(~keep, axis=1, stable=True)[:, :n_slots]
    ids = jnp.where(jnp.take_along_axis(keep, order, axis=1), jnp.take_along_axis(cand, order, axis=1),
                    n_rows_max // cr)
    n_ids = jnp.sum(keep, axis=1).astype(jnp.int32)
    y_frames, y_short = pl.pallas_call(
        functools.partial(_combine_kernel, n_tb, n_fb),
        grid_spec=pltpu.PrefetchScalarGridSpec(
            num_scalar_prefetch=2,
            grid=(n_tb,),
            in_specs=[pl.BlockSpec((tb, d), lambda i, n, ids: (i, 0)),
                      pl.BlockSpec((tb, n_e), lambda i, n, ids: (i, 0)),
                      pl.BlockSpec((1, d), lambda i, n, ids: (0, 0)),
                      pl.BlockSpec(memory_space=pl.ANY)],
            out_specs=[pl.BlockSpec((tb, d), lambda i, n, ids: (jnp.minimum(i, n_fb - 1), 0)),
                       pl.BlockSpec((tb, d), lambda i, n, ids: (jnp.maximum(i - n_fb, 0), 0))],
            scratch_shapes=[pltpu.VMEM((2, n_slots, cr, d), BF16), pltpu.SemaphoreType.DMA((2 * n_slots,))]),
        out_shape=[jax.ShapeDtypeStruct((nf, d), F32), jax.ShapeDtypeStruct((ns, d), F32)],
        compiler_params=_params(("arbitrary",)),
        name="combine",
    )(n_ids, ids.reshape(-1).astype(jnp.int32), h2, pos_c, row(norm_final), ys)

    y_prompt = y_frames.reshape(nb, seq, d)
    y_sample = y_short[:db * SEQ_S].reshape(db, SEQ_S, d)

    def with_meta(frames, short, width):
        meta = jnp.broadcast_to(short[db * SEQ_S:db * SEQ_S + N_META][None], (nb, N_META, width))
        return jnp.concatenate([meta, frames.reshape(nb, seq, width)], axis=1)[None]

    c_p = with_meta(c_frames, c_short, kv_rank)
    r_p = with_meta(r_frames, r_short, QK_ROPE)
    c_s = c_short[:db * SEQ_S].reshape(1, db, SEQ_S, kv_rank)
    r_s = r_short[:db * SEQ_S].reshape(1, db, SEQ_S, QK_ROPE)
    s_p = state_f[:, HIST_ROWS - POOL_HIST:][None]
    s_s = state_s[:db, SEQ_S - POOL_HIST:][None]
    return (y_prompt, y_sample, c_p, r_p, s_p, c_s, r_s, s_s)
```

```python
import functools

import jax
import jax.numpy as jnp
from jax import lax
from jax.experimental import pallas as pl
from jax.experimental.pallas import tpu as pltpu

CHUNK = 64
N_META = 16
N_HEADS = 8
QK_NOPE = 128
QK_ROPE = 64
ROPE_THETA = 10000.0
POOL_WINDOWS = (2, 4, 8, 16)
POOL_HIST = max(POOL_WINDOWS) - 1
TOP_K = 2
RMS_EPS = 1e-6

LANE = 128
BF16_SUBLANES = 16
SEQ_S = 16
F32_SUBLANES = 8
HIST_ROWS = 16
HIST_PAD = F32_SUBLANES * len(POOL_WINDOWS)
TOKEN_BLOCK = 256
ROW_BLOCK = 256
ATTN_BLOCK = 256
CACHE_BLOCK = 1024
KEY_BLOCKS_PER_ITER = 4
SCORES_AHEAD = 8
COMBINE_ROWS = 64
GATHER_SLOTS = 6
V7X_VMEM_BYTES = 64 * 1024 * 1024
VMEM_LIMIT = V7X_VMEM_BYTES // 8 * 7

F32 = jnp.float32
BF16 = jnp.bfloat16
NEG_INF = float("-inf")
LOG2_E = 1.4426950408889634


def _dot(a, b):
    return jnp.dot(a, b, preferred_element_type=F32)


def _dot_nt(a, b):
    return lax.dot_general(a, b, (((1,), (1,)), ((), ())), preferred_element_type=F32)


def _rms(x, g):
    return x * lax.rsqrt(jnp.mean(x * x, axis=-1, keepdims=True) + RMS_EPS) * g


def _const_spec(shape):
    nd = len(shape)
    return pl.BlockSpec(shape, lambda *_: (0,) * nd, pipeline_mode=pl.Buffered(1))


def _params(sem):
    return pltpu.CompilerParams(dimension_semantics=sem, vmem_limit_bytes=VMEM_LIMIT)


def _qkv_kernel(n_fb, scale, xp_ref, xs_ref, tab_ref, tabt_ref, g_ref, wdq_ref, gq_ref, wuq_ref, wuqt_ref, wukt_ref,
                wdkv_ref, gkv_ref, wukf_ref, wuvt_ref, q_ref, qt_ref, kvb_ref, kn_ref, kr_ref, vt_ref, cp_ref, rp_ref,
                cs_ref, rs_ref):
    i = pl.program_id(0)
    x = jnp.where(i < n_fb, xp_ref[...], xs_ref[...])
    u = _rms(x, g_ref[...]).astype(BF16)
    cq = _rms(_dot(u, wdq_ref[...]), gq_ref[...]).astype(BF16)
    cos = tab_ref[:, :LANE]
    sin = tab_ref[:, LANE:]
    hq = QK_NOPE + 2 * LANE

    kv = _dot(u, wdkv_ref[...])
    c = _rms(kv[:, :2 * LANE], gkv_ref[...])
    r = kv[:, 2 * LANE:3 * LANE] * cos + kv[:, 3 * LANE:] * sin
    c_bf = c.astype(BF16)
    r_bf = r.astype(BF16)

    kn_ref[...] = _dot(c_bf, wukf_ref[...]).astype(BF16)
    kr_ref[...] = r_bf
    vt = _dot_nt(wuvt_ref[...], c_bf).astype(BF16)
    hv = vt.shape[0] // N_HEADS
    hv_ext = vt_ref.shape[1] // N_HEADS
    for h in range(N_HEADS):
        vt_ref[0, h * hv_ext:h * hv_ext + hv, :] = vt[h * hv:(h + 1) * hv, :]
        vt_ref[0, h * hv_ext + hv:(h + 1) * hv_ext, :] = jnp.ones((hv_ext - hv, vt.shape[1]), BF16)

    @pl.when(i < n_fb)
    def _():
        qat = _dot_nt(wuqt_ref[...], cq)
        cos_t = tabt_ref[:QK_ROPE, :]
        sin_t = tabt_ref[LANE:LANE + QK_ROPE, :]
        pe0 = N_HEADS * QK_NOPE
        sw0 = pe0 + N_HEADS * QK_ROPE
        for h in range(N_HEADS):
            qt_ref[0, h, :QK_NOPE, :] = (qat[h * QK_NOPE:(h + 1) * QK_NOPE, :] * scale).astype(BF16)
            a = qat[pe0 + h * QK_ROPE:pe0 + (h + 1) * QK_ROPE, :]
            b = qat[sw0 + h * QK_ROPE:sw0 + (h + 1) * QK_ROPE, :]
            qt_ref[0, h, QK_NOPE:QK_NOPE + QK_ROPE, :] = ((a * cos_t + b * sin_t) * scale).astype(BF16)
            qt_ref[0, h, QK_NOPE + QK_ROPE:, :] = jnp.zeros((LANE - QK_ROPE, qt_ref.shape[3]), BF16)
        cp_ref[...] = c
        rp_ref[...] = r[:, :QK_ROPE]

    @pl.when(i >= n_fb)
    def _():
        qa = _dot(cq, wuq_ref[...])
        for h in range(N_HEADS):
            qn = qa[:, h * LANE:(h + 1) * LANE].astype(BF16)
            q_ref[:, h * hq:h * hq + 2 * LANE] = (_dot(qn, wukt_ref[h]) * scale).astype(BF16)
            a = qa[:, (N_HEADS + h) * LANE:(N_HEADS + h + 1) * LANE]
            b = qa[:, (2 * N_HEADS + h) * LANE:(2 * N_HEADS + h + 1) * LANE]
            q_ref[:, h * hq + 2 * LANE:(h + 1) * hq] = ((a * cos + b * sin) * scale).astype(BF16)
        kvb_ref[:, :2 * LANE] = c_bf
        kvb_ref[:, 2 * LANE:] = r_bf
        cs_ref[...] = c
        rs_ref[...] = r[:, :QK_ROPE]


def _attn_frames_kernel(qb, meta_col, qt_ref, kn_ref, kr_ref, vt_ref, knm_ref, krm_ref, vtm_ref, o_ref, m_ref, acc_ref):
    j = pl.program_id(0) % qb
    tq = o_ref.shape[0]
    hn = kn_ref.shape[1] // N_HEADS
    hv = o_ref.shape[1] // N_HEADS
    hx = vt_ref.shape[1] // N_HEADS

    def head_keys(kn, kr, rows, h):
        return jnp.concatenate([kn[rows, h * hn:(h + 1) * hn], kr[rows, :]], axis=1)

    def scores(h, rows):
        return _dot(head_keys(kn_ref, kr_ref, rows, h), qt_ref[0, h])

    m_ref[...] = jnp.full(m_ref.shape, NEG_INF, F32)
    acc_ref[...] = jnp.zeros_like(acc_ref)

    def unmasked(kbs):
        steps = [(kb, h) for kb in kbs for h in range(N_HEADS)]
        block_scores = lambda kb, h: scores(h, pl.ds(pl.multiple_of(kb * tq, tq), tq))
        ahead = [block_scores(*step) for step in steps[:SCORES_AHEAD]]
        for n, (kb, h) in enumerate(steps):
            s = ahead.pop(0)
            if n + SCORES_AHEAD < len(steps):
                ahead.append(block_scores(*steps[n + SCORES_AHEAD]))
            m_old = m_ref[h]
            m_new = jnp.maximum(m_old, jnp.max(s, axis=0, keepdims=True))
            alpha = jnp.exp2(m_old - m_new)
            p = jnp.exp2(s - m_new)
            m_ref[h] = m_new
            acc_ref[h] = alpha * acc_ref[h] + _dot(vt_ref[kb, h * hx:(h + 1) * hx, :], p.astype(BF16))

    def body(t, carry):
        unmasked([KEY_BLOCKS_PER_ITER * t + u for u in range(KEY_BLOCKS_PER_ITER)])
        return carry

    lax.fori_loop(0, j // KEY_BLOCKS_PER_ITER, body, 0)
    for left in range(1, KEY_BLOCKS_PER_ITER):
        @pl.when(j % KEY_BLOCKS_PER_ITER == left)
        def _():
            unmasked([j - left + u for u in range(left)])

    rows = pl.ds(pl.multiple_of(j * tq, tq), tq)
    visible = (lax.broadcasted_iota(jnp.int32, (tq, tq), 0) // CHUNK
               <= lax.broadcasted_iota(jnp.int32, (tq, tq), 1) // CHUNK)

    def last_scores(h):
        return scores(h, rows), _dot(head_keys(knm_ref, krm_ref, slice(None), h), qt_ref[0, h])

    ahead = [last_scores(h) for h in range(SCORES_AHEAD)]
    for h in range(N_HEADS):
        s, s_meta = ahead.pop(0)
        s = jnp.where(visible, s, NEG_INF)
        if h + SCORES_AHEAD < N_HEADS:
            ahead.append(last_scores(h + SCORES_AHEAD))
        m_old = m_ref[h]
        m_new = jnp.maximum(m_old, jnp.maximum(jnp.max(s, axis=0, keepdims=True),
                                               jnp.max(s_meta, axis=0, keepdims=True)))
        alpha = jnp.exp2(m_old - m_new)
        p = jnp.exp2(s - m_new)
        p_meta = jnp.exp2(s_meta - m_new)
        acc = (alpha * acc_ref[h] + _dot(vt_ref[j, h * hx:(h + 1) * hx, :], p.astype(BF16))
               + _dot(vtm_ref[0, h * hx:(h + 1) * hx, meta_col:meta_col + N_META], p_meta.astype(BF16)))
        o_ref[:, h * hv:(h + 1) * hv] = (acc[:hv] * (1.0 / acc[hv:hv + 1])).T.astype(BF16)


def _attn_short_kernel(n_cached_seq, q_ref, kvn_ref, cc_ref, cr_ref, wuv_ref, o_ref):
    s_id = pl.program_id(0)
    hq = q_ref.shape[1] // N_HEADS
    n_cache = cc_ref.shape[2]
    qs = jnp.concatenate([q_ref[:, h * hq:(h + 1) * hq] for h in range(N_HEADS)], axis=0)
    q_lat = qs[:, :2 * LANE]
    q_pe = qs[:, 2 * LANE:2 * LANE + QK_ROPE]
    kvn = kvn_ref[...]
    s = _dot_nt(qs, kvn)
    m = jnp.max(s, axis=1, keepdims=True)
    p = jnp.exp2(s - m)
    l = jnp.sum(p, axis=1, keepdims=True)
    acc = _dot(p.astype(BF16), kvn[:, :2 * LANE])
    has_cache = s_id < n_cached_seq
    start = 0
    while start < n_cache:
        size = min(CACHE_BLOCK, n_cache - start)
        ck = cc_ref[0, 0, start:start + size, :].astype(BF16)
        rk = cr_ref[0, 0, start:start + size, :].astype(BF16)
        s = _dot_nt(q_lat, ck) + _dot_nt(q_pe, rk)
        s = jnp.where(has_cache, s, NEG_INF)
        m_new = jnp.maximum(m, jnp.max(s, axis=1, keepdims=True))
        alpha = jnp.exp2(m - m_new)
        p = jnp.exp2(s - m_new)
        l = alpha * l + jnp.sum(p, axis=1, keepdims=True)
        acc = alpha * acc + _dot(p.astype(BF16), ck)
        m = m_new
        start += size
    o = (acc / l).astype(BF16)
    hv = wuv_ref.shape[2]
    for h in range(N_HEADS):
        o_ref[:, h * hv:(h + 1) * hv] = _dot(o[h * SEQ_S:(h + 1) * SEQ_S, :], wuv_ref[h]).astype(BF16)


def _proj_ffn_kernel(n_fb, xp_ref, xs_ref, op_ref, os_ref, wo_ref, g_ref, wg_ref, wu_ref, wd_ref, *cast_refs):
    n_cast = (len(cast_refs) - 1) // 2
    h_ref = cast_refs[n_cast]
    for src, dst in zip(cast_refs[:n_cast], cast_refs[n_cast + 1:]):
        dst[...] = src[...].astype(BF16)
    i = pl.program_id(0)
    x = jnp.where(i < n_fb, xp_ref[...], xs_ref[...])
    o = jnp.where(i < n_fb, op_ref[...], os_ref[...])
    h1 = x + _dot(o, wo_ref[...])
    u = _rms(h1, g_ref[...]).astype(BF16)
    act = jax.nn.silu(_dot(u, wg_ref[...])) * _dot(u, wu_ref[...])
    h_ref[...] = h1 + _dot(act.astype(BF16), wd_ref[...])


def _set_history(ext_ref, hist):
    for start in range(0, HIST_PAD, HIST_ROWS):
        ext_ref[start:start + HIST_ROWS, :] = hist


def _window_sums(ext_ref, stage_refs, t):
    grp = ext_ref.shape[1] // len(POOL_WINDOWS)
    end = HIST_PAD + t
    first = F32_SUBLANES
    s = ext_ref[first:end, :] + ext_ref[first - 1:end - 1, :]
    sums = [s[HIST_PAD - first:, :grp]]
    for ref, shift in zip(stage_refs, POOL_WINDOWS):
        rest = s[:, grp:]
        ref[first:end, :] = rest
        first += F32_SUBLANES
        s = rest[F32_SUBLANES:, :] + ref[first - shift:end - shift, :]
        sums.append(s[HIST_PAD - first:, :grp])
    return sums


def _pool_route(h1, ext_ref, stage_refs, cnt_rows, valid, base, gm_ref, pw_ref, ps_ref, gf_ref, wr_ref, br_ref,
                tri_ref):
    t = h1.shape[0]
    grp = h1.shape[1] // len(POOL_WINDOWS)
    u1 = _rms(h1, gm_ref[...])
    ext_ref[HIST_PAD:HIST_PAD + t, :] = u1
    sums = _window_sums(ext_ref, stage_refs, t)
    ys = []
    for g, w in enumerate(POOL_WINDOWS):
        cnt = float(w) if cnt_rows is None else jnp.minimum(float(w), cnt_rows)
        mean = sums[g] / cnt - u1[:, g * grp:(g + 1) * grp]
        ys.append(_dot(mean.astype(BF16), pw_ref[g]))
    h2 = h1 + jnp.concatenate(ys, axis=1) * ps_ref[...]
    u2 = _rms(h2, gf_ref[...])

    n_e = wr_ref.shape[1] // 2
    u2_hi = u2.astype(BF16)
    u2_lo = (u2 - u2_hi.astype(F32)).astype(BF16)
    parts = _dot(u2_hi, wr_ref[...]) + _dot(u2_lo, wr_ref[...])
    logits = parts[:, :n_e] + parts[:, n_e:] + br_ref[...]
    e_iota = lax.broadcasted_iota(jnp.int32, logits.shape, 1).astype(F32)
    v0 = jnp.max(logits, axis=1, keepdims=True)
    e0 = jnp.min(jnp.where(logits == v0, e_iota, float(n_e)), axis=1, keepdims=True)
    rest = jnp.where(e_iota == e0, NEG_INF, logits)
    v1 = jnp.max(rest, axis=1, keepdims=True)
    e1 = jnp.min(jnp.where(rest == v1, e_iota, float(n_e)), axis=1, keepdims=True)
    tt = jnp.exp(v1 - v0)
    g0 = 1.0 / (1.0 + tt)
    g1 = tt / (1.0 + tt)
    hit0 = e_iota == e0
    hit1 = e_iota == e1
    onehot = jnp.where(valid & (hit0 | hit1), 1.0, 0.0)
    rank = _dot(tri_ref[...], onehot.astype(BF16)) + base
    r0 = jnp.sum(jnp.where(hit0, rank, 0.0), axis=1, keepdims=True)
    r1 = jnp.sum(jnp.where(hit1, rank, 0.0), axis=1, keepdims=True)
    cols = (jnp.where(valid, e0, -1).astype(F32), jnp.where(valid, e1, -1).astype(F32), r0, r1, g0, g1)
    info = jnp.zeros(logits.shape, F32)
    for k, col in enumerate(cols):
        info = jnp.where(e_iota == k, col, info)
    new_base = base + jnp.sum(onehot, axis=0, keepdims=True)
    return h2, u2, info, new_base, u1


def _pool_frames_kernel(n_fb, h_ref, hist_ref, gm_ref, pw_ref, ps_ref, gf_ref, wr_ref, br_ref, tri_ref,
                        h2_ref, u2_ref, info_ref, cum_ref, tot_ref, state_ref, ext_ref, base_ref, *stage_refs):
    i = pl.program_id(0)

    @pl.when(i == 0)
    def _():
        base_ref[...] = jnp.zeros_like(base_ref)

    @pl.when(i >= n_fb)
    def _():
        h2_ref[...] = jnp.zeros_like(h2_ref)
        u2_ref[...] = jnp.zeros_like(u2_ref)
        info_ref[...] = jnp.zeros_like(info_ref)

    @pl.when(i < n_fb)
    def _():
        _set_history(ext_ref, _rms(hist_ref[...], gm_ref[...]))
        base = base_ref[...]
        cum_ref[0] = base
        h2, u2, info, new_base, u1 = _pool_route(h_ref[...], ext_ref, stage_refs, None, True, base, gm_ref, pw_ref,
                                                 ps_ref, gf_ref, wr_ref, br_ref, tri_ref)
        h2_ref[...] = h2
        u2_ref[...] = u2.astype(BF16)
        info_ref[...] = info
        base_ref[...] = new_base
        tot_ref[...] = new_base
        state_ref[0] = u1[u1.shape[0] - HIST_ROWS:, :]


def _pool_short_kernel(n_sample, n_valid, h2_hbm, u2_hbm, info_hbm, h_ref, hist_ref, base0_ref, gm_ref, pw_ref, ps_ref,
                       gf_ref, wr_ref, br_ref, tri_ref, h2_ref, u2_ref, info_ref, cum_ref, tot_ref, state_ref,
                       ext_ref, base_ref, *stage_refs):
    del h2_hbm, u2_hbm, info_hbm
    s_id = pl.program_id(0)

    @pl.when(s_id == 0)
    def _():
        base_ref[...] = base0_ref[...]

    _set_history(ext_ref, hist_ref[0])
    base = base_ref[...]
    cum_ref[0] = base
    pos1 = (lax.broadcasted_iota(jnp.int32, (SEQ_S, 1), 0) + 1).astype(F32)
    cnt_rows = jnp.where(s_id < n_sample, float(max(POOL_WINDOWS)), pos1)
    h2, u2, info, new_base, u1 = _pool_route(h_ref[...], ext_ref, stage_refs, cnt_rows, s_id < n_valid, base, gm_ref,
                                             pw_ref, ps_ref, gf_ref, wr_ref, br_ref, tri_ref)
    h2_ref[...] = h2
    u2_ref[...] = u2.astype(BF16)
    info_ref[...] = info
    base_ref[...] = new_base
    tot_ref[...] = new_base
    state_ref[0] = u1


def _expert_kernel(n_blocks, be_ref, lo_ref, hi_ref, ok_ref, u_hbm, pos_ref, gate_ref, wg_ref, wu_ref, wd_ref, ys_ref,
                   buf_ref, sel_ref, gsel_ref, acc_ref, gacc_ref, sem):
    del be_ref
    j = pl.program_id(0)
    rb = ys_ref.shape[0]
    n_slots, tc, _ = buf_ref.shape
    last_chunk = pos_ref.shape[0] - 1

    def n_chunks(jj):
        return jnp.where(ok_ref[jj] != 0, hi_ref[jj] - lo_ref[jj] + 1, 0)

    def chunk_copy(c, slot):
        return pltpu.make_async_copy(u_hbm.at[pl.ds(pl.multiple_of(c * tc, tc), tc), :], buf_ref.at[slot],
                                     sem.at[slot])

    def request(jj):
        n_req = jnp.minimum(n_chunks(jj), n_slots)
        for k in range(n_slots):
            @pl.when(k < n_req)
            def _():
                chunk_copy(lo_ref[jj] + k, k).start()

    def select(jj, c, live):
        rows = jj * rb + lax.broadcasted_iota(jnp.int32, (rb, tc), 0)
        pos = pos_ref[c]
        gate = gate_ref[c]
        m0 = (rows == pos[0:1, :]) & live
        m1 = (rows == pos[1:2, :]) & live
        sel = jnp.where(m0 | m1, 1.0, 0.0).astype(BF16)
        row_gate = jnp.sum(jnp.where(m0, gate[0:1, :], 0.0) + jnp.where(m1, gate[1:2, :], 0.0), axis=1, keepdims=True)
        return sel, row_gate

    def build_selection(jj):
        n_req = jnp.minimum(n_chunks(jj), n_slots)
        row_gate = jnp.zeros(gsel_ref.shape, F32)
        for k in range(n_slots):
            sel, g = select(jj, jnp.minimum(lo_ref[jj] + k, last_chunk), k < n_req)
            sel_ref[:, k * tc:(k + 1) * tc] = sel
            row_gate = row_gate + g
        gsel_ref[...] = row_gate

    @pl.when(j == 0)
    def _():
        buf_ref[...] = jnp.zeros_like(buf_ref)
        request(0)
        build_selection(0)

    n = n_chunks(j)
    lo = lo_ref[j]

    @pl.when(n > 0)
    def _():
        for k in range(n_slots):
            @pl.when(k < jnp.minimum(n, n_slots))
            def _():
                chunk_copy(lo + k, k).wait()

        acc_ref[...] = _dot(sel_ref[...], buf_ref[...].reshape(n_slots * tc, -1))
        gacc_ref[...] = gsel_ref[...]

        def overflow(k, carry):
            copy = chunk_copy(lo + k, 0)
            copy.start()
            copy.wait()
            sel, row_gate = select(j, lo + k, True)
            acc_ref[...] += _dot(sel, buf_ref[0])
            gacc_ref[...] += row_gate
            return carry

        lax.fori_loop(n_slots, jnp.maximum(n, n_slots), overflow, 0)

    @pl.when(j + 1 < n_blocks)
    def _():
        request(jnp.minimum(j + 1, n_blocks - 1))

    @pl.when(n > 0)
    def _():
        xg = acc_ref[...].astype(BF16)
        act = jax.nn.silu(_dot(xg, wg_ref[0])) * _dot(xg, wu_ref[0])
        ys_ref[...] = (_dot(act.astype(BF16), wd_ref[0]) * gacc_ref[...]).astype(BF16)
        build_selection(jnp.minimum(j + 1, n_blocks - 1))

    @pl.when(n == 0)
    def _():
        ys_ref[...] = jnp.zeros_like(ys_ref)


def _combine_kernel(n_blocks, n_fb, n_ref, ids_ref, h2_ref, pos_ref, g_ref, ys_hbm, yp_ref, yshort_ref, buf_ref, sem):
    i = pl.program_id(0)
    _, n_slots, cr, _ = buf_ref.shape
    per_vreg = LANE // cr
    par = i % 2
    nxt = jnp.minimum(i + 1, n_blocks - 1)

    def chunk_copy(ii, k, which):
        ch = ids_ref[ii * n_slots + k]
        return pltpu.make_async_copy(ys_hbm.at[pl.ds(pl.multiple_of(ch * cr, cr), cr), :], buf_ref.at[which, k],
                                     sem.at[which * n_slots + k])

    def request(ii, which):
        for k in range(n_slots):
            @pl.when(k < n_ref[ii])
            def _():
                chunk_copy(ii, k, which).start()

    @pl.when(i == 0)
    def _():
        buf_ref[...] = jnp.zeros_like(buf_ref)
        request(0, 0)

    @pl.when(i + 1 < n_blocks)
    def _():
        request(nxt, 1 - par)

    lane = lax.broadcasted_iota(jnp.int32, (1, LANE), 1)
    pieces = []
    for v in range(n_slots // per_vreg):
        row = jnp.zeros((1, LANE), jnp.int32)
        for q in range(per_vreg):
            k = v * per_vreg + q
            row = jnp.where(lane // cr == q, ids_ref[i * n_slots + k] * cr + lane % cr, row)
        pieces.append(row)
    slot_rows = jnp.concatenate(pieces, axis=1)
    sel = jnp.where((slot_rows == pos_ref[:, 0:1]) | (slot_rows == pos_ref[:, 1:2]), 1.0, 0.0).astype(BF16)

    for k in range(n_slots):
        @pl.when(k < n_ref[i])
        def _():
            chunk_copy(i, k, par).wait()

    moe = _dot(sel, buf_ref[par].reshape(n_slots * cr, -1))
    y = _rms(h2_ref[...] + moe, g_ref[...])

    @pl.when(i < n_fb)
    def _():
        yp_ref[...] = y

    @pl.when(i >= n_fb)
    def _():
        yshort_ref[...] = y


def _rope_table(pos):
    half = QK_ROPE // 2
    inv = ROPE_THETA ** (-jnp.arange(half, dtype=F32) / half)
    ang = pos.astype(F32)[:, None] * inv[None, :]
    cos = jnp.cos(ang)
    sin = jnp.sin(ang)
    zero = jnp.zeros((pos.shape[0], LANE - QK_ROPE), F32)
    return jnp.concatenate([cos, cos, zero, -sin, sin, zero], axis=1)


def _half_swap(w):
    half = QK_ROPE // 2
    return jnp.concatenate([w[..., half:], w[..., :half]], axis=-1)


def _pad_last(w, width):
    return jnp.pad(w, [(0, 0)] * (w.ndim - 1) + [(0, width - w.shape[-1])])


def kernel(x_prompt, x_sample, cache_kv_latent, cache_k_rope, state_pool, meta_tokens, norm_mix, norm_ffn, norm_final,
           mla_w_dq, mla_g_q, mla_w_uq, mla_w_dkv, mla_g_kv, mla_w_uk, mla_w_uv, mla_w_o, pool_w, pool_scale,
           ffn_w_gate, ffn_w_up, ffn_w_down, moe_w_router, moe_b_router, moe_w_gate, moe_w_up, moe_w_down):
    nb, seq, d = x_prompt.shape
    db, dseq, _ = x_sample.shape
    n_cache = cache_kv_latent.shape[2]
    q_rank = mla_w_dq.shape[2]
    kv_rank = mla_g_kv.shape[1]
    v_dim = mla_w_uv.shape[3]
    d_ff = ffn_w_gate.shape[2]
    n_e = moe_w_router.shape[2]
    d_e = moe_w_gate.shape[3]
    tb, rb, tq = TOKEN_BLOCK, ROW_BLOCK, ATTN_BLOCK
    assert norm_mix.shape[0] == 2 and cache_kv_latent.shape[0] == 1 and state_pool.shape[0] == 1
    assert dseq == SEQ_S and N_META == SEQ_S and meta_tokens.shape[0] == N_META
    assert POOL_WINDOWS == tuple(2 ** (k + 1) for k in range(len(POOL_WINDOWS))) and HIST_PAD % HIST_ROWS == 0
    assert kv_rank == 2 * LANE and QK_NOPE == LANE and QK_ROPE <= LANE and HIST_ROWS >= POOL_HIST
    assert seq % tb == 0 and tb == tq and tq % CHUNK == 0 and tb % SEQ_S == 0 and d % LANE == 0
    assert (n_cache - N_META) % CHUNK == 0 and dseq <= CHUNK

    nf = nb * seq
    n_valid_seq = db + 1
    ns = -(-(n_valid_seq * SEQ_S) // tb) * tb
    n_seq = ns // SEQ_S
    nt = nf + ns
    n_fb, n_tb = nf // tb, nt // tb
    sb = seq // tb
    meta_row = nf + db * SEQ_S
    scale = float((QK_NOPE + QK_ROPE) ** -0.5)
    hq = 3 * LANE

    xp = x_prompt.reshape(nf, d)
    xs = jnp.concatenate([x_sample.reshape(db * SEQ_S, d), meta_tokens.astype(x_prompt.dtype),
                          jnp.zeros((ns - n_valid_seq * SEQ_S, d), x_prompt.dtype)], axis=0)

    t_s = jnp.arange(SEQ_S)
    pos_short = jnp.concatenate([jnp.tile(n_cache + t_s, db), jnp.tile(t_s, n_seq - db)])
    tab_frames = _rope_table(N_META + jnp.arange(seq))
    tab = jnp.concatenate([tab_frames, _rope_table(pos_short)], axis=0)

    wuq = mla_w_uq[0].reshape(q_rank, N_HEADS, QK_NOPE + QK_ROPE)
    wuq_pe = wuq[:, :, QK_NOPE:]
    wuq2 = jnp.concatenate([wuq[:, :, :QK_NOPE].reshape(q_rank, -1),
                            _pad_last(wuq_pe, LANE).reshape(q_rank, -1),
                            _pad_last(_half_swap(wuq_pe), LANE).reshape(q_rank, -1)], axis=1).astype(BF16)
    wuq_t = jnp.concatenate([wuq[:, :, :QK_NOPE].reshape(q_rank, -1), wuq_pe.reshape(q_rank, -1),
                             _half_swap(wuq_pe).reshape(q_rank, -1)], axis=1).T.astype(BF16)
    wdkv_r =mla_w_dkv[0][:, kv_rank:]
    wdkv2 = jnp.concatenate([mla_w_dkv[0][:, :kv_rank], _pad_last(wdkv_r, LANE),
                             _pad_last(_half_swap(wdkv_r), LANE)], axis=1).astype(BF16)
    wuk_t = jnp.transpose(mla_w_uk[0], (1, 2, 0)).astype(BF16)
    wuk_flat = mla_w_uk[0].reshape(kv_rank, N_HEADS * QK_NOPE).astype(BF16)
    wuv = jnp.transpose(mla_w_uv[0], (1, 0, 2)).astype(BF16)
    wuv_t = mla_w_uv[0].reshape(kv_rank, N_HEADS * v_dim).T.astype(BF16)
    row = lambda v: v.reshape(1, -1)
    hk = 2 * LANE
    hx = v_dim + BF16_SUBLANES

    tok_p = pl.BlockSpec((tb, d), lambda i: (jnp.minimum(i, n_fb - 1), 0))
    tok_s = pl.BlockSpec((tb, d), lambda i: (jnp.maximum(i - n_fb, 0), 0))
    short_tok = lambda i: (jnp.maximum(i - n_fb, 0), 0)
    frame_tok = lambda i: (jnp.minimum(i, n_fb - 1), 0)
    q_short, qt_frames, kvb_short, kn_all, kr_all, vt_all, c_frames, r_frames, c_short, r_short = pl.pallas_call(
        functools.partial(_qkv_kernel, n_fb, scale * LOG2_E),
        grid=(n_tb,),
        in_specs=[tok_p, tok_s,
                  pl.BlockSpec((tb, 2 * LANE), lambda i: (jnp.where(i < n_fb, i % sb, sb + i - n_fb), 0)),
                  pl.BlockSpec((2 * LANE, tb), lambda i: (0, i % sb)),
                  _const_spec((1, d)), _const_spec((d, q_rank)), _const_spec((1, q_rank)),
                  _const_spec(wuq2.shape), _const_spec(wuq_t.shape), _const_spec(wuk_t.shape),
                  _const_spec(wdkv2.shape), _const_spec((1, kv_rank)), _const_spec(wuk_flat.shape),
                  _const_spec(wuv_t.shape)],
        out_specs=[pl.BlockSpec((tb, N_HEADS * hq), short_tok),
                   pl.BlockSpec((1, N_HEADS, hk, tb), lambda i: (jnp.minimum(i, n_fb - 1), 0, 0, 0)),
                   pl.BlockSpec((tb, hq), short_tok),
                   pl.BlockSpec((tb, N_HEADS * QK_NOPE), lambda i: (i, 0)),
                   pl.BlockSpec((tb, LANE), lambda i: (i, 0)),
                   pl.BlockSpec((1, N_HEADS * hx, tb), lambda i: (i, 0, 0)),
                   pl.BlockSpec((tb, kv_rank), frame_tok), pl.BlockSpec((tb, QK_ROPE), frame_tok),
                   pl.BlockSpec((tb, kv_rank), short_tok), pl.BlockSpec((tb, QK_ROPE), short_tok)],
        out_shape=[jax.ShapeDtypeStruct((ns, N_HEADS * hq), BF16),
                   jax.ShapeDtypeStruct((n_fb, N_HEADS, hk, tb), BF16),
                   jax.ShapeDtypeStruct((ns, hq), BF16),
                   jax.ShapeDtypeStruct((nt, N_HEADS * QK_NOPE), BF16), jax.ShapeDtypeStruct((nt, LANE), BF16),
                   jax.ShapeDtypeStruct((n_tb, N_HEADS * hx, tb), BF16),
                   jax.ShapeDtypeStruct((nf, kv_rank), F32), jax.ShapeDtypeStruct((nf, QK_ROPE), F32),
                   jax.ShapeDtypeStruct((ns, kv_rank), F32), jax.ShapeDtypeStruct((ns, QK_ROPE), F32)],
        compiler_params=_params(("arbitrary",)),
        name="qkv",
    )(xp, xs, tab, tab_frames.T, row(norm_mix[0]), mla_w_dq[0].astype(BF16), row(mla_g_q[0]), wuq2, wuq_t, wuk_t,
      wdkv2, row(mla_g_kv[0]), wuk_flat, wuv_t)

    qb = seq // tq
    once = pl.Buffered(1)
    o_frames = pl.pallas_call(
        functools.partial(_attn_frames_kernel, qb, meta_row % tb),
        grid=(n_fb,),
        in_specs=[pl.BlockSpec((1, N_HEADS, hk, tq), lambda i: (i, 0, 0, 0)),
                  pl.BlockSpec((seq, N_HEADS * QK_NOPE), lambda i: (i // qb, 0)),
                  pl.BlockSpec((seq, LANE), lambda i: (i // qb, 0)),
                  pl.BlockSpec((qb, N_HEADS * hx, tq), lambda i: (i // qb, 0, 0)),
                  pl.BlockSpec((N_META, N_HEADS * QK_NOPE), lambda i: (meta_row // N_META, 0), pipeline_mode=once),
                  pl.BlockSpec((N_META, LANE), lambda i: (meta_row // N_META, 0), pipeline_mode=once),
                  pl.BlockSpec((1, N_HEADS * hx, tb), lambda i: (meta_row // tb, 0, 0), pipeline_mode=once)],
        out_specs=pl.BlockSpec((tq, N_HEADS * v_dim), lambda i: (i, 0)),
        out_shape=jax.ShapeDtypeStruct((nf, N_HEADS * v_dim), BF16),
        scratch_shapes=[pltpu.VMEM((N_HEADS, 1, tq), F32), pltpu.VMEM((N_HEADS, hx, tq), F32)],
        compiler_params=_params(("parallel",)),
        name="attn_frames",
    )(qt_frames, kn_all, kr_all, vt_all, kn_all, kr_all, vt_all)

    o_short = pl.pallas_call(
        functools.partial(_attn_short_kernel, db),
        grid=(n_seq,),
        in_specs=[pl.BlockSpec((SEQ_S, N_HEADS * hq), lambda s: (s, 0)),
                  pl.BlockSpec((SEQ_S, hq), lambda s: (s, 0)),
                  pl.BlockSpec((1, 1, n_cache, kv_rank), lambda s: (0, jnp.minimum(s, db - 1), 0, 0)),
                  pl.BlockSpec((1, 1, n_cache, QK_ROPE), lambda s: (0, jnp.minimum(s, db - 1), 0, 0)),
                  _const_spec(wuv.shape)],
        out_specs=pl.BlockSpec((SEQ_S, N_HEADS * v_dim), lambda s: (s, 0)),
        out_shape=jax.ShapeDtypeStruct((ns, N_HEADS * v_dim), BF16),
        compiler_params=_params(("parallel",)),
        name="attn_short",
    )(q_short, kvb_short, cache_kv_latent, cache_k_rope, wuv)

    moe_f32 = [moe_w_gate[0].reshape(n_e * d, d_e), moe_w_up[0].reshape(n_e * d, d_e),
               moe_w_down[0].reshape(n_e * d_e, d)]
    cast_in, cast_out, cast_shapes = [], [], []
    for w in moe_f32:
        steps = max(s for s in range(1, n_tb + 1) if w.shape[0] % s == 0 and (w.shape[0] // s) % BF16_SUBLANES == 0)
        spec = pl.BlockSpec((w.shape[0] // steps, w.shape[1]), lambda i, last=steps - 1: (jnp.minimum(i, last), 0))
        cast_in.append(spec)
        cast_out.append(spec)
        cast_shapes.append(jax.ShapeDtypeStruct(w.shape, BF16))
    h1, wg_e, wu_e, wd_e = pl.pallas_call(
        functools.partial(_proj_ffn_kernel, n_fb),
        grid=(n_tb,),
        in_specs=[tok_p, tok_s,
                  pl.BlockSpec((tb, N_HEADS * v_dim), lambda i: (jnp.minimum(i, n_fb - 1), 0)),
                  pl.BlockSpec((tb, N_HEADS * v_dim), short_tok),
                  _const_spec((N_HEADS * v_dim, d)), _const_spec((1, d)),
                  _const_spec((d, d_ff)), _const_spec((d, d_ff)), _const_spec((d_ff, d))] + cast_in,
        out_specs=[pl.BlockSpec((tb, d), lambda i: (i, 0))] + cast_out,
        out_shape=[jax.ShapeDtypeStruct((nt, d), F32)] + cast_shapes,
        compiler_params=_params(("arbitrary",)),
        name="proj_ffn",
    )(xp, xs, o_frames, o_short, mla_w_o[0].astype(BF16), row(norm_ffn[0]),
      ffn_w_gate[0].astype(BF16), ffn_w_up[0].astype(BF16), ffn_w_down[0].astype(BF16), *moe_f32)
    wg_e = wg_e.reshape(n_e, d, d_e)
    wu_e = wu_e.reshape(n_e, d, d_e)
    wd_e = wd_e.reshape(n_e, d_e, d)

    route_w = [_const_spec((1, d)), _const_spec(pool_w.shape[1:]), _const_spec((1, d)), _const_spec((1, d)),
               _const_spec((d, 2 * n_e)), _const_spec((1, n_e))]
    wr_hi = moe_w_router[0].astype(BF16)
    wr_lo = (moe_w_router[0].astype(F32) - wr_hi.astype(F32)).astype(BF16)
    route_args = (row(norm_mix[1]), pool_w[0].astype(BF16), row(pool_scale[0]), row(norm_ffn[1]),
                  jnp.concatenate([wr_hi, wr_lo], axis=1), row(moe_b_router[0]))
    tri = lambda n: (jnp.arange(n)[:, None] > jnp.arange(n)[None, :]).astype(BF16)
    n_win = len(POOL_WINDOWS)
    stage_scratch = lambda t: [pltpu.VMEM((HIST_PAD + t, d // n_win * (n_win - 1 - k)), F32) for k in range(n_win - 1)]
    hist_blk = lambda i: jnp.where(i % sb == 0, meta_row // HIST_ROWS, i * (tb // HIST_ROWS) - 1)
    h2_f, u2_f, info_f, cum_f, tot_f, state_f = pl.pallas_call(
        functools.partial(_pool_frames_kernel, n_fb),
        grid=(n_tb,),
        in_specs=[pl.BlockSpec((tb, d), lambda i: (i, 0)),
                  pl.BlockSpec((HIST_ROWS, d), lambda i: (hist_blk(i), 0))] + route_w + [_const_spec((tb, tb))],
        out_specs=[pl.BlockSpec((tb, d), lambda i: (i, 0)), pl.BlockSpec((tb, d), lambda i: (i, 0)),
                   pl.BlockSpec((tb, n_e), lambda i: (i, 0)),
                   pl.BlockSpec((1, 1, n_e), lambda i: (jnp.minimum(i, n_fb - 1), 0, 0)),
                   pl.BlockSpec((1, n_e), lambda i: (0, 0)),
                   pl.BlockSpec((1, HIST_ROWS, d), lambda i: (jnp.minimum(i // sb, nb - 1), 0, 0))],
        out_shape=[jax.ShapeDtypeStruct((nt, d), F32), jax.ShapeDtypeStruct((nt, d), BF16),
                   jax.ShapeDtypeStruct((nt, n_e), F32), jax.ShapeDtypeStruct((n_fb, 1, n_e), F32),
                   jax.ShapeDtypeStruct((1, n_e), F32), jax.ShapeDtypeStruct((nb, HIST_ROWS, d), F32)],
        scratch_shapes=[pltpu.VMEM((HIST_PAD + tb, d), F32), pltpu.VMEM((1, n_e), F32)] + stage_scratch(tb),
        compiler_params=_params(("arbitrary",)),
        name="pool_route_frames",
    )(h1, h1, *route_args, tri(tb))

    hist_s = jnp.concatenate([
        jnp.pad(state_pool[0].astype(F32), [(0, 0), (HIST_ROWS - POOL_HIST, 0), (0, 0)]),
        jnp.zeros((n_seq - db, HIST_ROWS, d), F32)], axis=0)
    any_spec = pl.BlockSpec(memory_space=pl.ANY)
    short_blk = lambda s: (nf // SEQ_S + s, 0)
    h2, u2, info, cum_s, tot_s, state_s = pl.pallas_call(
        functools.partial(_pool_short_kernel, db, n_valid_seq),
        grid=(n_seq,),
        in_specs=[any_spec, any_spec, any_spec,
                  pl.BlockSpec((SEQ_S, d), short_blk),
                  pl.BlockSpec((1, HIST_ROWS, d), lambda s: (s, 0, 0)),
                  _const_spec((1, n_e))] + route_w + [_const_spec((SEQ_S, SEQ_S))],
        out_specs=[pl.BlockSpec((SEQ_S, d), short_blk), pl.BlockSpec((SEQ_S, d), short_blk),
                   pl.BlockSpec((SEQ_S, n_e), short_blk), pl.BlockSpec((1, 1, n_e), lambda s: (s, 0, 0)),
                   pl.BlockSpec((1, n_e), lambda s: (0, 0)),
                   pl.BlockSpec((1, SEQ_S, d), lambda s: (s, 0, 0))],
        out_shape=[jax.ShapeDtypeStruct((nt, d), F32), jax.ShapeDtypeStruct((nt, d), BF16),
                   jax.ShapeDtypeStruct((nt, n_e), F32), jax.ShapeDtypeStruct((n_seq, 1, n_e), F32),
                   jax.ShapeDtypeStruct((1, n_e), F32), jax.ShapeDtypeStruct((n_seq, SEQ_S, d), F32)],
        scratch_shapes=[pltpu.VMEM((HIST_PAD + SEQ_S, d), F32), pltpu.VMEM((1, n_e), F32)] + stage_scratch(SEQ_S),
        input_output_aliases={0: 0, 1: 1, 2: 2},
        compiler_params=_params(("arbitrary",)),
        name="pool_route_short",
    )(h2_f, u2_f, info_f, h1, hist_s, tot_f, *route_args, tri(SEQ_S))

    counts = tot_s[0].astype(jnp.int32)
    cum = jnp.concatenate([cum_f[:, 0], cum_s[::tb // SEQ_S, 0], tot_s], axis=0).astype(jnp.int32)
    padded = (counts + rb - 1) // rb * rb
    pend = jnp.cumsum(padded)
    pstart = pend - padded
    n_rows_max = -(-(TOP_K * (nf + n_valid_seq * SEQ_S)) // rb) * rb + n_e * rb
    n_rb = n_rows_max // rb
    e_tok = info[:, 0:TOP_K].astype(jnp.int32)
    pos = jnp.where(e_tok >= 0, pstart[jnp.maximum(e_tok, 0)] + info[:, 2:2 + TOP_K].astype(jnp.int32), -1)
    gates = info[:, 4:4 + TOP_K]
    pos_l = pos.reshape(n_tb, tb, TOP_K).transpose(0, 2, 1)
    gate_l = gates.reshape(n_tb, tb, TOP_K).transpose(0, 2, 1)
    pos_c = jnp.pad(pos, [(0, 0), (0, n_e - TOP_K)], constant_values=-1)

    blk_row = jnp.arange(n_rb, dtype=jnp.int32) * rb
    blk_e = jnp.minimum(jnp.sum(pend[None, :] <= blk_row[:, None], axis=1), n_e - 1).astype(jnp.int32)
    blk_ok = (blk_row < pend[-1]).astype(jnp.int32)
    r_lo = blk_row - pstart[blk_e]
    r_hi = jnp.minimum(r_lo + rb, counts[blk_e])
    cum_b = cum.T[blk_e]
    first = jnp.sum(cum_b[:, 1:] <= r_lo[:, None], axis=1)
    last = jnp.sum(cum_b[:, :-1] < r_hi[:, None], axis=1) - 1
    blk_lo = jnp.clip(first, 0, n_tb - 1).astype(jnp.int32)
    blk_hi = jnp.clip(last, blk_lo, n_tb - 1).astype(jnp.int32)

    ys = pl.pallas_call(
        functools.partial(_expert_kernel, n_rb),
        grid_spec=pltpu.PrefetchScalarGridSpec(
            num_scalar_prefetch=4,
            grid=(n_rb,),
            in_specs=[pl.BlockSpec(memory_space=pl.ANY),
                      _const_spec((n_tb, TOP_K, tb)), _const_spec((n_tb, TOP_K, tb)),
                      pl.BlockSpec((1, d, d_e), lambda j, be, lo, hi, ok: (be[j], 0, 0)),
                      pl.BlockSpec((1, d, d_e), lambda j, be, lo, hi, ok: (be[j], 0, 0)),
                      pl.BlockSpec((1, d_e, d), lambda j, be, lo, hi, ok: (be[j], 0, 0))],
            out_specs=pl.BlockSpec((rb, d), lambda j, be, lo, hi, ok: (j, 0)),
            scratch_shapes=[pltpu.VMEM((GATHER_SLOTS, tb, d), BF16), pltpu.VMEM((rb, GATHER_SLOTS * tb), BF16),
                            pltpu.VMEM((rb, 1), F32), pltpu.VMEM((rb, d), F32), pltpu.VMEM((rb, 1), F32),
                            pltpu.SemaphoreType.DMA((GATHER_SLOTS,))]),
        out_shape=jax.ShapeDtypeStruct((n_rows_max, d), BF16),
        compiler_params=_params(("arbitrary",)),
        name="experts",
    )(blk_e, blk_lo, blk_hi, blk_ok, u2, pos_l, gate_l, wg_e, wu_e, wd_e)

    cr = COMBINE_ROWS
    per_expert = tb // cr + 1
    n_slots = TOP_K * tb // cr + 2 * n_e
    assert rb % cr == 0 and LANE % cr == 0 and n_slots % (LANE // cr) == 0
    w_lo = pstart[None, :] + cum[:-1]
    w_hi = pstart[None, :] + cum[1:]
    c_lo = w_lo // cr
    n_chunks = jnp.where(w_hi > w_lo, (w_hi - 1) // cr - c_lo + 1, 0)
    q = jnp.arange(per_expert)
    cand = (c_lo[:, :, None] + q).reshape(n_tb, n_e * per_expert)
    keep = (q < n_chunks[:, :, None]).reshape(n_tb, n_e * per_expert)
    order = jnp.argsort(~keep, axis=1, stable=True)[:, :n_slots]
    ids = jnp.where(jnp.take_along_axis(keep, order, axis=1), jnp.take_along_axis(cand, order, axis=1),
                    n_rows_max // cr)
    n_ids = jnp.sum(keep, axis=1).astype(jnp.int32)
    y_frames, y_short = pl.pallas_call(
        functools.partial(_combine_kernel, n_tb, n_fb),
        grid_spec=pltpu.PrefetchScalarGridSpec(
            num_scalar_prefetch=2,
            grid=(n_tb,),
            in_specs=[pl.BlockSpec((tb, d), lambda i, n, ids: (i, 0)),
                      pl.BlockSpec((tb, n_e), lambda i, n, ids: (i, 0)),
                      pl.BlockSpec((1, d), lambda i, n, ids: (0, 0)),
                      pl.BlockSpec(memory_space=pl.ANY)],
            out_specs=[pl.BlockSpec((tb, d), lambda i, n, ids: (jnp.minimum(i, n_fb - 1), 0)),
                       pl.BlockSpec((tb, d), lambda i, n, ids: (jnp.maximum(i - n_fb, 0), 0))],
            scratch_shapes=[pltpu.VMEM((2, n_slots, cr, d), BF16), pltpu.SemaphoreType.DMA((2 * n_slots,))]),
        out_shape=[jax.ShapeDtypeStruct((nf, d), F32), jax.ShapeDtypeStruct((ns, d), F32)],
        compiler_params=_params(("arbitrary",)),
        name="combine",
    )(n_ids, ids.reshape(-1).astype(jnp.int32), h2, pos_c, row(norm_final), ys)

    y_prompt = y_frames.reshape(nb, seq, d)
    y_sample = y_short[:db * SEQ_S].reshape(db, SEQ_S, d)

    def with_meta(frames, short, width):
        meta = jnp.broadcast_to(short[db * SEQ_S:db * SEQ_S + N_META][None], (nb, N_META, width))
        return jnp.concatenate([meta, frames.reshape(nb, seq, width)], axis=1)[None]

    c_p = with_meta(c_frames, c_short, kv_rank)
    r_p = with_meta(r_frames, r_short, QK_ROPE)
    c_s = c_short[:db * SEQ_S].reshape(1, db, SEQ_S, kv_rank)
    r_s = r_short[:db * SEQ_S].reshape(1, db, SEQ_S, QK_ROPE)
    s_p = state_f[:, HIST_ROWS - POOL_HIST:][None]
    s_s = state_s[:db, SEQ_S - POOL_HIST:][None]
    return (y_prompt, y_sample, c_p, r_p, s_p, c_s, r_s, s_s)
```

```python
import functools

import jax
import jax.numpy as jnp
from jax import lax
from jax.experimental import pallas as pl
from jax.experimental.pallas import tpu as pltpu

CHUNK = 64
N_META = 16
N_HEADS = 8
QK_NOPE = 128
QK_ROPE = 64
ROPE_THETA = 10000.0
POOL_WINDOWS = (2, 4, 8, 16)
POOL_HIST = max(POOL_WINDOWS) - 1
TOP_K = 2
RMS_EPS = 1e-6

LANE = 128
BF16_SUBLANES = 16
SEQ_S = 16
F32_SUBLANES = 8
HIST_ROWS = 16
HIST_PAD = F32_SUBLANES * len(POOL_WINDOWS)
TOKEN_BLOCK = 256
ROW_BLOCK = 256
ATTN_BLOCK = 256
CACHE_BLOCK = 2048
KEY_BLOCKS_PER_ITER = 4
SCORES_AHEAD = 8
COMBINE_ROWS = 64
GATHER_SLOTS = 6
V7X_VMEM_BYTES = 64 * 1024 * 1024
VMEM_LIMIT = V7X_VMEM_BYTES // 8 * 7

F32 = jnp.float32
BF16 = jnp.bfloat16
NEG_INF = float("-inf")
LOG2_E = 1.4426950408889634


def _dot(a, b):
    return jnp.dot(a, b, preferred_element_type=F32)


def _dot_nt(a, b):
    return lax.dot_general(a, b, (((1,), (1,)), ((), ())), preferred_element_type=F32)


def _rms(x, g):
    return x * lax.rsqrt(jnp.mean(x * x, axis=-1, keepdims=True) + RMS_EPS) * g


def _const_spec(shape):
    nd = len(shape)
    return pl.BlockSpec(shape, lambda *_: (0,) * nd, pipeline_mode=pl.Buffered(1))


def _params(sem):
    return pltpu.CompilerParams(dimension_semantics=sem, vmem_limit_bytes=VMEM_LIMIT)


def _qkv_kernel(n_fb, scale, xp_ref, xs_ref, tab_ref, tabt_ref, g_ref, wdq_ref, gq_ref, wuq_ref, wuqt_ref, wukt_ref,
                wdkv_ref, gkv_ref, wukf_ref, wuvt_ref, q_ref, qt_ref, kvb_ref, kn_ref, kr_ref, vt_ref, cp_ref, rp_ref,
                cs_ref, rs_ref):
    i = pl.program_id(0)
    x = jnp.where(i < n_fb, xp_ref[...], xs_ref[...])
    u = _rms(x, g_ref[...]).astype(BF16)
    cq = _rms(_dot(u, wdq_ref[...]), gq_ref[...]).astype(BF16)
    cos = tab_ref[:, :LANE]
    sin = tab_ref[:, LANE:]
    hq = QK_NOPE + 2 * LANE

    kv = _dot(u, wdkv_ref[...])
    c = _rms(kv[:, :2 * LANE], gkv_ref[...])
    r = kv[:, 2 * LANE:3 * LANE] * cos + kv[:, 3 * LANE:] * sin
    c_bf = c.astype(BF16)
    r_bf = r.astype(BF16)

    kn_ref[...] = _dot(c_bf, wukf_ref[...]).astype(BF16)
    kr_ref[...] = r_bf
    vt = _dot_nt(wuvt_ref[...], c_bf).astype(BF16)
    hv = vt.shape[0] // N_HEADS
    hv_ext = vt_ref.shape[1] // N_HEADS
    for h in range(N_HEADS):
        vt_ref[0, h * hv_ext:h * hv_ext + hv, :] = vt[h * hv:(h + 1) * hv, :]
        vt_ref[0, h * hv_ext + hv:(h + 1) * hv_ext, :] = jnp.ones((hv_ext - hv, vt.shape[1]), BF16)

    @pl.when(i < n_fb)
    def _():
        qat = _dot_nt(wuqt_ref[...], cq)
        cos_t = tabt_ref[:QK_ROPE, :]
        sin_t = tabt_ref[LANE:LANE + QK_ROPE, :]
        pe0 = N_HEADS * QK_NOPE
        sw0 = pe0 + N_HEADS * QK_ROPE
        for h in range(N_HEADS):
            qt_ref[0, h, :QK_NOPE, :] = (qat[h * QK_NOPE:(h + 1) * QK_NOPE, :] * scale).astype(BF16)
            a = qat[pe0 + h * QK_ROPE:pe0 + (h + 1) * QK_ROPE, :]
            b = qat[sw0 + h * QK_ROPE:sw0 + (h + 1) * QK_ROPE, :]
            qt_ref[0, h, QK_NOPE:QK_NOPE + QK_ROPE, :] = ((a * cos_t + b * sin_t) * scale).astype(BF16)
            qt_ref[0, h, QK_NOPE + QK_ROPE:, :] = jnp.zeros((LANE - QK_ROPE, qt_ref.shape[3]), BF16)
        cp_ref[...] = c
        rp_ref[...] = r[:, :QK_ROPE]

    @pl.when(i >= n_fb)
    def _():
        qa = _dot(cq, wuq_ref[...])
        for h in range(N_HEADS):
            qn = qa[:, h * LANE:(h + 1) * LANE].astype(BF16)
            q_ref[:, h * hq:h * hq + 2 * LANE] = (_dot(qn, wukt_ref[h]) * scale).astype(BF16)
            a = qa[:, (N_HEADS + h) * LANE:(N_HEADS + h + 1) * LANE]
            b = qa[:, (2 * N_HEADS + h) * LANE:(2 * N_HEADS + h + 1) * LANE]
            q_ref[:, h * hq + 2 * LANE:(h + 1) * hq] = ((a * cos + b * sin) * scale).astype(BF16)
        kvb_ref[:, :2 * LANE] = c_bf
        kvb_ref[:, 2 * LANE:] = r_bf
        cs_ref[...] = c
        rs_ref[...] = r[:, :QK_ROPE]


def _attn_frames_kernel(qb, meta_col, qt_ref, kn_ref, kr_ref, vt_ref, knm_ref, krm_ref, vtm_ref, o_ref, m_ref, acc_ref):
    j = pl.program_id(0) % qb
    tq = o_ref.shape[0]
    hn = kn_ref.shape[1] // N_HEADS
    hv = o_ref.shape[1] // N_HEADS
    hx = vt_ref.shape[1] // N_HEADS

    def head_keys(kn, kr, rows, h):
        return jnp.concatenate([kn[rows, h * hn:(h + 1) * hn], kr[rows, :]], axis=1)

    def scores(h, rows):
        return _dot(head_keys(kn_ref, kr_ref, rows, h), qt_ref[0, h])

    m_ref[...] = jnp.full(m_ref.shape, NEG_INF, F32)
    acc_ref[...] = jnp.zeros_like(acc_ref)

    def unmasked(kbs):
        steps = [(kb, h) for kb in kbs for h in range(N_HEADS)]
        block_scores = lambda kb, h: scores(h, pl.ds(pl.multiple_of(kb * tq, tq), tq))
        ahead = [block_scores(*step) for step in steps[:SCORES_AHEAD]]
        for n, (kb, h) in enumerate(steps):
            s = ahead.pop(0)
            if n + SCORES_AHEAD < len(steps):
                ahead.append(block_scores(*steps[n + SCORES_AHEAD]))
            m_old = m_ref[h]
            m_new = jnp.maximum(m_old, jnp.max(s, axis=0, keepdims=True))
            alpha = jnp.exp2(m_old - m_new)
            p = jnp.exp2(s - m_new)
            m_ref[h] = m_new
            acc_ref[h] = alpha * acc_ref[h] + _dot(vt_ref[kb, h * hx:(h + 1) * hx, :], p.astype(BF16))

    def body(t, carry):
        unmasked([KEY_BLOCKS_PER_ITER * t + u for u in range(KEY_BLOCKS_PER_ITER)])
        return carry

    lax.fori_loop(0, j // KEY_BLOCKS_PER_ITER, body, 0)
    for left in range(1, KEY_BLOCKS_PER_ITER):
        @pl.when(j % KEY_BLOCKS_PER_ITER == left)
        def _():
            unmasked([j - left + u for u in range(left)])

    rows = pl.ds(pl.multiple_of(j * tq, tq), tq)
    visible = (lax.broadcasted_iota(jnp.int32, (tq, tq), 0) // CHUNK
               <= lax.broadcasted_iota(jnp.int32, (tq, tq), 1) // CHUNK)

    def last_scores(h):
        return scores(h, rows), _dot(head_keys(knm_ref, krm_ref, slice(None), h), qt_ref[0, h])

    ahead = [last_scores(h) for h in range(SCORES_AHEAD)]
    for h in range(N_HEADS):
        s, s_meta = ahead.pop(0)
        s = jnp.where(visible, s, NEG_INF)
        if h + SCORES_AHEAD < N_HEADS:
            ahead.append(last_scores(h + SCORES_AHEAD))
        m_old = m_ref[h]
        m_new = jnp.maximum(m_old, jnp.maximum(jnp.max(s, axis=0, keepdims=True),
                                               jnp.max(s_meta, axis=0, keepdims=True)))
        alpha = jnp.exp2(m_old - m_new)
        p = jnp.exp2(s - m_new)
        p_meta = jnp.exp2(s_meta - m_new)
        acc = (alpha * acc_ref[h] + _dot(vt_ref[j, h * hx:(h + 1) * hx, :], p.astype(BF16))
               + _dot(vtm_ref[0, h * hx:(h + 1) * hx, meta_col:meta_col + N_META], p_meta.astype(BF16)))
        o_ref[:, h * hv:(h + 1) * hv] = (acc[:hv] * (1.0 / acc[hv:hv + 1])).T.astype(BF16)


def _attn_short_kernel(n_cached_seq, q_ref, kvn_ref, cc_ref, cr_ref, wuv_ref, o_ref):
    s_id = pl.program_id(0)
    hq = q_ref.shape[1] // N_HEADS
    n_cache = cc_ref.shape[2]
    qs = jnp.concatenate([q_ref[:, h * hq:(h + 1) * hq] for h in range(N_HEADS)], axis=0)
    q_lat = qs[:, :2 * LANE]
    q_pe = qs[:, 2 * LANE:2 * LANE + QK_ROPE]
    kvn = kvn_ref[...]
    s = _dot_nt(qs, kvn)
    m = jnp.max(s, axis=1, keepdims=True)
    p = jnp.exp2(s - m)
    l = jnp.sum(p, axis=1, keepdims=True)
    acc = _dot(p.astype(BF16), kvn[:, :2 * LANE])
    has_cache = s_id < n_cached_seq
    start = 0
    while start < n_cache:
        size = min(CACHE_BLOCK, n_cache - start)
        ck = cc_ref[0, 0, start:start + size, :].astype(BF16)
        rk = cr_ref[0, 0, start:start + size, :].astype(BF16)
        s = _dot_nt(q_lat, ck) + _dot_nt(q_pe, rk)
        s = jnp.where(has_cache, s, NEG_INF)
        m_new = jnp.maximum(m, jnp.max(s, axis=1, keepdims=True))
        alpha = jnp.exp2(m - m_new)
        p = jnp.exp2(s - m_new)
        l = alpha * l + jnp.sum(p, axis=1, keepdims=True)
        acc = alpha * acc + _dot(p.astype(BF16), ck)
        m = m_new
        start += size
    o = (acc / l).astype(BF16)
    hv = wuv_ref.shape[2]
    for h in range(N_HEADS):
        o_ref[:, h * hv:(h + 1) * hv] = _dot(o[h * SEQ_S:(h + 1) * SEQ_S, :], wuv_ref[h]).astype(BF16)


def _proj_ffn_kernel(n_fb, xp_ref, xs_ref, op_ref, os_ref, wo_ref, g_ref, wg_ref, wu_ref, wd_ref, *cast_refs):
    n_cast = (len(cast_refs) - 1) // 2
    h_ref = cast_refs[n_cast]
    for src, dst in zip(cast_refs[:n_cast], cast_refs[n_cast + 1:]):
        dst[...] = src[...].astype(BF16)
    i = pl.program_id(0)
    x = jnp.where(i < n_fb, xp_ref[...], xs_ref[...])
    o = jnp.where(i < n_fb, op_ref[...], os_ref[...])
    h1 = x + _dot(o, wo_ref[...])
    u = _rms(h1, g_ref[...]).astype(BF16)
    act = jax.nn.silu(_dot(u, wg_ref[...])) * _dot(u, wu_ref[...])
    h_ref[...] = h1 + _dot(act.astype(BF16), wd_ref[...])


def _set_history(ext_ref, hist):
    for start in range(0, HIST_PAD, HIST_ROWS):
        ext_ref[start:start + HIST_ROWS, :] = hist


def _window_sums(ext_ref, stage_refs, t):
    grp = ext_ref.shape[1] // len(POOL_WINDOWS)
    end = HIST_PAD + t
    first = F32_SUBLANES
    s = ext_ref[first:end, :] + ext_ref[first - 1:end - 1, :]
    sums = [s[HIST_PAD - first:, :grp]]
    for ref, shift in zip(stage_refs, POOL_WINDOWS):
        rest = s[:, grp:]
        ref[first:end, :] = rest
        first += F32_SUBLANES
        s = rest[F32_SUBLANES:, :] + ref[first - shift:end - shift, :]
        sums.append(s[HIST_PAD - first:, :grp])
    return sums


def _pool_route(h1, ext_ref, stage_refs, cnt_rows, valid, base, gm_ref, pw_ref, ps_ref, gf_ref, wr_ref, br_ref,
                tri_ref):
    t = h1.shape[0]
    grp = h1.shape[1] // len(POOL_WINDOWS)
    u1 = _rms(h1, gm_ref[...])
    ext_ref[HIST_PAD:HIST_PAD + t, :] = u1
    sums = _window_sums(ext_ref, stage_refs, t)
    ys = []
    for g, w in enumerate(POOL_WINDOWS):
        cnt = float(w) if cnt_rows is None else jnp.minimum(float(w), cnt_rows)
        mean = sums[g] / cnt - u1[:, g * grp:(g + 1) * grp]
        ys.append(_dot(mean.astype(BF16), pw_ref[g]))
    h2 = h1 + jnp.concatenate(ys, axis=1) * ps_ref[...]
    u2 = _rms(h2, gf_ref[...])

    n_e = wr_ref.shape[1] // 2
    u2_hi = u2.astype(BF16)
    u2_lo = (u2 - u2_hi.astype(F32)).astype(BF16)
    parts = _dot(u2_hi, wr_ref[...]) + _dot(u2_lo, wr_ref[...])
    logits = parts[:, :n_e] + parts[:, n_e:] + br_ref[...]
    e_iota = lax.broadcasted_iota(jnp.int32, logits.shape, 1).astype(F32)
    v0 = jnp.max(logits, axis=1, keepdims=True)
    e0 = jnp.min(jnp.where(logits == v0, e_iota, float(n_e)), axis=1, keepdims=True)
    rest = jnp.where(e_iota == e0, NEG_INF, logits)
    v1 = jnp.max(rest, axis=1, keepdims=True)
    e1 = jnp.min(jnp.where(rest == v1, e_iota, float(n_e)), axis=1, keepdims=True)
    tt = jnp.exp(v1 - v0)
    g0 = 1.0 / (1.0 + tt)
    g1 = tt / (1.0 + tt)
    hit0 = e_iota == e0
    hit1 = e_iota == e1
    onehot = jnp.where(valid & (hit0 | hit1), 1.0, 0.0)
    rank = _dot(tri_ref[...], onehot.astype(BF16)) + base
    r0 = jnp.sum(jnp.where(hit0, rank, 0.0), axis=1, keepdims=True)
    r1 = jnp.sum(jnp.where(hit1, rank, 0.0), axis=1, keepdims=True)
    cols = (jnp.where(valid, e0, -1).astype(F32), jnp.where(valid, e1, -1).astype(F32), r0, r1, g0, g1)
    info = jnp.zeros(logits.shape, F32)
    for k, col in enumerate(cols):
        info = jnp.where(e_iota == k, col, info)
    new_base = base + jnp.sum(onehot, axis=0, keepdims=True)
    return h2, u2, info, new_base, u1


def _pool_frames_kernel(n_fb, h_ref, hist_ref, gm_ref, pw_ref, ps_ref, gf_ref, wr_ref, br_ref, tri_ref,
                        h2_ref, u2_ref, info_ref, cum_ref, tot_ref, state_ref, ext_ref, base_ref, *stage_refs):
    i = pl.program_id(0)

    @pl.when(i == 0)
    def _():
        base_ref[...] = jnp.zeros_like(base_ref)

    @pl.when(i >= n_fb)
    def _():
        h2_ref[...] = jnp.zeros_like(h2_ref)
        u2_ref[...] = jnp.zeros_like(u2_ref)
        info_ref[...] = jnp.zeros_like(info_ref)

    @pl.when(i < n_fb)
    def _():
        _set_history(ext_ref, _rms(hist_ref[...], gm_ref[...]))
        base = base_ref[...]
        cum_ref[0] = base
        h2, u2, info, new_base, u1 = _pool_route(h_ref[...], ext_ref, stage_refs, None, True, base, gm_ref, pw_ref,
                                                 ps_ref, gf_ref, wr_ref, br_ref, tri_ref)
        h2_ref[...] = h2
        u2_ref[...] = u2.astype(BF16)
        info_ref[...] = info
        base_ref[...] = new_base
        tot_ref[...] = new_base
        state_ref[0] = u1[u1.shape[0] - HIST_ROWS:, :]


def _pool_short_kernel(n_sample, n_valid, h2_hbm, u2_hbm, info_hbm, h_ref, hist_ref, base0_ref, gm_ref, pw_ref, ps_ref,
                       gf_ref, wr_ref, br_ref, tri_ref, h2_ref, u2_ref, info_ref, cum_ref, tot_ref, state_ref,
                       ext_ref, base_ref, *stage_refs):
    del h2_hbm, u2_hbm, info_hbm
    s_id = pl.program_id(0)

    @pl.when(s_id == 0)
    def _():
        base_ref[...] = base0_ref[...]

    _set_history(ext_ref, hist_ref[0])
    base = base_ref[...]
    cum_ref[0] = base
    pos1 = (lax.broadcasted_iota(jnp.int32, (SEQ_S, 1), 0) + 1).astype(F32)
    cnt_rows = jnp.where(s_id < n_sample, float(max(POOL_WINDOWS)), pos1)
    h2, u2, info, new_base, u1 = _pool_route(h_ref[...], ext_ref, stage_refs, cnt_rows, s_id < n_valid, base, gm_ref,
                                             pw_ref, ps_ref, gf_ref, wr_ref, br_ref, tri_ref)
    h2_ref[...] = h2
    u2_ref[...] = u2.astype(BF16)
    info_ref[...] = info
    base_ref[...] = new_base
    tot_ref[...] = new_base
    state_ref[0] = u1


def _expert_kernel(n_blocks, be_ref, lo_ref, hi_ref, ok_ref, u_hbm, pos_ref, gate_ref, wg_ref, wu_ref, wd_ref, ys_ref,
                   buf_ref, sel_ref, gsel_ref, acc_ref, gacc_ref, sem):
    del be_ref
    j = pl.program_id(0)
    rb = ys_ref.shape[0]
    n_slots, tc, _ = buf_ref.shape
    last_chunk = pos_ref.shape[0] - 1

    def n_chunks(jj):
        return jnp.where(ok_ref[jj] != 0, hi_ref[jj] - lo_ref[jj] + 1, 0)

    def chunk_copy(c, slot):
        return pltpu.make_async_copy(u_hbm.at[pl.ds(pl.multiple_of(c * tc, tc), tc), :], buf_ref.at[slot],
                                     sem.at[slot])

    def request(jj):
        n_req = jnp.minimum(n_chunks(jj), n_slots)
        for k in range(n_slots):
            @pl.when(k < n_req)
            def _():
                chunk_copy(lo_ref[jj] + k, k).start()

    def select(jj, c, live):
        rows = jj * rb + lax.broadcasted_iota(jnp.int32, (rb, tc), 0)
        pos = pos_ref[c]
        gate = gate_ref[c]
        m0 = (rows == pos[0:1, :]) & live
        m1 = (rows == pos[1:2, :]) & live
        sel = jnp.where(m0 | m1, 1.0, 0.0).astype(BF16)
        row_gate = jnp.sum(jnp.where(m0, gate[0:1, :], 0.0) + jnp.where(m1, gate[1:2, :], 0.0), axis=1, keepdims=True)
        return sel, row_gate

    def build_selection(jj):
        n_req = jnp.minimum(n_chunks(jj), n_slots)
        row_gate = jnp.zeros(gsel_ref.shape, F32)
        for k in range(n_slots):
            sel, g = select(jj, jnp.minimum(lo_ref[jj] + k, last_chunk), k < n_req)
            sel_ref[:, k * tc:(k + 1) * tc] = sel
            row_gate = row_gate + g
        gsel_ref[...] = row_gate

    @pl.when(j == 0)
    def _():
        buf_ref[...] = jnp.zeros_like(buf_ref)
        request(0)
        build_selection(0)

    n = n_chunks(j)
    lo = lo_ref[j]

    @pl.when(n > 0)
    def _():
        for k in range(n_slots):
            @pl.when(k < jnp.minimum(n, n_slots))
            def _():
                chunk_copy(lo + k, k).wait()

        acc_ref[...] = _dot(sel_ref[...], buf_ref[...].reshape(n_slots * tc, -1))
        gacc_ref[...] = gsel_ref[...]

        def overflow(k, carry):
            copy = chunk_copy(lo + k, 0)
            copy.start()
            copy.wait()
            sel, row_gate = select(j, lo + k, True)
            acc_ref[...] += _dot(sel, buf_ref[0])
            gacc_ref[...] += row_gate
            return carry

        lax.fori_loop(n_slots, jnp.maximum(n, n_slots), overflow, 0)

    @pl.when(j + 1 < n_blocks)
    def _():
        request(jnp.minimum(j + 1, n_blocks - 1))

    @pl.when(n > 0)
    def _():
        xg = acc_ref[...].astype(BF16)
        act = jax.nn.silu(_dot(xg, wg_ref[0])) * _dot(xg, wu_ref[0])
        ys_ref[...] = (_dot(act.astype(BF16), wd_ref[0]) * gacc_ref[...]).astype(BF16)
        build_selection(jnp.minimum(j + 1, n_blocks - 1))

    @pl.when(n == 0)
    def _():
        ys_ref[...] = jnp.zeros_like(ys_ref)


def _combine_kernel(n_blocks, n_fb, n_ref, ids_ref, h2_ref, pos_ref, g_ref, ys_hbm, yp_ref, yshort_ref, buf_ref, sem):
    i = pl.program_id(0)
    _, n_slots, cr, _ = buf_ref.shape
    per_vreg = LANE // cr
    par = i % 2
    nxt = jnp.minimum(i + 1, n_blocks - 1)

    def chunk_copy(ii, k, which):
        ch = ids_ref[ii * n_slots + k]
        return pltpu.make_async_copy(ys_hbm.at[pl.ds(pl.multiple_of(ch * cr, cr), cr), :], buf_ref.at[which, k],
                                     sem.at[which * n_slots + k])

    def request(ii, which):
        for k in range(n_slots):
            @pl.when(k < n_ref[ii])
            def _():
                chunk_copy(ii, k, which).start()

    @pl.when(i == 0)
    def _():
        buf_ref[...] = jnp.zeros_like(buf_ref)
        request(0, 0)

    @pl.when(i + 1 < n_blocks)
    def _():
        request(nxt, 1 - par)

    lane = lax.broadcasted_iota(jnp.int32, (1, LANE), 1)
    pieces = []
    for v in range(n_slots // per_vreg):
        row = jnp.zeros((1, LANE), jnp.int32)
        for q in range(per_vreg):
            k = v * per_vreg + q
            row = jnp.where(lane // cr == q, ids_ref[i * n_slots + k] * cr + lane % cr, row)
        pieces.append(row)
    slot_rows = jnp.concatenate(pieces, axis=1)
    sel = jnp.where((slot_rows == pos_ref[:, 0:1]) | (slot_rows == pos_ref[:, 1:2]), 1.0, 0.0).astype(BF16)

    for k in range(n_slots):
        @pl.when(k < n_ref[i])
        def _():
            chunk_copy(i, k, par).wait()

    moe = _dot(sel, buf_ref[par].reshape(n_slots * cr, -1))
    y = _rms(h2_ref[...] + moe, g_ref[...])

    @pl.when(i < n_fb)
    def _():
        yp_ref[...] = y

    @pl.when(i >= n_fb)
    def _():
        yshort_ref[...] = y


def _rope_table(pos):
    half = QK_ROPE // 2
    inv = ROPE_THETA ** (-jnp.arange(half, dtype=F32) / half)
    ang = pos.astype(F32)[:, None] * inv[None, :]
    cos = jnp.cos(ang)
    sin = jnp.sin(ang)
    zero = jnp.zeros((pos.shape[0], LANE - QK_ROPE), F32)
    return jnp.concatenate([cos, cos, zero, -sin, sin, zero], axis=1)


def _half_swap(w):
    half = QK_ROPE // 2
    return jnp.concatenate([w[..., half:], w[..., :half]], axis=-1)


def _pad_last(w, width):
    return jnp.pad(w, [(0, 0)] * (w.ndim - 1) + [(0, width - w.shape[-1])])


def kernel(x_prompt, x_sample, cache_kv_latent, cache_k_rope, state_pool, meta_tokens, norm_mix, norm_ffn, norm_final,
           mla_w_dq, mla_g_q, mla_w_uq, mla_w_dkv, mla_g_kv, mla_w_uk, mla_w_uv, mla_w_o, pool_w, pool_scale,
           ffn_w_gate, ffn_w_up, ffn_w_down, moe_w_router, moe_b_router, moe_w_gate, moe_w_up, moe_w_down):
    nb, seq, d = x_prompt.shape
    db, dseq, _ = x_sample.shape
    n_cache = cache_kv_latent.shape[2]
    q_rank = mla_w_dq.shape[2]
    kv_rank = mla_g_kv.shape[1]
    v_dim = mla_w_uv.shape[3]
    d_ff = ffn_w_gate.shape[2]
    n_e = moe_w_router.shape[2]
    d_e = moe_w_gate.shape[3]
    tb, rb, tq = TOKEN_BLOCK, ROW_BLOCK, ATTN_BLOCK
    assert norm_mix.shape[0] == 2 and cache_kv_latent.shape[0] == 1 and state_pool.shape[0] == 1
    assert dseq == SEQ_S and N_META == SEQ_S and meta_tokens.shape[0] == N_META
    assert POOL_WINDOWS == tuple(2 ** (k + 1) for k in range(len(POOL_WINDOWS))) and HIST_PAD % HIST_ROWS == 0
    assert kv_rank == 2 * LANE and QK_NOPE == LANE and QK_ROPE <= LANE and HIST_ROWS >= POOL_HIST
    assert seq % tb == 0 and tb == tq and tq % CHUNK == 0 and tb % SEQ_S == 0 and d % LANE == 0
    assert (n_cache - N_META) % CHUNK == 0 and dseq <= CHUNK

    nf = nb * seq
    n_valid_seq = db + 1
    ns = -(-(n_valid_seq * SEQ_S) // tb) * tb
    n_seq = ns // SEQ_S
    nt = nf + ns
    n_fb, n_tb = nf // tb, nt // tb
    sb = seq // tb
    meta_row = nf + db * SEQ_S
    scale = float((QK_NOPE + QK_ROPE) ** -0.5)
    hq = 3 * LANE

    xp = x_prompt.reshape(nf, d)
    xs = jnp.concatenate([x_sample.reshape(db * SEQ_S, d), meta_tokens.astype(x_prompt.dtype),
                          jnp.zeros((ns - n_valid_seq * SEQ_S, d), x_prompt.dtype)], axis=0)

    t_s = jnp.arange(SEQ_S)
    pos_short = jnp.concatenate([jnp.tile(n_cache + t_s, db), jnp.tile(t_s, n_seq - db)])
    tab_frames = _rope_table(N_META + jnp.arange(seq))
    tab = jnp.concatenate([tab_frames, _rope_table(pos_short)], axis=0)

    wuq = mla_w_uq[0].reshape(q_rank, N_HEADS, QK_NOPE + QK_ROPE)
    wuq_pe = wuq[:, :, QK_NOPE:]
    wuq2 = jnp.concatenate([wuq[:, :, :QK_NOPE].reshape(q_rank, -1),
                            _pad_last(wuq_pe, LANE).reshape(q_rank, -1),
                            _pad_last(_half_swap(wuq_pe), LANE).reshape(q_rank, -1)], axis=1).astype(BF16)
    wuq_t = jnp.concatenate([wuq[:, :, :QK_NOPE].reshape(q_rank, -1), wuq_pe.reshape(q_rank, -1),
                             _half_swap(wuq_pe).reshape(q_rank, -1)], axis=1).T.astype(BF16)
    wdkv_r =mla_w_dkv[0][:, kv_rank:]
    wdkv2 = jnp.concatenate([mla_w_dkv[0][:, :kv_rank], _pad_last(wdkv_r, LANE),
                             _pad_last(_half_swap(wdkv_r), LANE)], axis=1).astype(BF16)
    wuk_t = jnp.transpose(mla_w_uk[0], (1, 2, 0)).astype(BF16)
    wuk_flat = mla_w_uk[0].reshape(kv_rank, N_HEADS * QK_NOPE).astype(BF16)
    wuv = jnp.transpose(mla_w_uv[0], (1, 0, 2)).astype(BF16)
    wuv_t = mla_w_uv[0].reshape(kv_rank, N_HEADS * v_dim).T.astype(BF16)
    row = lambda v: v.reshape(1, -1)
    hk = 2 * LANE
    hx = v_dim + BF16_SUBLANES

    tok_p = pl.BlockSpec((tb, d), lambda i: (jnp.minimum(i, n_fb - 1), 0))
    tok_s = pl.BlockSpec((tb, d), lambda i: (jnp.maximum(i - n_fb, 0), 0))
    short_tok = lambda i: (jnp.maximum(i - n_fb, 0), 0)
    frame_tok = lambda i: (jnp.minimum(i, n_fb - 1), 0)
    q_short, qt_frames, kvb_short, kn_all, kr_all, vt_all, c_frames, r_frames, c_short, r_short = pl.pallas_call(
        functools.partial(_qkv_kernel, n_fb, scale * LOG2_E),
        grid=(n_tb,),
        in_specs=[tok_p, tok_s,
                  pl.BlockSpec((tb, 2 * LANE), lambda i: (jnp.where(i < n_fb, i % sb, sb + i - n_fb), 0)),
                  pl.BlockSpec((2 * LANE, tb), lambda i: (0, i % sb)),
                  _const_spec((1, d)), _const_spec((d, q_rank)), _const_spec((1, q_rank)),
                  _const_spec(wuq2.shape), _const_spec(wuq_t.shape), _const_spec(wuk_t.shape),
                  _const_spec(wdkv2.shape), _const_spec((1, kv_rank)), _const_spec(wuk_flat.shape),
                  _const_spec(wuv_t.shape)],
        out_specs=[pl.BlockSpec((tb, N_HEADS * hq), short_tok),
                   pl.BlockSpec((1, N_HEADS, hk, tb), lambda i: (jnp.minimum(i, n_fb - 1), 0, 0, 0)),
                   pl.BlockSpec((tb, hq), short_tok),
                   pl.BlockSpec((tb, N_HEADS * QK_NOPE), lambda i: (i, 0)),
                   pl.BlockSpec((tb, LANE), lambda i: (i, 0)),
                   pl.BlockSpec((1, N_HEADS * hx, tb), lambda i: (i, 0, 0)),
                   pl.BlockSpec((tb, kv_rank), frame_tok), pl.BlockSpec((tb, QK_ROPE), frame_tok),
                   pl.BlockSpec((tb, kv_rank), short_tok), pl.BlockSpec((tb, QK_ROPE), short_tok)],
        out_shape=[jax.ShapeDtypeStruct((ns, N_HEADS * hq), BF16),
                   jax.ShapeDtypeStruct((n_fb, N_HEADS, hk, tb), BF16),
                   jax.ShapeDtypeStruct((ns, hq), BF16),
                   jax.ShapeDtypeStruct((nt, N_HEADS * QK_NOPE), BF16), jax.ShapeDtypeStruct((nt, LANE), BF16),
                   jax.ShapeDtypeStruct((n_tb, N_HEADS * hx, tb), BF16),
                   jax.ShapeDtypeStruct((nf, kv_rank), F32), jax.ShapeDtypeStruct((nf, QK_ROPE), F32),
                   jax.ShapeDtypeStruct((ns, kv_rank), F32), jax.ShapeDtypeStruct((ns, QK_ROPE), F32)],
        compiler_params=_params(("arbitrary",)),
        name="qkv",
    )(xp, xs, tab, tab_frames.T, row(norm_mix[0]), mla_w_dq[0].astype(BF16), row(mla_g_q[0]), wuq2, wuq_t, wuk_t,
      wdkv2, row(mla_g_kv[0]), wuk_flat, wuv_t)

    qb = seq // tq
    once = pl.Buffered(1)
    o_frames = pl.pallas_call(
        functools.partial(_attn_frames_kernel, qb, meta_row % tb),
        grid=(n_fb,),
        in_specs=[pl.BlockSpec((1, N_HEADS, hk, tq), lambda i: (i, 0, 0, 0)),
                  pl.BlockSpec((seq, N_HEADS * QK_NOPE), lambda i: (i // qb, 0)),
                  pl.BlockSpec((seq, LANE), lambda i: (i // qb, 0)),
                  pl.BlockSpec((qb, N_HEADS * hx, tq), lambda i: (i // qb, 0, 0)),
                  pl.BlockSpec((N_META, N_HEADS * QK_NOPE), lambda i: (meta_row // N_META, 0), pipeline_mode=once),
                  pl.BlockSpec((N_META, LANE), lambda i: (meta_row // N_META, 0), pipeline_mode=once),
                  pl.BlockSpec((1, N_HEADS * hx, tb), lambda i: (meta_row // tb, 0, 0), pipeline_mode=once)],
        out_specs=pl.BlockSpec((tq, N_HEADS * v_dim), lambda i: (i, 0)),
        out_shape=jax.ShapeDtypeStruct((nf, N_HEADS * v_dim), BF16),
        scratch_shapes=[pltpu.VMEM((N_HEADS, 1, tq), F32), pltpu.VMEM((N_HEADS, hx, tq), F32)],
        compiler_params=_params(("parallel",)),
        name="attn_frames",
    )(qt_frames, kn_all, kr_all, vt_all, kn_all, kr_all, vt_all)

    o_short = pl.pallas_call(
        functools.partial(_attn_short_kernel, db),
        grid=(n_seq,),
        in_specs=[pl.BlockSpec((SEQ_S, N_HEADS * hq), lambda s: (s, 0)),
                  pl.BlockSpec((SEQ_S, hq), lambda s: (s, 0)),
                  pl.BlockSpec((1, 1, n_cache, kv_rank), lambda s: (0, jnp.minimum(s, db - 1), 0, 0)),
                  pl.BlockSpec((1, 1, n_cache, QK_ROPE), lambda s: (0, jnp.minimum(s, db - 1), 0, 0)),
                  _const_spec(wuv.shape)],
        out_specs=pl.BlockSpec((SEQ_S, N_HEADS * v_dim), lambda s: (s, 0)),
        out_shape=jax.ShapeDtypeStruct((ns, N_HEADS * v_dim), BF16),
        compiler_params=_params(("parallel",)),
        name="attn_short",
    )(q_short, kvb_short, cache_kv_latent, cache_k_rope, wuv)

    moe_f32 = [moe_w_gate[0].reshape(n_e * d, d_e), moe_w_up[0].reshape(n_e * d, d_e),
               moe_w_down[0].reshape(n_e * d_e, d)]
    cast_in, cast_out, cast_shapes = [], [], []
    for w in moe_f32:
        steps = max(s for s in range(1, n_tb + 1) if w.shape[0] % s == 0 and (w.shape[0] // s) % BF16_SUBLANES == 0)
        spec = pl.BlockSpec((w.shape[0] // steps, w.shape[1]), lambda i, last=steps - 1: (jnp.minimum(i, last), 0))
        cast_in.append(spec)
        cast_out.append(spec)
        cast_shapes.append(jax.ShapeDtypeStruct(w.shape, BF16))
    h1, wg_e, wu_e, wd_e = pl.pallas_call(
        functools.partial(_proj_ffn_kernel, n_fb),
        grid=(n_tb,),
        in_specs=[tok_p, tok_s,
                  pl.BlockSpec((tb, N_HEADS * v_dim), lambda i: (jnp.minimum(i, n_fb - 1), 0)),
                  pl.BlockSpec((tb, N_HEADS * v_dim), short_tok),
                  _const_spec((N_HEADS * v_dim, d)), _const_spec((1, d)),
                  _const_spec((d, d_ff)), _const_spec((d, d_ff)), _const_spec((d_ff, d))] + cast_in,
        out_specs=[pl.BlockSpec((tb, d), lambda i: (i, 0))] + cast_out,
        out_shape=[jax.ShapeDtypeStruct((nt, d), F32)] + cast_shapes,
        compiler_params=_params(("arbitrary",)),
        name="proj_ffn",
    )(xp, xs, o_frames, o_short, mla_w_o[0].astype(BF16), row(norm_ffn[0]),
      ffn_w_gate[0].astype(BF16), ffn_w_up[0].astype(BF16), ffn_w_down[0].astype(BF16), *moe_f32)
    wg_e = wg_e.reshape(n_e, d, d_e)
    wu_e = wu_e.reshape(n_e, d, d_e)
    wd_e = wd_e.reshape(n_e, d_e, d)

    route_w = [_const_spec((1, d)), _const_spec(pool_w.shape[1:]), _const_spec((1, d)), _const_spec((1, d)),
               _const_spec((d, 2 * n_e)), _const_spec((1, n_e))]
    wr_hi = moe_w_router[0].astype(BF16)
    wr_lo = (moe_w_router[0].astype(F32) - wr_hi.astype(F32)).astype(BF16)
    route_args = (row(norm_mix[1]), pool_w[0].astype(BF16), row(pool_scale[0]), row(norm_ffn[1]),
                  jnp.concatenate([wr_hi, wr_lo], axis=1), row(moe_b_router[0]))
    tri = lambda n: (jnp.arange(n)[:, None] > jnp.arange(n)[None, :]).astype(BF16)
    n_win = len(POOL_WINDOWS)
    stage_scratch = lambda t: [pltpu.VMEM((HIST_PAD + t, d // n_win * (n_win - 1 - k)), F32) for k in range(n_win - 1)]
    hist_blk = lambda i: jnp.where(i % sb == 0, meta_row // HIST_ROWS, i * (tb // HIST_ROWS) - 1)
    h2_f, u2_f, info_f, cum_f, tot_f, state_f = pl.pallas_call(
        functools.partial(_pool_frames_kernel, n_fb),
        grid=(n_tb,),
        in_specs=[pl.BlockSpec((tb, d), lambda i: (i, 0)),
                  pl.BlockSpec((HIST_ROWS, d), lambda i: (hist_blk(i), 0))] + route_w + [_const_spec((tb, tb))],
        out_specs=[pl.BlockSpec((tb, d), lambda i: (i, 0)), pl.BlockSpec((tb, d), lambda i: (i, 0)),
                   pl.BlockSpec((tb, n_e), lambda i: (i, 0)),
                   pl.BlockSpec((1, 1, n_e), lambda i: (jnp.minimum(i, n_fb - 1), 0, 0)),
                   pl.BlockSpec((1, n_e), lambda i: (0, 0)),
                   pl.BlockSpec((1, HIST_ROWS, d), lambda i: (jnp.minimum(i // sb, nb - 1), 0, 0))],
        out_shape=[jax.ShapeDtypeStruct((nt, d), F32), jax.ShapeDtypeStruct((nt, d), BF16),
                   jax.ShapeDtypeStruct((nt, n_e), F32), jax.ShapeDtypeStruct((n_fb, 1, n_e), F32),
                   jax.ShapeDtypeStruct((1, n_e), F32), jax.ShapeDtypeStruct((nb, HIST_ROWS, d), F32)],
        scratch_shapes=[pltpu.VMEM((HIST_PAD + tb, d), F32), pltpu.VMEM((1, n_e), F32)] + stage_scratch(tb),
        compiler_params=_params(("arbitrary",)),
        name="pool_route_frames",
    )(h1, h1, *route_args, tri(tb))

    hist_s = jnp.concatenate([
        jnp.pad(state_pool[0].astype(F32), [(0, 0), (HIST_ROWS - POOL_HIST, 0), (0, 0)]),
        jnp.zeros((n_seq - db, HIST_ROWS, d), F32)], axis=0)
    any_spec = pl.BlockSpec(memory_space=pl.ANY)
    short_blk = lambda s: (nf // SEQ_S + s, 0)
    h2, u2, info, cum_s, tot_s, state_s = pl.pallas_call(
        functools.partial(_pool_short_kernel, db, n_valid_seq),
        grid=(n_seq,),
        in_specs=[any_spec, any_spec, any_spec,
                  pl.BlockSpec((SEQ_S, d), short_blk),
                  pl.BlockSpec((1, HIST_ROWS, d), lambda s: (s, 0, 0)),
                  _const_spec((1, n_e))] + route_w + [_const_spec((SEQ_S, SEQ_S))],
        out_specs=[pl.BlockSpec((SEQ_S, d), short_blk), pl.BlockSpec((SEQ_S, d), short_blk),
                   pl.BlockSpec((SEQ_S, n_e), short_blk), pl.BlockSpec((1, 1, n_e), lambda s: (s, 0, 0)),
                   pl.BlockSpec((1, n_e), lambda s: (0, 0)),
                   pl.BlockSpec((1, SEQ_S, d), lambda s: (s, 0, 0))],
        out_shape=[jax.ShapeDtypeStruct((nt, d), F32), jax.ShapeDtypeStruct((nt, d), BF16),
                   jax.ShapeDtypeStruct((nt, n_e), F32), jax.ShapeDtypeStruct((n_seq, 1, n_e), F32),
                   jax.ShapeDtypeStruct((1, n_e), F32), jax.ShapeDtypeStruct((n_seq, SEQ_S, d), F32)],
        scratch_shapes=[pltpu.VMEM((HIST_PAD + SEQ_S, d), F32), pltpu.VMEM((1, n_e), F32)] + stage_scratch(SEQ_S),
        input_output_aliases={0: 0, 1: 1, 2: 2},
        compiler_params=_params(("arbitrary",)),
        name="pool_route_short",
    )(h2_f, u2_f, info_f, h1, hist_s, tot_f, *route_args, tri(SEQ_S))

    counts = tot_s[0].astype(jnp.int32)
    cum = jnp.concatenate([cum_f[:, 0], cum_s[::tb // SEQ_S, 0], tot_s], axis=0).astype(jnp.int32)
    padded = (counts + rb - 1) // rb * rb
    pend = jnp.cumsum(padded)
    pstart = pend - padded
    n_rows_max = -(-(TOP_K * (nf + n_valid_seq * SEQ_S)) // rb) * rb + n_e * rb
    n_rb = n_rows_max // rb
    e_tok = info[:, 0:TOP_K].astype(jnp.int32)
    pos = jnp.where(e_tok >= 0, pstart[jnp.maximum(e_tok, 0)] + info[:, 2:2 + TOP_K].astype(jnp.int32), -1)
    gates = info[:, 4:4 + TOP_K]
    pos_l = pos.reshape(n_tb, tb, TOP_K).transpose(0, 2, 1)
    gate_l = gates.reshape(n_tb, tb, TOP_K).transpose(0, 2, 1)
    pos_c = jnp.pad(pos, [(0, 0), (0, n_e - TOP_K)], constant_values=-1)

    blk_row = jnp.arange(n_rb, dtype=jnp.int32) * rb
    blk_e = jnp.minimum(jnp.sum(pend[None, :] <= blk_row[:, None], axis=1), n_e - 1).astype(jnp.int32)
    blk_ok = (blk_row < pend[-1]).astype(jnp.int32)
    r_lo = blk_row - pstart[blk_e]
    r_hi = jnp.minimum(r_lo + rb, counts[blk_e])
    cum_b = cum.T[blk_e]
    first = jnp.sum(cum_b[:, 1:] <= r_lo[:, None], axis=1)
    last = jnp.sum(cum_b[:, :-1] < r_hi[:, None], axis=1) - 1
    blk_lo = jnp.clip(first, 0, n_tb - 1).astype(jnp.int32)
    blk_hi = jnp.clip(last, blk_lo, n_tb - 1).astype(jnp.int32)

    ys = pl.pallas_call(
        functools.partial(_expert_kernel, n_rb),
        grid_spec=pltpu.PrefetchScalarGridSpec(
            num_scalar_prefetch=4,
            grid=(n_rb,),
            in_specs=[pl.BlockSpec(memory_space=pl.ANY),
                      _const_spec((n_tb, TOP_K, tb)), _const_spec((n_tb, TOP_K, tb)),
                      pl.BlockSpec((1, d, d_e), lambda j, be, lo, hi, ok: (be[j], 0, 0)),
                      pl.BlockSpec((1, d, d_e), lambda j, be, lo, hi, ok: (be[j], 0, 0)),
                      pl.BlockSpec((1, d_e, d), lambda j, be, lo, hi, ok: (be[j], 0, 0))],
            out_specs=pl.BlockSpec((rb, d), lambda j, be, lo, hi, ok: (j, 0)),
            scratch_shapes=[pltpu.VMEM((GATHER_SLOTS, tb, d), BF16), pltpu.VMEM((rb, GATHER_SLOTS * tb), BF16),
                            pltpu.VMEM((rb, 1), F32), pltpu.VMEM((rb, d), F32), pltpu.VMEM((rb, 1), F32),
                            pltpu.SemaphoreType.DMA((GATHER_SLOTS,))]),
        out_shape=jax.ShapeDtypeStruct((n_rows_max, d), BF16),
        compiler_params=_params(("arbitrary",)),
        name="experts",
    )(blk_e, blk_lo, blk_hi, blk_ok, u2, pos_l, gate_l, wg_e, wu_e, wd_e)

    cr = COMBINE_ROWS
    per_expert = tb // cr + 1
    n_slots = TOP_K * tb // cr + 2 * n_e
    assert rb % cr == 0 and LANE % cr == 0 and n_slots % (LANE // cr) == 0
    w_lo = pstart[None, :] + cum[:-1]
    w_hi = pstart[None, :] + cum[1:]
    c_lo = w_lo // cr
    n_chunks = jnp.where(w_hi > w_lo, (w_hi - 1) // cr - c_lo + 1, 0)
    q = jnp.arange(per_expert)
    cand = (c_lo[:, :, None] + q).reshape(n_tb, n_e * per_expert)
    keep = (q < n_chunks[:, :, None]).reshape(n_tb, n_e * per_expert)
    order = jnp.argsort(~keep, axis=1, stable=True)[:, :n_slots]
    ids = jnp.where(jnp.take_along_axis(keep, order, axis=1), jnp.take_along_axis(cand, order, axis=1),
                    n_rows_max // cr)
    n_ids = jnp.sum(keep, axis=1).astype(jnp.int32)
    y_frames, y_short = pl.pallas_call(
        functools.partial(_combine_kernel, n_tb, n_fb),
        grid_spec=pltpu.PrefetchScalarGridSpec(
            num_scalar_prefetch=2,
            grid=(n_tb,),
            in_specs=[pl.BlockSpec((tb, d), lambda i, n, ids: (i, 0)),
                      pl.BlockSpec((tb, n_e), lambda i, n, ids: (i, 0)),
                      pl.BlockSpec((1, d), lambda i, n, ids: (0, 0)),
                      pl.BlockSpec(memory_space=pl.ANY)],
            out_specs=[pl.BlockSpec((tb, d), lambda i, n, ids: (jnp.minimum(i, n_fb - 1), 0)),
                       pl.BlockSpec((tb, d), lambda i, n, ids: (jnp.maximum(i - n_fb, 0), 0))],
            scratch_shapes=[pltpu.VMEM((2, n_slots, cr, d), BF16), pltpu.SemaphoreType.DMA((2 * n_slots,))]),
        out_shape=[jax.ShapeDtypeStruct((nf, d), F32), jax.ShapeDtypeStruct((ns, d), F32)],
        compiler_params=_params(("arbitrary",)),
        name="combine",
    )(n_ids, ids.reshape(-1).astype(jnp.int32), h2, pos_c, row(norm_final), ys)

    y_prompt = y_frames.reshape(nb, seq, d)
    y_sample = y_short[:db * SEQ_S].reshape(db, SEQ_S, d)

    def with_meta(frames, short, width):
        meta = jnp.broadcast_to(short[db * SEQ_S:db * SEQ_S + N_META][None], (nb, N_META, width))
        return jnp.concatenate([meta, frames.reshape(nb, seq, width)], axis=1)[None]

    c_p = with_meta(c_frames, c_short, kv_rank)
    r_p = with_meta(r_frames, r_short, QK_ROPE)
    c_s = c_short[:db * SEQ_S].reshape(1, db, SEQ_S, kv_rank)
    r_s = r_short[:db * SEQ_S].reshape(1, db, SEQ_S, QK_ROPE)
    s_p = state_f[:, HIST_ROWS - POOL_HIST:][None]
    s_s = state_s[:db, SEQ_S - POOL_HIST:][None]
    return (y_prompt, y_sample, c_p, r_p, s_p, c_s, r_s, s_s)
```

```python
import functools

import jax
import jax.numpy as jnp
from jax import lax
from jax.experimental import pallas as pl
from jax.experimental.pallas import tpu as pltpu

CHUNK = 64
N_META = 16
N_HEADS = 8
QK_NOPE = 128
QK_ROPE = 64
ROPE_THETA = 10000.0
POOL_WINDOWS = (2, 4, 8, 16)
POOL_HIST = max(POOL_WINDOWS) - 1
TOP_K = 2
RMS_EPS = 1e-6

LANE = 128
BF16_SUBLANES = 16
SEQ_S = 16
F32_SUBLANES = 8
HIST_ROWS = 16
HIST_PAD = F32_SUBLANES * len(POOL_WINDOWS)
TOKEN_BLOCK = 256
ROW_BLOCK = 256
ATTN_BLOCK = 256
FFN_SLAB = 1024
CACHE_BLOCK = 2048
KEY_BLOCKS_PER_ITER = 4
SCORES_AHEAD = 8
COMBINE_ROWS = 64
GATHER_SLOTS = 6
V7X_VMEM_BYTES = 64 * 1024 * 1024
VMEM_LIMIT = V7X_VMEM_BYTES // 8 * 7

F32 = jnp.float32
BF16 = jnp.bfloat16
NEG_INF = float("-inf")
LOG2_E = 1.4426950408889634


def _dot(a, b):
    return jnp.dot(a, b, preferred_element_type=F32)


def _dot_nt(a, b):
    return lax.dot_general(a, b, (((1,), (1,)), ((), ())), preferred_element_type=F32)


def _rms(x, g):
    return x * lax.rsqrt(jnp.mean(x * x, axis=-1, keepdims=True) + RMS_EPS) * g


def _swiglu(u, wg, wu, wd):
    d_hidden = wg.shape[1]
    out = None
    for start in range(0, d_hidden, FFN_SLAB):
        cols = slice(start, min(start + FFN_SLAB, d_hidden))
        act = jax.nn.silu(_dot(u, wg[:, cols])) * _dot(u, wu[:, cols])
        part = _dot(act.astype(BF16), wd[cols, :])
        out = part if out is None else out + part
    return out


def _const_spec(shape):
    nd = len(shape)
    return pl.BlockSpec(shape, lambda *_: (0,) * nd, pipeline_mode=pl.Buffered(1))


def _params(sem):
    return pltpu.CompilerParams(dimension_semantics=sem, vmem_limit_bytes=VMEM_LIMIT)


def _qkv_kernel(n_fb, scale, xp_ref, xs_ref, tab_ref, tabt_ref, g_ref, wdq_ref, gq_ref, wuq_ref, wuqt_ref, wukt_ref,
                wdkv_ref, gkv_ref, wukf_ref, wuvt_ref, q_ref, qt_ref, kvb_ref, kn_ref, kr_ref, vt_ref, cp_ref, rp_ref,
                cs_ref, rs_ref):
    i = pl.program_id(0)
    x = jnp.where(i < n_fb, xp_ref[...], xs_ref[...])
    u = _rms(x, g_ref[...]).astype(BF16)
    cq = _rms(_dot(u, wdq_ref[...]), gq_ref[...]).astype(BF16)
    cos = tab_ref[:, :LANE]
    sin = tab_ref[:, LANE:]
    hq = QK_NOPE + 2 * LANE

    kv = _dot(u, wdkv_ref[...])
    c = _rms(kv[:, :2 * LANE], gkv_ref[...])
    r = kv[:, 2 * LANE:3 * LANE] * cos + kv[:, 3 * LANE:] * sin
    c_bf = c.astype(BF16)
    r_bf = r.astype(BF16)

    kn_ref[...] = _dot(c_bf, wukf_ref[...]).astype(BF16)
    kr_ref[...] = r_bf
    vt = _dot_nt(wuvt_ref[...], c_bf).astype(BF16)
    hv = vt.shape[0] // N_HEADS
    hv_ext = vt_ref.shape[1] // N_HEADS
    for h in range(N_HEADS):
        vt_ref[0, h * hv_ext:h * hv_ext + hv, :] = vt[h * hv:(h + 1) * hv, :]
        vt_ref[0, h * hv_ext + hv:(h + 1) * hv_ext, :] = jnp.ones((hv_ext - hv, vt.shape[1]), BF16)

    @pl.when(i < n_fb)
    def _():
        qat = _dot_nt(wuqt_ref[...], cq)
        cos_t = tabt_ref[:QK_ROPE, :]
        sin_t = tabt_ref[LANE:LANE + QK_ROPE, :]
        pe0 = N_HEADS * QK_NOPE
        sw0 = pe0 + N_HEADS * QK_ROPE
        for h in range(N_HEADS):
            qt_ref[0, h, :QK_NOPE, :] = (qat[h * QK_NOPE:(h + 1) * QK_NOPE, :] * scale).astype(BF16)
            a = qat[pe0 + h * QK_ROPE:pe0 + (h + 1) * QK_ROPE, :]
            b = qat[sw0 + h * QK_ROPE:sw0 + (h + 1) * QK_ROPE, :]
            qt_ref[0, h, QK_NOPE:QK_NOPE + QK_ROPE, :] = ((a * cos_t + b * sin_t) * scale).astype(BF16)
            qt_ref[0, h, QK_NOPE + QK_ROPE:, :] = jnp.zeros((LANE - QK_ROPE, qt_ref.shape[3]), BF16)
        cp_ref[...] = c
        rp_ref[...] = r[:, :QK_ROPE]

    @pl.when(i >= n_fb)
    def _():
        qa = _dot(cq, wuq_ref[...])
        for h in range(N_HEADS):
            qn = qa[:, h * LANE:(h + 1) * LANE].astype(BF16)
            q_ref[:, h * hq:h * hq + 2 * LANE] = (_dot(qn, wukt_ref[h]) * scale).astype(BF16)
            a = qa[:, (N_HEADS + h) * LANE:(N_HEADS + h + 1) * LANE]
            b = qa[:, (2 * N_HEADS + h) * LANE:(2 * N_HEADS + h + 1) * LANE]
            q_ref[:, h * hq + 2 * LANE:(h + 1) * hq] = ((a * cos + b * sin) * scale).astype(BF16)
        kvb_ref[:, :2 * LANE] = c_bf
        kvb_ref[:, 2 * LANE:] = r_bf
        cs_ref[...] = c
        rs_ref[...] = r[:, :QK_ROPE]


def _attn_frames_kernel(qb, meta_col, qt_ref, kn_ref, kr_ref, vt_ref, knm_ref, krm_ref, vtm_ref, o_ref, m_ref, acc_ref):
    j = pl.program_id(0) % qb
    tq = o_ref.shape[0]
    hn = kn_ref.shape[1] // N_HEADS
    hv = o_ref.shape[1] // N_HEADS
    hx = vt_ref.shape[1] // N_HEADS

    def head_keys(kn, kr, rows, h):
        return jnp.concatenate([kn[rows, h * hn:(h + 1) * hn], kr[rows, :]], axis=1)

    def scores(h, rows):
        return _dot(head_keys(kn_ref, kr_ref, rows, h), qt_ref[0, h])

    m_ref[...] = jnp.full(m_ref.shape, NEG_INF, F32)
    acc_ref[...] = jnp.zeros_like(acc_ref)

    def unmasked(kbs):
        steps = [(kb, h) for kb in kbs for h in range(N_HEADS)]
        block_scores = lambda kb, h: scores(h, pl.ds(pl.multiple_of(kb * tq, tq), tq))
        ahead = [block_scores(*step) for step in steps[:SCORES_AHEAD]]
        for n, (kb, h) in enumerate(steps):
            s = ahead.pop(0)
            if n + SCORES_AHEAD < len(steps):
                ahead.append(block_scores(*steps[n + SCORES_AHEAD]))
            m_old = m_ref[h]
            m_new = jnp.maximum(m_old, jnp.max(s, axis=0, keepdims=True))
            alpha = jnp.exp2(m_old - m_new)
            p = jnp.exp2(s - m_new)
            m_ref[h] = m_new
            acc_ref[h] = alpha * acc_ref[h] + _dot(vt_ref[kb, h * hx:(h + 1) * hx, :], p.astype(BF16))

    def body(t, carry):
        unmasked([KEY_BLOCKS_PER_ITER * t + u for u in range(KEY_BLOCKS_PER_ITER)])
        return carry

    lax.fori_loop(0, j // KEY_BLOCKS_PER_ITER, body, 0)
    for left in range(1, KEY_BLOCKS_PER_ITER):
        @pl.when(j % KEY_BLOCKS_PER_ITER == left)
        def _():
            unmasked([j - left + u for u in range(left)])

    rows = pl.ds(pl.multiple_of(j * tq, tq), tq)
    visible = (lax.broadcasted_iota(jnp.int32, (tq, tq), 0) // CHUNK
               <= lax.broadcasted_iota(jnp.int32, (tq, tq), 1) // CHUNK)

    def last_scores(h):
        return scores(h, rows), _dot(head_keys(knm_ref, krm_ref, slice(None), h), qt_ref[0, h])

    ahead = [last_scores(h) for h in range(SCORES_AHEAD)]
    for h in range(N_HEADS):
        s, s_meta = ahead.pop(0)
        s = jnp.where(visible, s, NEG_INF)
        if h + SCORES_AHEAD < N_HEADS:
            ahead.append(last_scores(h + SCORES_AHEAD))
        m_old = m_ref[h]
        m_new = jnp.maximum(m_old, jnp.maximum(jnp.max(s, axis=0, keepdims=True),
                                               jnp.max(s_meta, axis=0, keepdims=True)))
        alpha = jnp.exp2(m_old - m_new)
        p = jnp.exp2(s - m_new)
        p_meta = jnp.exp2(s_meta - m_new)
        acc = (alpha * acc_ref[h] + _dot(vt_ref[j, h * hx:(h + 1) * hx, :], p.astype(BF16))
               + _dot(vtm_ref[0, h * hx:(h + 1) * hx, meta_col:meta_col + N_META], p_meta.astype(BF16)))
        o_ref[:, h * hv:(h + 1) * hv] = (acc[:hv] * (1.0 / acc[hv:hv + 1])).T.astype(BF16)


def _attn_short_kernel(n_cached_seq, q_ref, kvn_ref, cc_ref, cr_ref, wuv_ref, o_ref):
    s_id = pl.program_id(0)
    hq = q_ref.shape[1] // N_HEADS
    n_cache = cc_ref.shape[2]
    qs = jnp.concatenate([q_ref[:, h * hq:(h + 1) * hq] for h in range(N_HEADS)], axis=0)
    q_lat = qs[:, :2 * LANE]
    q_pe = qs[:, 2 * LANE:2 * LANE + QK_ROPE]
    kvn = kvn_ref[...]
    s = _dot_nt(qs, kvn)
    m = jnp.max(s, axis=1, keepdims=True)
    p = jnp.exp2(s - m)
    l = jnp.sum(p, axis=1, keepdims=True)
    acc = _dot(p.astype(BF16), kvn[:, :2 * LANE])
    has_cache = s_id < n_cached_seq
    start = 0
    while start < n_cache:
        size = min(CACHE_BLOCK, n_cache - start)
        ck = cc_ref[0, 0, start:start + size, :].astype(BF16)
        rk = cr_ref[0, 0, start:start + size, :].astype(BF16)
        s = _dot_nt(q_lat, ck) + _dot_nt(q_pe, rk)
        s = jnp.where(has_cache, s, NEG_INF)
        m_new = jnp.maximum(m, jnp.max(s, axis=1, keepdims=True))
        alpha = jnp.exp2(m - m_new)
        p = jnp.exp2(s - m_new)
        l = alpha * l + jnp.sum(p, axis=1, keepdims=True)
        acc = alpha * acc + _dot(p.astype(BF16), ck)
        m = m_new
        start += size
    o = (acc / l).astype(BF16)
    hv = wuv_ref.shape[2]
    for h in range(N_HEADS):
        o_ref[:, h * hv:(h + 1) * hv] = _dot(o[h * SEQ_S:(h + 1) * SEQ_S, :], wuv_ref[h]).astype(BF16)


def _proj_ffn_kernel(n_fb, xp_ref, xs_ref, op_ref, os_ref, wo_ref, g_ref, wg_ref, wu_ref, wd_ref, *cast_refs):
    n_cast = (len(cast_refs) - 1) // 2
    h_ref = cast_refs[n_cast]
    for src, dst in zip(cast_refs[:n_cast], cast_refs[n_cast + 1:]):
        dst[...] = src[...].astype(BF16)
    i = pl.program_id(0)
    x = jnp.where(i < n_fb, xp_ref[...], xs_ref[...])
    o = jnp.where(i < n_fb, op_ref[...], os_ref[...])
    h1 = x + _dot(o, wo_ref[...])
    u = _rms(h1, g_ref[...]).astype(BF16)
    h_ref[...] = h1 + _swiglu(u, wg_ref, wu_ref, wd_ref)


def _set_history(ext_ref, hist):
    for start in range(0, HIST_PAD, HIST_ROWS):
        ext_ref[start:start + HIST_ROWS, :] = hist


def _window_sums(ext_ref, stage_refs, t):
    grp = ext_ref.shape[1] // len(POOL_WINDOWS)
    end = HIST_PAD + t
    first = F32_SUBLANES
    s = ext_ref[first:end, :] + ext_ref[first - 1:end - 1, :]
    sums = [s[HIST_PAD - first:, :grp]]
    for ref, shift in zip(stage_refs, POOL_WINDOWS):
        rest = s[:, grp:]
        ref[first:end, :] = rest
        first += F32_SUBLANES
        s = rest[F32_SUBLANES:, :] + ref[first - shift:end - shift, :]
        sums.append(s[HIST_PAD - first:, :grp])
    return sums


def _pool_route(h1, ext_ref, stage_refs, cnt_rows, valid, base, gm_ref, pw_ref, ps_ref, gf_ref, wr_ref, br_ref,
                tri_ref):
    t = h1.shape[0]
    grp = h1.shape[1] // len(POOL_WINDOWS)
    u1 = _rms(h1, gm_ref[...])
    ext_ref[HIST_PAD:HIST_PAD + t, :] = u1
    sums = _window_sums(ext_ref, stage_refs, t)
    ys = []
    for g, w in enumerate(POOL_WINDOWS):
        cnt = float(w) if cnt_rows is None else jnp.minimum(float(w), cnt_rows)
        mean = sums[g] / cnt - u1[:, g * grp:(g + 1) * grp]
        ys.append(_dot(mean.astype(BF16), pw_ref[g]))
    h2 = h1 + jnp.concatenate(ys, axis=1) * ps_ref[...]
    u2 = _rms(h2, gf_ref[...])

    n_e = wr_ref.shape[1] // 2
    u2_hi = u2.astype(BF16)
    u2_lo = (u2 - u2_hi.astype(F32)).astype(BF16)
    parts = _dot(u2_hi, wr_ref[...]) + _dot(u2_lo, wr_ref[...])
    logits = parts[:, :n_e] + parts[:, n_e:] + br_ref[...]
    e_iota = lax.broadcasted_iota(jnp.int32, logits.shape, 1).astype(F32)
    v0 = jnp.max(logits, axis=1, keepdims=True)
    e0 = jnp.min(jnp.where(logits == v0, e_iota, float(n_e)), axis=1, keepdims=True)
    rest = jnp.where(e_iota == e0, NEG_INF, logits)
    v1 = jnp.max(rest, axis=1, keepdims=True)
    e1 = jnp.min(jnp.where(rest == v1, e_iota, float(n_e)), axis=1, keepdims=True)
    tt = jnp.exp(v1 - v0)
    g0 = 1.0 / (1.0 + tt)
    g1 = tt / (1.0 + tt)
    hit0 = e_iota == e0
    hit1 = e_iota == e1
    onehot = jnp.where(valid & (hit0 | hit1), 1.0, 0.0)
    rank = _dot(tri_ref[...], onehot.astype(BF16)) + base
    r0 = jnp.sum(jnp.where(hit0, rank, 0.0), axis=1, keepdims=True)
    r1 = jnp.sum(jnp.where(hit1, rank, 0.0), axis=1, keepdims=True)
    cols = (jnp.where(valid, e0, -1).astype(F32), jnp.where(valid, e1, -1).astype(F32), r0, r1, g0, g1)
    info = jnp.zeros(logits.shape, F32)
    for k, col in enumerate(cols):
        info = jnp.where(e_iota == k, col, info)
    new_base = base + jnp.sum(onehot, axis=0, keepdims=True)
    return h2, u2, info, new_base, u1


def _pool_frames_kernel(n_fb, h_ref, hist_ref, gm_ref, pw_ref, ps_ref, gf_ref, wr_ref, br_ref, tri_ref,
                        h2_ref, u2_ref, info_ref, cum_ref, tot_ref, state_ref, ext_ref, base_ref, *stage_refs):
    i = pl.program_id(0)

    @pl.when(i == 0)
    def _():
        base_ref[...] = jnp.zeros_like(base_ref)

    @pl.when(i >= n_fb)
    def _():
        h2_ref[...] = jnp.zeros_like(h2_ref)
        u2_ref[...] = jnp.zeros_like(u2_ref)
        info_ref[...] = jnp.zeros_like(info_ref)

    @pl.when(i < n_fb)
    def _():
        _set_history(ext_ref, _rms(hist_ref[...], gm_ref[...]))
        base = base_ref[...]
        cum_ref[0] = base
        h2, u2, info, new_base, u1 = _pool_route(h_ref[...], ext_ref, stage_refs, None, True, base, gm_ref, pw_ref,
                                                 ps_ref, gf_ref, wr_ref, br_ref, tri_ref)
        h2_ref[...] = h2
        u2_ref[...] = u2.astype(BF16)
        info_ref[...] = info
        base_ref[...] = new_base
        tot_ref[...] = new_base
        state_ref[0] = u1[u1.shape[0] - HIST_ROWS:, :]


def _pool_short_kernel(n_sample, n_valid, h2_hbm, u2_hbm, info_hbm, h_ref, hist_ref, base0_ref, gm_ref, pw_ref, ps_ref,
                       gf_ref, wr_ref, br_ref, tri_ref, h2_ref, u2_ref, info_ref, cum_ref, tot_ref, state_ref,
                       ext_ref, base_ref, *stage_refs):
    del h2_hbm, u2_hbm, info_hbm
    s_id = pl.program_id(0)

    @pl.when(s_id == 0)
    def _():
        base_ref[...] = base0_ref[...]

    _set_history(ext_ref, hist_ref[0])
    base = base_ref[...]
    cum_ref[0] = base
    pos1 = (lax.broadcasted_iota(jnp.int32, (SEQ_S, 1), 0) + 1).astype(F32)
    cnt_rows = jnp.where(s_id < n_sample, float(max(POOL_WINDOWS)), pos1)
    h2, u2, info, new_base, u1 = _pool_route(h_ref[...], ext_ref, stage_refs, cnt_rows, s_id < n_valid, base, gm_ref,
                                             pw_ref, ps_ref, gf_ref, wr_ref, br_ref, tri_ref)
    h2_ref[...] = h2
    u2_ref[...] = u2.astype(BF16)
    info_ref[...] = info
    base_ref[...] = new_base
    tot_ref[...] = new_base
    state_ref[0] = u1


def _expert_kernel(n_blocks, be_ref, lo_ref, hi_ref, ok_ref, u_hbm, pos_ref, gate_ref, wg_ref, wu_ref, wd_ref, ys_ref,
                   buf_ref, sel_ref, gsel_ref, acc_ref, gacc_ref, sem):
    del be_ref
    j = pl.program_id(0)
    rb = ys_ref.shape[0]
    n_slots, tc, _ = buf_ref.shape
    last_chunk = pos_ref.shape[0] - 1

    def n_chunks(jj):
        return jnp.where(ok_ref[jj] != 0, hi_ref[jj] - lo_ref[jj] + 1, 0)

    def chunk_copy(c, slot):
        return pltpu.make_async_copy(u_hbm.at[pl.ds(pl.multiple_of(c * tc, tc), tc), :], buf_ref.at[slot],
                                     sem.at[slot])

    def request(jj):
        n_req = jnp.minimum(n_chunks(jj), n_slots)
        for k in range(n_slots):
            @pl.when(k < n_req)
            def _():
                chunk_copy(lo_ref[jj] + k, k).start()

    def select(jj, c, live):
        rows = jj * rb + lax.broadcasted_iota(jnp.int32, (rb, tc), 0)
        pos = pos_ref[c]
        gate = gate_ref[c]
        m0 = (rows == pos[0:1, :]) & live
        m1 = (rows == pos[1:2, :]) & live
        sel = jnp.where(m0 | m1, 1.0, 0.0).astype(BF16)
        row_gate = jnp.sum(jnp.where(m0, gate[0:1, :], 0.0) + jnp.where(m1, gate[1:2, :], 0.0), axis=1, keepdims=True)
        return sel, row_gate

    def build_selection(jj):
        n_req = jnp.minimum(n_chunks(jj), n_slots)
        row_gate = jnp.zeros(gsel_ref.shape, F32)
        for k in range(n_slots):
            sel, g = select(jj, jnp.minimum(lo_ref[jj] + k, last_chunk), k < n_req)
            sel_ref[:, k * tc:(k + 1) * tc] = sel
            row_gate = row_gate + g
        gsel_ref[...] = row_gate

    @pl.when(j == 0)
    def _():
        buf_ref[...] = jnp.zeros_like(buf_ref)
        request(0)
        build_selection(0)

    n = n_chunks(j)
    lo = lo_ref[j]

    @pl.when(n > 0)
    def _():
        for k in range(n_slots):
            @pl.when(k < jnp.minimum(n, n_slots))
            def _():
                chunk_copy(lo + k, k).wait()

        acc_ref[...] = _dot(sel_ref[...], buf_ref[...].reshape(n_slots * tc, -1))
        gacc_ref[...] = gsel_ref[...]

        def overflow(k, carry):
            copy = chunk_copy(lo + k, 0)
            copy.start()
            copy.wait()
            sel, row_gate = select(j, lo + k, True)
            acc_ref[...] += _dot(sel, buf_ref[0])
            gacc_ref[...] += row_gate
            return carry

        lax.fori_loop(n_slots, jnp.maximum(n, n_slots), overflow, 0)

    @pl.when(j + 1 < n_blocks)
    def _():
        request(jnp.minimum(j + 1, n_blocks - 1))

    @pl.when(n > 0)
    def _():
        xg = acc_ref[...].astype(BF16)
        ys_ref[...] = (_swiglu(xg, wg_ref.at[0], wu_ref.at[0], wd_ref.at[0]) * gacc_ref[...]).astype(BF16)
        build_selection(jnp.minimum(j + 1, n_blocks - 1))

    @pl.when(n == 0)
    def _():
        ys_ref[...] = jnp.zeros_like(ys_ref)


def _combine_kernel(n_blocks, n_fb, n_ref, ids_ref, h2_ref, pos_ref, g_ref, ys_hbm, yp_ref, yshort_ref, buf_ref, sem):
    i = pl.program_id(0)
    _, n_slots, cr, _ = buf_ref.shape
    per_vreg = LANE // cr
    par = i % 2
    nxt = jnp.minimum(i + 1, n_blocks - 1)

    def chunk_copy(ii, k, which):
        ch = ids_ref[ii * n_slots + k]
        return pltpu.make_async_copy(ys_hbm.at[pl.ds(pl.multiple_of(ch * cr, cr), cr), :], buf_ref.at[which, k],
                                     sem.at[which * n_slots + k])

    def request(ii, which):
        for k in range(n_slots):
            @pl.when(k < n_ref[ii])
            def _():
                chunk_copy(ii, k, which).start()

    @pl.when(i == 0)
    def _():
        buf_ref[...] = jnp.zeros_like(buf_ref)
        request(0, 0)

    @pl.when(i + 1 < n_blocks)
    def _():
        request(nxt, 1 - par)

    lane = lax.broadcasted_iota(jnp.int32, (1, LANE), 1)
    pieces = []
    for v in range(n_slots // per_vreg):
        row = jnp.zeros((1, LANE), jnp.int32)
        for q in range(per_vreg):
            k = v * per_vreg + q
            row = jnp.where(lane // cr == q, ids_ref[i * n_slots + k] * cr + lane % cr, row)
        pieces.append(row)
    slot_rows = jnp.concatenate(pieces, axis=1)
    sel = jnp.where((slot_rows == pos_ref[:, 0:1]) | (slot_rows == pos_ref[:, 1:2]), 1.0, 0.0).astype(BF16)

    for k in range(n_slots):
        @pl.when(k < n_ref[i])
        def _():
            chunk_copy(i, k, par).wait()

    moe = _dot(sel, buf_ref[par].reshape(n_slots * cr, -1))
    y = _rms(h2_ref[...] + moe, g_ref[...])

    @pl.when(i < n_fb)
    def _():
        yp_ref[...] = y

    @pl.when(i >= n_fb)
    def _():
        yshort_ref[...] = y


def _rope_table(pos):
    half = QK_ROPE // 2
    inv = ROPE_THETA ** (-jnp.arange(half, dtype=F32) / half)
    ang = pos.astype(F32)[:, None] * inv[None, :]
    cos = jnp.cos(ang)
    sin = jnp.sin(ang)
    zero = jnp.zeros((pos.shape[0], LANE - QK_ROPE), F32)
    return jnp.concatenate([cos, cos, zero, -sin, sin, zero], axis=1)


def _half_swap(w):
    half = QK_ROPE // 2
    return jnp.concatenate([w[..., half:], w[..., :half]], axis=-1)


def _pad_last(w, width):
    return jnp.pad(w, [(0, 0)] * (w.ndim - 1) + [(0, width - w.shape[-1])])


def kernel(x_prompt, x_sample, cache_kv_latent, cache_k_rope, state_pool, meta_tokens, norm_mix, norm_ffn, norm_final,
           mla_w_dq, mla_g_q, mla_w_uq, mla_w_dkv, mla_g_kv, mla_w_uk, mla_w_uv, mla_w_o, pool_w, pool_scale,
           ffn_w_gate, ffn_w_up, ffn_w_down, moe_w_router, moe_b_router, moe_w_gate, moe_w_up, moe_w_down):
    nb, seq, d = x_prompt.shape
    db, dseq, _ = x_sample.shape
    n_cache = cache_kv_latent.shape[2]
    q_rank = mla_w_dq.shape[2]
    kv_rank = mla_g_kv.shape[1]
    v_dim = mla_w_uv.shape[3]
    d_ff = ffn_w_gate.shape[2]
    n_e = moe_w_router.shape[2]
    d_e = moe_w_gate.shape[3]
    tb, rb, tq = TOKEN_BLOCK, ROW_BLOCK, ATTN_BLOCK
    assert norm_mix.shape[0] == 2 and cache_kv_latent.shape[0] == 1 and state_pool.shape[0] == 1
    assert dseq == SEQ_S and N_META == SEQ_S and meta_tokens.shape[0] == N_META
    assert POOL_WINDOWS == tuple(2 ** (k + 1) for k in range(len(POOL_WINDOWS))) and HIST_PAD % HIST_ROWS == 0
    assert kv_rank == 2 * LANE and QK_NOPE == LANE and QK_ROPE <= LANE and HIST_ROWS >= POOL_HIST
    assert seq % tb == 0 and tb == tq and tq % CHUNK == 0 and tb % SEQ_S == 0 and d % LANE == 0
    assert (n_cache - N_META) % CHUNK == 0 and dseq <= CHUNK

    nf = nb * seq
    n_valid_seq = db + 1
    ns = -(-(n_valid_seq * SEQ_S) // tb) * tb
    n_seq = ns // SEQ_S
    nt = nf + ns
    n_fb, n_tb = nf // tb, nt // tb
    sb = seq // tb
    meta_row = nf + db * SEQ_S
    scale = float((QK_NOPE + QK_ROPE) ** -0.5)
    hq = 3 * LANE

    xp = x_prompt.reshape(nf, d)
    xs = jnp.concatenate([x_sample.reshape(db * SEQ_S, d), meta_tokens.astype(x_prompt.dtype),
                          jnp.zeros((ns - n_valid_seq * SEQ_S, d), x_prompt.dtype)], axis=0)

    t_s = jnp.arange(SEQ_S)
    pos_short = jnp.concatenate([jnp.tile(n_cache + t_s, db), jnp.tile(t_s, n_seq - db)])
    tab_frames = _rope_table(N_META + jnp.arange(seq))
    tab = jnp.concatenate([tab_frames, _rope_table(pos_short)], axis=0)

    wuq = mla_w_uq[0].reshape(q_rank, N_HEADS, QK_NOPE + QK_ROPE)
    wuq_pe = wuq[:, :, QK_NOPE:]
    wuq2 = jnp.concatenate([wuq[:, :, :QK_NOPE].reshape(q_rank, -1),
                            _pad_last(wuq_pe, LANE).reshape(q_rank, -1),
                            _pad_last(_half_swap(wuq_pe), LANE).reshape(q_rank, -1)], axis=1).astype(BF16)
    wuq_t = jnp.concatenate([wuq[:, :, :QK_NOPE].reshape(q_rank, -1), wuq_pe.reshape(q_rank, -1),
                             _half_swap(wuq_pe).reshape(q_rank, -1)], axis=1).T.astype(BF16)
    wdkv_r =mla_w_dkv[0][:, kv_rank:]
    wdkv2 = jnp.concatenate([mla_w_dkv[0][:, :kv_rank], _pad_last(wdkv_r, LANE),
                             _pad_last(_half_swap(wdkv_r), LANE)], axis=1).astype(BF16)
    wuk_t = jnp.transpose(mla_w_uk[0], (1, 2, 0)).astype(BF16)
    wuk_flat = mla_w_uk[0].reshape(kv_rank, N_HEADS * QK_NOPE).astype(BF16)
    wuv = jnp.transpose(mla_w_uv[0], (1, 0, 2)).astype(BF16)
    wuv_t = mla_w_uv[0].reshape(kv_rank, N_HEADS * v_dim).T.astype(BF16)
    row = lambda v: v.reshape(1, -1)
    hk = 2 * LANE
    hx = v_dim + BF16_SUBLANES

    tok_p = pl.BlockSpec((tb, d), lambda i: (jnp.minimum(i, n_fb - 1), 0))
    tok_s = pl.BlockSpec((tb, d), lambda i: (jnp.maximum(i - n_fb, 0), 0))
    short_tok = lambda i: (jnp.maximum(i - n_fb, 0), 0)
    frame_tok = lambda i: (jnp.minimum(i, n_fb - 1), 0)
    q_short, qt_frames, kvb_short, kn_all, kr_all, vt_all, c_frames, r_frames, c_short, r_short = pl.pallas_call(
        functools.partial(_qkv_kernel, n_fb, scale * LOG2_E),
        grid=(n_tb,),
        in_specs=[tok_p, tok_s,
                  pl.BlockSpec((tb, 2 * LANE), lambda i: (jnp.where(i < n_fb, i % sb, sb + i - n_fb), 0)),
                  pl.BlockSpec((2 * LANE, tb), lambda i: (0, i % sb)),
                  _const_spec((1, d)), _const_spec((d, q_rank)), _const_spec((1, q_rank)),
                  _const_spec(wuq2.shape), _const_spec(wuq_t.shape), _const_spec(wuk_t.shape),
                  _const_spec(wdkv2.shape), _const_spec((1, kv_rank)), _const_spec(wuk_flat.shape),
                  _const_spec(wuv_t.shape)],
        out_specs=[pl.BlockSpec((tb, N_HEADS * hq), short_tok),
                   pl.BlockSpec((1, N_HEADS, hk, tb), lambda i: (jnp.minimum(i, n_fb - 1), 0, 0, 0)),
                   pl.BlockSpec((tb, hq), short_tok),
                   pl.BlockSpec((tb, N_HEADS * QK_NOPE), lambda i: (i, 0)),
                   pl.BlockSpec((tb, LANE), lambda i: (i, 0)),
                   pl.BlockSpec((1, N_HEADS * hx, tb), lambda i: (i, 0, 0)),
                   pl.BlockSpec((tb, kv_rank), frame_tok), pl.BlockSpec((tb, QK_ROPE), frame_tok),
                   pl.BlockSpec((tb, kv_rank), short_tok), pl.BlockSpec((tb, QK_ROPE), short_tok)],
        out_shape=[jax.ShapeDtypeStruct((ns, N_HEADS * hq), BF16),
                   jax.ShapeDtypeStruct((n_fb, N_HEADS, hk, tb), BF16),
                   jax.ShapeDtypeStruct((ns, hq), BF16),
                   jax.ShapeDtypeStruct((nt, N_HEADS * QK_NOPE), BF16), jax.ShapeDtypeStruct((nt, LANE), BF16),
                   jax.ShapeDtypeStruct((n_tb, N_HEADS * hx, tb), BF16),
                   jax.ShapeDtypeStruct((nf, kv_rank), F32), jax.ShapeDtypeStruct((nf, QK_ROPE), F32),
                   jax.ShapeDtypeStruct((ns, kv_rank), F32), jax.ShapeDtypeStruct((ns, QK_ROPE), F32)],
        compiler_params=_params(("arbitrary",)),
        name="qkv",
    )(xp, xs, tab, tab_frames.T, row(norm_mix[0]), mla_w_dq[0].astype(BF16), row(mla_g_q[0]), wuq2, wuq_t, wuk_t,
      wdkv2, row(mla_g_kv[0]), wuk_flat, wuv_t)

    qb = seq // tq
    once = pl.Buffered(1)
    o_frames = pl.pallas_call(
        functools.partial(_attn_frames_kernel, qb, meta_row % tb),
        grid=(n_fb,),
        in_specs=[pl.BlockSpec((1, N_HEADS, hk, tq), lambda i: (i, 0, 0, 0)),
                  pl.BlockSpec((seq, N_HEADS * QK_NOPE), lambda i: (i // qb, 0)),
                  pl.BlockSpec((seq, LANE), lambda i: (i // qb, 0)),
                  pl.BlockSpec((qb, N_HEADS * hx, tq), lambda i: (i // qb, 0, 0)),
                  pl.BlockSpec((N_META, N_HEADS * QK_NOPE), lambda i: (meta_row // N_META, 0), pipeline_mode=once),
                  pl.BlockSpec((N_META, LANE), lambda i: (meta_row // N_META, 0), pipeline_mode=once),
                  pl.BlockSpec((1, N_HEADS * hx, tb), lambda i: (meta_row // tb, 0, 0), pipeline_mode=once)],
        out_specs=pl.BlockSpec((tq, N_HEADS * v_dim), lambda i: (i, 0)),
        out_shape=jax.ShapeDtypeStruct((nf, N_HEADS * v_dim), BF16),
        scratch_shapes=[pltpu.VMEM((N_HEADS, 1, tq), F32), pltpu.VMEM((N_HEADS, hx, tq), F32)],
        compiler_params=_params(("parallel",)),
        name="attn_frames",
    )(qt_frames, kn_all, kr_all, vt_all, kn_all, kr_all, vt_all)

    o_short = pl.pallas_call(
        functools.partial(_attn_short_kernel, db),
        grid=(n_seq,),
        in_specs=[pl.BlockSpec((SEQ_S, N_HEADS * hq), lambda s: (s, 0)),
                  pl.BlockSpec((SEQ_S, hq), lambda s: (s, 0)),
                  pl.BlockSpec((1, 1, n_cache, kv_rank), lambda s: (0, jnp.minimum(s, db - 1), 0, 0)),
                  pl.BlockSpec((1, 1, n_cache, QK_ROPE), lambda s: (0, jnp.minimum(s, db - 1), 0, 0)),
                  _const_spec(wuv.shape)],
        out_specs=pl.BlockSpec((SEQ_S, N_HEADS * v_dim), lambda s: (s, 0)),
        out_shape=jax.ShapeDtypeStruct((ns, N_HEADS * v_dim), BF16),
        compiler_params=_params(("parallel",)),
        name="attn_short",
    )(q_short, kvb_short, cache_kv_latent, cache_k_rope, wuv)

    moe_f32 = [moe_w_gate[0].reshape(n_e * d, d_e), moe_w_up[0].reshape(n_e * d, d_e),
               moe_w_down[0].reshape(n_e * d_e, d)]
    cast_in, cast_out, cast_shapes = [], [], []
    for w in moe_f32:
        steps = max(s for s in range(1, n_tb + 1) if w.shape[0] % s == 0 and (w.shape[0] // s) % BF16_SUBLANES == 0)
        spec = pl.BlockSpec((w.shape[0] // steps, w.shape[1]), lambda i, last=steps - 1: (jnp.minimum(i, last), 0))
        cast_in.append(spec)
        cast_out.append(spec)
        cast_shapes.append(jax.ShapeDtypeStruct(w.shape, BF16))
    h1, wg_e, wu_e, wd_e = pl.pallas_call(
        functools.partial(_proj_ffn_kernel, n_fb),
        grid=(n_tb,),
        in_specs=[tok_p, tok_s,
                  pl.BlockSpec((tb, N_HEADS * v_dim), lambda i: (jnp.minimum(i, n_fb - 1), 0)),
                  pl.BlockSpec((tb, N_HEADS * v_dim), short_tok),
                  _const_spec((N_HEADS * v_dim, d)), _const_spec((1, d)),
                  _const_spec((d, d_ff)), _const_spec((d, d_ff)), _const_spec((d_ff, d))] + cast_in,
        out_specs=[pl.BlockSpec((tb, d), lambda i: (i, 0))] + cast_out,
        out_shape=[jax.ShapeDtypeStruct((nt, d), F32)] + cast_shapes,
        compiler_params=_params(("arbitrary",)),
        name="proj_ffn",
    )(xp, xs, o_frames, o_short, mla_w_o[0].astype(BF16), row(norm_ffn[0]),
      ffn_w_gate[0].astype(BF16), ffn_w_up[0].astype(BF16), ffn_w_down[0].astype(BF16), *moe_f32)
    wg_e = wg_e.reshape(n_e, d, d_e)
    wu_e = wu_e.reshape(n_e, d, d_e)
    wd_e = wd_e.reshape(n_e, d_e, d)

    route_w = [_const_spec((1, d)), _const_spec(pool_w.shape[1:]), _const_spec((1, d)), _const_spec((1, d)),
               _const_spec((d, 2 * n_e)), _const_spec((1, n_e))]
    wr_hi = moe_w_router[0].astype(BF16)
    wr_lo = (moe_w_router[0].astype(F32) - wr_hi.astype(F32)).astype(BF16)
    route_args = (row(norm_mix[1]), pool_w[0].astype(BF16), row(pool_scale[0]), row(norm_ffn[1]),
                  jnp.concatenate([wr_hi, wr_lo], axis=1), row(moe_b_router[0]))
    tri = lambda n: (jnp.arange(n)[:, None] > jnp.arange(n)[None, :]).astype(BF16)
    n_win = len(POOL_WINDOWS)
    stage_scratch = lambda t: [pltpu.VMEM((HIST_PAD + t, d // n_win * (n_win - 1 - k)), F32) for k in range(n_win - 1)]
    hist_blk = lambda i: jnp.where(i % sb == 0, meta_row // HIST_ROWS, i * (tb // HIST_ROWS) - 1)
    h2_f, u2_f, info_f, cum_f, tot_f, state_f = pl.pallas_call(
        functools.partial(_pool_frames_kernel, n_fb),
        grid=(n_tb,),
        in_specs=[pl.BlockSpec((tb, d), lambda i: (i, 0)),
                  pl.BlockSpec((HIST_ROWS, d), lambda i: (hist_blk(i), 0))] + route_w + [_const_spec((tb, tb))],
        out_specs=[pl.BlockSpec((tb, d), lambda i: (i, 0)), pl.BlockSpec((tb, d), lambda i: (i, 0)),
                   pl.BlockSpec((tb, n_e), lambda i: (i, 0)),
                   pl.BlockSpec((1, 1, n_e), lambda i: (jnp.minimum(i, n_fb - 1), 0, 0)),
                   pl.BlockSpec((1, n_e), lambda i: (0, 0)),
                   pl.BlockSpec((1, HIST_ROWS, d), lambda i: (jnp.minimum(i // sb, nb - 1), 0, 0))],
        out_shape=[jax.ShapeDtypeStruct((nt, d), F32), jax.ShapeDtypeStruct((nt, d), BF16),
                   jax.ShapeDtypeStruct((nt, n_e), F32), jax.ShapeDtypeStruct((n_fb, 1, n_e), F32),
                   jax.ShapeDtypeStruct((1, n_e), F32), jax.ShapeDtypeStruct((nb, HIST_ROWS, d), F32)],
        scratch_shapes=[pltpu.VMEM((HIST_PAD + tb, d), F32), pltpu.VMEM((1, n_e), F32)] + stage_scratch(tb),
        compiler_params=_params(("arbitrary",)),
        name="pool_route_frames",
    )(h1, h1, *route_args, tri(tb))

    hist_s = jnp.concatenate([
        jnp.pad(state_pool[0].astype(F32), [(0, 0), (HIST_ROWS - POOL_HIST, 0), (0, 0)]),
        jnp.zeros((n_seq - db, HIST_ROWS, d), F32)], axis=0)
    any_spec = pl.BlockSpec(memory_space=pl.ANY)
    short_blk = lambda s: (nf // SEQ_S + s, 0)
    h2, u2, info, cum_s, tot_s, state_s = pl.pallas_call(
        functools.partial(_pool_short_kernel, db, n_valid_seq),
        grid=(n_seq,),
        in_specs=[any_spec, any_spec, any_spec,
                  pl.BlockSpec((SEQ_S, d), short_blk),
                  pl.BlockSpec((1, HIST_ROWS, d), lambda s: (s, 0, 0)),
                  _const_spec((1, n_e))] + route_w + [_const_spec((SEQ_S, SEQ_S))],
        out_specs=[pl.BlockSpec((SEQ_S, d), short_blk), pl.BlockSpec((SEQ_S, d), short_blk),
                   pl.BlockSpec((SEQ_S, n_e), short_blk), pl.BlockSpec((1, 1, n_e), lambda s: (s, 0, 0)),
                   pl.BlockSpec((1, n_e), lambda s: (0, 0)),
                   pl.BlockSpec((1, SEQ_S, d), lambda s: (s, 0, 0))],
        out_shape=[jax.ShapeDtypeStruct((nt, d), F32), jax.ShapeDtypeStruct((nt, d), BF16),
                   jax.ShapeDtypeStruct((nt, n_e), F32), jax.ShapeDtypeStruct((n_seq, 1, n_e), F32),
                   jax.ShapeDtypeStruct((1, n_e), F32), jax.ShapeDtypeStruct((n_seq, SEQ_S, d), F32)],
        scratch_shapes=[pltpu.VMEM((HIST_PAD + SEQ_S, d), F32), pltpu.VMEM((1, n_e), F32)] + stage_scratch(SEQ_S),
        input_output_aliases={0: 0, 1: 1, 2: 2},
        compiler_params=_params(("arbitrary",)),
        name="pool_route_short",
    )(h2_f, u2_f, info_f, h1, hist_s, tot_f, *route_args, tri(SEQ_S))

    counts = tot_s[0].astype(jnp.int32)
    cum = jnp.concatenate([cum_f[:, 0], cum_s[::tb // SEQ_S, 0], tot_s], axis=0).astype(jnp.int32)
    padded = (counts + rb - 1) // rb * rb
    pend = jnp.cumsum(padded)
    pstart = pend - padded
    n_rows_max = -(-(TOP_K * (nf + n_valid_seq * SEQ_S)) // rb) * rb + n_e * rb
    n_rb = n_rows_max // rb
    e_tok = info[:, 0:TOP_K].astype(jnp.int32)
    pos = jnp.where(e_tok >= 0, pstart[jnp.maximum(e_tok, 0)] + info[:, 2:2 + TOP_K].astype(jnp.int32), -1)
    gates = info[:, 4:4 + TOP_K]
    pos_l = pos.reshape(n_tb, tb, TOP_K).transpose(0, 2, 1)
    gate_l = gates.reshape(n_tb, tb, TOP_K).transpose(0, 2, 1)
    pos_c = jnp.pad(pos, [(0, 0), (0, n_e - TOP_K)], constant_values=-1)

    blk_row = jnp.arange(n_rb, dtype=jnp.int32) * rb
    blk_e = jnp.minimum(jnp.sum(pend[None, :] <= blk_row[:, None], axis=1), n_e - 1).astype(jnp.int32)
    blk_ok = (blk_row < pend[-1]).astype(jnp.int32)
    r_lo = blk_row - pstart[blk_e]
    r_hi = jnp.minimum(r_lo + rb, counts[blk_e])
    cum_b = cum.T[blk_e]
    first = jnp.sum(cum_b[:, 1:] <= r_lo[:, None], axis=1)
    last = jnp.sum(cum_b[:, :-1] < r_hi[:, None], axis=1) - 1
    blk_lo = jnp.clip(first, 0, n_tb - 1).astype(jnp.int32)
    blk_hi = jnp.clip(last, blk_lo, n_tb - 1).astype(jnp.int32)

    ys = pl.pallas_call(
        functools.partial(_expert_kernel, n_rb),
        grid_spec=pltpu.PrefetchScalarGridSpec(
            num_scalar_prefetch=4,
            grid=(n_rb,),
            in_specs=[pl.BlockSpec(memory_space=pl.ANY),
                      _const_spec((n_tb, TOP_K, tb)), _const_spec((n_tb, TOP_K, tb)),
                      pl.BlockSpec((1, d, d_e), lambda j, be, lo, hi, ok: (be[j], 0, 0)),
                      pl.BlockSpec((1, d, d_e), lambda j, be, lo, hi, ok: (be[j], 0, 0)),
                      pl.BlockSpec((1, d_e, d), lambda j, be, lo, hi, ok: (be[j], 0, 0))],
            out_specs=pl.BlockSpec((rb, d), lambda j, be, lo, hi, ok: (j, 0)),
            scratch_shapes=[pltpu.VMEM((GATHER_SLOTS, tb, d), BF16), pltpu.VMEM((rb, GATHER_SLOTS * tb), BF16),
                            pltpu.VMEM((rb, 1), F32), pltpu.VMEM((rb, d), F32), pltpu.VMEM((rb, 1), F32),
                            pltpu.SemaphoreType.DMA((GATHER_SLOTS,))]),
        out_shape=jax.ShapeDtypeStruct((n_rows_max, d), BF16),
        compiler_params=_params(("arbitrary",)),
        name="experts",
    )(blk_e, blk_lo, blk_hi, blk_ok, u2, pos_l, gate_l, wg_e, wu_e, wd_e)

    cr = COMBINE_ROWS
    per_expert = tb // cr + 1
    n_slots = TOP_K * tb // cr + 2 * n_e
    assert rb % cr == 0 and LANE % cr == 0 and n_slots % (LANE // cr) == 0
    w_lo = pstart[None, :] + cum[:-1]
    w_hi = pstart[None, :] + cum[1:]
    c_lo = w_lo // cr
    n_chunks = jnp.where(w_hi > w_lo, (w_hi - 1) // cr - c_lo + 1, 0)
    q = jnp.arange(per_expert)
    cand = (c_lo[:, :, None] + q).reshape(n_tb, n_e * per_expert)
    keep = (q < n_chunks[:, :, None]).reshape(n_tb, n_e * per_expert)
    order = jnp.argsort(~keep, axis=1, stable=True)[:, :n_slots]
    ids = jnp.where(jnp.take_along_axis(keep, order, axis=1), jnp.take_along_axis(cand, order, axis=1),
                    n_rows_max // cr)
    n_ids = jnp.sum(keep, axis=1).astype(jnp.int32)
    y_frames, y_short = pl.pallas_call(
        functools.partial(_combine_kernel, n_tb, n_fb),
        grid_spec=pltpu.PrefetchScalarGridSpec(
            num_scalar_prefetch=2,
            grid=(n_tb,),
            in_specs=[pl.BlockSpec((tb, d), lambda i, n, ids: (i, 0)),
                      pl.BlockSpec((tb, n_e), lambda i, n, ids: (i, 0)),
                      pl.BlockSpec((1, d), lambda i, n, ids: (0, 0)),
                      pl.BlockSpec(memory_space=pl.ANY)],
            out_specs=[pl.BlockSpec((tb, d), lambda i, n, ids: (jnp.minimum(i, n_fb - 1), 0)),
                       pl.BlockSpec((tb, d), lambda i, n, ids: (jnp.maximum(i - n_fb, 0), 0))],
            scratch_shapes=[pltpu.VMEM((2, n_slots, cr, d), BF16), pltpu.SemaphoreType.DMA((2 * n_slots,))]),
        out_shape=[jax.ShapeDtypeStruct((nf, d), F32), jax.ShapeDtypeStruct((ns, d), F32)],
        compiler_params=_params(("arbitrary",)),
        name="combine",
    )(n_ids, ids.reshape(-1).astype(jnp.int32), h2, pos_c, row(norm_final), ys)

    y_prompt = y_frames.reshape(nb, seq, d)
    y_sample = y_short[:db * SEQ_S].reshape(db, SEQ_S, d)

    def with_meta(frames, short, width):
        meta = jnp.broadcast_to(short[db * SEQ_S:db * SEQ_S + N_META][None], (nb, N_META, width))
        return jnp.concatenate([meta, frames.reshape(nb, seq, width)], axis=1)[None]

    c_p = with_meta(c_frames, c_short, kv_rank)
    r_p = with_meta(r_frames, r_short, QK_ROPE)
    c_s = c_short[:db * SEQ_S].reshape(1, db, SEQ_S, kv_rank)
    r_s = r_short[:db * SEQ_S].reshape(1, db, SEQ_S, QK_ROPE)
    s_p = state_f[:, HIST_ROWS - POOL_HIST:][None]
    s_s = state_s[:db, SEQ_S - POOL_HIST:][None]
    return (y_prompt, y_sample, c_p, r_p, s_p, c_s, r_s, s_s)
```
